```python
import math
import jax, jax.numpy as jnp
from jax import lax
import numpy as np

D_MODEL = 1024
BATCH = 8
SEQ = 16384
DEPTH = 4

HEAD_DIM = 64
N_HEADS_SB = 4
N_HEADS_DIL = 12
W_SB = N_HEADS_SB * HEAD_DIM
W_DIL = N_HEADS_DIL * HEAD_DIM
MIX_WIDTH = W_SB + W_DIL
IN_COLS = 4 * W_SB + 4 * W_DIL
DIL_PATTERNS = ((128, 1), (512, 4), (2048, 16))
ROPE_THETA = 500000.0
ROPE_DIM = HEAD_DIM // 4
BLOCK = 128
EPS = 1e-6

kernel_name = "hybrid_stickbreak_dilated_gated"


def _rmsnorm(x, g):
    xf = x.astype(jnp.float32)
    y = xf * lax.rsqrt(jnp.mean(xf * xf, axis=-1, keepdims=True) + EPS)
    return (y * g.astype(jnp.float32)).astype(x.dtype)


def _partial_rope(x, pos):
    half = ROPE_DIM // 2
    inv_freq = 1.0 / (ROPE_THETA ** (jnp.arange(half, dtype=jnp.float32) * 2.0 / ROPE_DIM))
    ang = pos.astype(jnp.float32)[:, None] * inv_freq[None, :]
    cos = jnp.cos(ang)[None, :, None, :]
    sin = jnp.sin(ang)[None, :, None, :]
    xf = x.astype(jnp.float32)
    x1 = xf[..., :half]
    x2 = xf[..., half:ROPE_DIM]
    rot = jnp.concatenate([x1 * cos - x2 * sin, x2 * cos + x1 * sin], axis=-1)
    return jnp.concatenate([rot, xf[..., ROPE_DIM:]], axis=-1).astype(x.dtype)


def _stick_breaking(q, k, v):
    B, S, H, Dh = q.shape
    nblk = S // BLOCK
    scale = 1.0 / math.sqrt(Dh)
    qt = q.transpose(0, 2, 1, 3)
    kt = k.transpose(0, 2, 1, 3)
    vt = v.transpose(0, 2, 1, 3)
    idx = jnp.arange(BLOCK)
    tri_in = (idx[:, None] > idx[None, :]).astype(jnp.float32)
    blk_idx = jnp.arange(nblk)
    tri_blk = (blk_idx[:, None] > blk_idx[None, :]).astype(jnp.float32)
    outs = []
    for i in range(nblk):
        nk = i + 1
        lk = nk * BLOCK
        qblk = qt[:, :, i * BLOCK:(i + 1) * BLOCK]
        z = jnp.einsum('bhqd,bhkd->bhqk', qblk, kt[:, :, :lk],
                       preferred_element_type=jnp.float32) * scale
        qpos = i * BLOCK + idx
        mask = jnp.arange(lk)[None, :] < qpos[:, None]
        ls_z = jax.nn.log_sigmoid(z)
        log_surv = jnp.where(mask, ls_z - z, 0.0)
        lsb = log_surv.reshape(B, H, BLOCK, nk, BLOCK)
        within = jnp.einsum('bhqnj,js->bhqns', lsb, tri_in,
                            precision=lax.Precision.HIGHEST)
        totals = jnp.sum(lsb, axis=-1)
        later = jnp.einsum('bhqm,mn->bhqn', totals, tri_blk[:nk, :nk],
                           precision=lax.Precision.HIGHEST)
        acc = (within + later[..., None]).reshape(B, H, BLOCK, lk)
        a = jnp.where(mask, jnp.exp(ls_z + acc), 0.0)
        o = jnp.einsum('bhqk,bhkd->bhqd', a, vt[:, :, :lk],
                       preferred_element_type=jnp.float32)
        outs.append(o.astype(q.dtype))
    o = jnp.concatenate(outs, axis=2)
    return o.transpose(0, 2, 1, 3)


def _dilated_window(q, k, v, window, dilation):
    B, S, H, Dh = q.shape
    r = dilation
    span = window // r
    L = S // r
    nb = -(-L // BLOCK)
    Lp = nb * BLOCK
    scale = 1.0 / math.sqrt(Dh)

    def to_blocks(t):
        t = t.reshape(B, L, r, H, Dh).transpose(0, 2, 1, 3, 4)
        t = jnp.pad(t, ((0, 0), (0, 0), (0, Lp - L), (0, 0), (0, 0)))
        return t.reshape(B, r, nb, BLOCK, H, Dh)

    def with_prev(t):
        prev = jnp.concatenate([jnp.zeros_like(t[:, :, :1]), t[:, :, :-1]], axis=2)
        return jnp.concatenate([prev, t], axis=3)

    qb = to_blocks(q)
    kk = with_prev(to_blocks(k))
    vv = with_prev(to_blocks(v))

    i = jnp.arange(BLOCK)[:, None]
    j = jnp.arange(2 * BLOCK)[None, :]
    n = jnp.arange(nb)[:, None, None]
    dist = BLOCK + i - j
    mask = (dist >= 0) & (dist <= span) & ((n > 0) | (j >= BLOCK))
    mask = mask[None, None, :, None]

    s = jnp.einsum('brnqhd,brnkhd->brnhqk', qb, kk,
                   preferred_element_type=jnp.float32) * scale
    s = jnp.where(mask, s, -jnp.inf)
    m = jnp.max(s, axis=-1, keepdims=True)
    p = jnp.exp(s - m)
    l = jnp.sum(p, axis=-1, keepdims=True)
    o = jnp.einsum('brnhqk,brnkhd->brnqhd', p, vv,
                   preferred_element_type=jnp.float32)
    o = o / jnp.transpose(l, (0, 1, 2, 4, 3, 5))
    log_den = (m + jnp.log(l))[..., 0].transpose(0, 1, 2, 4, 3)

    o = o.reshape(B, r, Lp, H, Dh)[:, :, :L].transpose(0, 2, 1, 3, 4).reshape(B, S, H, Dh)
    log_den = log_den.reshape(B, r, Lp, H)[:, :, :L].transpose(0, 2, 1, 3).reshape(B, S, H)
    return o, log_den


def _dilated_mixture(q, k, v):
    outs, dens = [], []
    for window, dilation in DIL_PATTERNS:
        o, ld = _dilated_window(q, k, v, window, dilation)
        outs.append(o)
        dens.append(ld)
    outs = jnp.stack(outs, axis=0)
    alpha = jax.nn.softmax(jnp.stack(dens, axis=0), axis=0)
    return jnp.sum(alpha[..., None] * outs, axis=0).astype(q.dtype)


def _fwd_setup_inputs(seed: int = 0) -> dict:
    key = jax.random.key(seed)
    ks = jax.random.split(key, 6)
    x = jax.random.normal(ks[0], (BATCH, SEQ, D_MODEL), jnp.float32)
    norm_g = 1.0 + 0.02 * jax.random.normal(ks[1], (DEPTH, D_MODEL), jnp.float32)
    w_in = jax.random.normal(ks[2], (DEPTH, D_MODEL, IN_COLS), jnp.float32) * D_MODEL ** -0.5
    q_norm_g = 1.0 + 0.02 * jax.random.normal(ks[3], (DEPTH, HEAD_DIM), jnp.float32)
    k_norm_g = 1.0 + 0.02 * jax.random.normal(ks[4], (DEPTH, HEAD_DIM), jnp.float32)
    w_out = jax.random.normal(ks[5], (DEPTH, MIX_WIDTH, D_MODEL), jnp.float32) * (
        MIX_WIDTH ** -0.5 * (2 * DEPTH) ** -0.5)
    return {"x": x, "norm_g": norm_g, "w_in": w_in, "q_norm_g": q_norm_g,
            "k_norm_g": k_norm_g, "w_out": w_out}


def _fwd_reference(x, norm_g, w_in, q_norm_g, k_norm_g, w_out):
    B, S, _ = x.shape
    pos = jnp.arange(S)
    cuts = np.cumsum([W_SB] * 4 + [W_DIL] * 3).tolist()
    for layer in range(DEPTH):
        h = _rmsnorm(x, norm_g[layer])
        proj = jnp.einsum('bsd,dc->bsc', h, w_in[layer])
        qa, ka, va, ga, qd, kd, vd, gd = jnp.split(proj, cuts, axis=-1)
        heads = lambda t, n: t.reshape(B, S, n, HEAD_DIM)

        oa = _stick_breaking(heads(qa, N_HEADS_SB), heads(ka, N_HEADS_SB), heads(va, N_HEADS_SB))
        oa = oa.reshape(B, S, W_SB) * jax.nn.silu(ga)

        qd = _partial_rope(_rmsnorm(heads(qd, N_HEADS_DIL), q_norm_g[layer]), pos)
        kd = _partial_rope(_rmsnorm(heads(kd, N_HEADS_DIL), k_norm_g[layer]), pos)
        od = _dilated_mixture(qd, kd, heads(vd, N_HEADS_DIL))
        od = od.reshape(B, S, W_DIL) * jax.nn.silu(gd)

        y = jnp.einsum('bsc,cd->bsd', jnp.concatenate([oa, od], axis=-1), w_out[layer])
        x = x + y.astype(x.dtype)
    return x


import jax as _jax
import jax.numpy as _jnp

TWIN_FORMAT = 'train_step'
FWD_PARAMS = ['x', 'norm_g', 'w_in', 'q_norm_g', 'k_norm_g', 'w_out']
TWIN_WEIGHTS = ['norm_g', 'w_in', 'q_norm_g', 'k_norm_g', 'w_out']
TWIN_DIFF_INPUT = 'x'
TWIN_INPUTS = ['x', 'norm_g', 'w_in', 'q_norm_g', 'k_norm_g', 'w_out', 'loss_target', 'm_norm_g', 'm_w_in', 'm_q_norm_g', 'm_k_norm_g', 'm_w_out', 'v_norm_g', 'v_w_in', 'v_q_norm_g', 'v_k_norm_g', 'v_w_out']
TWIN_OUTPUTS = ['loss', 'grad_x', 'grad_norm_g', 'grad_w_in', 'grad_q_norm_g', 'grad_k_norm_g', 'grad_w_out', 'delta_norm_g', 'delta_w_in', 'delta_q_norm_g', 'delta_k_norm_g', 'delta_w_out', 'new_m_norm_g', 'new_m_w_in', 'new_m_q_norm_g', 'new_m_k_norm_g', 'new_m_w_out', 'new_v_norm_g', 'new_v_w_in', 'new_v_q_norm_g', 'new_v_k_norm_g', 'new_v_w_out']
TWIN_LEAF_KINDS = {'loss': 'loss', 'grad_x': 'grad_x', 'grad_norm_g': 'grad_w', 'grad_w_in': 'grad_w', 'grad_q_norm_g': 'grad_w', 'grad_k_norm_g': 'grad_w', 'grad_w_out': 'grad_w', 'delta_norm_g': 'delta_w', 'delta_w_in': 'delta_w', 'delta_q_norm_g': 'delta_w', 'delta_k_norm_g': 'delta_w', 'delta_w_out': 'delta_w', 'new_m_norm_g': 'new_m', 'new_m_w_in': 'new_m', 'new_m_q_norm_g': 'new_m', 'new_m_k_norm_g': 'new_m', 'new_m_w_out': 'new_m', 'new_v_norm_g': 'new_v', 'new_v_w_in': 'new_v', 'new_v_q_norm_g': 'new_v', 'new_v_k_norm_g': 'new_v', 'new_v_w_out': 'new_v'}


def _forward(args):
    return _fwd_reference(*[args[k] for k in FWD_PARAMS])


def _output_shape():
    def fwd():
        inp = _fwd_setup_inputs(0)
        return _fwd_reference(*[inp[k] for k in FWD_PARAMS])
    out = _jax.eval_shape(fwd)
    return out.shape, out.dtype

N_MICROBATCH = 1
ADAM_LR = 0.001
ADAM_B1 = 0.9
ADAM_B2 = 0.999
ADAM_EPS = 1e-08
ADAM_WD = 0.01
ADAM_STEP = 10
PER_EXAMPLE_BATCH_AXIS = {'x': 0, 'loss_target': 0}
SHARED_INPUTS = []
_WEIGHT_DTYPES = {'norm_g': _jnp.float32, 'w_in': _jnp.float32, 'q_norm_g': _jnp.float32, 'k_norm_g': _jnp.float32, 'w_out': _jnp.float32}
MOMENT_SCALE = {'norm_g': 1.202344e+00, 'w_in': 3.553258e-02, 'q_norm_g': 3.820644e-01, 'k_norm_g': 3.846304e-01, 'w_out': 1.171711e-01}


def _to_microbatches(a, axis):
    t = _jnp.moveaxis(a, axis, 0)
    t = t.reshape((N_MICROBATCH, t.shape[0] // N_MICROBATCH) + t.shape[1:])
    return _jnp.moveaxis(t, 1, axis + 1)


def setup_inputs(seed: int = 0) -> dict:
    inp = _fwd_setup_inputs(seed)
    key = _jax.random.fold_in(_jax.random.key(seed), 7919)
    shape, _ = _output_shape()
    out = dict(inp)
    out["loss_target"] = _jax.random.normal(_jax.random.fold_in(key, 0), shape, _jnp.float32)
    for i, name in enumerate(TWIN_WEIGHTS):
        w = inp[name].astype(_jnp.float32)
        if MOMENT_SCALE is None:
            s = _jnp.sqrt(_jnp.mean(_jnp.square(w)) + 1e-30)
        else:
            s = MOMENT_SCALE[name]
        km, kv = _jax.random.split(_jax.random.fold_in(key, i + 1))
        out[name] = w
        out["m_" + name] = s * _jax.random.normal(km, w.shape, _jnp.float32)
        out["v_" + name] = (s * s) * _jax.random.uniform(kv, w.shape, _jnp.float32, 0.5, 1.5)
    if N_MICROBATCH > 1:
        for name, axis in PER_EXAMPLE_BATCH_AXIS.items():
            out[name] = _to_microbatches(out[name], axis)
    return {'x': out['x'], 'norm_g': out['norm_g'], 'w_in': out['w_in'], 'q_norm_g': out['q_norm_g'], 'k_norm_g': out['k_norm_g'], 'w_out': out['w_out'], 'loss_target': out['loss_target'], 'm_norm_g': out['m_norm_g'], 'm_w_in': out['m_w_in'], 'm_q_norm_g': out['m_q_norm_g'], 'm_k_norm_g': out['m_k_norm_g'], 'm_w_out': out['m_w_out'], 'v_norm_g': out['v_norm_g'], 'v_w_in': out['v_w_in'], 'v_q_norm_g': out['v_q_norm_g'], 'v_k_norm_g': out['v_k_norm_g'], 'v_w_out': out['v_w_out']}


def _loss(weights, diff, rest, loss_target):
    with _jax.named_scope("forward"):
        args = {**rest, TWIN_DIFF_INPUT: diff, **{k: w.astype(_WEIGHT_DTYPES[k]) for k, w in weights.items()}}
        y = _forward(args)
    with _jax.named_scope("loss_head"):
        err = _jnp.square(y.astype(_jnp.float32) - loss_target)
        return 0.5 * _jnp.sum(_jnp.mean(err, axis=-1)) if err.ndim else 0.5 * err


def _adamw(w, g, m, v):
    m = ADAM_B1 * m + (1.0 - ADAM_B1) * g
    v = ADAM_B2 * v + (1.0 - ADAM_B2) * _jnp.square(g)
    m_hat = m / (1.0 - ADAM_B1 ** ADAM_STEP)
    v_hat = v / (1.0 - ADAM_B2 ** ADAM_STEP)
    delta = -ADAM_LR * (m_hat / (_jnp.sqrt(v_hat) + ADAM_EPS) + ADAM_WD * w)
    return delta, m, v


def reference(x, norm_g, w_in, q_norm_g, k_norm_g, w_out, loss_target, m_norm_g, m_w_in, m_q_norm_g, m_k_norm_g, m_w_out, v_norm_g, v_w_in, v_q_norm_g, v_k_norm_g, v_w_out):
    given = dict(x=x, norm_g=norm_g, w_in=w_in, q_norm_g=q_norm_g, k_norm_g=k_norm_g, w_out=w_out, loss_target=loss_target, m_norm_g=m_norm_g, m_w_in=m_w_in, m_q_norm_g=m_q_norm_g, m_k_norm_g=m_k_norm_g, m_w_out=m_w_out, v_norm_g=v_norm_g, v_w_in=v_w_in, v_q_norm_g=v_q_norm_g, v_k_norm_g=v_k_norm_g, v_w_out=v_w_out)
    weights = {n: given[n] for n in TWIN_WEIGHTS}
    shared = {n: given[n] for n in SHARED_INPUTS}
    per_example = {n: given[n] for n in ['x']}
    grad_fn = _jax.value_and_grad(_loss, argnums=(0, 1))

    def one_microbatch(ex, loss_target):
        ex = dict(ex)
        diff = ex.pop(TWIN_DIFF_INPUT)
        return grad_fn(weights, diff, {**shared, **ex}, loss_target)

    if N_MICROBATCH == 1:
        loss, (grad_w, grad_x) = one_microbatch(per_example, given["loss_target"])
    else:
        def body(carry, xs):
            loss_sum, grad_sum = carry
            l_k, (gw_k, gx_k) = one_microbatch(xs[0], xs[1])
            with _jax.named_scope("update"):
                return (loss_sum + l_k, _jax.tree.map(_jnp.add, grad_sum, gw_k)), gx_k

        init = (_jnp.zeros((), _jnp.float32), _jax.tree.map(_jnp.zeros_like, weights))
        (loss, grad_w), grad_x = _jax.lax.scan(body, init, (per_example, given["loss_target"]))
    with _jax.named_scope("update"):
        delta_w, new_m, new_v = {}, {}, {}
        for n in TWIN_WEIGHTS:
            delta_w[n], new_m[n], new_v[n] = _adamw(weights[n], grad_w[n], given["m_" + n], given["v_" + n])
    return (loss, grad_x, *[grad_w[n] for n in TWIN_WEIGHTS], *[delta_w[n] for n in TWIN_WEIGHTS],
            *[new_m[n] for n in TWIN_WEIGHTS], *[new_v[n] for n in TWIN_WEIGHTS])
```

```python
import functools
import math

import jax
import jax.numpy as jnp
from jax import lax
from jax.experimental import pallas as pl
from jax.experimental.pallas import tpu as pltpu

F32 = jnp.float32
BF16 = jnp.bfloat16

EPS = 1e-6
HEAD_DIM = 64
BLOCK = 128
LANES = 128
W_SB = 256
W_DIL = 768
MIX = W_SB + W_DIL
IN_COLS = 4 * W_SB + 4 * W_DIL
N_DEV = 8
COLS_PER_DEV = IN_COLS // N_DEV
ROWS_PER_DEV = MIX // N_DEV
QK_SCALE = 1.0 / math.sqrt(HEAD_DIM)
DIL_PATTERNS = ((128, 1), (512, 4), (2048, 16))
ROPE_THETA = 500000.0
ROPE_DIM = HEAD_DIM // 4
ROPE_HALF = ROPE_DIM // 2
DEAD_LOG = -110.0
NEG_BIG = -1e30

ADAM_LR = 0.001
ADAM_B1 = 0.9
ADAM_B2 = 0.999
ADAM_EPS = 1e-08
ADAM_WD = 0.01
ADAM_STEP = 10

SMALL_W = 1280
MESH_ID = pl.DeviceIdType.MESH
MIB = 1 << 20


def _params(vmem_mib):
    return pltpu.CompilerParams(vmem_limit_bytes=vmem_mib * MIB)


def _dot(a, b):
    return jnp.dot(a, b, preferred_element_type=F32)


def _dot_nt(a, b):
    return lax.dot_general(a, b, (((1,), (1,)), ((), ())), preferred_element_type=F32)


def _dot_tn(a, b):
    return lax.dot_general(a, b, (((0,), (0,)), ((), ())), preferred_element_type=F32)


def _dot_exact(x, m01):
    hi = x.astype(BF16)
    r1 = x - hi.astype(F32)
    mid = r1.astype(BF16)
    lo = (r1 - mid.astype(F32)).astype(BF16)
    return _dot(hi, m01) + _dot(mid, m01) + _dot(lo, m01)


def _lane_lo(shape):
    return lax.broadcasted_iota(jnp.int32, shape, 1) < HEAD_DIM


def _col_of(b, lo, h):
    keep = lo if h == 0 else jnp.logical_not(lo)
    return jnp.max(jnp.where(keep, b, NEG_BIG), axis=1, keepdims=True)


def _tri_constants():
    j = jnp.arange(BLOCK)
    ones = jnp.ones((BLOCK, BLOCK), F32)
    excl = (j[:, None] > j[None, :]).astype(F32)
    incl = (j[:, None] >= j[None, :]).astype(F32)
    tri_a = jnp.concatenate([excl, ones], axis=1).astype(BF16)
    tri_b = jnp.concatenate([incl, ones], axis=1).astype(BF16)
    d = jnp.arange(2 * LANES)
    bd = (d[:, None] // HEAD_DIM == d[None, :] // HEAD_DIM).astype(BF16)
    return tri_a, tri_b, bd


def _rope_tables(seq):
    inv_freq = 1.0 / (ROPE_THETA ** (jnp.arange(ROPE_HALF, dtype=F32) * 2.0 / ROPE_DIM))
    ang = jnp.arange(seq).astype(F32)[:, None] * inv_freq[None, :]
    cos, sin = jnp.cos(ang), jnp.sin(ang)
    one = jnp.ones((seq, HEAD_DIM - ROPE_DIM), F32)
    zero8 = jnp.zeros((seq, ROPE_HALF), F32)
    zero_rest = jnp.zeros((seq, HEAD_DIM - ROPE_DIM), F32)
    c = jnp.concatenate([cos, cos, one], axis=1)
    s_next = jnp.concatenate([-sin, zero8, zero_rest], axis=1)
    s_prev = jnp.concatenate([zero8, sin, zero_rest], axis=1)
    rep = (2 * LANES) // HEAD_DIM
    return tuple(jnp.tile(t, (1, rep)) for t in (c, s_next, s_prev))


def _roll_lanes(x, shift):
    return jnp.concatenate([pltpu.roll(x[:, :LANES], shift, 1), pltpu.roll(x[:, LANES:], shift, 1)], axis=1)


def _rope(x, c, s_next, s_prev):
    return x * c + _roll_lanes(x, LANES - ROPE_HALF) * s_next + _roll_lanes(x, ROPE_HALF) * s_prev


def _rope_t(dy, c, s_next, s_prev):
    return dy * c + _roll_lanes(dy * s_next, ROPE_HALF) + _roll_lanes(dy * s_prev, LANES - ROPE_HALF)


def _cast_bf16(w, name):
    nl, r, c = w.shape

    def body(w_ref, o_ref):
        o_ref[...] = w_ref[...].astype(BF16)

    return pl.pallas_call(
        body, name=name, grid=(nl,),
        in_specs=[pl.BlockSpec((None, r, c), lambda l: (l, 0, 0))],
        out_specs=pl.BlockSpec((None, r, c), lambda l: (l, 0, 0)),
        out_shape=jax.ShapeDtypeStruct(w.shape, BF16),
        compiler_params=_params(24),
    )(w)


def _flips():
    return [(dx, dy, dc) for dx in (0, 1) for dy in (0, 1) for dc in (0, 1) if (dx, dy, dc) != (0, 0, 0)]


def _place():
    x, y, c = lax.axis_index("x"), lax.axis_index("y"), lax.axis_index("c")
    return x, y, c, 4 * x + 2 * y + c


def _peer(x, y, c, flip):
    dx, dy, dc = flip
    return (1 - x if dx else x, 1 - y if dy else y, 1 - c if dc else c)


def _all_gather_weights(win_bf, wout_bf):
    flips = _flips()

    def body(win_ref, wout_ref, oin_ref, oout_ref, send_sems, recv_sems, local_sems):
        x, y, c, me = _place()
        local = [pltpu.make_async_copy(win_ref, oin_ref.at[me], local_sems.at[0]),
                 pltpu.make_async_copy(wout_ref, oout_ref.at[me], local_sems.at[1])]
        for cp in local:
            cp.start()
        copies = []
        for k, flip in enumerate(flips):
            for a, (src, dst) in enumerate(((win_ref, oin_ref), (wout_ref, oout_ref))):
                cp = pltpu.make_async_remote_copy(
                    src_ref=src, dst_ref=dst.at[me], send_sem=send_sems.at[2 * k + a], recv_sem=recv_sems.at[2 * k + a],
                    device_id=_peer(x, y, c, flip), device_id_type=MESH_ID)
                cp.start()
                copies.append(cp)
        for cp in copies:
            cp.wait()
        for cp in local:
            cp.wait()

    any_spec = pl.BlockSpec(memory_space=pl.ANY)
    n = 2 * len(flips)
    return pl.pallas_call(
        body, name="all_gather_weights",
        in_specs=[any_spec, any_spec], out_specs=[any_spec, any_spec],
        out_shape=[jax.ShapeDtypeStruct((N_DEV,) + win_bf.shape, BF16), jax.ShapeDtypeStruct((N_DEV,) + wout_bf.shape, BF16)],
        scratch_shapes=[pltpu.SemaphoreType.DMA((n,)), pltpu.SemaphoreType.DMA((n,)), pltpu.SemaphoreType.DMA((2,))],
    )(win_bf, wout_bf)


_F32_ROUTES = {1: ((256, 256, 0, 0),), 2: ((0, 512, 1, 0),), 3: ((0, 512, 1, 512),), 4: ((0, 512, 1, 1024),),
               6: ((256, 256, 0, 256),), 7: ((0, 512, 0, 512),)}


def _norm_proj(x, g, w_all, layer):
    seq, d = x.shape
    tm = min(256, seq)

    def body(x_ref, g_ref, w_ref, pbf_ref, gates_ref, qk_ref, h_ref):
        xf = x_ref[...]
        rs = lax.rsqrt(jnp.mean(xf * xf, axis=-1, keepdims=True) + EPS)
        h = (xf * rs * g_ref[...]).astype(BF16)
        h_ref[...] = h
        targets = (gates_ref, qk_ref)
        for n in range(N_DEV):
            acc = _dot(h, w_ref[n])
            pbf_ref[:, n * COLS_PER_DEV:(n + 1) * COLS_PER_DEV] = acc.astype(BF16)
            for lo, width, tgt, dst in _F32_ROUTES.get(n, ()):
                targets[tgt][:, dst:dst + width] = acc[:, lo:lo + width]

    row = lambda w: pl.BlockSpec((tm, w), lambda i: (i, 0))
    return pl.pallas_call(
        body, name="norm_proj", grid=(seq // tm,),
        in_specs=[row(d), pl.BlockSpec((1, d), lambda i: (0, 0)),
                  pl.BlockSpec((N_DEV, None, d, COLS_PER_DEV), lambda i: (0, layer, 0, 0))],
        out_specs=[row(IN_COLS), row(MIX), row(2 * W_DIL), row(d)],
        out_shape=[jax.ShapeDtypeStruct((seq, IN_COLS), BF16), jax.ShapeDtypeStruct((seq, MIX), F32),
                   jax.ShapeDtypeStruct((seq, 2 * W_DIL), F32), jax.ShapeDtypeStruct((seq, d), BF16)],
        compiler_params=_params(48),
    )(x, g, w_all)


def _head_sums(v, bd):
    hi = v.astype(BF16)
    lo = (v - hi.astype(F32)).astype(BF16)
    return _dot(hi, bd) + _dot(lo, bd)


def _qk_prep(qk_raw, gains, tables, bd):
    seq = qk_raw.shape[0]
    tm = min(1024, seq)
    chunks = W_DIL // (2 * LANES)

    def body(x_ref, g_ref, c_ref, sn_ref, sp_ref, bd_ref, o_ref):
        j = pl.program_id(1)
        x = x_ref[...]
        rs = lax.rsqrt(_head_sums(x * x, bd_ref[...]) * (1.0 / HEAD_DIM) + EPS)
        y = _rope(x * rs * g_ref[...], c_ref[...], sn_ref[...], sp_ref[...])
        o_ref[...] = (y * jnp.where(j < chunks, QK_SCALE, 1.0)).astype(BF16)

    tab = pl.BlockSpec((tm, 2 * LANES), lambda i, j: (i, 0))
    return pl.pallas_call(
        body, name="qk_prep", grid=(seq // tm, 2 * chunks),
        in_specs=[pl.BlockSpec((tm, 2 * LANES), lambda i, j: (i, j)),
                  pl.BlockSpec((None, 1, 2 * LANES), lambda i, j: (j // chunks, 0, 0)),
                  tab, tab, tab, pl.BlockSpec((2 * LANES, 2 * LANES), lambda i, j: (0, 0))],
        out_specs=pl.BlockSpec((tm, 2 * LANES), lambda i, j: (i, j)),
        out_shape=jax.ShapeDtypeStruct((seq, 2 * W_DIL), BF16),
        compiler_params=_params(32),
    )(qk_raw, gains, *tables, bd)


def _sb_scores(qm_h, kb, tri, r_h, causal):
    z = _dot_nt(qm_h, kb)
    ls = jnp.minimum(z, 0.0) - jnp.log1p(jnp.exp(-jnp.abs(z)))
    lsv = ls - z
    if causal is not None:
        lsv = jnp.where(causal, lsv, 0.0)
    ct = _dot_exact(lsv, tri)
    a = jnp.exp(ls + ct[:, :BLOCK] + r_h)
    if causal is not None:
        a = jnp.where(causal, a, 0.0)
    return ls, a, ct[:, BLOCK:]


def _sb_fwd(proj_bf, tri_a):
    seq = proj_bf.shape[0]
    pairs = W_SB // LANES

    def body(q_ref, k_ref, v_ref, tri_ref, o_ref, r_ref, acc_ref):
        i = pl.program_id(1)
        lo = _lane_lo((BLOCK, LANES))
        q = q_ref[...].astype(F32) * QK_SCALE
        qm = (jnp.where(lo, q, 0.0).astype(BF16), jnp.where(lo, 0.0, q).astype(BF16))
        causal = lax.broadcasted_iota(jnp.int32, (BLOCK, BLOCK), 1) < lax.broadcasted_iota(jnp.int32, (BLOCK, BLOCK), 0)
        tri = tri_ref[...]
        r_ref[...] = jnp.zeros_like(r_ref)
        acc_ref[...] = jnp.zeros_like(acc_ref)

        def block(j, mask):
            rows = pl.ds(pl.multiple_of(j * BLOCK, BLOCK), BLOCK)
            kb, vb = k_ref[rows, :], v_ref[rows, :]
            for h in range(2):
                _, a, tot = _sb_scores(qm[h], kb, tri, r_ref[h], mask)
                a_hi = a.astype(BF16)
                a_lo = (a - a_hi.astype(F32)).astype(BF16)
                acc_ref[h] += _dot(a_hi, vb) + _dot(a_lo, vb)
                r_ref[h] += tot

        def alive():
            return (jnp.max(jnp.maximum(r_ref[0], r_ref[1])) > DEAD_LOG).astype(jnp.int32)

        block(i, causal)

        def step(carry):
            block(carry[0], None)
            return carry[0] - 1, alive()

        lax.while_loop(lambda carry: jnp.logical_and(carry[0] >= 0, carry[1] > 0), step, (i - 1, alive()))
        o_ref[...] = jnp.where(lo, acc_ref[0], acc_ref[1])

    return pl.pallas_call(
        body, name="sb_fwd", grid=(pairs, seq // BLOCK),
        in_specs=[pl.BlockSpec((BLOCK, LANES), lambda p, i: (i, p)),
                  pl.BlockSpec((seq, LANES), lambda p, i: (0, pairs + p)),
                  pl.BlockSpec((seq, LANES), lambda p, i: (0, 2 * pairs + p)),
                  pl.BlockSpec((BLOCK, 2 * BLOCK), lambda p, i: (0, 0))],
        out_specs=pl.BlockSpec((BLOCK, LANES), lambda p, i: (i, p)),
        out_shape=jax.ShapeDtypeStruct((seq, W_SB), F32),
        scratch_shapes=[pltpu.VMEM((2, BLOCK, LANES), F32), pltpu.VMEM((2, BLOCK, LANES), F32)],
        compiler_params=_params(40),
    )(proj_bf, proj_bf, proj_bf, tri_a)


def _dil_fwd(qkn, proj_bf, r):
    seq = qkn.shape[0]
    ln = seq // r
    nblk = ln // BLOCK
    tq = min(512, ln)
    per = tq // BLOCK
    pairs = W_DIL // LANES
    q_unit, p_unit, o_unit = 2 * W_DIL // LANES, IN_COLS // LANES, W_DIL // LANES
    v_off = (4 * W_SB + 2 * W_DIL) // LANES

    def body(q_ref, kc_ref, kp_ref, vc_ref, vp_ref, o_ref, ld_ref, kf, vf):
        n = pl.program_id(2)
        kf[0:BLOCK, :] = kp_ref[...]
        kf[BLOCK:, :] = kc_ref[...]
        vf[0:BLOCK, :] = vp_ref[...]
        vf[BLOCK:, :] = vc_ref[...]
        lo = _lane_lo((BLOCK, LANES))
        row = lax.broadcasted_iota(jnp.int32, (BLOCK, 2 * BLOCK), 0)
        col = lax.broadcasted_iota(jnp.int32, (BLOCK, 2 * BLOCK), 1)
        band = jnp.logical_and(col >= row, col <= row + BLOCK)
        first = jnp.logical_and(band, jnp.logical_or(col >= BLOCK, n > 0))
        for i in range(per):
            qb = q_ref[i * BLOCK:(i + 1) * BLOCK, :]
            kcat = kf[i * BLOCK:(i + 2) * BLOCK, :]
            vcat = vf[i * BLOCK:(i + 2) * BLOCK, :]
            valid = first if i == 0 else band
            outs, lds = [], []
            for h in range(2):
                qh = jnp.where(lo if h == 0 else jnp.logical_not(lo), qb, jnp.zeros_like(qb))
                s = jnp.where(valid, _dot_nt(qh, kcat), NEG_BIG)
                m = jnp.max(s, axis=1, keepdims=True)
                p = jnp.exp(s - m)
                l = jnp.sum(p, axis=1, keepdims=True)
                outs.append(_dot(p.astype(BF16), vcat) / l)
                lds.append(jnp.broadcast_to(m + jnp.log(l), (BLOCK, LANES)))
            o_ref[i * BLOCK:(i + 1) * BLOCK, :] = jnp.where(lo, outs[0], outs[1])
            ld_ref[i * BLOCK:(i + 1) * BLOCK, :] = jnp.where(lo, lds[0], lds[1])

    prev = lambda n: jnp.maximum(n * per - 1, 0)
    cur = lambda unit, off: pl.BlockSpec((tq, LANES), lambda c, p, n: (n, c * unit + off + p))
    edge = lambda unit, off: pl.BlockSpec((BLOCK, LANES), lambda c, p, n: (prev(n), c * unit + off + p))
    qk2 = qkn.reshape(ln, r * 2 * W_DIL)
    pj2 = proj_bf.reshape(ln, r * IN_COLS)
    o, ld = pl.pallas_call(
        body, name=f"dil_fwd_r{r}", grid=(r, pairs, ln // tq),
        in_specs=[cur(q_unit, 0), cur(q_unit, pairs), edge(q_unit, pairs), cur(p_unit, v_off), edge(p_unit, v_off)],
        out_specs=[cur(o_unit, 0), cur(o_unit, 0)],
        out_shape=[jax.ShapeDtypeStruct((ln, r * W_DIL), F32), jax.ShapeDtypeStruct((ln, r * W_DIL), F32)],
        scratch_shapes=[pltpu.VMEM((tq + BLOCK, LANES), BF16), pltpu.VMEM((tq + BLOCK, LANES), BF16)],
        compiler_params=_params(32),
    )(qk2, qk2, qk2, pj2, pj2)
    return o.reshape(seq, W_DIL), ld.reshape(seq, W_DIL)


def _silu_parts(g):
    sig = jax.nn.sigmoid(g)
    return g * sig, sig * (1.0 + g * (1.0 - sig))


def _out_proj(x, oa, o_p, ld_p, gates, wout_all, layer):
    seq, d = x.shape
    tm = min(256, seq)

    def body(x_ref, oa_ref, o0, o1, o2, l0, l1, l2, g_ref, w_ref, xn_ref, cat_ref, od_ref, lse_ref):
        lds = (l0[...], l1[...], l2[...])
        m = jnp.maximum(jnp.maximum(lds[0], lds[1]), lds[2])
        es = [jnp.exp(v - m) for v in lds]
        tot = es[0] + es[1] + es[2]
        lse_ref[...] = m + jnp.log(tot)
        inv = 1.0 / tot
        od = (es[0] * inv) * o0[...] + (es[1] * inv) * o1[...] + (es[2] * inv) * o2[...]
        od_ref[...] = od
        silu, _ = _silu_parts(g_ref[...])
        cat_ref[:, :W_SB] = (oa_ref[...] * silu[:, :W_SB]).astype(BF16)
        cat_ref[:, W_SB:] = (od * silu[:, W_SB:]).astype(BF16)
        y = x_ref[...]
        for b in range(N_DEV):
            y = y + _dot(cat_ref[:, b * ROWS_PER_DEV:(b + 1) * ROWS_PER_DEV], w_ref[b])
        xn_ref[...] = y

    row = lambda w: pl.BlockSpec((tm, w), lambda i: (i, 0))
    return pl.pallas_call(
        body, name="out_proj", grid=(seq // tm,),
        in_specs=[row(d), row(W_SB)] + [row(W_DIL)] * 6 + [row(MIX),
                  pl.BlockSpec((N_DEV, None, ROWS_PER_DEV, d), lambda i: (0, layer, 0, 0))],
        out_specs=[row(d), row(MIX), row(W_DIL), row(W_DIL)],
        out_shape=[jax.ShapeDtypeStruct((seq, d), F32), jax.ShapeDtypeStruct((seq, MIX), BF16),
                   jax.ShapeDtypeStruct((seq, W_DIL), F32), jax.ShapeDtypeStruct((seq, W_DIL), F32)],
        compiler_params=_params(48),
    )(x, oa, *o_p, *ld_p, gates, wout_all)


def _loss_head(y, target):
    seq, d = y.shape
    tm = min(512, seq)

    def body(y_ref, t_ref, part_ref, dy_ref):
        @pl.when(pl.program_id(0) == 0)
        def _():
            part_ref[...] = jnp.zeros_like(part_ref)

        diff = y_ref[...] - t_ref[...]
        dy_ref[...] = diff * (1.0 / d)
        part_ref[...] += jnp.sum((diff * diff).reshape(tm // 8, 8, d), axis=0) * (0.5 / d)

    row = pl.BlockSpec((tm, d), lambda i: (i, 0))
    return pl.pallas_call(
        body, name="loss_head", grid=(seq // tm,),
        in_specs=[row, row], out_specs=[pl.BlockSpec((8, d), lambda i: (0, 0)), row],
        out_shape=[jax.ShapeDtypeStruct((8, d), F32), jax.ShapeDtypeStruct((seq, d), F32)],
        compiler_params=_params(32),
    )(y, target)


def _out_proj_bwd(dy, wout_all, layer, cat, gates, oa, od, bd, dwout_acc):
    seq, d = dy.shape
    tm = min(256, seq)

    def body(dy_ref, w_ref, cat_ref, g_ref, oa_ref, od_ref, bd_ref, acc_in, doa_ref, dod_ref, delta_ref, dg_ref, dw_ref, dcat):
        del acc_in

        @pl.when(pl.program_id(0) == 0)
        def _():
            dw_ref[...] = jnp.zeros_like(dw_ref)

        dyb = dy_ref[...].astype(BF16)
        dw = _dot_tn(cat_ref[...], dyb)
        for b in range(N_DEV):
            dw_ref[b] += dw[b * ROWS_PER_DEV:(b + 1) * ROWS_PER_DEV, :]
            dcat[:, b * ROWS_PER_DEV:(b + 1) * ROWS_PER_DEV] = _dot_nt(dyb, w_ref[b])
        silu, dsilu = _silu_parts(g_ref[...])
        dc = dcat[...]
        dmix = dc * silu
        oa_v, od_v = oa_ref[...], od_ref[...]
        dg_ref[:, :W_SB] = dc[:, :W_SB] * oa_v * dsilu[:, :W_SB]
        dg_ref[:, W_SB:] = dc[:, W_SB:] * od_v * dsilu[:, W_SB:]
        doa_ref[...] = dmix[:, :W_SB]
        dod = dmix[:, W_SB:]
        dod_ref[...] = dod.astype(BF16)
        prod = dod * od_v
        for k in range(W_DIL // (2 * LANES)):
            sl = slice(k * 2 * LANES, (k + 1) * 2 * LANES)
            delta_ref[:, sl] = _head_sums(prod[:, sl], bd_ref[...])

    row = lambda w: pl.BlockSpec((tm, w), lambda i: (i, 0))
    slab = pl.BlockSpec((N_DEV, None, ROWS_PER_DEV, d), lambda i: (0, layer, 0, 0))
    return pl.pallas_call(
        body, name="out_proj_bwd", grid=(seq // tm,),
        in_specs=[row(d), slab, row(MIX), row(MIX), row(W_SB), row(W_DIL),
                  pl.BlockSpec((2 * LANES, 2 * LANES), lambda i: (0, 0)), pl.BlockSpec(memory_space=pl.ANY)],
        out_specs=[row(W_SB), row(W_DIL), row(W_DIL), row(MIX), slab],
        out_shape=[jax.ShapeDtypeStruct((seq, W_SB), F32), jax.ShapeDtypeStruct((seq, W_DIL), BF16),
                   jax.ShapeDtypeStruct((seq, W_DIL), F32), jax.ShapeDtypeStruct((seq, MIX), F32),
                   jax.ShapeDtypeStruct(dwout_acc.shape, F32)],
        scratch_shapes=[pltpu.VMEM((tm, MIX), F32)],
        input_output_aliases={7: 4},
        compiler_params=_params(48),
    )(dy, wout_all, cat, gates, oa, od, bd, dwout_acc)


def _sb_bwd(proj_bf, d_oa, oa, tri_a, tri_b):
    seq = proj_bf.shape[0]
    pairs = W_SB // LANES
    nq = seq // BLOCK

    def body(q_ref, k_ref, v_ref, do_ref, o_ref, tria_ref, trib_ref, dq_ref, dk_hbm, dv_hbm,
             r_ref, sfx_ref, dtot_ref, dq_acc, dk_acc, dv_acc, sems):
        p, i = pl.program_id(0), pl.program_id(1)

        @pl.when(i == 0)
        def _():
            dk_acc[...] = jnp.zeros_like(dk_acc)
            dv_acc[...] = jnp.zeros_like(dv_acc)

        lo = _lane_lo((BLOCK, LANES))
        q = q_ref[...].astype(F32) * QK_SCALE
        qm = (jnp.where(lo, q, 0.0).astype(BF16), jnp.where(lo, 0.0, q).astype(BF16))
        do = do_ref[...]
        dom = (jnp.where(lo, do, 0.0).astype(BF16), jnp.where(lo, 0.0, do).astype(BF16))
        prod = (dom[0].astype(F32) + dom[1].astype(F32)) * o_ref[...]
        tri, trib = tria_ref[...], trib_ref[...]
        ones = tri[:, BLOCK:]
        dtot_ref[0] = _dot_exact(jnp.where(lo, prod, 0.0), ones)
        dtot_ref[1] = _dot_exact(jnp.where(lo, 0.0, prod), ones)
        causal = lax.broadcasted_iota(jnp.int32, (BLOCK, BLOCK), 1) < lax.broadcasted_iota(jnp.int32, (BLOCK, BLOCK), 0)
        for ref in (r_ref, sfx_ref, dq_acc):
            ref[...] = jnp.zeros_like(ref)

        def block(j, mask):
            rows = pl.ds(pl.multiple_of(j * BLOCK, BLOCK), BLOCK)
            kb, vb = k_ref[rows, :], v_ref[rows, :]
            for h in range(2):
                ls, a, tot = _sb_scores(qm[h], kb, tri, r_ref[h], mask)
                beta = jnp.exp(ls)
                pw = a * _dot_nt(dom[h], vb)
                cp = _dot_exact(pw, trib)
                before = dtot_ref[h] - sfx_ref[h] - cp[:, :BLOCK]
                dz = pw * (1.0 - beta) - before * beta
                if mask is not None:
                    dz = jnp.where(mask, dz, 0.0)
                dzb = dz.astype(BF16)
                dq_acc[h] += _dot(dzb, kb)
                dk_acc[rows, :] += _dot_tn(dzb, qm[h])
                dv_acc[rows, :] += _dot_tn(a.astype(BF16), dom[h])
                r_ref[h] += tot
                sfx_ref[h] += cp[:, BLOCK:]

        def alive():
            return (jnp.max(jnp.maximum(r_ref[0], r_ref[1])) > DEAD_LOG).astype(jnp.int32)

        block(i, causal)

        def step(carry):
            block(carry[0], None)
            return carry[0] - 1, alive()

        lax.while_loop(lambda carry: jnp.logical_and(carry[0] >= 0, carry[1] > 0), step, (i - 1, alive()))
        dq_ref[...] = jnp.where(lo, dq_acc[0], dq_acc[1]) * QK_SCALE

        @pl.when(i == nq - 1)
        def _():
            outs = [pltpu.make_async_copy(dk_acc, dk_hbm.at[p], sems.at[0]),
                    pltpu.make_async_copy(dv_acc, dv_hbm.at[p], sems.at[1])]
            for cp in outs:
                cp.start()
            for cp in outs:
                cp.wait()

    blk = pl.BlockSpec((BLOCK, LANES), lambda p, i: (i, p))
    const = pl.BlockSpec((BLOCK, 2 * BLOCK), lambda p, i: (0, 0))
    any_spec = pl.BlockSpec(memory_space=pl.ANY)
    acc = pltpu.VMEM((2, BLOCK, LANES), F32)
    return pl.pallas_call(
        body, name="sb_bwd", grid=(pairs, nq),
        in_specs=[blk, pl.BlockSpec((seq, LANES), lambda p, i: (0, pairs + p)),
                  pl.BlockSpec((seq, LANES), lambda p, i: (0, 2 * pairs + p)), blk, blk, const, const],
        out_specs=[blk, any_spec, any_spec],
        out_shape=[jax.ShapeDtypeStruct((seq, W_SB), F32), jax.ShapeDtypeStruct((pairs, seq, LANES), F32),
                   jax.ShapeDtypeStruct((pairs, seq, LANES), F32)],
        scratch_shapes=[acc, acc, acc, acc, pltpu.VMEM((seq, LANES), F32), pltpu.VMEM((seq, LANES), F32),
                        pltpu.SemaphoreType.DMA((2,))],
        compiler_params=_params(56),
    )(proj_bf, proj_bf, proj_bf, d_oa, oa, tri_a, tri_b)


def _dil_bwd(qkn, proj_bf, d_od, lse, delta, r):
    seq = qkn.shape[0]
    ln = seq // r
    nblk = ln // BLOCK
    tq = min(512, ln)
    per = tq // BLOCK
    ntile = ln // tq
    pairs = W_DIL // LANES
    q_unit, p_unit, o_unit = 2 * W_DIL // LANES, IN_COLS // LANES, W_DIL // LANES
    v_off = (4 * W_SB + 2 * W_DIL) // LANES

    def body(qc, qn, doc, don, lsc, lsn, dlc, dln, kc, kp, vc, vp, dq_ref, dk_ref, dv_ref, qf, dof, lsf, dlf, kf, vf):
        n = pl.program_id(2)
        for full, a, b in ((qf, qc, qn), (dof, doc, don), (lsf, lsc, lsn), (dlf, dlc, dln)):
            full[0:tq, :] = a[...]
            full[tq:, :] = b[...]
        for full, a, b in ((kf, kp, kc), (vf, vp, vc)):
            full[0:BLOCK, :] = a[...]
            full[BLOCK:, :] = b[...]
        for ref in (dq_ref, dk_ref, dv_ref):
            ref[...] = jnp.zeros_like(ref)
        lo = _lane_lo((BLOCK, LANES))
        row = lax.broadcasted_iota(jnp.int32, (BLOCK, BLOCK), 0)
        col = lax.broadcasted_iota(jnp.int32, (BLOCK, BLOCK), 1)
        same = col <= row
        before = col >= row
        pairs_ab = [(a, a, same) for a in range(per)] + [(a, a - 1, before) for a in range(per + 1)]
        for a, b, mask in pairs_ab:
            if b < 0:
                mask = jnp.logical_and(mask, n > 0)
            if a == per:
                mask = jnp.logical_and(mask, n < ntile - 1)
            qs = slice(a * BLOCK, (a + 1) * BLOCK)
            ks = slice((b + 1) * BLOCK, (b + 2) * BLOCK)
            qb, dob, kb, vb = qf[qs, :], dof[qs, :], kf[ks, :], vf[ks, :]
            lsb, dlb = lsf[qs, :], dlf[qs, :]
            for h in range(2):
                keep = lo if h == 0 else jnp.logical_not(lo)
                zero = jnp.zeros_like(qb)
                qh, doh, kh = jnp.where(keep, qb, zero), jnp.where(keep, dob, zero), jnp.where(keep, kb, zero)
                s = _dot_nt(qh, kb)
                pw = jnp.where(mask, jnp.exp(s - _col_of(lsb, lo, h)), 0.0)
                ds = (pw * (_dot_nt(doh, vb) - _col_of(dlb, lo, h))).astype(BF16)
                if a < per:
                    dq_ref[qs, :] += _dot(ds, kh)
                if b >= 0:
                    kso = slice(b * BLOCK, (b + 1) * BLOCK)
                    dk_ref[kso, :] += _dot_tn(ds, qh)
                    dv_ref[kso, :] += _dot_tn(pw.astype(BF16), doh)

    prev = lambda n: jnp.maximum(n * per - 1, 0)
    nxt = lambda n: jnp.minimum((n + 1) * per, nblk - 1)
    cur = lambda unit, off: pl.BlockSpec((tq, LANES), lambda c, p, n: (n, c * unit + off + p))
    edge = lambda unit, off, f: pl.BlockSpec((BLOCK, LANES), lambda c, p, n: (f(n), c * unit + off + p))
    qk2 = qkn.reshape(ln, r * 2 * W_DIL)
    pj2 = proj_bf.reshape(ln, r * IN_COLS)
    do2, ls2, dl2 = (t.reshape(ln, r * W_DIL) for t in (d_od, lse, delta))
    wide = lambda dt: pltpu.VMEM((tq + BLOCK, LANES), dt)
    outs = pl.pallas_call(
        body, name=f"dil_bwd_r{r}", grid=(r, pairs, ntile),
        in_specs=[cur(q_unit, 0), edge(q_unit, 0, nxt), cur(o_unit, 0), edge(o_unit, 0, nxt),
                  cur(o_unit, 0), edge(o_unit, 0, nxt), cur(o_unit, 0), edge(o_unit, 0, nxt),
                  cur(q_unit, pairs), edge(q_unit, pairs, prev), cur(p_unit, v_off), edge(p_unit, v_off, prev)],
        out_specs=[cur(o_unit, 0)] * 3,
        out_shape=[jax.ShapeDtypeStruct((ln, r * W_DIL), F32)] * 3,
        scratch_shapes=[wide(BF16), wide(BF16), wide(F32), wide(F32), wide(BF16), wide(BF16)],
        compiler_params=_params(32),
    )(qk2, qk2, do2, do2, ls2, ls2, dl2, dl2, qk2, qk2, pj2, pj2)
    return tuple(t.reshape(seq, W_DIL) for t in outs)


def _assemble_dproj(d_qa, d_ka, d_va, d_gates, dq_p, dk_p, dv_p, qk_raw, gains, tables, bd):
    seq = qk_raw.shape[0]
    tm = min(256, seq)
    chunks = W_DIL // (2 * LANES)

    def body(dqa, dka0, dka1, dva0, dva1, dg, q0, q1, q2, k0, k1, k2, v0, v1, v2, x_ref, g_ref, c_ref, sn_ref, sp_ref,
             bd_ref, dp_ref, gpart_ref):
        @pl.when(pl.program_id(0) == 0)
        def _():
            gpart_ref[...] = jnp.zeros_like(gpart_ref)

        def put(first_col, v):
            dp_ref[:, first_col:first_col + v.shape[1]] = v.astype(BF16)

        put(0, dqa[...])
        put(W_SB, dka0[...])
        put(W_SB + LANES, dka1[...])
        put(2 * W_SB, dva0[...])
        put(2 * W_SB + LANES, dva1[...])
        put(3 * W_SB, dg[:, :W_SB])
        put(4 * W_SB + 2 * W_DIL, v0[...] + v1[...] + v2[...])
        put(4 * W_SB + 3 * W_DIL, dg[:, W_SB:])
        c, sn, sp, bdm = c_ref[...], sn_ref[...], sp_ref[...], bd_ref[...]
        for which, parts in enumerate(((q0, q1, q2), (k0, k1, k2))):
            scale = QK_SCALE if which == 0 else 1.0
            for k in range(chunks):
                sl = slice(k * 2 * LANES, (k + 1) * 2 * LANES)
                dyv = (parts[0][:, sl] + parts[1][:, sl] + parts[2][:, sl]) * scale
                dxn = _rope_t(dyv, c, sn, sp)
                x = x_ref[:, which * W_DIL + k * 2 * LANES:which * W_DIL + (k + 1) * 2 * LANES]
                rs = lax.rsqrt(_head_sums(x * x, bdm) * (1.0 / HEAD_DIM) + EPS)
                xhat = x * rs
                gpart_ref[which] += jnp.sum((dxn * xhat).reshape(tm // 8, 8, 2 * LANES), axis=0)
                dxhat = dxn * g_ref[which]
                mean = _head_sums(dxhat * xhat, bdm) * (1.0 / HEAD_DIM)
                put(4 * W_SB + which * W_DIL + k * 2 * LANES, rs * (dxhat - xhat * mean))

    row = lambda w: pl.BlockSpec((tm, w), lambda i: (i, 0))
    pair = lambda p: pl.BlockSpec((None, tm, LANES), lambda i: (p, i, 0))
    const = lambda shape: pl.BlockSpec(shape, lambda i: tuple(0 for _ in shape))
    return pl.pallas_call(
        body, name="assemble_dproj", grid=(seq // tm,),
        in_specs=[row(W_SB), pair(0), pair(1), pair(0), pair(1), row(MIX)] + [row(W_DIL)] * 9 +
                 [row(2 * W_DIL), const((2, 1, 2 * LANES)), row(2 * LANES), row(2 * LANES), row(2 * LANES),
                  const((2 * LANES, 2 * LANES))],
        out_specs=[row(IN_COLS), const((2, 8, 2 * LANES))],
        out_shape=[jax.ShapeDtypeStruct((seq, IN_COLS), BF16), jax.ShapeDtypeStruct((2, 8, 2 * LANES), F32)],
        compiler_params=_params(48),
    )(d_qa, d_ka, d_ka, d_va, d_va, d_gates, *dq_p, *dk_p, *dv_p, qk_raw, gains, *tables, bd)


def _dw_in(h, dproj, layer, dwin_acc):
    seq, d = h.shape
    tm = min(2048, seq)

    def body(h_ref, dp_ref, acc_in, dw_ref):
        del acc_in

        @pl.when(pl.program_id(1) == 0)
        def _():
            dw_ref[...] = jnp.zeros_like(dw_ref)

        dw_ref[...] += _dot_tn(h_ref[...], dp_ref[...])

    slab = pl.BlockSpec((None, None, d, COLS_PER_DEV), lambda n, i: (n, layer, 0, 0))
    return pl.pallas_call(
        body, name="dw_in", grid=(N_DEV, seq // tm),
        in_specs=[pl.BlockSpec((tm, d), lambda n, i: (i, 0)), pl.BlockSpec((tm, COLS_PER_DEV), lambda n, i: (i, n)),
                  pl.BlockSpec(memory_space=pl.ANY)],
        out_specs=slab, out_shape=jax.ShapeDtypeStruct(dwin_acc.shape, F32),
        input_output_aliases={2: 0},
        compiler_params=_params(40),
    )(h, dproj, dwin_acc)


def _dx_norm(dproj, w_all, layer, x, g, dx_next):
    seq, d = x.shape
    tm = min(256, seq)

    def body(dp_ref, w_ref, x_ref, g_ref, dn_ref, dx_ref, gpart_ref):
        @pl.when(pl.program_id(0) == 0)
        def _():
            gpart_ref[...] = jnp.zeros_like(gpart_ref)

        dh = jnp.zeros((tm, d), F32)
        for n in range(N_DEV):
            dh = dh + _dot_nt(dp_ref[:, n * COLS_PER_DEV:(n + 1) * COLS_PER_DEV], w_ref[n])
        xf = x_ref[...]
        rs = lax.rsqrt(jnp.mean(xf * xf, axis=-1, keepdims=True) + EPS)
        xhat = xf * rs
        gpart_ref[...] += jnp.sum((dh * xhat).reshape(tm // 8, 8, d), axis=0)
        dxhat = dh * g_ref[...]
        mean = jnp.mean(dxhat * xhat, axis=-1, keepdims=True)
        dx_ref[...] = rs * (dxhat - xhat * mean) + dn_ref[...]

    row = lambda w: pl.BlockSpec((tm, w), lambda i: (i, 0))
    return pl.pallas_call(
        body, name="dx_norm", grid=(seq // tm,),
        in_specs=[row(IN_COLS), pl.BlockSpec((N_DEV, None, d, COLS_PER_DEV), lambda i: (0, layer, 0, 0)), row(d),
                  pl.BlockSpec((1, d), lambda i: (0, 0)), row(d)],
        out_specs=[row(d), pl.BlockSpec((8, d), lambda i: (0, 0))],
        out_shape=[jax.ShapeDtypeStruct((seq, d), F32), jax.ShapeDtypeStruct((8, d), F32)],
        compiler_params=_params(48),
    )(dproj, w_all, x, g, dx_next)


def _exchange_grads(dwin, dwout, small):
    flips = _flips()

    def body(dwin_ref, dwout_ref, small_ref, rin_ref, rout_ref, rsmall_ref, send_sems, recv_sems, local_sems):
        x, y, c, me = _place()
        local = [pltpu.make_async_copy(dwin_ref.at[me], rin_ref.at[me], local_sems.at[0]),
                 pltpu.make_async_copy(dwout_ref.at[me], rout_ref.at[me], local_sems.at[1]),
                 pltpu.make_async_copy(small_ref, rsmall_ref.at[me], local_sems.at[2])]
        for cp in local:
            cp.start()
        copies = []
        for k, flip in enumerate(flips):
            px, py, pc = _peer(x, y, c, flip)
            to = 4 * px + 2 * py + pc
            for a, (src, dst) in enumerate(((dwin_ref.at[to], rin_ref), (dwout_ref.at[to], rout_ref), (small_ref, rsmall_ref))):
                cp = pltpu.make_async_remote_copy(
                    src_ref=src, dst_ref=dst.at[me], send_sem=send_sems.at[3 * k + a], recv_sem=recv_sems.at[3 * k + a],
                    device_id=(px, py, pc), device_id_type=MESH_ID)
                cp.start()
                copies.append(cp)
        for cp in copies:
            cp.wait()
        for cp in local:
            cp.wait()

    any_spec = pl.BlockSpec(memory_space=pl.ANY)
    vmem = pl.BlockSpec(memory_space=pltpu.VMEM)
    n = 3 * len(flips)
    return pl.pallas_call(
        body, name="exchange_grads",
        in_specs=[any_spec, any_spec, vmem], out_specs=[any_spec, any_spec, vmem],
        out_shape=[jax.ShapeDtypeStruct(dwin.shape, F32), jax.ShapeDtypeStruct(dwout.shape, F32),
                   jax.ShapeDtypeStruct((N_DEV,) + small.shape, F32)],
        scratch_shapes=[pltpu.SemaphoreType.DMA((n,)), pltpu.SemaphoreType.DMA((n,)), pltpu.SemaphoreType.DMA((3,))],
    )(dwin, dwout, small)


def _adamw_math(g, w, m, v):
    m = ADAM_B1 * m + (1.0 - ADAM_B1) * g
    v = ADAM_B2 * v + (1.0 - ADAM_B2) * (g * g)
    m_hat = m / (1.0 - ADAM_B1 ** ADAM_STEP)
    v_hat = v / (1.0 - ADAM_B2 ** ADAM_STEP)
    delta = -ADAM_LR * (m_hat / (jnp.sqrt(v_hat) + ADAM_EPS) + ADAM_WD * w)
    return delta, m, v


def _adamw(parts, w, m, v, name):
    nl, r, c = w.shape
    tr = min(r, (256 * 512) // c)

    def body(p_ref, w_ref, m_ref, v_ref, g_ref, d_ref, nm_ref, nv_ref):
        g = p_ref[0]
        for s in range(1, N_DEV):
            g = g + p_ref[s]
        g_ref[...] = g
        d_ref[...], nm_ref[...], nv_ref[...] = _adamw_math(g, w_ref[...], m_ref[...], v_ref[...])

    blk = pl.BlockSpec((None, tr, c), lambda l, i: (l, i, 0))
    return pl.pallas_call(
        body, name=name, grid=(nl, r // tr),
        in_specs=[pl.BlockSpec((N_DEV, None, tr, c), lambda l, i: (0, l, i, 0)), blk, blk, blk],
        out_specs=[blk] * 4, out_shape=[jax.ShapeDtypeStruct(w.shape, F32)] * 4,
        compiler_params=_params(32),
    )(parts, w, m, v)


def _adamw_small(parts, w, m, v):
    def body(p_ref, w_ref, m_ref, v_ref, g_ref, d_ref, nm_ref, nv_ref):
        g = p_ref[0]
        for s in range(1, N_DEV):
            g = g + p_ref[s]
        g_ref[...] = g
        d_ref[...], nm_ref[...], nv_ref[...] = _adamw_math(g, w_ref[...], m_ref[...], v_ref[...])

    vmem = pl.BlockSpec(memory_space=pltpu.VMEM)
    return pl.pallas_call(
        body, name="adamw_small", in_specs=[vmem] * 4, out_specs=[vmem] * 4,
        out_shape=[jax.ShapeDtypeStruct(w.shape, F32)] * 4,
    )(parts, w, m, v)


def _pack_small(a, b, c):
    pad = jnp.zeros((a.shape[0], SMALL_W - a.shape[1] - b.shape[1] - c.shape[1]), F32)
    return jnp.concatenate([a, b, c, pad], axis=1)


def _unpack_small(t, d):
    return t[:, :d], t[:, d:d + HEAD_DIM], t[:, d + HEAD_DIM:d + 2 * HEAD_DIM]


def _local_step(x0, target, norm_g, q_norm_g, k_norm_g, win_all, wout_all):
    seq, d = x0.shape
    depth = norm_g.shape[0]
    tri_a, tri_b, bd = _tri_constants()
    tables = _rope_tables(seq)
    rep = (2 * LANES) // HEAD_DIM

    saved = []
    xl = x0
    for layer in range(depth):
        gains = jnp.stack([jnp.tile(q_norm_g[layer], rep), jnp.tile(k_norm_g[layer], rep)])[:, None, :]
        proj_bf, gates, qk_raw, h = _norm_proj(xl, norm_g[layer][None, :], win_all, layer)
        qkn = _qk_prep(qk_raw, gains, tables, bd)
        oa = _sb_fwd(proj_bf, tri_a)
        o_p, ld_p = zip(*[_dil_fwd(qkn, proj_bf, r) for _, r in DIL_PATTERNS])
        x_next, cat, od, lse = _out_proj(xl, oa, o_p, ld_p, gates, wout_all, layer)
        saved.append((xl, gains, proj_bf, gates, qk_raw, h, qkn, oa, cat, od, lse))
        xl = x_next

    loss_part, dx = _loss_head(xl, target)

    dwin = jnp.zeros((N_DEV, depth, d, COLS_PER_DEV), F32)
    dwout = jnp.zeros((N_DEV, depth, ROWS_PER_DEV, d), F32)
    g_norm, g_q, g_k = [None] * depth, [None] * depth, [None] * depth
    for layer in reversed(range(depth)):
        xl, gains, proj_bf, gates, qk_raw, h, qkn, oa, cat, od, lse = saved[layer]
        d_oa, d_od, delta, d_gates, dwout = _out_proj_bwd(dx, wout_all, layer, cat, gates, oa, od, bd, dwout)
        d_qa, d_ka, d_va = _sb_bwd(proj_bf, d_oa, oa, tri_a, tri_b)
        dq_p, dk_p, dv_p = zip(*[_dil_bwd(qkn, proj_bf, d_od, lse, delta, r) for _, r in DIL_PATTERNS])
        dproj, gqk = _assemble_dproj(d_qa, d_ka, d_va, d_gates, dq_p, dk_p, dv_p, qk_raw, gains, tables, bd)
        dwin = _dw_in(h, dproj, layer, dwin)
        dx, gn = _dx_norm(dproj, win_all, layer, xl, norm_g[layer][None, :], dx)
        g_norm[layer] = jnp.sum(gn, axis=0)
        gqk = jnp.sum(gqk, axis=1).reshape(2, rep, HEAD_DIM).sum(axis=1)
        g_q[layer], g_k[layer] = gqk[0], gqk[1]

    small = _pack_small(jnp.stack(g_norm), jnp.stack(g_q), jnp.stack(g_k))
    return loss_part, dx, dwin, dwout, small


def kernel(x, norm_g, w_in, q_norm_g, k_norm_g, w_out, loss_target, m_norm_g, m_w_in, m_q_norm_g, m_k_norm_g, m_w_out,
           v_norm_g, v_w_in, v_q_norm_g, v_k_norm_g, v_w_out):
    d = w_in.shape[1]
    seq = x.shape[1]
    win_all, wout_all = _all_gather_weights(_cast_bf16(w_in, "cast_w_in"), _cast_bf16(w_out, "cast_w_out"))
    loss_part, dx, dwin, dwout, small = _local_step(
        x.reshape(seq, d), loss_target.reshape(seq, d), norm_g, q_norm_g, k_norm_g, win_all, wout_all)
    loss = lax.psum(jnp.sum(loss_part), ("x", "y", "c"))
    rin, rout, rsmall = _exchange_grads(dwin, dwout, small)

    g_in, d_in, nm_in, nv_in = _adamw(rin, w_in, m_w_in, v_w_in, "adamw_w_in")
    g_out, d_out, nm_out, nv_out = _adamw(rout, w_out, m_w_out, v_w_out, "adamw_w_out")
    small_out = _adamw_small(rsmall, _pack_small(norm_g, q_norm_g, k_norm_g), _pack_small(m_norm_g, m_q_norm_g, m_k_norm_g),
                             _pack_small(v_norm_g, v_q_norm_g, v_k_norm_g))
    (g_n, g_qn, g_kn), (d_n, d_qn, d_kn), (nm_n, nm_qn, nm_kn), (nv_n, nv_qn, nv_kn) = (_unpack_small(t, d) for t in small_out)

    return (loss, dx.reshape(x.shape), g_n, g_in, g_qn, g_kn, g_out, d_n, d_in, d_qn, d_kn, d_out,
            nm_n, nm_in, nm_qn, nm_kn, nm_out, nv_n, nv_in, nv_qn, nv_kn, nv_out)
```

```python
import functools
import math

import jax
import jax.numpy as jnp
from jax import lax
from jax.experimental import pallas as pl
from jax.experimental.pallas import tpu as pltpu

F32 = jnp.float32
BF16 = jnp.bfloat16

EPS = 1e-6
HEAD_DIM = 64
BLOCK = 128
LANES = 128
W_SB = 256
W_DIL = 768
MIX = W_SB + W_DIL
IN_COLS = 4 * W_SB + 4 * W_DIL
N_DEV = 8
COLS_PER_DEV = IN_COLS // N_DEV
ROWS_PER_DEV = MIX // N_DEV
QK_SCALE = 1.0 / math.sqrt(HEAD_DIM)
DIL_PATTERNS = ((128, 1), (512, 4), (2048, 16))
DIL_TILE = 2048
ROPE_THETA = 500000.0
ROPE_DIM = HEAD_DIM // 4
ROPE_HALF = ROPE_DIM // 2
DEAD_LOG = -110.0
NEG_BIG = -1e30

ADAM_LR = 0.001
ADAM_B1 = 0.9
ADAM_B2 = 0.999
ADAM_EPS = 1e-08
ADAM_WD = 0.01
ADAM_STEP = 10

SMALL_W = 1280
MESH_ID = pl.DeviceIdType.MESH
MIB = 1 << 20


def _params(vmem_mib):
    return pltpu.CompilerParams(vmem_limit_bytes=vmem_mib * MIB)


def _dot(a, b):
    return jnp.dot(a, b, preferred_element_type=F32)


def _dot_nt(a, b):
    return lax.dot_general(a, b, (((1,), (1,)), ((), ())), preferred_element_type=F32)


def _dot_tn(a, b):
    return lax.dot_general(a, b, (((0,), (0,)), ((), ())), preferred_element_type=F32)


def _dot_exact(x, m01):
    hi = x.astype(BF16)
    r1 = x - hi.astype(F32)
    mid = r1.astype(BF16)
    lo = (r1 - mid.astype(F32)).astype(BF16)
    return _dot(hi, m01) + _dot(mid, m01) + _dot(lo, m01)


def _lane_lo(shape):
    return lax.broadcasted_iota(jnp.int32, shape, 1) < HEAD_DIM


def _col_of(b, lo, h):
    keep = lo if h == 0 else jnp.logical_not(lo)
    return jnp.max(jnp.where(keep, b, NEG_BIG), axis=1, keepdims=True)


def _tri_constants():
    j = jnp.arange(BLOCK)
    ones = jnp.ones((BLOCK, BLOCK), F32)
    excl = (j[:, None] > j[None, :]).astype(F32)
    incl = (j[:, None] >= j[None, :]).astype(F32)
    tri_a = jnp.concatenate([excl, ones], axis=1).astype(BF16)
    tri_b = jnp.concatenate([incl, ones], axis=1).astype(BF16)
    d = jnp.arange(2 * LANES)
    bd = (d[:, None] // HEAD_DIM == d[None, :] // HEAD_DIM).astype(BF16)
    return tri_a, tri_b, bd


def _rope_tables(seq):
    inv_freq = 1.0 / (ROPE_THETA ** (jnp.arange(ROPE_HALF, dtype=F32) * 2.0 / ROPE_DIM))
    ang = jnp.arange(seq).astype(F32)[:, None] * inv_freq[None, :]
    cos, sin = jnp.cos(ang), jnp.sin(ang)
    one = jnp.ones((seq, HEAD_DIM - ROPE_DIM), F32)
    zero8 = jnp.zeros((seq, ROPE_HALF), F32)
    zero_rest = jnp.zeros((seq, HEAD_DIM - ROPE_DIM), F32)
    c = jnp.concatenate([cos, cos, one], axis=1)
    s_next = jnp.concatenate([-sin, zero8, zero_rest], axis=1)
    s_prev = jnp.concatenate([zero8, sin, zero_rest], axis=1)
    rep = (2 * LANES) // HEAD_DIM
    return tuple(jnp.tile(t, (1, rep)) for t in (c, s_next, s_prev))


def _roll_lanes(x, shift):
    return jnp.concatenate([pltpu.roll(x[:, :LANES], shift, 1), pltpu.roll(x[:, LANES:], shift, 1)], axis=1)


def _rope(x, c, s_next, s_prev):
    return x * c + _roll_lanes(x, LANES - ROPE_HALF) * s_next + _roll_lanes(x, ROPE_HALF) * s_prev


def _rope_t(dy, c, s_next, s_prev):
    return dy * c + _roll_lanes(dy * s_next, ROPE_HALF) + _roll_lanes(dy * s_prev, LANES - ROPE_HALF)


def _cast_bf16(w, name):
    nl, r, c = w.shape

    def body(w_ref, o_ref):
        o_ref[...] = w_ref[...].astype(BF16)

    return pl.pallas_call(
        body, name=name, grid=(nl,),
        in_specs=[pl.BlockSpec((None, r, c), lambda l: (l, 0, 0))],
        out_specs=pl.BlockSpec((None, r, c), lambda l: (l, 0, 0)),
        out_shape=jax.ShapeDtypeStruct(w.shape, BF16),
        compiler_params=_params(24),
    )(w)


def _flips():
    return [(dx, dy, dc) for dx in (0, 1) for dy in (0, 1) for dc in (0, 1) if (dx, dy, dc) != (0, 0, 0)]


def _place():
    x, y, c = lax.axis_index("x"), lax.axis_index("y"), lax.axis_index("c")
    return x, y, c, 4 * x + 2 * y + c


def _peer(x, y, c, flip):
    dx, dy, dc = flip
    return (1 - x if dx else x, 1 - y if dy else y, 1 - c if dc else c)


def _all_gather_weights(win_bf, wout_bf):
    flips = _flips()

    def body(win_ref, wout_ref, oin_ref, oout_ref, send_sems, recv_sems, local_sems):
        x, y, c, me = _place()
        local = [pltpu.make_async_copy(win_ref, oin_ref.at[me], local_sems.at[0]),
                 pltpu.make_async_copy(wout_ref, oout_ref.at[me], local_sems.at[1])]
        for cp in local:
            cp.start()
        copies = []
        for k, flip in enumerate(flips):
            for a, (src, dst) in enumerate(((win_ref, oin_ref), (wout_ref, oout_ref))):
                cp = pltpu.make_async_remote_copy(
                    src_ref=src, dst_ref=dst.at[me], send_sem=send_sems.at[2 * k + a], recv_sem=recv_sems.at[2 * k + a],
                    device_id=_peer(x, y, c, flip), device_id_type=MESH_ID)
                cp.start()
                copies.append(cp)
        for cp in copies:
            cp.wait()
        for cp in local:
            cp.wait()

    any_spec = pl.BlockSpec(memory_space=pl.ANY)
    n = 2 * len(flips)
    return pl.pallas_call(
        body, name="all_gather_weights",
        in_specs=[any_spec, any_spec], out_specs=[any_spec, any_spec],
        out_shape=[jax.ShapeDtypeStruct((N_DEV,) + win_bf.shape, BF16), jax.ShapeDtypeStruct((N_DEV,) + wout_bf.shape, BF16)],
        scratch_shapes=[pltpu.SemaphoreType.DMA((n,)), pltpu.SemaphoreType.DMA((n,)), pltpu.SemaphoreType.DMA((2,))],
    )(win_bf, wout_bf)


_F32_ROUTES = {1: ((256, 256, 0, 0),), 2: ((0, 512, 1, 0),), 3: ((0, 512, 1, 512),), 4: ((0, 512, 1, 1024),),
               5: ((0, 512, 1, 1536),), 6: ((0, 256, 1, 2048), (256, 256, 0, 256)), 7: ((0, 512, 0, 512),)}


def _norm_proj(x, g, w_all, layer):
    seq, d = x.shape
    tm = min(256, seq)

    def body(x_ref, g_ref, w_ref, pbf_ref, gates_ref, qk_ref, h_ref):
        xf = x_ref[...]
        rs = lax.rsqrt(jnp.mean(xf * xf, axis=-1, keepdims=True) + EPS)
        h = (xf * rs * g_ref[...]).astype(BF16)
        h_ref[...] = h
        targets = (gates_ref, qk_ref)
        for n in range(N_DEV):
            acc = _dot(h, w_ref[n])
            pbf_ref[:, n * COLS_PER_DEV:(n + 1) * COLS_PER_DEV] = acc.astype(BF16)
            for lo, width, tgt, dst in _F32_ROUTES.get(n, ()):
                targets[tgt][:, dst:dst + width] = acc[:, lo:lo + width]

    row = lambda w: pl.BlockSpec((tm, w), lambda i: (i, 0))
    return pl.pallas_call(
        body, name="norm_proj", grid=(seq // tm,),
        in_specs=[row(d), pl.BlockSpec((1, d), lambda i: (0, 0)),
                  pl.BlockSpec((N_DEV, None, d, COLS_PER_DEV), lambda i: (0, layer, 0, 0))],
        out_specs=[row(IN_COLS), row(MIX), row(3 * W_DIL), row(d)],
        out_shape=[jax.ShapeDtypeStruct((seq, IN_COLS), BF16), jax.ShapeDtypeStruct((seq, MIX), F32),
                   jax.ShapeDtypeStruct((seq, 3 * W_DIL), F32), jax.ShapeDtypeStruct((seq, d), BF16)],
        compiler_params=_params(48),
    )(x, g, w_all)


def _head_sums(v, bd):
    hi = v.astype(BF16)
    lo = (v - hi.astype(F32)).astype(BF16)
    return _dot(hi, bd) + _dot(lo, bd)


def _qk_prep(qk_raw, gains, tables, bd):
    seq = qk_raw.shape[0]
    tm = min(1024, seq)
    chunks = W_DIL // (2 * LANES)

    def body(x_ref, g_ref, c_ref, sn_ref, sp_ref, bd_ref, o_ref):
        j = pl.program_id(1)
        x = x_ref[...]
        rs = lax.rsqrt(_head_sums(x * x, bd_ref[...]) * (1.0 / HEAD_DIM) + EPS)
        y = _rope(x * rs * g_ref[...], c_ref[...], sn_ref[...], sp_ref[...])
        o_ref[...] = y * jnp.where(j < chunks, QK_SCALE, 1.0)

    tab = pl.BlockSpec((tm, 2 * LANES), lambda i, j: (i, 0))
    return pl.pallas_call(
        body, name="qk_prep", grid=(seq // tm, 2 * chunks),
        in_specs=[pl.BlockSpec((tm, 2 * LANES), lambda i, j: (i, j)),
                  pl.BlockSpec((None, 1, 2 * LANES), lambda i, j: (j // chunks, 0, 0)),
                  tab, tab, tab, pl.BlockSpec((2 * LANES, 2 * LANES), lambda i, j: (0, 0))],
        out_specs=pl.BlockSpec((tm, 2 * LANES), lambda i, j: (i, j)),
        out_shape=jax.ShapeDtypeStruct((seq, 2 * W_DIL), F32),
        compiler_params=_params(32),
    )(qk_raw, gains, *tables, bd)


def _stack_heads(x, lo):
    return jnp.concatenate([jnp.where(lo, x, 0.0), jnp.where(lo, 0.0, x)], axis=0)


def _unstack_heads(y, lo):
    return jnp.where(lo, y[:BLOCK], y[BLOCK:])


def _stacked_causal():
    row = lax.broadcasted_iota(jnp.int32, (2 * BLOCK, BLOCK), 0) & (BLOCK - 1)
    return lax.broadcasted_iota(jnp.int32, (2 * BLOCK, BLOCK), 1) < row


def _sb_scores(qs, kb, tri, r, causal):
    z = _dot_nt(qs, kb)
    ls = jnp.minimum(z, 0.0) - jnp.log1p(jnp.exp(-jnp.abs(z)))
    lsv = ls - z
    if causal is not None:
        lsv = jnp.where(causal, lsv, 0.0)
    ct = _dot_exact(lsv, tri)
    a = jnp.exp(ls + ct[:, :BLOCK] + r)
    if causal is not None:
        a = jnp.where(causal, a, 0.0)
    return ls, a, ct[:, BLOCK:]


def _sb_fwd(proj_bf, tri_a):
    seq = proj_bf.shape[0]
    pairs = W_SB // LANES

    def body(q_ref, k_ref, v_ref, tri_ref, o_ref, r_ref, acc_ref):
        i = pl.program_id(1)
        lo = _lane_lo((BLOCK, LANES))
        qs = _stack_heads(q_ref[...].astype(F32) * QK_SCALE, lo).astype(BF16)
        causal = _stacked_causal()
        tri = tri_ref[...]
        r_ref[...] = jnp.zeros_like(r_ref)
        acc_ref[...] = jnp.zeros_like(acc_ref)

        def block(j, mask):
            rows = pl.ds(pl.multiple_of(j * BLOCK, BLOCK), BLOCK)
            kb, vb = k_ref[rows, :], v_ref[rows, :]
            _, a, tot = _sb_scores(qs, kb, tri, r_ref[...], mask)
            a_hi = a.astype(BF16)
            a_lo = (a - a_hi.astype(F32)).astype(BF16)
            acc_ref[...] += _dot(a_hi, vb) + _dot(a_lo, vb)
            r_ref[...] += tot

        def alive():
            return (jnp.max(r_ref[...]) > DEAD_LOG).astype(jnp.int32)

        block(i, causal)

        def step(carry):
            block(carry[0], None)
            return carry[0] - 1, alive()

        lax.while_loop(lambda carry: jnp.logical_and(carry[0] >= 0, carry[1] > 0), step, (i - 1, alive()))
        o_ref[...] = _unstack_heads(acc_ref[...], lo)

    return pl.pallas_call(
        body, name="sb_fwd", grid=(pairs, seq // BLOCK),
        in_specs=[pl.BlockSpec((BLOCK, LANES), lambda p, i: (i, p)),
                  pl.BlockSpec((seq, LANES), lambda p, i: (0, pairs + p)),
                  pl.BlockSpec((seq, LANES), lambda p, i: (0, 2 * pairs + p)),
                  pl.BlockSpec((BLOCK, 2 * BLOCK), lambda p, i: (0, 0))],
        out_specs=pl.BlockSpec((BLOCK, LANES), lambda p, i: (i, p)),
        out_shape=jax.ShapeDtypeStruct((seq, W_SB), F32),
        scratch_shapes=[pltpu.VMEM((2 * BLOCK, LANES), F32), pltpu.VMEM((2 * BLOCK, LANES), F32)],
        compiler_params=_params(40),
    )(proj_bf, proj_bf, proj_bf, tri_a)


def _class_rows(c, first, count, r):
    start = c + first * r
    return pl.ds(start, count) if r == 1 else pl.ds(start, count, stride=r)


def _dil_tiling(seq, r):
    tile = min(DIL_TILE, seq)
    edge = BLOCK * r
    return tile, edge, tile // edge


def _band_mask(n_rows, n_keys):
    row = lax.broadcasted_iota(jnp.int32, (n_rows, n_keys), 0) & (BLOCK - 1)
    col = lax.broadcasted_iota(jnp.int32, (n_rows, n_keys), 1)
    return jnp.logical_and(col >= row, col <= row + BLOCK), col


def _unroll(trips):
    return max(u for u in (1, 2, 3, 4) if trips % u == 0)


def _for_each_class(r, per, fn):
    if r == 1:
        fn(0)
    else:
        lax.fori_loop(0, r, lambda c, carry: (fn(c), carry)[1], 0, unroll=_unroll(r) if per == 1 else 1)


def _for_later_blocks(per, fn):
    if per > 1:
        lax.fori_loop(1, per, lambda a, carry: (fn(a), carry)[1], 0, unroll=_unroll(per - 1))


def _dil_fwd(qkn, qkv_raw, r):
    seq = qkn.shape[0]
    tile, edge, per = _dil_tiling(seq, r)
    pairs = W_DIL // LANES

    def body(q_ref, kc_ref, kp_ref, vc_ref, vp_ref, o_ref, ld_ref):
        n = pl.program_id(1)
        lo = _lane_lo((BLOCK, LANES))
        band, col = _band_mask(2 * BLOCK, 2 * BLOCK)
        first_band = jnp.logical_and(band, jnp.logical_or(col >= BLOCK, n > 0))

        def block(qb, kcat, vcat, valid, out_rows):
            qs = _stack_heads(qb, lo).astype(BF16)
            s = jnp.where(valid, _dot_nt(qs, kcat.astype(BF16)), NEG_BIG)
            m = jnp.max(s, axis=1, keepdims=True)
            p = jnp.exp(s - m)
            l = jnp.sum(p, axis=1, keepdims=True)
            o = _dot(p.astype(BF16), vcat.astype(BF16)) / l
            ld = jnp.broadcast_to(m + jnp.log(l), (2 * BLOCK, LANES))
            o_ref[out_rows, :] = _unstack_heads(o, lo)
            ld_ref[out_rows, :] = _unstack_heads(ld, lo)

        def one_class(c):
            rows0 = _class_rows(c, 0, BLOCK, r)
            block(q_ref[rows0, :], jnp.concatenate([kp_ref[rows0, :], kc_ref[rows0, :]], axis=0),
                  jnp.concatenate([vp_ref[rows0, :], vc_ref[rows0, :]], axis=0), first_band, rows0)

            def later(a):
                rows = _class_rows(c, a * BLOCK, BLOCK, r)
                keys = _class_rows(c, (a - 1) * BLOCK, 2 * BLOCK, r)
                block(q_ref[rows, :], kc_ref[keys, :], vc_ref[keys, :], band, rows)

            _for_later_blocks(per, later)

        _for_each_class(r, per, one_class)

    per_edge = tile // edge
    cur = lambda off: pl.BlockSpec((tile, LANES), lambda p, n: (n, off + p))
    before = lambda off: pl.BlockSpec((edge, LANES), lambda p, n: (jnp.maximum(n * per_edge - 1, 0), off + p))
    return pl.pallas_call(
        body, name=f"dil_fwd_r{r}", grid=(pairs, seq // tile),
        in_specs=[cur(0), cur(pairs), before(pairs), cur(2 * pairs), before(2 * pairs)],
        out_specs=[cur(0), cur(0)],
        out_shape=[jax.ShapeDtypeStruct((seq, W_DIL), F32), jax.ShapeDtypeStruct((seq, W_DIL), F32)],
        compiler_params=_params(32),
    )(qkn, qkn, qkn, qkv_raw, qkv_raw)


def _silu_parts(g):
    sig = jax.nn.sigmoid(g)
    return g * sig, sig * (1.0 + g * (1.0 - sig))


def _out_proj(x, oa, o_p, ld_p, gates, wout_all, layer):
    seq, d = x.shape
    tm = min(256, seq)

    def body(x_ref, oa_ref, o0, o1, o2, l0, l1, l2, g_ref, w_ref, xn_ref, cat_ref, od_ref, lse_ref):
        lds = (l0[...], l1[...], l2[...])
        m = jnp.maximum(jnp.maximum(lds[0], lds[1]), lds[2])
        es = [jnp.exp(v - m) for v in lds]
        tot = es[0] + es[1] + es[2]
        lse_ref[...] = m + jnp.log(tot)
        inv = 1.0 / tot
        od = (es[0] * inv) * o0[...] + (es[1] * inv) * o1[...] + (es[2] * inv) * o2[...]
        od_ref[...] = od
        silu, _ = _silu_parts(g_ref[...])
        cat_ref[:, :W_SB] = (oa_ref[...] * silu[:, :W_SB]).astype(BF16)
        cat_ref[:, W_SB:] = (od * silu[:, W_SB:]).astype(BF16)
        y = x_ref[...]
        for b in range(N_DEV):
            y = y + _dot(cat_ref[:, b * ROWS_PER_DEV:(b + 1) * ROWS_PER_DEV], w_ref[b])
        xn_ref[...] = y

    row = lambda w: pl.BlockSpec((tm, w), lambda i: (i, 0))
    return pl.pallas_call(
        body, name="out_proj", grid=(seq // tm,),
        in_specs=[row(d), row(W_SB)] + [row(W_DIL)] * 6 + [row(MIX),
                  pl.BlockSpec((N_DEV, None, ROWS_PER_DEV, d), lambda i: (0, layer, 0, 0))],
        out_specs=[row(d), row(MIX), row(W_DIL), row(W_DIL)],
        out_shape=[jax.ShapeDtypeStruct((seq, d), F32), jax.ShapeDtypeStruct((seq, MIX), BF16),
                   jax.ShapeDtypeStruct((seq, W_DIL), F32), jax.ShapeDtypeStruct((seq, W_DIL), F32)],
        compiler_params=_params(48),
    )(x, oa, *o_p, *ld_p, gates, wout_all)


def _loss_head(y, target):
    seq, d = y.shape
    tm = min(512, seq)

    def body(y_ref, t_ref, part_ref, dy_ref):
        @pl.when(pl.program_id(0) == 0)
        def _():
            part_ref[...] = jnp.zeros_like(part_ref)

        diff = y_ref[...] - t_ref[...]
        dy_ref[...] = diff * (1.0 / d)
        part_ref[...] += jnp.sum((diff * diff).reshape(tm // 8, 8, d), axis=0) * (0.5 / d)

    row = pl.BlockSpec((tm, d), lambda i: (i, 0))
    return pl.pallas_call(
        body, name="loss_head", grid=(seq // tm,),
        in_specs=[row, row], out_specs=[pl.BlockSpec((8, d), lambda i: (0, 0)), row],
        out_shape=[jax.ShapeDtypeStruct((8, d), F32), jax.ShapeDtypeStruct((seq, d), F32)],
        compiler_params=_params(32),
    )(y, target)


def _out_proj_bwd(dy, wout_all, layer, cat, gates, oa, od, bd, dwout_acc):
    seq, d = dy.shape
    tm = min(256, seq)

    def body(dy_ref, w_ref, cat_ref, g_ref, oa_ref, od_ref, bd_ref, acc_in, doa_ref, dod_ref, delta_ref, dg_ref, dw_ref, dcat):
        del acc_in

        @pl.when(pl.program_id(0) == 0)
        def _():
            dw_ref[...] = jnp.zeros_like(dw_ref)

        dyb = dy_ref[...].astype(BF16)
        dw = _dot_tn(cat_ref[...], dyb)
        for b in range(N_DEV):
            dw_ref[b] += dw[b * ROWS_PER_DEV:(b + 1) * ROWS_PER_DEV, :]
            dcat[:, b * ROWS_PER_DEV:(b + 1) * ROWS_PER_DEV] = _dot_nt(dyb, w_ref[b])
        silu, dsilu = _silu_parts(g_ref[...])
        dc = dcat[...]
        dmix = dc * silu
        oa_v, od_v = oa_ref[...], od_ref[...]
        dg_ref[:, :W_SB] = dc[:, :W_SB] * oa_v * dsilu[:, :W_SB]
        dg_ref[:, W_SB:] = dc[:, W_SB:] * od_v * dsilu[:, W_SB:]
        doa_ref[...] = dmix[:, :W_SB]
        dod = dmix[:, W_SB:]
        dod_ref[...] = dod
        prod = dod * od_v
        for k in range(W_DIL // (2 * LANES)):
            sl = slice(k * 2 * LANES, (k + 1) * 2 * LANES)
            delta_ref[:, sl] = _head_sums(prod[:, sl], bd_ref[...])

    row = lambda w: pl.BlockSpec((tm, w), lambda i: (i, 0))
    slab = pl.BlockSpec((N_DEV, None, ROWS_PER_DEV, d), lambda i: (0, layer, 0, 0))
    return pl.pallas_call(
        body, name="out_proj_bwd", grid=(seq // tm,),
        in_specs=[row(d), slab, row(MIX), row(MIX), row(W_SB), row(W_DIL),
                  pl.BlockSpec((2 * LANES, 2 * LANES), lambda i: (0, 0)), pl.BlockSpec(memory_space=pl.ANY)],
        out_specs=[row(W_SB), row(W_DIL), row(W_DIL), row(MIX), slab],
        out_shape=[jax.ShapeDtypeStruct((seq, W_SB), F32), jax.ShapeDtypeStruct((seq, W_DIL), F32),
                   jax.ShapeDtypeStruct((seq, W_DIL), F32), jax.ShapeDtypeStruct((seq, MIX), F32),
                   jax.ShapeDtypeStruct(dwout_acc.shape, F32)],
        scratch_shapes=[pltpu.VMEM((tm, MIX), F32)],
        input_output_aliases={7: 4},
        compiler_params=_params(48),
    )(dy, wout_all, cat, gates, oa, od, bd, dwout_acc)


def _sb_bwd(proj_bf, d_oa, oa, tri_a, tri_b):
    seq = proj_bf.shape[0]
    pairs = W_SB // LANES
    nq = seq // BLOCK

    def body(q_ref, k_ref, v_ref, do_ref, o_ref, tria_ref, trib_ref, dq_ref, dk_hbm, dv_hbm,
             r_ref, sfx_ref, dtot_ref, dq_acc, dk_acc, dv_acc, sems):
        p, i = pl.program_id(0), pl.program_id(1)

        @pl.when(i == 0)
        def _():
            dk_acc[...] = jnp.zeros_like(dk_acc)
            dv_acc[...] = jnp.zeros_like(dv_acc)

        lo = _lane_lo((BLOCK, LANES))
        qs = _stack_heads(q_ref[...].astype(F32) * QK_SCALE, lo).astype(BF16)
        dos = _stack_heads(do_ref[...], lo).astype(BF16)
        o2 = o_ref[...]
        tri, trib = tria_ref[...], trib_ref[...]
        dtot_ref[...] = _dot_exact(dos.astype(F32) * jnp.concatenate([o2, o2], axis=0), tri[:, BLOCK:])
        causal = _stacked_causal()
        for ref in (r_ref, sfx_ref, dq_acc):
            ref[...] = jnp.zeros_like(ref)

        def block(j, mask):
            rows = pl.ds(pl.multiple_of(j * BLOCK, BLOCK), BLOCK)
            kb, vb = k_ref[rows, :], v_ref[rows, :]
            ls, a, tot = _sb_scores(qs, kb, tri, r_ref[...], mask)
            beta = jnp.exp(ls)
            pw = a * _dot_nt(dos, vb)
            cp = _dot_exact(pw, trib)
            before = dtot_ref[...] - sfx_ref[...] - cp[:, :BLOCK]
            dz = pw * (1.0 - beta) - before * beta
            if mask is not None:
                dz = jnp.where(mask, dz, 0.0)
            dzb = dz.astype(BF16)
            dq_acc[...] += _dot(dzb, kb)
            dk_acc[rows, :] += _dot_tn(dzb, qs)
            dv_acc[rows, :] += _dot_tn(a.astype(BF16), dos)
            r_ref[...] += tot
            sfx_ref[...] += cp[:, BLOCK:]

        def alive():
            return (jnp.max(r_ref[...]) > DEAD_LOG).astype(jnp.int32)

        block(i, causal)

        def step(carry):
            block(carry[0], None)
            return carry[0] - 1, alive()

        lax.while_loop(lambda carry: jnp.logical_and(carry[0] >= 0, carry[1] > 0), step, (i - 1, alive()))
        dq_ref[...] = _unstack_heads(dq_acc[...], lo) * QK_SCALE

        @pl.when(i == nq - 1)
        def _():
            outs = [pltpu.make_async_copy(dk_acc, dk_hbm.at[p], sems.at[0]),
                    pltpu.make_async_copy(dv_acc, dv_hbm.at[p], sems.at[1])]
            for cp in outs:
                cp.start()
            for cp in outs:
                cp.wait()

    blk = pl.BlockSpec((BLOCK, LANES), lambda p, i: (i, p))
    const = pl.BlockSpec((BLOCK, 2 * BLOCK), lambda p, i: (0, 0))
    any_spec = pl.BlockSpec(memory_space=pl.ANY)
    acc = pltpu.VMEM((2 * BLOCK, LANES), F32)
    return pl.pallas_call(
        body, name="sb_bwd", grid=(pairs, nq),
        in_specs=[blk, pl.BlockSpec((seq, LANES), lambda p, i: (0, pairs + p)),
                  pl.BlockSpec((seq, LANES), lambda p, i: (0, 2 * pairs + p)), blk, blk, const, const],
        out_specs=[blk, any_spec, any_spec],
        out_shape=[jax.ShapeDtypeStruct((seq, W_SB), F32), jax.ShapeDtypeStruct((pairs, seq, LANES), F32),
                   jax.ShapeDtypeStruct((pairs, seq, LANES), F32)],
        scratch_shapes=[acc, acc, acc, acc, pltpu.VMEM((seq, LANES), F32), pltpu.VMEM((seq, LANES), F32),
                        pltpu.SemaphoreType.DMA((2,))],
        compiler_params=_params(56),
    )(proj_bf, proj_bf, proj_bf, d_oa, oa, tri_a, tri_b)


def _dil_bwd(qkn, qkv_raw, d_od, lse, delta, r):
    seq = qkn.shape[0]
    tile, edge, per = _dil_tiling(seq, r)
    ntile = seq // tile
    pairs = W_DIL // LANES

    def body(qc, qn, doc, don, lsc, lsn, dlc, dln, kc, kp, vc, vp, dq_ref, dk_ref, dv_ref):
        n = pl.program_id(1)
        lo = _lane_lo((BLOCK, LANES))
        band, col = _band_mask(2 * BLOCK, 2 * BLOCK)
        first_band = jnp.logical_and(band, jnp.logical_or(col >= BLOCK, n > 0))
        after_band = jnp.logical_and(_band_mask(2 * BLOCK, BLOCK)[0], n < ntile - 1)

        def grads(qb, dob, lsb, dlb, keys, vals, valid):
            qs, dos = _stack_heads(qb, lo).astype(BF16), _stack_heads(dob, lo).astype(BF16)
            lse_s = jnp.concatenate([_col_of(lsb, lo, 0), _col_of(lsb, lo, 1)], axis=0)
            delta_s = jnp.concatenate([_col_of(dlb, lo, 0), _col_of(dlb, lo, 1)], axis=0)
            kb, vb = keys.astype(BF16), vals.astype(BF16)
            pw = jnp.where(valid, jnp.exp(_dot_nt(qs, kb) - lse_s), 0.0)
            ds = (pw * (_dot_nt(dos, vb) - delta_s)).astype(BF16)
            return _unstack_heads(_dot(ds, kb), lo), _dot_tn(ds, qs), _dot_tn(pw.astype(BF16), dos)

        def one_class(c):
            rows0 = _class_rows(c, 0, BLOCK, r)
            dq, dk, dv = grads(qc[rows0, :], doc[rows0, :], lsc[rows0, :], dlc[rows0, :],
                               jnp.concatenate([kp[rows0, :], kc[rows0, :]], axis=0),
                               jnp.concatenate([vp[rows0, :], vc[rows0, :]], axis=0), first_band)
            dq_ref[rows0, :] = dq
            dk_ref[rows0, :] = dk[BLOCK:]
            dv_ref[rows0, :] = dv[BLOCK:]

            def later(a):
                rows = _class_rows(c, a * BLOCK, BLOCK, r)
                prev = _class_rows(c, (a - 1) * BLOCK, BLOCK, r)
                both = _class_rows(c, (a - 1) * BLOCK, 2 * BLOCK, r)
                dq, dk, dv = grads(qc[rows, :], doc[rows, :], lsc[rows, :], dlc[rows, :], kc[both, :], vc[both, :], band)
                dq_ref[rows, :] = dq
                dk_ref[prev, :] += dk[:BLOCK]
                dv_ref[prev, :] += dv[:BLOCK]
                dk_ref[rows, :] = dk[BLOCK:]
                dv_ref[rows, :] = dv[BLOCK:]

            _for_later_blocks(per, later)
            last = _class_rows(c, (per - 1) * BLOCK, BLOCK, r)
            _, dk, dv = grads(qn[rows0, :], don[rows0, :], lsn[rows0, :], dln[rows0, :], kc[last, :], vc[last, :], after_band)
            dk_ref[last, :] += dk
            dv_ref[last, :] += dv

        _for_each_class(r, per, one_class)

    per_edge = tile // edge
    cur = lambda off: pl.BlockSpec((tile, LANES), lambda p, n: (n, off + p))
    before = lambda off: pl.BlockSpec((edge, LANES), lambda p, n: (jnp.maximum(n * per_edge - 1, 0), off + p))
    after = lambda off: pl.BlockSpec((edge, LANES), lambda p, n: (jnp.minimum((n + 1) * per_edge, seq // edge - 1), off + p))
    return pl.pallas_call(
        body, name=f"dil_bwd_r{r}", grid=(pairs, ntile),
        in_specs=[cur(0), after(0)] * 4 + [cur(pairs), before(pairs), cur(2 * pairs), before(2 * pairs)],
        out_specs=[cur(0)] * 3,
        out_shape=[jax.ShapeDtypeStruct((seq, W_DIL), F32)] * 3,
        compiler_params=_params(48),
    )(qkn, qkn, d_od, d_od, lse, lse, delta, delta, qkn, qkn, qkv_raw, qkv_raw)


def _assemble_dproj(d_qa, d_ka, d_va, d_gates, dq_p, dk_p, dv_p, qk_raw, gains, tables, bd):
    seq = qk_raw.shape[0]
    tm = min(256, seq)
    chunks = W_DIL // (2 * LANES)

    def body(dqa, dka0, dka1, dva0, dva1, dg, q0, q1, q2, k0, k1, k2, v0, v1, v2, x_ref, g_ref, c_ref, sn_ref, sp_ref,
             bd_ref, dp_ref, gpart_ref):
        @pl.when(pl.program_id(0) == 0)
        def _():
            gpart_ref[...] = jnp.zeros_like(gpart_ref)

        def put(first_col, v):
            dp_ref[:, first_col:first_col + v.shape[1]] = v.astype(BF16)

        put(0, dqa[...])
        put(W_SB, dka0[...])
        put(W_SB + LANES, dka1[...])
        put(2 * W_SB, dva0[...])
        put(2 * W_SB + LANES, dva1[...])
        put(3 * W_SB, dg[:, :W_SB])
        put(4 * W_SB + 2 * W_DIL, v0[...] + v1[...] + v2[...])
        put(4 * W_SB + 3 * W_DIL, dg[:, W_SB:])
        c, sn, sp, bdm = c_ref[...], sn_ref[...], sp_ref[...], bd_ref[...]
        for which, parts in enumerate(((q0, q1, q2), (k0, k1, k2))):
            scale = QK_SCALE if which == 0 else 1.0
            for k in range(chunks):
                sl = slice(k * 2 * LANES, (k + 1) * 2 * LANES)
                dyv = (parts[0][:, sl] + parts[1][:, sl] + parts[2][:, sl]) * scale
                dxn = _rope_t(dyv, c, sn, sp)
                x = x_ref[:, which * W_DIL + k * 2 * LANES:which * W_DIL + (k + 1) * 2 * LANES]
                rs = lax.rsqrt(_head_sums(x * x, bdm) * (1.0 / HEAD_DIM) + EPS)
                xhat = x * rs
                gpart_ref[which] += jnp.sum((dxn * xhat).reshape(tm // 8, 8, 2 * LANES), axis=0)
                dxhat = dxn * g_ref[which]
                mean = _head_sums(dxhat * xhat, bdm) * (1.0 / HEAD_DIM)
                put(4 * W_SB + which * W_DIL + k * 2 * LANES, rs * (dxhat - xhat * mean))

    row = lambda w: pl.BlockSpec((tm, w), lambda i: (i, 0))
    pair = lambda p: pl.BlockSpec((None, tm, LANES), lambda i: (p, i, 0))
    const = lambda shape: pl.BlockSpec(shape, lambda i: tuple(0 for _ in shape))
    return pl.pallas_call(
        body, name="assemble_dproj", grid=(seq // tm,),
        in_specs=[row(W_SB), pair(0), pair(1), pair(0), pair(1), row(MIX)] + [row(W_DIL)] * 9 +
                 [row(2 * W_DIL), const((2, 1, 2 * LANES)), row(2 * LANES), row(2 * LANES), row(2 * LANES),
                  const((2 * LANES, 2 * LANES))],
        out_specs=[row(IN_COLS), const((2, 8, 2 * LANES))],
        out_shape=[jax.ShapeDtypeStruct((seq, IN_COLS), BF16), jax.ShapeDtypeStruct((2, 8, 2 * LANES), F32)],
        compiler_params=_params(48),
    )(d_qa, d_ka, d_ka, d_va, d_va, d_gates, *dq_p, *dk_p, *dv_p, qk_raw, gains, *tables, bd)


def _dw_in(h, dproj, layer, dwin_acc):
    seq, d = h.shape
    tm = min(2048, seq)

    def body(h_ref, dp_ref, acc_in, dw_ref):
        del acc_in

        @pl.when(pl.program_id(1) == 0)
        def _():
            dw_ref[...] = jnp.zeros_like(dw_ref)

        dw_ref[...] += _dot_tn(h_ref[...], dp_ref[...])

    slab = pl.BlockSpec((None, None, d, COLS_PER_DEV), lambda n, i: (n, layer, 0, 0))
    return pl.pallas_call(
        body, name="dw_in", grid=(N_DEV, seq // tm),
        in_specs=[pl.BlockSpec((tm, d), lambda n, i: (i, 0)), pl.BlockSpec((tm, COLS_PER_DEV), lambda n, i: (i, n)),
                  pl.BlockSpec(memory_space=pl.ANY)],
        out_specs=slab, out_shape=jax.ShapeDtypeStruct(dwin_acc.shape, F32),
        input_output_aliases={2: 0},
        compiler_params=_params(40),
    )(h, dproj, dwin_acc)


def _dx_norm(dproj, w_all, layer, x, g, dx_next):
    seq, d = x.shape
    tm = min(256, seq)

    def body(dp_ref, w_ref, x_ref, g_ref, dn_ref, dx_ref, gpart_ref):
        @pl.when(pl.program_id(0) == 0)
        def _():
            gpart_ref[...] = jnp.zeros_like(gpart_ref)

        dh = jnp.zeros((tm, d), F32)
        for n in range(N_DEV):
            dh = dh + _dot_nt(dp_ref[:, n * COLS_PER_DEV:(n + 1) * COLS_PER_DEV], w_ref[n])
        xf = x_ref[...]
        rs = lax.rsqrt(jnp.mean(xf * xf, axis=-1, keepdims=True) + EPS)
        xhat = xf * rs
        gpart_ref[...] += jnp.sum((dh * xhat).reshape(tm // 8, 8, d), axis=0)
        dxhat = dh * g_ref[...]
        mean = jnp.mean(dxhat * xhat, axis=-1, keepdims=True)
        dx_ref[...] = rs * (dxhat - xhat * mean) + dn_ref[...]

    row = lambda w: pl.BlockSpec((tm, w), lambda i: (i, 0))
    return pl.pallas_call(
        body, name="dx_norm", grid=(seq // tm,),
        in_specs=[row(IN_COLS), pl.BlockSpec((N_DEV, None, d, COLS_PER_DEV), lambda i: (0, layer, 0, 0)), row(d),
                  pl.BlockSpec((1, d), lambda i: (0, 0)), row(d)],
        out_specs=[row(d), pl.BlockSpec((8, d), lambda i: (0, 0))],
        out_shape=[jax.ShapeDtypeStruct((seq, d), F32), jax.ShapeDtypeStruct((8, d), F32)],
        compiler_params=_params(48),
    )(dproj, w_all, x, g, dx_next)


def _exchange_grads(dwin, dwout, small):
    flips = _flips()

    def body(dwin_ref, dwout_ref, small_ref, rin_ref, rout_ref, rsmall_ref, send_sems, recv_sems, local_sems):
        x, y, c, me = _place()
        local = [pltpu.make_async_copy(dwin_ref.at[me], rin_ref.at[me], local_sems.at[0]),
                 pltpu.make_async_copy(dwout_ref.at[me], rout_ref.at[me], local_sems.at[1]),
                 pltpu.make_async_copy(small_ref, rsmall_ref.at[me], local_sems.at[2])]
        for cp in local:
            cp.start()
        copies = []
        for k, flip in enumerate(flips):
            px, py, pc = _peer(x, y, c, flip)
            to = 4 * px + 2 * py + pc
            for a, (src, dst) in enumerate(((dwin_ref.at[to], rin_ref), (dwout_ref.at[to], rout_ref), (small_ref, rsmall_ref))):
                cp = pltpu.make_async_remote_copy(
                    src_ref=src, dst_ref=dst.at[me], send_sem=send_sems.at[3 * k + a], recv_sem=recv_sems.at[3 * k + a],
                    device_id=(px, py, pc), device_id_type=MESH_ID)
                cp.start()
                copies.append(cp)
        for cp in copies:
            cp.wait()
        for cp in local:
            cp.wait()

    any_spec = pl.BlockSpec(memory_space=pl.ANY)
    vmem = pl.BlockSpec(memory_space=pltpu.VMEM)
    n = 3 * len(flips)
    return pl.pallas_call(
        body, name="exchange_grads",
        in_specs=[any_spec, any_spec, vmem], out_specs=[any_spec, any_spec, vmem],
        out_shape=[jax.ShapeDtypeStruct(dwin.shape, F32), jax.ShapeDtypeStruct(dwout.shape, F32),
                   jax.ShapeDtypeStruct((N_DEV,) + small.shape, F32)],
        scratch_shapes=[pltpu.SemaphoreType.DMA((n,)), pltpu.SemaphoreType.DMA((n,)), pltpu.SemaphoreType.DMA((3,))],
    )(dwin, dwout, small)


def _adamw_math(g, w, m, v):
    m = ADAM_B1 * m + (1.0 - ADAM_B1) * g
    v = ADAM_B2 * v + (1.0 - ADAM_B2) * (g * g)
    m_hat = m / (1.0 - ADAM_B1 ** ADAM_STEP)
    v_hat = v / (1.0 - ADAM_B2 ** ADAM_STEP)
    delta = -ADAM_LR * (m_hat / (jnp.sqrt(v_hat) + ADAM_EPS) + ADAM_WD * w)
    return delta, m, v


def _adamw(parts, w, m, v, name):
    nl, r, c = w.shape
    tr = min(r, (256 * 512) // c)

    def body(p_ref, w_ref, m_ref, v_ref, g_ref, d_ref, nm_ref, nv_ref):
        g = p_ref[0]
        for s in range(1, N_DEV):
            g = g + p_ref[s]
        g_ref[...] = g
        d_ref[...], nm_ref[...], nv_ref[...] = _adamw_math(g, w_ref[...], m_ref[...], v_ref[...])

    blk = pl.BlockSpec((None, tr, c), lambda l, i: (l, i, 0))
    return pl.pallas_call(
        body, name=name, grid=(nl, r // tr),
        in_specs=[pl.BlockSpec((N_DEV, None, tr, c), lambda l, i: (0, l, i, 0)), blk, blk, blk],
        out_specs=[blk] * 4, out_shape=[jax.ShapeDtypeStruct(w.shape, F32)] * 4,
        compiler_params=_params(32),
    )(parts, w, m, v)


def _adamw_small(parts, w, m, v):
    def body(p_ref, w_ref, m_ref, v_ref, g_ref, d_ref, nm_ref, nv_ref):
        g = p_ref[0]
        for s in range(1, N_DEV):
            g = g + p_ref[s]
        g_ref[...] = g
        d_ref[...], nm_ref[...], nv_ref[...] = _adamw_math(g, w_ref[...], m_ref[...], v_ref[...])

    vmem = pl.BlockSpec(memory_space=pltpu.VMEM)
    return pl.pallas_call(
        body, name="adamw_small", in_specs=[vmem] * 4, out_specs=[vmem] * 4,
        out_shape=[jax.ShapeDtypeStruct(w.shape, F32)] * 4,
    )(parts, w, m, v)


def _pack_small(a, b, c):
    pad = jnp.zeros((a.shape[0], SMALL_W - a.shape[1] - b.shape[1] - c.shape[1]), F32)
    return jnp.concatenate([a, b, c, pad], axis=1)


def _unpack_small(t, d):
    return t[:, :d], t[:, d:d + HEAD_DIM], t[:, d + HEAD_DIM:d + 2 * HEAD_DIM]


def _local_step(x0, target, norm_g, q_norm_g, k_norm_g, win_all, wout_all):
    seq, d = x0.shape
    depth = norm_g.shape[0]
    tri_a, tri_b, bd = _tri_constants()
    tables = _rope_tables(seq)
    rep = (2 * LANES) // HEAD_DIM

    saved = []
    xl = x0
    for layer in range(depth):
        gains = jnp.stack([jnp.tile(q_norm_g[layer], rep), jnp.tile(k_norm_g[layer], rep)])[:, None, :]
        proj_bf, gates, qk_raw, h = _norm_proj(xl, norm_g[layer][None, :], win_all, layer)
        qkn = _qk_prep(qk_raw, gains, tables, bd)
        oa = _sb_fwd(proj_bf, tri_a)
        o_p, ld_p = zip(*[_dil_fwd(qkn, qk_raw, r) for _, r in DIL_PATTERNS])
        x_next, cat, od, lse = _out_proj(xl, oa, o_p, ld_p, gates, wout_all, layer)
        saved.append((xl, gains, proj_bf, gates, qk_raw, h, qkn, oa, cat, od, lse))
        xl = x_next

    loss_part, dx = _loss_head(xl, target)

    dwin = jnp.zeros((N_DEV, depth, d, COLS_PER_DEV), F32)
    dwout = jnp.zeros((N_DEV, depth, ROWS_PER_DEV, d), F32)
    g_norm, g_q, g_k = [None] * depth, [None] * depth, [None] * depth
    for layer in reversed(range(depth)):
        xl, gains, proj_bf, gates, qk_raw, h, qkn, oa, cat, od, lse = saved[layer]
        d_oa, d_od, delta, d_gates, dwout = _out_proj_bwd(dx, wout_all, layer, cat, gates, oa, od, bd, dwout)
        d_qa, d_ka, d_va = _sb_bwd(proj_bf, d_oa, oa, tri_a, tri_b)
        dq_p, dk_p, dv_p = zip(*[_dil_bwd(qkn, qk_raw, d_od, lse, delta, r) for _, r in DIL_PATTERNS])
        dproj, gqk = _assemble_dproj(d_qa, d_ka, d_va, d_gates, dq_p, dk_p, dv_p, qk_raw, gains, tables, bd)
        dwin = _dw_in(h, dproj, layer, dwin)
        dx, gn = _dx_norm(dproj, win_all, layer, xl, norm_g[layer][None, :], dx)
        g_norm[layer] = jnp.sum(gn, axis=0)
        gqk = jnp.sum(gqk, axis=1).reshape(2, rep, HEAD_DIM).sum(axis=1)
        g_q[layer], g_k[layer] = gqk[0], gqk[1]

    small = _pack_small(jnp.stack(g_norm), jnp.stack(g_q), jnp.stack(g_k))
    return loss_part, dx, dwin, dwout, small


def kernel(x, norm_g, w_in, q_norm_g, k_norm_g, w_out, loss_target, m_norm_g, m_w_in, m_q_norm_g, m_k_norm_g, m_w_out,
           v_norm_g, v_w_in, v_q_norm_g, v_k_norm_g, v_w_out):
    d = w_in.shape[1]
    seq = x.shape[1]
    win_all, wout_all = _all_gather_weights(_cast_bf16(w_in, "cast_w_in"), _cast_bf16(w_out, "cast_w_out"))
    loss_part, dx, dwin, dwout, small = _local_step(
        x.reshape(seq, d), loss_target.reshape(seq, d), norm_g, q_norm_g, k_norm_g, win_all, wout_all)
    loss = lax.psum(jnp.sum(loss_part), ("x", "y", "c"))
    rin, rout, rsmall = _exchange_grads(dwin, dwout, small)

    g_in, d_in, nm_in, nv_in = _adamw(rin, w_in, m_w_in, v_w_in, "adamw_w_in")
    g_out, d_out, nm_out, nv_out = _adamw(rout, w_out, m_w_out, v_w_out, "adamw_w_out")
    small_out = _adamw_small(rsmall, _pack_small(norm_g, q_norm_g, k_norm_g), _pack_small(m_norm_g, m_q_norm_g, m_k_norm_g),
                             _pack_small(v_norm_g, v_q_norm_g, v_k_norm_g))
    (g_n, g_qn, g_kn), (d_n, d_qn, d_kn), (nm_n, nm_qn, nm_kn), (nv_n, nv_qn, nv_kn) = (_unpack_small(t, d) for t in small_out)

    return (loss, dx.reshape(x.shape), g_n, g_in, g_qn, g_kn, g_out, d_n, d_in, d_qn, d_kn, d_out,
            nm_n, nm_in, nm_qn, nm_kn, nm_out, nv_n, nv_in, nv_qn, nv_kn, nv_out)
```

```python
import functools
import math

import jax
import jax.numpy as jnp
from jax import lax
from jax.experimental import pallas as pl
from jax.experimental.pallas import tpu as pltpu

F32 = jnp.float32
BF16 = jnp.bfloat16

EPS = 1e-6
HEAD_DIM = 64
BLOCK = 128
LANES = 128
W_SB = 256
W_DIL = 768
MIX = W_SB + W_DIL
IN_COLS = 4 * W_SB + 4 * W_DIL
N_DEV = 8
COLS_PER_DEV = IN_COLS // N_DEV
ROWS_PER_DEV = MIX // N_DEV
QK_SCALE = 1.0 / math.sqrt(HEAD_DIM)
DIL_PATTERNS = ((128, 1), (512, 4), (2048, 16))
DIL_TILE = 2048
ROPE_THETA = 500000.0
ROPE_DIM = HEAD_DIM // 4
ROPE_HALF = ROPE_DIM // 2
DEAD_LOG = -110.0
SB_PEEL = 3
NEG_BIG = -1e30

ADAM_LR = 0.001
ADAM_B1 = 0.9
ADAM_B2 = 0.999
ADAM_EPS = 1e-08
ADAM_WD = 0.01
ADAM_STEP = 10

SMALL_W = 1280
MESH_ID = pl.DeviceIdType.MESH
MIB = 1 << 20


def _params(vmem_mib):
    return pltpu.CompilerParams(vmem_limit_bytes=vmem_mib * MIB)


def _dot(a, b):
    return jnp.dot(a, b, preferred_element_type=F32)


def _dot_nt(a, b):
    return lax.dot_general(a, b, (((1,), (1,)), ((), ())), preferred_element_type=F32)


def _dot_tn(a, b):
    return lax.dot_general(a, b, (((0,), (0,)), ((), ())), preferred_element_type=F32)


def _dot_exact(x, m01):
    hi = x.astype(BF16)
    r1 = x - hi.astype(F32)
    mid = r1.astype(BF16)
    lo = (r1 - mid.astype(F32)).astype(BF16)
    return _dot(hi, m01) + _dot(mid, m01) + _dot(lo, m01)


def _lane_lo(shape):
    return lax.broadcasted_iota(jnp.int32, shape, 1) < HEAD_DIM


def _col_of(b, lo, h):
    keep = lo if h == 0 else jnp.logical_not(lo)
    return jnp.max(jnp.where(keep, b, NEG_BIG), axis=1, keepdims=True)


def _tri_constants():
    j = jnp.arange(BLOCK)
    ones = jnp.ones((BLOCK, BLOCK), F32)
    excl = (j[:, None] > j[None, :]).astype(F32)
    incl = (j[:, None] >= j[None, :]).astype(F32)
    tri_a = jnp.concatenate([excl, ones], axis=1).astype(BF16)
    tri_b = jnp.concatenate([incl, ones], axis=1).astype(BF16)
    d = jnp.arange(2 * LANES)
    bd = (d[:, None] // HEAD_DIM == d[None, :] // HEAD_DIM).astype(BF16)
    return tri_a, tri_b, bd


def _rope_tables(seq):
    inv_freq = 1.0 / (ROPE_THETA ** (jnp.arange(ROPE_HALF, dtype=F32) * 2.0 / ROPE_DIM))
    ang = jnp.arange(seq).astype(F32)[:, None] * inv_freq[None, :]
    cos, sin = jnp.cos(ang), jnp.sin(ang)
    one = jnp.ones((seq, HEAD_DIM - ROPE_DIM), F32)
    zero8 = jnp.zeros((seq, ROPE_HALF), F32)
    zero_rest = jnp.zeros((seq, HEAD_DIM - ROPE_DIM), F32)
    c = jnp.concatenate([cos, cos, one], axis=1)
    s_next = jnp.concatenate([-sin, zero8, zero_rest], axis=1)
    s_prev = jnp.concatenate([zero8, sin, zero_rest], axis=1)
    rep = (2 * LANES) // HEAD_DIM
    return tuple(jnp.tile(t, (1, rep)) for t in (c, s_next, s_prev))


def _roll_lanes(x, shift):
    return jnp.concatenate([pltpu.roll(x[:, :LANES], shift, 1), pltpu.roll(x[:, LANES:], shift, 1)], axis=1)


def _rope(x, c, s_next, s_prev):
    return x * c + _roll_lanes(x, LANES - ROPE_HALF) * s_next + _roll_lanes(x, ROPE_HALF) * s_prev


def _rope_t(dy, c, s_next, s_prev):
    return dy * c + _roll_lanes(dy * s_next, ROPE_HALF) + _roll_lanes(dy * s_prev, LANES - ROPE_HALF)


def _cast_bf16(w, name):
    nl, r, c = w.shape

    def body(w_ref, o_ref):
        o_ref[...] = w_ref[...].astype(BF16)

    return pl.pallas_call(
        body, name=name, grid=(nl,),
        in_specs=[pl.BlockSpec((None, r, c), lambda l: (l, 0, 0))],
        out_specs=pl.BlockSpec((None, r, c), lambda l: (l, 0, 0)),
        out_shape=jax.ShapeDtypeStruct(w.shape, BF16),
        compiler_params=_params(24),
    )(w)


def _flips():
    return [(dx, dy, dc) for dx in (0, 1) for dy in (0, 1) for dc in (0, 1) if (dx, dy, dc) != (0, 0, 0)]


def _place():
    x, y, c = lax.axis_index("x"), lax.axis_index("y"), lax.axis_index("c")
    return x, y, c, 4 * x + 2 * y + c


def _peer(x, y, c, flip):
    dx, dy, dc = flip
    return (1 - x if dx else x, 1 - y if dy else y, 1 - c if dc else c)


def _all_gather_weights(win_bf, wout_bf):
    flips = _flips()

    def body(win_ref, wout_ref, oin_ref, oout_ref, send_sems, recv_sems, local_sems):
        x, y, c, me = _place()
        local = [pltpu.make_async_copy(win_ref, oin_ref.at[me], local_sems.at[0]),
                 pltpu.make_async_copy(wout_ref, oout_ref.at[me], local_sems.at[1])]
        for cp in local:
            cp.start()
        copies = []
        for k, flip in enumerate(flips):
            for a, (src, dst) in enumerate(((win_ref, oin_ref), (wout_ref, oout_ref))):
                cp = pltpu.make_async_remote_copy(
                    src_ref=src, dst_ref=dst.at[me], send_sem=send_sems.at[2 * k + a], recv_sem=recv_sems.at[2 * k + a],
                    device_id=_peer(x, y, c, flip), device_id_type=MESH_ID)
                cp.start()
                copies.append(cp)
        for cp in copies:
            cp.wait()
        for cp in local:
            cp.wait()

    any_spec = pl.BlockSpec(memory_space=pl.ANY)
    n = 2 * len(flips)
    return pl.pallas_call(
        body, name="all_gather_weights",
        in_specs=[any_spec, any_spec], out_specs=[any_spec, any_spec],
        out_shape=[jax.ShapeDtypeStruct((N_DEV,) + win_bf.shape, BF16), jax.ShapeDtypeStruct((N_DEV,) + wout_bf.shape, BF16)],
        scratch_shapes=[pltpu.SemaphoreType.DMA((n,)), pltpu.SemaphoreType.DMA((n,)), pltpu.SemaphoreType.DMA((2,))],
    )(win_bf, wout_bf)


_F32_ROUTES = {1: ((256, 256, 0, 0),), 2: ((0, 512, 1, 0),), 3: ((0, 512, 1, 512),), 4: ((0, 512, 1, 1024),),
               5: ((0, 512, 1, 1536),), 6: ((0, 256, 1, 2048), (256, 256, 0, 256)), 7: ((0, 512, 0, 512),)}


def _norm_proj(x, g, w_all, layer):
    seq, d = x.shape
    tm = min(256, seq)

    def body(x_ref, g_ref, w_ref, pbf_ref, gates_ref, qk_ref, h_ref):
        xf = x_ref[...]
        rs = lax.rsqrt(jnp.mean(xf * xf, axis=-1, keepdims=True) + EPS)
        h = (xf * rs * g_ref[...]).astype(BF16)
        h_ref[...] = h
        targets = (gates_ref, qk_ref)
        for n in range(N_DEV):
            acc = _dot(h, w_ref[n])
            pbf_ref[:, n * COLS_PER_DEV:(n + 1) * COLS_PER_DEV] = acc.astype(BF16)
            for lo, width, tgt, dst in _F32_ROUTES.get(n, ()):
                targets[tgt][:, dst:dst + width] = acc[:, lo:lo + width]

    row = lambda w: pl.BlockSpec((tm, w), lambda i: (i, 0))
    return pl.pallas_call(
        body, name="norm_proj", grid=(seq // tm,),
        in_specs=[row(d), pl.BlockSpec((1, d), lambda i: (0, 0)),
                  pl.BlockSpec((N_DEV, None, d, COLS_PER_DEV), lambda i: (0, layer, 0, 0))],
        out_specs=[row(IN_COLS), row(MIX), row(3 * W_DIL), row(d)],
        out_shape=[jax.ShapeDtypeStruct((seq, IN_COLS), BF16), jax.ShapeDtypeStruct((seq, MIX), F32),
                   jax.ShapeDtypeStruct((seq, 3 * W_DIL), F32), jax.ShapeDtypeStruct((seq, d), BF16)],
        compiler_params=_params(48),
    )(x, g, w_all)


def _head_sums(v, bd):
    hi = v.astype(BF16)
    lo = (v - hi.astype(F32)).astype(BF16)
    return _dot(hi, bd) + _dot(lo, bd)


def _qk_prep(qk_raw, gains, tables, bd):
    seq = qk_raw.shape[0]
    tm = min(1024, seq)
    chunks = W_DIL // (2 * LANES)

    def body(x_ref, g_ref, c_ref, sn_ref, sp_ref, bd_ref, o_ref):
        j = pl.program_id(1)
        x = x_ref[...]
        rs = lax.rsqrt(_head_sums(x * x, bd_ref[...]) * (1.0 / HEAD_DIM) + EPS)
        y = _rope(x * rs * g_ref[...], c_ref[...], sn_ref[...], sp_ref[...])
        o_ref[...] = y * jnp.where(j < chunks, QK_SCALE, 1.0)

    tab = pl.BlockSpec((tm, 2 * LANES), lambda i, j: (i, 0))
    return pl.pallas_call(
        body, name="qk_prep", grid=(seq // tm, 2 * chunks),
        in_specs=[pl.BlockSpec((tm, 2 * LANES), lambda i, j: (i, j)),
                  pl.BlockSpec((None, 1, 2 * LANES), lambda i, j: (j // chunks, 0, 0)),
                  tab, tab, tab, pl.BlockSpec((2 * LANES, 2 * LANES), lambda i, j: (0, 0))],
        out_specs=pl.BlockSpec((tm, 2 * LANES), lambda i, j: (i, j)),
        out_shape=jax.ShapeDtypeStruct((seq, 2 * W_DIL), F32),
        compiler_params=_params(32),
    )(qk_raw, gains, *tables, bd)


def _stack_heads(x, lo):
    return jnp.concatenate([jnp.where(lo, x, 0.0), jnp.where(lo, 0.0, x)], axis=0)


def _unstack_heads(y, lo):
    return jnp.where(lo, y[:BLOCK], y[BLOCK:])


def _stacked_causal():
    row = lax.broadcasted_iota(jnp.int32, (2 * BLOCK, BLOCK), 0) & (BLOCK - 1)
    return lax.broadcasted_iota(jnp.int32, (2 * BLOCK, BLOCK), 1) < row


def _keep(x, *conds):
    for cond in conds:
        if cond is not None:
            x = jnp.where(cond, x, 0.0)
    return x


def _sb_weights(qs, kbs, tri, r, masks, lives):
    zs = [_dot_nt(qs, kb) for kb in kbs]
    lss = [jnp.minimum(z, 0.0) - jnp.log1p(jnp.exp(-jnp.abs(z))) for z in zs]
    cts = [_dot_exact(_keep(ls - z, mask, live), tri) for ls, z, mask, live in zip(lss, zs, masks, lives)]
    weights = []
    for ls, ct, mask, live in zip(lss, cts, masks, lives):
        weights.append(_keep(jnp.exp(ls + ct[:, :BLOCK] + r), mask, live))
        r = r + ct[:, BLOCK:]
    return lss, weights, r


def _sb_fwd(proj_bf, tri_a):
    seq = proj_bf.shape[0]
    pairs = W_SB // LANES

    def body(q_ref, k_ref, v_ref, tri_ref, o_ref, r_ref, acc_ref):
        i = pl.program_id(1)
        lo = _lane_lo((BLOCK, LANES))
        qs = _stack_heads(q_ref[...].astype(F32) * QK_SCALE, lo).astype(BF16)
        causal = _stacked_causal()
        tri = tri_ref[...]

        def blocks(js, r, masks, lives):
            rows = [pl.ds(pl.multiple_of(j * BLOCK, BLOCK), BLOCK) for j in js]
            _, weights, r = _sb_weights(qs, [k_ref[rw, :] for rw in rows], tri, r, masks, lives)
            out = 0.0
            for a, rw in zip(weights, rows):
                a_hi = a.astype(BF16)
                a_lo = (a - a_hi.astype(F32)).astype(BF16)
                vb = v_ref[rw, :]
                out = out + _dot(a_hi, vb) + _dot(a_lo, vb)
            return out, r

        acc_ref[...], r_ref[...] = blocks(
            [jnp.maximum(i - k, 0) for k in range(SB_PEEL)], jnp.zeros((2 * BLOCK, LANES), F32),
            [causal] + [None] * (SB_PEEL - 1), [None] + [i >= k for k in range(1, SB_PEEL)])

        def alive():
            return (jnp.max(r_ref[...]) > DEAD_LOG).astype(jnp.int32)

        def step(carry):
            out, r_ref[...] = blocks([carry[0]], r_ref[...], [None], [None])
            acc_ref[...] += out
            return carry[0] - 1, alive()

        lax.while_loop(lambda carry: jnp.logical_and(carry[0] >= 0, carry[1] > 0), step, (i - SB_PEEL, alive()))
        o_ref[...] = _unstack_heads(acc_ref[...], lo)

    return pl.pallas_call(
        body, name="sb_fwd", grid=(pairs, seq // BLOCK),
        in_specs=[pl.BlockSpec((BLOCK, LANES), lambda p, i: (i, p)),
                  pl.BlockSpec((seq, LANES), lambda p, i: (0, pairs + p)),
                  pl.BlockSpec((seq, LANES), lambda p, i: (0, 2 * pairs + p)),
                  pl.BlockSpec((BLOCK, 2 * BLOCK), lambda p, i: (0, 0))],
        out_specs=pl.BlockSpec((BLOCK, LANES), lambda p, i: (i, p)),
        out_shape=jax.ShapeDtypeStruct((seq, W_SB), F32),
        scratch_shapes=[pltpu.VMEM((2 * BLOCK, LANES), F32), pltpu.VMEM((2 * BLOCK, LANES), F32)],
        compiler_params=_params(40),
    )(proj_bf, proj_bf, proj_bf, tri_a)


def _class_rows(c, first, count, r):
    start = c + first * r
    return pl.ds(start, count) if r == 1 else pl.ds(start, count, stride=r)


def _dil_tiling(seq, r):
    tile = min(DIL_TILE, seq)
    edge = BLOCK * r
    return tile, edge, tile // edge


def _band_mask(n_rows, n_keys):
    row = lax.broadcasted_iota(jnp.int32, (n_rows, n_keys), 0) & (BLOCK - 1)
    col = lax.broadcasted_iota(jnp.int32, (n_rows, n_keys), 1)
    return jnp.logical_and(col >= row, col <= row + BLOCK), col


def _unroll(trips):
    return max(u for u in (1, 2, 3, 4) if trips % u == 0)


def _for_each_class(r, per, fn):
    if r == 1:
        fn(0)
    else:
        lax.fori_loop(0, r, lambda c, carry: (fn(c), carry)[1], 0, unroll=_unroll(r) if per == 1 else 1)


def _for_later_blocks(per, fn):
    if per > 1:
        lax.fori_loop(1, per, lambda a, carry: (fn(a), carry)[1], 0, unroll=_unroll(per - 1))


def _dil_fwd(qkn, qkv_raw, r):
    seq = qkn.shape[0]
    tile, edge, per = _dil_tiling(seq, r)
    pairs = W_DIL // LANES

    def body(q_ref, kc_ref, kp_ref, vc_ref, vp_ref, o_ref, ld_ref):
        n = pl.program_id(1)
        lo = _lane_lo((BLOCK, LANES))
        band, col = _band_mask(2 * BLOCK, 2 * BLOCK)
        first_band = jnp.logical_and(band, jnp.logical_or(col >= BLOCK, n > 0))

        def block(qb, kcat, vcat, valid, out_rows):
            qs = _stack_heads(qb, lo).astype(BF16)
            s = jnp.where(valid, _dot_nt(qs, kcat.astype(BF16)), NEG_BIG)
            m = jnp.max(s, axis=1, keepdims=True)
            p = jnp.exp(s - m)
            l = jnp.sum(p, axis=1, keepdims=True)
            o = _dot(p.astype(BF16), vcat.astype(BF16)) / l
            ld = jnp.broadcast_to(m + jnp.log(l), (2 * BLOCK, LANES))
            o_ref[out_rows, :] = _unstack_heads(o, lo)
            ld_ref[out_rows, :] = _unstack_heads(ld, lo)

        def one_class(c):
            rows0 = _class_rows(c, 0, BLOCK, r)
            block(q_ref[rows0, :], jnp.concatenate([kp_ref[rows0, :], kc_ref[rows0, :]], axis=0),
                  jnp.concatenate([vp_ref[rows0, :], vc_ref[rows0, :]], axis=0), first_band, rows0)

            def later(a):
                rows = _class_rows(c, a * BLOCK, BLOCK, r)
                keys = _class_rows(c, (a - 1) * BLOCK, 2 * BLOCK, r)
                block(q_ref[rows, :], kc_ref[keys, :], vc_ref[keys, :], band, rows)

            _for_later_blocks(per, later)

        _for_each_class(r, per, one_class)

    per_edge = tile // edge
    cur = lambda off: pl.BlockSpec((tile, LANES), lambda p, n: (n, off + p))
    before = lambda off: pl.BlockSpec((edge, LANES), lambda p, n: (jnp.maximum(n * per_edge - 1, 0), off + p))
    return pl.pallas_call(
        body, name=f"dil_fwd_r{r}", grid=(pairs, seq // tile),
        in_specs=[cur(0), cur(pairs), before(pairs), cur(2 * pairs), before(2 * pairs)],
        out_specs=[cur(0), cur(0)],
        out_shape=[jax.ShapeDtypeStruct((seq, W_DIL), F32), jax.ShapeDtypeStruct((seq, W_DIL), F32)],
        compiler_params=_params(32),
    )(qkn, qkn, qkn, qkv_raw, qkv_raw)


def _silu_parts(g):
    sig = jax.nn.sigmoid(g)
    return g * sig, sig * (1.0 + g * (1.0 - sig))


def _out_proj(x, oa, o_p, ld_p, gates, wout_all, layer):
    seq, d = x.shape
    tm = min(256, seq)

    def body(x_ref, oa_ref, o0, o1, o2, l0, l1, l2, g_ref, w_ref, xn_ref, cat_ref, od_ref, lse_ref):
        lds = (l0[...], l1[...], l2[...])
        m = jnp.maximum(jnp.maximum(lds[0], lds[1]), lds[2])
        es = [jnp.exp(v - m) for v in lds]
        tot = es[0] + es[1] + es[2]
        lse_ref[...] = m + jnp.log(tot)
        inv = 1.0 / tot
        od = (es[0] * inv) * o0[...] + (es[1] * inv) * o1[...] + (es[2] * inv) * o2[...]
        od_ref[...] = od
        silu, _ = _silu_parts(g_ref[...])
        cat_ref[:, :W_SB] = (oa_ref[...] * silu[:, :W_SB]).astype(BF16)
        cat_ref[:, W_SB:] = (od * silu[:, W_SB:]).astype(BF16)
        y = x_ref[...]
        for b in range(N_DEV):
            y = y + _dot(cat_ref[:, b * ROWS_PER_DEV:(b + 1) * ROWS_PER_DEV], w_ref[b])
        xn_ref[...] = y

    row = lambda w: pl.BlockSpec((tm, w), lambda i: (i, 0))
    return pl.pallas_call(
        body, name="out_proj", grid=(seq // tm,),
        in_specs=[row(d), row(W_SB)] + [row(W_DIL)] * 6 + [row(MIX),
                  pl.BlockSpec((N_DEV, None, ROWS_PER_DEV, d), lambda i: (0, layer, 0, 0))],
        out_specs=[row(d), row(MIX), row(W_DIL), row(W_DIL)],
        out_shape=[jax.ShapeDtypeStruct((seq, d), F32), jax.ShapeDtypeStruct((seq, MIX), BF16),
                   jax.ShapeDtypeStruct((seq, W_DIL), F32), jax.ShapeDtypeStruct((seq, W_DIL), F32)],
        compiler_params=_params(48),
    )(x, oa, *o_p, *ld_p, gates, wout_all)


def _loss_head(y, target):
    seq, d = y.shape
    tm = min(512, seq)

    def body(y_ref, t_ref, part_ref, dy_ref):
        @pl.when(pl.program_id(0) == 0)
        def _():
            part_ref[...] = jnp.zeros_like(part_ref)

        diff = y_ref[...] - t_ref[...]
        dy_ref[...] = diff * (1.0 / d)
        part_ref[...] += jnp.sum((diff * diff).reshape(tm // 8, 8, d), axis=0) * (0.5 / d)

    row = pl.BlockSpec((tm, d), lambda i: (i, 0))
    return pl.pallas_call(
        body, name="loss_head", grid=(seq // tm,),
        in_specs=[row, row], out_specs=[pl.BlockSpec((8, d), lambda i: (0, 0)), row],
        out_shape=[jax.ShapeDtypeStruct((8, d), F32), jax.ShapeDtypeStruct((seq, d), F32)],
        compiler_params=_params(32),
    )(y, target)


def _out_proj_bwd(dy, wout_all, layer, cat, gates, oa, od, bd, dwout_acc):
    seq, d = dy.shape
    tm = min(256, seq)

    def body(dy_ref, w_ref, cat_ref, g_ref, oa_ref, od_ref, bd_ref, acc_in, doa_ref, dod_ref, delta_ref, dg_ref, dw_ref, dcat):
        del acc_in

        @pl.when(pl.program_id(0) == 0)
        def _():
            dw_ref[...] = jnp.zeros_like(dw_ref)

        dyb = dy_ref[...].astype(BF16)
        dw = _dot_tn(cat_ref[...], dyb)
        for b in range(N_DEV):
            dw_ref[b] += dw[b * ROWS_PER_DEV:(b + 1) * ROWS_PER_DEV, :]
            dcat[:, b * ROWS_PER_DEV:(b + 1) * ROWS_PER_DEV] = _dot_nt(dyb, w_ref[b])
        silu, dsilu = _silu_parts(g_ref[...])
        dc = dcat[...]
        dmix = dc * silu
        oa_v, od_v = oa_ref[...], od_ref[...]
        dg_ref[:, :W_SB] = dc[:, :W_SB] * oa_v * dsilu[:, :W_SB]
        dg_ref[:, W_SB:] = dc[:, W_SB:] * od_v * dsilu[:, W_SB:]
        doa_ref[...] = dmix[:, :W_SB]
        dod = dmix[:, W_SB:]
        dod_ref[...] = dod
        prod = dod * od_v
        for k in range(W_DIL // (2 * LANES)):
            sl = slice(k * 2 * LANES, (k + 1) * 2 * LANES)
            delta_ref[:, sl] = _head_sums(prod[:, sl], bd_ref[...])

    row = lambda w: pl.BlockSpec((tm, w), lambda i: (i, 0))
    slab = pl.BlockSpec((N_DEV, None, ROWS_PER_DEV, d), lambda i: (0, layer, 0, 0))
    return pl.pallas_call(
        body, name="out_proj_bwd", grid=(seq // tm,),
        in_specs=[row(d), slab, row(MIX), row(MIX), row(W_SB), row(W_DIL),
                  pl.BlockSpec((2 * LANES, 2 * LANES), lambda i: (0, 0)), pl.BlockSpec(memory_space=pl.ANY)],
        out_specs=[row(W_SB), row(W_DIL), row(W_DIL), row(MIX), slab],
        out_shape=[jax.ShapeDtypeStruct((seq, W_SB), F32), jax.ShapeDtypeStruct((seq, W_DIL), F32),
                   jax.ShapeDtypeStruct((seq, W_DIL), F32), jax.ShapeDtypeStruct((seq, MIX), F32),
                   jax.ShapeDtypeStruct(dwout_acc.shape, F32)],
        scratch_shapes=[pltpu.VMEM((tm, MIX), F32)],
        input_output_aliases={7: 4},
        compiler_params=_params(48),
    )(dy, wout_all, cat, gates, oa, od, bd, dwout_acc)


def _sb_bwd(proj_bf, d_oa, oa, tri_a, tri_b):
    seq = proj_bf.shape[0]
    pairs = W_SB // LANES
    nq = seq // BLOCK

    def body(q_ref, k_ref, v_ref, do_ref, o_ref, tria_ref, trib_ref, dq_ref, dk_hbm, dv_hbm,
             r_ref, sfx_ref, dtot_ref, dq_acc, dk_acc, dv_acc, sems):
        p, i = pl.program_id(0), pl.program_id(1)

        @pl.when(i == 0)
        def _():
            dk_acc[...] = jnp.zeros_like(dk_acc)
            dv_acc[...] = jnp.zeros_like(dv_acc)

        lo = _lane_lo((BLOCK, LANES))
        qs = _stack_heads(q_ref[...].astype(F32) * QK_SCALE, lo).astype(BF16)
        dos = _stack_heads(do_ref[...], lo).astype(BF16)
        o2 = o_ref[...]
        tri, trib = tria_ref[...], trib_ref[...]
        dtot_ref[...] = _dot_exact(dos.astype(F32) * jnp.concatenate([o2, o2], axis=0), tri[:, BLOCK:])
        causal = _stacked_causal()

        def blocks(js, r, sfx, masks, lives):
            rows = [pl.ds(pl.multiple_of(j * BLOCK, BLOCK), BLOCK) for j in js]
            kbs = [k_ref[rw, :] for rw in rows]
            dovs = [_dot_nt(dos, v_ref[rw, :]) for rw in rows]
            lss, weights, r = _sb_weights(qs, kbs, tri, r, masks, lives)
            pws = [a * dov for a, dov in zip(weights, dovs)]
            cps = [_dot_exact(pw, trib) for pw in pws]
            dzs = []
            for ls, pw, cp, mask, live in zip(lss, pws, cps, masks, lives):
                beta = jnp.exp(ls)
                before = dtot_ref[...] - sfx - cp[:, :BLOCK]
                dzs.append(_keep(pw * (1.0 - beta) - before * beta, mask, live).astype(BF16))
                sfx = sfx + cp[:, BLOCK:]
            dq = 0.0
            for dzb, kb in zip(dzs, kbs):
                dq = dq + _dot(dzb, kb)
            for dzb, a, rw in zip(dzs, weights, rows):
                dk_acc[rw, :] += _dot_tn(dzb, qs)
                dv_acc[rw, :] += _dot_tn(a.astype(BF16), dos)
            return dq, r, sfx

        zero = jnp.zeros((2 * BLOCK, LANES), F32)
        dq_acc[...], r_ref[...], sfx_ref[...] = blocks(
            [jnp.maximum(i - k, 0) for k in range(SB_PEEL)], zero, zero,
            [causal] + [None] * (SB_PEEL - 1), [None] + [i >= k for k in range(1, SB_PEEL)])

        def alive():
            return (jnp.max(r_ref[...]) > DEAD_LOG).astype(jnp.int32)

        def step(carry):
            dq, r_ref[...], sfx_ref[...] = blocks([carry[0]], r_ref[...], sfx_ref[...], [None], [None])
            dq_acc[...] += dq
            return carry[0] - 1, alive()

        lax.while_loop(lambda carry: jnp.logical_and(carry[0] >= 0, carry[1] > 0), step, (i - SB_PEEL, alive()))
        dq_ref[...] = _unstack_heads(dq_acc[...], lo) * QK_SCALE

        @pl.when(i == nq - 1)
        def _():
            outs = [pltpu.make_async_copy(dk_acc, dk_hbm.at[p], sems.at[0]),
                    pltpu.make_async_copy(dv_acc, dv_hbm.at[p], sems.at[1])]
            for cp in outs:
                cp.start()
            for cp in outs:
                cp.wait()

    blk = pl.BlockSpec((BLOCK, LANES), lambda p, i: (i, p))
    const = pl.BlockSpec((BLOCK, 2 * BLOCK), lambda p, i: (0, 0))
    any_spec = pl.BlockSpec(memory_space=pl.ANY)
    acc = pltpu.VMEM((2 * BLOCK, LANES), F32)
    return pl.pallas_call(
        body, name="sb_bwd", grid=(pairs, nq),
        in_specs=[blk, pl.BlockSpec((seq, LANES), lambda p, i: (0, pairs + p)),
                  pl.BlockSpec((seq, LANES), lambda p, i: (0, 2 * pairs + p)), blk, blk, const, const],
        out_specs=[blk, any_spec, any_spec],
        out_shape=[jax.ShapeDtypeStruct((seq, W_SB), F32), jax.ShapeDtypeStruct((pairs, seq, LANES), F32),
                   jax.ShapeDtypeStruct((pairs, seq, LANES), F32)],
        scratch_shapes=[acc, acc, acc, acc, pltpu.VMEM((seq, LANES), F32), pltpu.VMEM((seq, LANES), F32),
                        pltpu.SemaphoreType.DMA((2,))],
        compiler_params=_params(56),
    )(proj_bf, proj_bf, proj_bf, d_oa, oa, tri_a, tri_b)


def _dil_bwd(qkn, qkv_raw, d_od, lse, delta, r):
    seq = qkn.shape[0]
    tile, edge, per = _dil_tiling(seq, r)
    ntile = seq // tile
    pairs = W_DIL // LANES

    def body(qc, doc, lsc, dlc, kc, kp, vc, vp, dq_ref, dk_ref, dv_ref, dk_carry, dv_carry):
        n = pl.program_id(1)

        @pl.when(n == 0)
        def _():
            dk_carry[...] = jnp.zeros_like(dk_carry)
            dv_carry[...] = jnp.zeros_like(dv_carry)

        dk_ref[...] = dk_carry[...]
        dv_ref[...] = dv_carry[...]

        @pl.when(n < ntile)
        def _():
            lo = _lane_lo((BLOCK, LANES))
            band, col = _band_mask(2 * BLOCK, 2 * BLOCK)
            first_band = jnp.logical_and(band, jnp.logical_or(col >= BLOCK, n > 0))

            def grads(rows, keys, vals, valid):
                qs, dos = _stack_heads(qc[rows, :], lo).astype(BF16), _stack_heads(doc[rows, :], lo).astype(BF16)
                lsb, dlb = lsc[rows, :], dlc[rows, :]
                lse_s = jnp.concatenate([_col_of(lsb, lo, 0), _col_of(lsb, lo, 1)], axis=0)
                delta_s = jnp.concatenate([_col_of(dlb, lo, 0), _col_of(dlb, lo, 1)], axis=0)
                kb, vb = keys.astype(BF16), vals.astype(BF16)
                pw = jnp.where(valid, jnp.exp(_dot_nt(qs, kb) - lse_s), 0.0)
                ds = (pw * (_dot_nt(dos, vb) - delta_s)).astype(BF16)
                return _unstack_heads(_dot(ds, kb), lo), _dot_tn(ds, qs), _dot_tn(pw.astype(BF16), dos)

            def one_class(c):
                rows0 = _class_rows(c, 0, BLOCK, r)
                last = _class_rows(c, (per - 1) * BLOCK, BLOCK, r)
                dq, dk, dv = grads(rows0, jnp.concatenate([kp[rows0, :], kc[rows0, :]], axis=0),
                                   jnp.concatenate([vp[rows0, :], vc[rows0, :]], axis=0), first_band)
                dq_ref[rows0, :] = dq
                dk_ref[last, :] += dk[:BLOCK]
                dv_ref[last, :] += dv[:BLOCK]
                dk_carry[rows0, :] = dk[BLOCK:]
                dv_carry[rows0, :] = dv[BLOCK:]

                def later(a):
                    rows = _class_rows(c, a * BLOCK, BLOCK, r)
                    prev = _class_rows(c, (a - 1) * BLOCK, BLOCK, r)
                    both = _class_rows(c, (a - 1) * BLOCK, 2 * BLOCK, r)
                    dq, dk, dv = grads(rows, kc[both, :], vc[both, :], band)
                    dq_ref[rows, :] = dq
                    dk_carry[prev, :] += dk[:BLOCK]
                    dv_carry[prev, :] += dv[:BLOCK]
                    dk_carry[rows, :] = dk[BLOCK:]
                    dv_carry[rows, :] = dv[BLOCK:]

                _for_later_blocks(per, later)

            _for_each_class(r, per, one_class)

    per_edge = tile // edge
    here = lambda n: jnp.minimum(n, ntile - 1)
    cur = lambda off: pl.BlockSpec((tile, LANES), lambda p, n: (here(n), off + p))
    before = lambda off: pl.BlockSpec((edge, LANES), lambda p, n: (jnp.maximum(here(n) * per_edge - 1, 0), off + p))
    lagged = pl.BlockSpec((tile, LANES), lambda p, n: (jnp.maximum(n - 1, 0), p))
    carry = pltpu.VMEM((tile, LANES), F32)
    return pl.pallas_call(
        body, name=f"dil_bwd_r{r}", grid=(pairs, ntile + 1),
        in_specs=[cur(0)] * 4 + [cur(pairs), before(pairs), cur(2 * pairs), before(2 * pairs)],
        out_specs=[cur(0), lagged, lagged],
        out_shape=[jax.ShapeDtypeStruct((seq, W_DIL), F32)] * 3,
        scratch_shapes=[carry, carry],
        compiler_params=_params(40),
    )(qkn, d_od, lse, delta, qkn, qkn, qkv_raw, qkv_raw)


def _assemble_dproj(d_qa, d_ka, d_va, d_gates, dq_p, dk_p, dv_p, qk_raw, gains, tables, bd):
    seq = qk_raw.shape[0]
    tm = min(256, seq)
    chunks = W_DIL // (2 * LANES)

    def body(dqa, dka0, dka1, dva0, dva1, dg, q0, q1, q2, k0, k1, k2, v0, v1, v2, x_ref, g_ref, c_ref, sn_ref, sp_ref,
             bd_ref, dp_ref, gpart_ref):
        @pl.when(pl.program_id(0) == 0)
        def _():
            gpart_ref[...] = jnp.zeros_like(gpart_ref)

        def put(first_col, v):
            dp_ref[:, first_col:first_col + v.shape[1]] = v.astype(BF16)

        put(0, dqa[...])
        put(W_SB, dka0[...])
        put(W_SB + LANES, dka1[...])
        put(2 * W_SB, dva0[...])
        put(2 * W_SB + LANES, dva1[...])
        put(3 * W_SB, dg[:, :W_SB])
        put(4 * W_SB + 2 * W_DIL, v0[...] + v1[...] + v2[...])
        put(4 * W_SB + 3 * W_DIL, dg[:, W_SB:])
        c, sn, sp, bdm = c_ref[...], sn_ref[...], sp_ref[...], bd_ref[...]
        for which, parts in enumerate(((q0, q1, q2), (k0, k1, k2))):
            scale = QK_SCALE if which == 0 else 1.0
            for k in range(chunks):
                sl = slice(k * 2 * LANES, (k + 1) * 2 * LANES)
                dyv = (parts[0][:, sl] + parts[1][:, sl] + parts[2][:, sl]) * scale
                dxn = _rope_t(dyv, c, sn, sp)
                x = x_ref[:, which * W_DIL + k * 2 * LANES:which * W_DIL + (k + 1) * 2 * LANES]
                rs = lax.rsqrt(_head_sums(x * x, bdm) * (1.0 / HEAD_DIM) + EPS)
                xhat = x * rs
                gpart_ref[which] += jnp.sum((dxn * xhat).reshape(tm // 8, 8, 2 * LANES), axis=0)
                dxhat = dxn * g_ref[which]
                mean = _head_sums(dxhat * xhat, bdm) * (1.0 / HEAD_DIM)
                put(4 * W_SB + which * W_DIL + k * 2 * LANES, rs * (dxhat - xhat * mean))

    row = lambda w: pl.BlockSpec((tm, w), lambda i: (i, 0))
    pair = lambda p: pl.BlockSpec((None, tm, LANES), lambda i: (p, i, 0))
    const = lambda shape: pl.BlockSpec(shape, lambda i: tuple(0 for _ in shape))
    return pl.pallas_call(
        body, name="assemble_dproj", grid=(seq // tm,),
        in_specs=[row(W_SB), pair(0), pair(1), pair(0), pair(1), row(MIX)] + [row(W_DIL)] * 9 +
                 [row(2 * W_DIL), const((2, 1, 2 * LANES)), row(2 * LANES), row(2 * LANES), row(2 * LANES),
                  const((2 * LANES, 2 * LANES))],
        out_specs=[row(IN_COLS), const((2, 8, 2 * LANES))],
        out_shape=[jax.ShapeDtypeStruct((seq, IN_COLS), BF16), jax.ShapeDtypeStruct((2, 8, 2 * LANES), F32)],
        compiler_params=_params(48),
    )(d_qa, d_ka, d_ka, d_va, d_va, d_gates, *dq_p, *dk_p, *dv_p, qk_raw, gains, *tables, bd)


def _dw_in(h, dproj, layer, dwin_acc):
    seq, d = h.shape
    tm = min(2048, seq)

    def body(h_ref, dp_ref, acc_in, dw_ref):
        del acc_in

        @pl.when(pl.program_id(1) == 0)
        def _():
            dw_ref[...] = jnp.zeros_like(dw_ref)

        dw_ref[...] += _dot_tn(h_ref[...], dp_ref[...])

    slab = pl.BlockSpec((None, None, d, COLS_PER_DEV), lambda n, i: (n, layer, 0, 0))
    return pl.pallas_call(
        body, name="dw_in", grid=(N_DEV, seq // tm),
        in_specs=[pl.BlockSpec((tm, d), lambda n, i: (i, 0)), pl.BlockSpec((tm, COLS_PER_DEV), lambda n, i: (i, n)),
                  pl.BlockSpec(memory_space=pl.ANY)],
        out_specs=slab, out_shape=jax.ShapeDtypeStruct(dwin_acc.shape, F32),
        input_output_aliases={2: 0},
        compiler_params=_params(40),
    )(h, dproj, dwin_acc)


def _dx_norm(dproj, w_all, layer, x, g, dx_next):
    seq, d = x.shape
    tm = min(256, seq)

    def body(dp_ref, w_ref, x_ref, g_ref, dn_ref, dx_ref, gpart_ref):
        @pl.when(pl.program_id(0) == 0)
        def _():
            gpart_ref[...] = jnp.zeros_like(gpart_ref)

        dh = jnp.zeros((tm, d), F32)
        for n in range(N_DEV):
            dh = dh + _dot_nt(dp_ref[:, n * COLS_PER_DEV:(n + 1) * COLS_PER_DEV], w_ref[n])
        xf = x_ref[...]
        rs = lax.rsqrt(jnp.mean(xf * xf, axis=-1, keepdims=True) + EPS)
        xhat = xf * rs
        gpart_ref[...] += jnp.sum((dh * xhat).reshape(tm // 8, 8, d), axis=0)
        dxhat = dh * g_ref[...]
        mean = jnp.mean(dxhat * xhat, axis=-1, keepdims=True)
        dx_ref[...] = rs * (dxhat - xhat * mean) + dn_ref[...]

    row = lambda w: pl.BlockSpec((tm, w), lambda i: (i, 0))
    return pl.pallas_call(
        body, name="dx_norm", grid=(seq // tm,),
        in_specs=[row(IN_COLS), pl.BlockSpec((N_DEV, None, d, COLS_PER_DEV), lambda i: (0, layer, 0, 0)), row(d),
                  pl.BlockSpec((1, d), lambda i: (0, 0)), row(d)],
        out_specs=[row(d), pl.BlockSpec((8, d), lambda i: (0, 0))],
        out_shape=[jax.ShapeDtypeStruct((seq, d), F32), jax.ShapeDtypeStruct((8, d), F32)],
        compiler_params=_params(48),
    )(dproj, w_all, x, g, dx_next)


def _exchange_grads(dwin, dwout, small):
    flips = _flips()

    def body(dwin_ref, dwout_ref, small_ref, rin_ref, rout_ref, rsmall_ref, send_sems, recv_sems, local_sems):
        x, y, c, me = _place()
        local = [pltpu.make_async_copy(dwin_ref.at[me], rin_ref.at[me], local_sems.at[0]),
                 pltpu.make_async_copy(dwout_ref.at[me], rout_ref.at[me], local_sems.at[1]),
                 pltpu.make_async_copy(small_ref, rsmall_ref.at[me], local_sems.at[2])]
        for cp in local:
            cp.start()
        copies = []
        for k, flip in enumerate(flips):
            px, py, pc = _peer(x, y, c, flip)
            to = 4 * px + 2 * py + pc
            for a, (src, dst) in enumerate(((dwin_ref.at[to], rin_ref), (dwout_ref.at[to], rout_ref), (small_ref, rsmall_ref))):
                cp = pltpu.make_async_remote_copy(
                    src_ref=src, dst_ref=dst.at[me], send_sem=send_sems.at[3 * k + a], recv_sem=recv_sems.at[3 * k + a],
                    device_id=(px, py, pc), device_id_type=MESH_ID)
                cp.start()
                copies.append(cp)
        for cp in copies:
            cp.wait()
        for cp in local:
            cp.wait()

    any_spec = pl.BlockSpec(memory_space=pl.ANY)
    vmem = pl.BlockSpec(memory_space=pltpu.VMEM)
    n = 3 * len(flips)
    return pl.pallas_call(
        body, name="exchange_grads",
        in_specs=[any_spec, any_spec, vmem], out_specs=[any_spec, any_spec, vmem],
        out_shape=[jax.ShapeDtypeStruct(dwin.shape, F32), jax.ShapeDtypeStruct(dwout.shape, F32),
                   jax.ShapeDtypeStruct((N_DEV,) + small.shape, F32)],
        scratch_shapes=[pltpu.SemaphoreType.DMA((n,)), pltpu.SemaphoreType.DMA((n,)), pltpu.SemaphoreType.DMA((3,))],
    )(dwin, dwout, small)


def _adamw_math(g, w, m, v):
    m = ADAM_B1 * m + (1.0 - ADAM_B1) * g
    v = ADAM_B2 * v + (1.0 - ADAM_B2) * (g * g)
    m_hat = m / (1.0 - ADAM_B1 ** ADAM_STEP)
    v_hat = v / (1.0 - ADAM_B2 ** ADAM_STEP)
    delta = -ADAM_LR * (m_hat / (jnp.sqrt(v_hat) + ADAM_EPS) + ADAM_WD * w)
    return delta, m, v


def _adamw(parts, w, m, v, name):
    nl, r, c = w.shape
    tr = min(r, (256 * 512) // c)

    def body(p_ref, w_ref, m_ref, v_ref, g_ref, d_ref, nm_ref, nv_ref):
        g = p_ref[0]
        for s in range(1, N_DEV):
            g = g + p_ref[s]
        g_ref[...] = g
        d_ref[...], nm_ref[...], nv_ref[...] = _adamw_math(g, w_ref[...], m_ref[...], v_ref[...])

    blk = pl.BlockSpec((None, tr, c), lambda l, i: (l, i, 0))
    return pl.pallas_call(
        body, name=name, grid=(nl, r // tr),
        in_specs=[pl.BlockSpec((N_DEV, None, tr, c), lambda l, i: (0, l, i, 0)), blk, blk, blk],
        out_specs=[blk] * 4, out_shape=[jax.ShapeDtypeStruct(w.shape, F32)] * 4,
        compiler_params=_params(32),
    )(parts, w, m, v)


def _adamw_small(parts, w, m, v):
    def body(p_ref, w_ref, m_ref, v_ref, g_ref, d_ref, nm_ref, nv_ref):
        g = p_ref[0]
        for s in range(1, N_DEV):
            g = g + p_ref[s]
        g_ref[...] = g
        d_ref[...], nm_ref[...], nv_ref[...] = _adamw_math(g, w_ref[...], m_ref[...], v_ref[...])

    vmem = pl.BlockSpec(memory_space=pltpu.VMEM)
    return pl.pallas_call(
        body, name="adamw_small", in_specs=[vmem] * 4, out_specs=[vmem] * 4,
        out_shape=[jax.ShapeDtypeStruct(w.shape, F32)] * 4,
    )(parts, w, m, v)


def _pack_small(a, b, c):
    pad = jnp.zeros((a.shape[0], SMALL_W - a.shape[1] - b.shape[1] - c.shape[1]), F32)
    return jnp.concatenate([a, b, c, pad], axis=1)


def _unpack_small(t, d):
    return t[:, :d], t[:, d:d + HEAD_DIM], t[:, d + HEAD_DIM:d + 2 * HEAD_DIM]


def _local_step(x0, target, norm_g, q_norm_g, k_norm_g, win_all, wout_all):
    seq, d = x0.shape
    depth = norm_g.shape[0]
    tri_a, tri_b, bd = _tri_constants()
    tables = _rope_tables(seq)
    rep = (2 * LANES) // HEAD_DIM

    saved = []
    xl = x0
    for layer in range(depth):
        gains = jnp.stack([jnp.tile(q_norm_g[layer], rep), jnp.tile(k_norm_g[layer], rep)])[:, None, :]
        proj_bf, gates, qk_raw, h = _norm_proj(xl, norm_g[layer][None, :], win_all, layer)
        qkn = _qk_prep(qk_raw, gains, tables, bd)
        oa = _sb_fwd(proj_bf, tri_a)
        o_p, ld_p = zip(*[_dil_fwd(qkn, qk_raw, r) for _, r in DIL_PATTERNS])
        x_next, cat, od, lse = _out_proj(xl, oa, o_p, ld_p, gates, wout_all, layer)
        saved.append((xl, gains, proj_bf, gates, qk_raw, h, qkn, oa, cat, od, lse))
        xl = x_next

    loss_part, dx = _loss_head(xl, target)

    dwin = jnp.zeros((N_DEV, depth, d, COLS_PER_DEV), F32)
    dwout = jnp.zeros((N_DEV, depth, ROWS_PER_DEV, d), F32)
    g_norm, g_q, g_k = [None] * depth, [None] * depth, [None] * depth
    for layer in reversed(range(depth)):
        xl, gains, proj_bf, gates, qk_raw, h, qkn, oa, cat, od, lse = saved[layer]
        d_oa, d_od, delta, d_gates, dwout = _out_proj_bwd(dx, wout_all, layer, cat, gates, oa, od, bd, dwout)
        d_qa, d_ka, d_va = _sb_bwd(proj_bf, d_oa, oa, tri_a, tri_b)
        dq_p, dk_p, dv_p = zip(*[_dil_bwd(qkn, qk_raw, d_od, lse, delta, r) for _, r in DIL_PATTERNS])
        dproj, gqk = _assemble_dproj(d_qa, d_ka, d_va, d_gates, dq_p, dk_p, dv_p, qk_raw, gains, tables, bd)
        dwin = _dw_in(h, dproj, layer, dwin)
        dx, gn = _dx_norm(dproj, win_all, layer, xl, norm_g[layer][None, :], dx)
        g_norm[layer] = jnp.sum(gn, axis=0)
        gqk = jnp.sum(gqk, axis=1).reshape(2, rep, HEAD_DIM).sum(axis=1)
        g_q[layer], g_k[layer] = gqk[0], gqk[1]

    small = _pack_small(jnp.stack(g_norm), jnp.stack(g_q), jnp.stack(g_k))
    return loss_part, dx, dwin, dwout, small


def kernel(x, norm_g, w_in, q_norm_g, k_norm_g, w_out, loss_target, m_norm_g, m_w_in, m_q_norm_g, m_k_norm_g, m_w_out,
           v_norm_g, v_w_in, v_q_norm_g, v_k_norm_g, v_w_out):
    d = w_in.shape[1]
    seq = x.shape[1]
    win_all, wout_all = _all_gather_weights(_cast_bf16(w_in, "cast_w_in"), _cast_bf16(w_out, "cast_w_out"))
    loss_part, dx, dwin, dwout, small = _local_step(
        x.reshape(seq, d), loss_target.reshape(seq, d), norm_g, q_norm_g, k_norm_g, win_all, wout_all)
    loss = lax.psum(jnp.sum(loss_part), ("x", "y", "c"))
    rin, rout, rsmall = _exchange_grads(dwin, dwout, small)

    g_in, d_in, nm_in, nv_in = _adamw(rin, w_in, m_w_in, v_w_in, "adamw_w_in")
    g_out, d_out, nm_out, nv_out = _adamw(rout, w_out, m_w_out, v_w_out, "adamw_w_out")
    small_out = _adamw_small(rsmall, _pack_small(norm_g, q_norm_g, k_norm_g), _pack_small(m_norm_g, m_q_norm_g, m_k_norm_g),
                             _pack_small(v_norm_g, v_q_norm_g, v_k_norm_g))
    (g_n, g_qn, g_kn), (d_n, d_qn, d_kn), (nm_n, nm_qn, nm_kn), (nv_n, nv_qn, nv_kn) = (_unpack_small(t, d) for t in small_out)

    return (loss, dx.reshape(x.shape), g_n, g_in, g_qn, g_kn, g_out, d_n, d_in, d_qn, d_kn, d_out,
            nm_n, nm_in, nm_qn, nm_kn, nm_out, nv_n, nv_in, nv_qn, nv_kn, nv_out)
```

```python
import functools
import math

import jax
import jax.numpy as jnp
from jax import lax
from jax.experimental import pallas as pl
from jax.experimental.pallas import tpu as pltpu

F32 = jnp.float32
BF16 = jnp.bfloat16

EPS = 1e-6
HEAD_DIM = 64
BLOCK = 128
LANES = 128
W_SB = 256
W_DIL = 768
MIX = W_SB + W_DIL
IN_COLS = 4 * W_SB + 4 * W_DIL
N_DEV = 8
COLS_PER_DEV = IN_COLS // N_DEV
ROWS_PER_DEV = MIX // N_DEV
QK_SCALE = 1.0 / math.sqrt(HEAD_DIM)
DIL_PATTERNS = ((128, 1), (512, 4), (2048, 16))
DIL_TILE = 2048
DIL_GROUP = 4
ROPE_THETA = 500000.0
ROPE_DIM = HEAD_DIM // 4
ROPE_HALF = ROPE_DIM // 2
DEAD_LOG = -110.0
SB_PEEL = 3
NEG_BIG = -1e30

ADAM_LR = 0.001
ADAM_B1 = 0.9
ADAM_B2 = 0.999
ADAM_EPS = 1e-08
ADAM_WD = 0.01
ADAM_STEP = 10

SMALL_W = 1280
MESH_ID = pl.DeviceIdType.MESH
MIB = 1 << 20


def _params(vmem_mib):
    return pltpu.CompilerParams(vmem_limit_bytes=vmem_mib * MIB)


def _dot(a, b):
    return jnp.dot(a, b, preferred_element_type=F32)


def _dot_nt(a, b):
    return lax.dot_general(a, b, (((1,), (1,)), ((), ())), preferred_element_type=F32)


def _dot_tn(a, b):
    return lax.dot_general(a, b, (((0,), (0,)), ((), ())), preferred_element_type=F32)


def _dot_exact(x, m01):
    hi = x.astype(BF16)
    r1 = x - hi.astype(F32)
    mid = r1.astype(BF16)
    lo = (r1 - mid.astype(F32)).astype(BF16)
    return _dot(hi, m01) + _dot(mid, m01) + _dot(lo, m01)


def _lane_lo(shape):
    return lax.broadcasted_iota(jnp.int32, shape, 1) < HEAD_DIM


def _col_of(b, lo, h):
    keep = lo if h == 0 else jnp.logical_not(lo)
    return jnp.max(jnp.where(keep, b, NEG_BIG), axis=1, keepdims=True)


def _tri_constants():
    j = jnp.arange(BLOCK)
    ones = jnp.ones((BLOCK, BLOCK), F32)
    excl = (j[:, None] > j[None, :]).astype(F32)
    incl = (j[:, None] >= j[None, :]).astype(F32)
    tri_a = jnp.concatenate([excl, ones], axis=1).astype(BF16)
    tri_b = jnp.concatenate([incl, ones], axis=1).astype(BF16)
    d = jnp.arange(2 * LANES)
    bd = (d[:, None] // HEAD_DIM == d[None, :] // HEAD_DIM).astype(BF16)
    return tri_a, tri_b, bd


def _rope_tables(seq):
    inv_freq = 1.0 / (ROPE_THETA ** (jnp.arange(ROPE_HALF, dtype=F32) * 2.0 / ROPE_DIM))
    ang = jnp.arange(seq).astype(F32)[:, None] * inv_freq[None, :]
    cos, sin = jnp.cos(ang), jnp.sin(ang)
    one = jnp.ones((seq, HEAD_DIM - ROPE_DIM), F32)
    zero8 = jnp.zeros((seq, ROPE_HALF), F32)
    zero_rest = jnp.zeros((seq, HEAD_DIM - ROPE_DIM), F32)
    c = jnp.concatenate([cos, cos, one], axis=1)
    s_next = jnp.concatenate([-sin, zero8, zero_rest], axis=1)
    s_prev = jnp.concatenate([zero8, sin, zero_rest], axis=1)
    rep = (2 * LANES) // HEAD_DIM
    return tuple(jnp.tile(t, (1, rep)) for t in (c, s_next, s_prev))


def _roll_lanes(x, shift):
    return jnp.concatenate([pltpu.roll(x[:, :LANES], shift, 1), pltpu.roll(x[:, LANES:], shift, 1)], axis=1)


def _rope(x, c, s_next, s_prev):
    return x * c + _roll_lanes(x, LANES - ROPE_HALF) * s_next + _roll_lanes(x, ROPE_HALF) * s_prev


def _rope_t(dy, c, s_next, s_prev):
    return dy * c + _roll_lanes(dy * s_next, ROPE_HALF) + _roll_lanes(dy * s_prev, LANES - ROPE_HALF)


def _cast_bf16(w, name):
    nl, r, c = w.shape

    def body(w_ref, o_ref):
        o_ref[...] = w_ref[...].astype(BF16)

    return pl.pallas_call(
        body, name=name, grid=(nl,),
        in_specs=[pl.BlockSpec((None, r, c), lambda l: (l, 0, 0))],
        out_specs=pl.BlockSpec((None, r, c), lambda l: (l, 0, 0)),
        out_shape=jax.ShapeDtypeStruct(w.shape, BF16),
        compiler_params=_params(24),
    )(w)


def _flips():
    return [(dx, dy, dc) for dx in (0, 1) for dy in (0, 1) for dc in (0, 1) if (dx, dy, dc) != (0, 0, 0)]


def _place():
    x, y, c = lax.axis_index("x"), lax.axis_index("y"), lax.axis_index("c")
    return x, y, c, 4 * x + 2 * y + c


def _peer(x, y, c, flip):
    dx, dy, dc = flip
    return (1 - x if dx else x, 1 - y if dy else y, 1 - c if dc else c)


def _all_gather_weights(win_bf, wout_bf):
    flips = _flips()

    def body(win_ref, wout_ref, oin_ref, oout_ref, send_sems, recv_sems, local_sems):
        x, y, c, me = _place()
        local = [pltpu.make_async_copy(win_ref, oin_ref.at[me], local_sems.at[0]),
                 pltpu.make_async_copy(wout_ref, oout_ref.at[me], local_sems.at[1])]
        for cp in local:
            cp.start()
        copies = []
        for k, flip in enumerate(flips):
            for a, (src, dst) in enumerate(((win_ref, oin_ref), (wout_ref, oout_ref))):
                cp = pltpu.make_async_remote_copy(
                    src_ref=src, dst_ref=dst.at[me], send_sem=send_sems.at[2 * k + a], recv_sem=recv_sems.at[2 * k + a],
                    device_id=_peer(x, y, c, flip), device_id_type=MESH_ID)
                cp.start()
                copies.append(cp)
        for cp in copies:
            cp.wait()
        for cp in local:
            cp.wait()

    any_spec = pl.BlockSpec(memory_space=pl.ANY)
    n = 2 * len(flips)
    return pl.pallas_call(
        body, name="all_gather_weights",
        in_specs=[any_spec, any_spec], out_specs=[any_spec, any_spec],
        out_shape=[jax.ShapeDtypeStruct((N_DEV,) + win_bf.shape, BF16), jax.ShapeDtypeStruct((N_DEV,) + wout_bf.shape, BF16)],
        scratch_shapes=[pltpu.SemaphoreType.DMA((n,)), pltpu.SemaphoreType.DMA((n,)), pltpu.SemaphoreType.DMA((2,))],
    )(win_bf, wout_bf)


_F32_ROUTES = {1: ((256, 256, 0, 0),), 2: ((0, 512, 1, 0),), 3: ((0, 512, 1, 512),), 4: ((0, 512, 1, 1024),),
               5: ((0, 512, 1, 1536),), 6: ((0, 256, 1, 2048), (256, 256, 0, 256)), 7: ((0, 512, 0, 512),)}


def _norm_proj(x, g, w_all, layer):
    seq, d = x.shape
    tm = min(256, seq)

    def body(x_ref, g_ref, w_ref, pbf_ref, gates_ref, qk_ref, h_ref):
        xf = x_ref[...]
        rs = lax.rsqrt(jnp.mean(xf * xf, axis=-1, keepdims=True) + EPS)
        h = (xf * rs * g_ref[...]).astype(BF16)
        h_ref[...] = h
        targets = (gates_ref, qk_ref)
        for n in range(N_DEV):
            acc = _dot(h, w_ref[n])
            pbf_ref[:, n * COLS_PER_DEV:(n + 1) * COLS_PER_DEV] = acc.astype(BF16)
            for lo, width, tgt, dst in _F32_ROUTES.get(n, ()):
                targets[tgt][:, dst:dst + width] = acc[:, lo:lo + width]

    row = lambda w: pl.BlockSpec((tm, w), lambda i: (i, 0))
    return pl.pallas_call(
        body, name="norm_proj", grid=(seq // tm,),
        in_specs=[row(d), pl.BlockSpec((1, d), lambda i: (0, 0)),
                  pl.BlockSpec((N_DEV, None, d, COLS_PER_DEV), lambda i: (0, layer, 0, 0))],
        out_specs=[row(IN_COLS), row(MIX), row(3 * W_DIL), row(d)],
        out_shape=[jax.ShapeDtypeStruct((seq, IN_COLS), BF16), jax.ShapeDtypeStruct((seq, MIX), F32),
                   jax.ShapeDtypeStruct((seq, 3 * W_DIL), F32), jax.ShapeDtypeStruct((seq, d), BF16)],
        compiler_params=_params(48),
    )(x, g, w_all)


def _head_sums(v, bd):
    hi = v.astype(BF16)
    lo = (v - hi.astype(F32)).astype(BF16)
    return _dot(hi, bd) + _dot(lo, bd)


def _qk_prep(qk_raw, gains, tables, bd):
    seq = qk_raw.shape[0]
    tm = min(1024, seq)
    chunks = W_DIL // (2 * LANES)

    def body(x_ref, g_ref, c_ref, sn_ref, sp_ref, bd_ref, o_ref):
        j = pl.program_id(1)
        x = x_ref[...]
        rs = lax.rsqrt(_head_sums(x * x, bd_ref[...]) * (1.0 / HEAD_DIM) + EPS)
        y = _rope(x * rs * g_ref[...], c_ref[...], sn_ref[...], sp_ref[...])
        o_ref[...] = y * jnp.where(j < chunks, QK_SCALE, 1.0)

    tab = pl.BlockSpec((tm, 2 * LANES), lambda i, j: (i, 0))
    return pl.pallas_call(
        body, name="qk_prep", grid=(seq // tm, 2 * chunks),
        in_specs=[pl.BlockSpec((tm, 2 * LANES), lambda i, j: (i, j)),
                  pl.BlockSpec((None, 1, 2 * LANES), lambda i, j: (j // chunks, 0, 0)),
                  tab, tab, tab, pl.BlockSpec((2 * LANES, 2 * LANES), lambda i, j: (0, 0))],
        out_specs=pl.BlockSpec((tm, 2 * LANES), lambda i, j: (i, j)),
        out_shape=jax.ShapeDtypeStruct((seq, 2 * W_DIL), F32),
        compiler_params=_params(32),
    )(qk_raw, gains, *tables, bd)


def _stack_heads(x, lo):
    return jnp.concatenate([jnp.where(lo, x, 0.0), jnp.where(lo, 0.0, x)], axis=0)


def _unstack_heads(y, lo):
    return jnp.where(lo, y[:BLOCK], y[BLOCK:])


def _stacked_causal():
    row = lax.broadcasted_iota(jnp.int32, (2 * BLOCK, BLOCK), 0) & (BLOCK - 1)
    return lax.broadcasted_iota(jnp.int32, (2 * BLOCK, BLOCK), 1) < row


def _keep(x, *conds):
    for cond in conds:
        if cond is not None:
            x = jnp.where(cond, x, 0.0)
    return x


def _sb_weights(qs, kbs, tri, r, masks, lives):
    zs = [_dot_nt(qs, kb) for kb in kbs]
    lss = [jnp.minimum(z, 0.0) - jnp.log1p(jnp.exp(-jnp.abs(z))) for z in zs]
    cts = [_dot_exact(_keep(ls - z, mask, live), tri) for ls, z, mask, live in zip(lss, zs, masks, lives)]
    weights = []
    for ls, ct, mask, live in zip(lss, cts, masks, lives):
        weights.append(_keep(jnp.exp(ls + ct[:, :BLOCK] + r), mask, live))
        r = r + ct[:, BLOCK:]
    return lss, weights, r


def _sb_fwd(proj_bf, tri_a):
    seq = proj_bf.shape[0]
    pairs = W_SB // LANES

    def body(q_ref, k_ref, v_ref, tri_ref, o_ref, r_ref, acc_ref):
        i = pl.program_id(1)
        lo = _lane_lo((BLOCK, LANES))
        qs = _stack_heads(q_ref[...].astype(F32) * QK_SCALE, lo).astype(BF16)
        causal = _stacked_causal()
        tri = tri_ref[...]

        def blocks(js, r, masks, lives):
            rows = [pl.ds(pl.multiple_of(j * BLOCK, BLOCK), BLOCK) for j in js]
            _, weights, r = _sb_weights(qs, [k_ref[rw, :] for rw in rows], tri, r, masks, lives)
            out = 0.0
            for a, rw in zip(weights, rows):
                a_hi = a.astype(BF16)
                a_lo = (a - a_hi.astype(F32)).astype(BF16)
                vb = v_ref[rw, :]
                out = out + _dot(a_hi, vb) + _dot(a_lo, vb)
            return out, r

        acc_ref[...], r_ref[...] = blocks(
            [jnp.maximum(i - k, 0) for k in range(SB_PEEL)], jnp.zeros((2 * BLOCK, LANES), F32),
            [causal] + [None] * (SB_PEEL - 1), [None] + [i >= k for k in range(1, SB_PEEL)])

        def alive():
            return (jnp.max(r_ref[...]) > DEAD_LOG).astype(jnp.int32)

        def step(carry):
            out, r_ref[...] = blocks([carry[0]], r_ref[...], [None], [None])
            acc_ref[...] += out
            return carry[0] - 1, alive()

        lax.while_loop(lambda carry: jnp.logical_and(carry[0] >= 0, carry[1] > 0), step, (i - SB_PEEL, alive()))
        o_ref[...] = _unstack_heads(acc_ref[...], lo)

    return pl.pallas_call(
        body, name="sb_fwd", grid=(pairs, seq // BLOCK),
        in_specs=[pl.BlockSpec((BLOCK, LANES), lambda p, i: (i, p)),
                  pl.BlockSpec((seq, LANES), lambda p, i: (0, pairs + p)),
                  pl.BlockSpec((seq, LANES), lambda p, i: (0, 2 * pairs + p)),
                  pl.BlockSpec((BLOCK, 2 * BLOCK), lambda p, i: (0, 0))],
        out_specs=pl.BlockSpec((BLOCK, LANES), lambda p, i: (i, p)),
        out_shape=jax.ShapeDtypeStruct((seq, W_SB), F32),
        scratch_shapes=[pltpu.VMEM((2 * BLOCK, LANES), F32), pltpu.VMEM((2 * BLOCK, LANES), F32)],
        compiler_params=_params(40),
    )(proj_bf, proj_bf, proj_bf, tri_a)


def _class_rows(c, first, count, r):
    start = c + first * r
    return pl.ds(start, count) if r == 1 else pl.ds(start, count, stride=r)


def _dil_tiling(seq, r):
    tile = min(DIL_TILE, seq)
    edge = BLOCK * r
    return tile, edge, tile // edge


def _band_mask(n_rows, n_keys):
    row = lax.broadcasted_iota(jnp.int32, (n_rows, n_keys), 0) & (BLOCK - 1)
    col = lax.broadcasted_iota(jnp.int32, (n_rows, n_keys), 1)
    return jnp.logical_and(col >= row, col <= row + BLOCK), col


def _for_each_group(r, per, run):
    g = DIL_GROUP
    assert per == 1 and r % g == 0 or per % g == 0
    loop = lambda lo, hi, fn: lax.fori_loop(lo, hi, lambda t, carry: (fn(t), carry)[1], 0)
    if per == 1:
        loop(0, r // g, lambda t: run([(t * g + u, 0, True) for u in range(g)]))
        return

    def one_class(c):
        run([(c, 0, True)] + [(c, a, False) for a in range(1, g)])
        if per > g:
            loop(1, per // g, lambda t: run([(c, t * g + u, False) for u in range(g)]))

    if r == 1:
        one_class(0)
    else:
        loop(0, r, one_class)


def _dil_keys(blocks, r, cur_ref, before_ref):
    out = []
    for c, a, first in blocks:
        if first:
            rows = _class_rows(c, 0, BLOCK, r)
            out.append(jnp.concatenate([before_ref[rows, :], cur_ref[rows, :]], axis=0).astype(BF16))
        else:
            out.append(cur_ref[_class_rows(c, (a - 1) * BLOCK, 2 * BLOCK, r), :].astype(BF16))
    return out


def _dil_fwd(qkn, qkv_raw, r):
    seq = qkn.shape[0]
    tile, edge, per = _dil_tiling(seq, r)
    pairs = W_DIL // LANES

    def body(q_ref, kc_ref, kp_ref, vc_ref, vp_ref, o_ref, ld_ref):
        n = pl.program_id(1)
        lo = _lane_lo((BLOCK, LANES))
        band, col = _band_mask(2 * BLOCK, 2 * BLOCK)
        first_band = jnp.logical_and(band, jnp.logical_or(col >= BLOCK, n > 0))

        def run(blocks):
            rows = [_class_rows(c, a * BLOCK, BLOCK, r) for c, a, _ in blocks]
            keys = _dil_keys(blocks, r, kc_ref, kp_ref)
            vals = _dil_keys(blocks, r, vc_ref, vp_ref)
            scores = [_dot_nt(_stack_heads(q_ref[rw, :], lo).astype(BF16), kb) for rw, kb in zip(rows, keys)]
            probs, sums, lds = [], [], []
            for s, (_, _, first) in zip(scores, blocks):
                s = jnp.where(first_band if first else band, s, NEG_BIG)
                m = jnp.max(s, axis=1, keepdims=True)
                p = jnp.exp(s - m)
                l = jnp.sum(p, axis=1, keepdims=True)
                probs.append(p.astype(BF16))
                sums.append(l)
                lds.append(m + jnp.log(l))
            outs = [_dot(p, vb) / l for p, vb, l in zip(probs, vals, sums)]
            for rw, o, ld in zip(rows, outs, lds):
                o_ref[rw, :] = _unstack_heads(o, lo)
                ld_ref[rw, :] = _unstack_heads(jnp.broadcast_to(ld, (2 * BLOCK, LANES)), lo)

        _for_each_group(r, per, run)

    per_edge = tile // edge
    cur = lambda off: pl.BlockSpec((tile, LANES), lambda p, n: (n, off + p))
    before = lambda off: pl.BlockSpec((edge, LANES), lambda p, n: (jnp.maximum(n * per_edge - 1, 0), off + p))
    return pl.pallas_call(
        body, name=f"dil_fwd_r{r}", grid=(pairs, seq // tile),
        in_specs=[cur(0), cur(pairs), before(pairs), cur(2 * pairs), before(2 * pairs)],
        out_specs=[cur(0), cur(0)],
        out_shape=[jax.ShapeDtypeStruct((seq, W_DIL), F32), jax.ShapeDtypeStruct((seq, W_DIL), F32)],
        compiler_params=_params(32),
    )(qkn, qkn, qkn, qkv_raw, qkv_raw)


def _silu_parts(g):
    sig = jax.nn.sigmoid(g)
    return g * sig, sig * (1.0 + g * (1.0 - sig))


def _out_proj(x, oa, o_p, ld_p, gates, wout_all, layer):
    seq, d = x.shape
    tm = min(256, seq)

    def body(x_ref, oa_ref, o0, o1, o2, l0, l1, l2, g_ref, w_ref, xn_ref, cat_ref, od_ref, lse_ref):
        lds = (l0[...], l1[...], l2[...])
        m = jnp.maximum(jnp.maximum(lds[0], lds[1]), lds[2])
        es = [jnp.exp(v - m) for v in lds]
        tot = es[0] + es[1] + es[2]
        lse_ref[...] = m + jnp.log(tot)
        inv = 1.0 / tot
        od = (es[0] * inv) * o0[...] + (es[1] * inv) * o1[...] + (es[2] * inv) * o2[...]
        od_ref[...] = od
        silu, _ = _silu_parts(g_ref[...])
        cat_ref[:, :W_SB] = (oa_ref[...] * silu[:, :W_SB]).astype(BF16)
        cat_ref[:, W_SB:] = (od * silu[:, W_SB:]).astype(BF16)
        y = x_ref[...]
        for b in range(N_DEV):
            y = y + _dot(cat_ref[:, b * ROWS_PER_DEV:(b + 1) * ROWS_PER_DEV], w_ref[b])
        xn_ref[...] = y

    row = lambda w: pl.BlockSpec((tm, w), lambda i: (i, 0))
    return pl.pallas_call(
        body, name="out_proj", grid=(seq // tm,),
        in_specs=[row(d), row(W_SB)] + [row(W_DIL)] * 6 + [row(MIX),
                  pl.BlockSpec((N_DEV, None, ROWS_PER_DEV, d), lambda i: (0, layer, 0, 0))],
        out_specs=[row(d), row(MIX), row(W_DIL), row(W_DIL)],
        out_shape=[jax.ShapeDtypeStruct((seq, d), F32), jax.ShapeDtypeStruct((seq, MIX), BF16),
                   jax.ShapeDtypeStruct((seq, W_DIL), F32), jax.ShapeDtypeStruct((seq, W_DIL), F32)],
        compiler_params=_params(48),
    )(x, oa, *o_p, *ld_p, gates, wout_all)


def _loss_head(y, target):
    seq, d = y.shape
    tm = min(512, seq)

    def body(y_ref, t_ref, part_ref, dy_ref):
        @pl.when(pl.program_id(0) == 0)
        def _():
            part_ref[...] = jnp.zeros_like(part_ref)

        diff = y_ref[...] - t_ref[...]
        dy_ref[...] = diff * (1.0 / d)
        part_ref[...] += jnp.sum((diff * diff).reshape(tm // 8, 8, d), axis=0) * (0.5 / d)

    row = pl.BlockSpec((tm, d), lambda i: (i, 0))
    return pl.pallas_call(
        body, name="loss_head", grid=(seq // tm,),
        in_specs=[row, row], out_specs=[pl.BlockSpec((8, d), lambda i: (0, 0)), row],
        out_shape=[jax.ShapeDtypeStruct((8, d), F32), jax.ShapeDtypeStruct((seq, d), F32)],
        compiler_params=_params(32),
    )(y, target)


def _out_proj_bwd(dy, wout_all, layer, cat, gates, oa, od, bd, dwout_acc):
    seq, d = dy.shape
    tm = min(256, seq)

    def body(dy_ref, w_ref, cat_ref, g_ref, oa_ref, od_ref, bd_ref, acc_in, doa_ref, dod_ref, delta_ref, dg_ref, dw_ref, dcat):
        del acc_in

        @pl.when(pl.program_id(0) == 0)
        def _():
            dw_ref[...] = jnp.zeros_like(dw_ref)

        dyb = dy_ref[...].astype(BF16)
        dw = _dot_tn(cat_ref[...], dyb)
        for b in range(N_DEV):
            dw_ref[b] += dw[b * ROWS_PER_DEV:(b + 1) * ROWS_PER_DEV, :]
            dcat[:, b * ROWS_PER_DEV:(b + 1) * ROWS_PER_DEV] = _dot_nt(dyb, w_ref[b])
        silu, dsilu = _silu_parts(g_ref[...])
        dc = dcat[...]
        dmix = dc * silu
        oa_v, od_v = oa_ref[...], od_ref[...]
        dg_ref[:, :W_SB] = dc[:, :W_SB] * oa_v * dsilu[:, :W_SB]
        dg_ref[:, W_SB:] = dc[:, W_SB:] * od_v * dsilu[:, W_SB:]
        doa_ref[...] = dmix[:, :W_SB]
        dod = dmix[:, W_SB:]
        dod_ref[...] = dod
        prod = dod * od_v
        for k in range(W_DIL // (2 * LANES)):
            sl = slice(k * 2 * LANES, (k + 1) * 2 * LANES)
            delta_ref[:, sl] = _head_sums(prod[:, sl], bd_ref[...])

    row = lambda w: pl.BlockSpec((tm, w), lambda i: (i, 0))
    slab = pl.BlockSpec((N_DEV, None, ROWS_PER_DEV, d), lambda i: (0, layer, 0, 0))
    return pl.pallas_call(
        body, name="out_proj_bwd", grid=(seq // tm,),
        in_specs=[row(d), slab, row(MIX), row(MIX), row(W_SB), row(W_DIL),
                  pl.BlockSpec((2 * LANES, 2 * LANES), lambda i: (0, 0)), pl.BlockSpec(memory_space=pl.ANY)],
        out_specs=[row(W_SB), row(W_DIL), row(W_DIL), row(MIX), slab],
        out_shape=[jax.ShapeDtypeStruct((seq, W_SB), F32), jax.ShapeDtypeStruct((seq, W_DIL), F32),
                   jax.ShapeDtypeStruct((seq, W_DIL), F32), jax.ShapeDtypeStruct((seq, MIX), F32),
                   jax.ShapeDtypeStruct(dwout_acc.shape, F32)],
        scratch_shapes=[pltpu.VMEM((tm, MIX), F32)],
        input_output_aliases={7: 4},
        compiler_params=_params(48),
    )(dy, wout_all, cat, gates, oa, od, bd, dwout_acc)


def _sb_bwd(proj_bf, d_oa, oa, tri_a, tri_b):
    seq = proj_bf.shape[0]
    pairs = W_SB // LANES
    nq = seq // BLOCK

    def body(q_ref, k_ref, v_ref, do_ref, o_ref, tria_ref, trib_ref, dq_ref, dk_hbm, dv_hbm,
             r_ref, sfx_ref, dtot_ref, dq_acc, dk_acc, dv_acc, sems):
        p, i = pl.program_id(0), pl.program_id(1)

        @pl.when(i == 0)
        def _():
            dk_acc[...] = jnp.zeros_like(dk_acc)
            dv_acc[...] = jnp.zeros_like(dv_acc)

        lo = _lane_lo((BLOCK, LANES))
        qs = _stack_heads(q_ref[...].astype(F32) * QK_SCALE, lo).astype(BF16)
        dos = _stack_heads(do_ref[...], lo).astype(BF16)
        o2 = o_ref[...]
        tri, trib = tria_ref[...], trib_ref[...]
        dtot_ref[...] = _dot_exact(dos.astype(F32) * jnp.concatenate([o2, o2], axis=0), tri[:, BLOCK:])
        causal = _stacked_causal()

        def blocks(js, r, sfx, masks, lives):
            rows = [pl.ds(pl.multiple_of(j * BLOCK, BLOCK), BLOCK) for j in js]
            kbs = [k_ref[rw, :] for rw in rows]
            dovs = [_dot_nt(dos, v_ref[rw, :]) for rw in rows]
            lss, weights, r = _sb_weights(qs, kbs, tri, r, masks, lives)
            pws = [a * dov for a, dov in zip(weights, dovs)]
            cps = [_dot_exact(pw, trib) for pw in pws]
            dzs = []
            for ls, pw, cp, mask, live in zip(lss, pws, cps, masks, lives):
                beta = jnp.exp(ls)
                before = dtot_ref[...] - sfx - cp[:, :BLOCK]
                dzs.append(_keep(pw * (1.0 - beta) - before * beta, mask, live).astype(BF16))
                sfx = sfx + cp[:, BLOCK:]
            dq = 0.0
            for dzb, kb in zip(dzs, kbs):
                dq = dq + _dot(dzb, kb)
            for dzb, a, rw in zip(dzs, weights, rows):
                dk_acc[rw, :] += _dot_tn(dzb, qs)
                dv_acc[rw, :] += _dot_tn(a.astype(BF16), dos)
            return dq, r, sfx

        zero = jnp.zeros((2 * BLOCK, LANES), F32)
        dq_acc[...], r_ref[...], sfx_ref[...] = blocks(
            [jnp.maximum(i - k, 0) for k in range(SB_PEEL)], zero, zero,
            [causal] + [None] * (SB_PEEL - 1), [None] + [i >= k for k in range(1, SB_PEEL)])

        def alive():
            return (jnp.max(r_ref[...]) > DEAD_LOG).astype(jnp.int32)

        def step(carry):
            dq, r_ref[...], sfx_ref[...] = blocks([carry[0]], r_ref[...], sfx_ref[...], [None], [None])
            dq_acc[...] += dq
            return carry[0] - 1, alive()

        lax.while_loop(lambda carry: jnp.logical_and(carry[0] >= 0, carry[1] > 0), step, (i - SB_PEEL, alive()))
        dq_ref[...] = _unstack_heads(dq_acc[...], lo) * QK_SCALE

        @pl.when(i == nq - 1)
        def _():
            outs = [pltpu.make_async_copy(dk_acc, dk_hbm.at[p], sems.at[0]),
                    pltpu.make_async_copy(dv_acc, dv_hbm.at[p], sems.at[1])]
            for cp in outs:
                cp.start()
            for cp in outs:
                cp.wait()

    blk = pl.BlockSpec((BLOCK, LANES), lambda p, i: (i, p))
    const = pl.BlockSpec((BLOCK, 2 * BLOCK), lambda p, i: (0, 0))
    any_spec = pl.BlockSpec(memory_space=pl.ANY)
    acc = pltpu.VMEM((2 * BLOCK, LANES), F32)
    return pl.pallas_call(
        body, name="sb_bwd", grid=(pairs, nq),
        in_specs=[blk, pl.BlockSpec((seq, LANES), lambda p, i: (0, pairs + p)),
                  pl.BlockSpec((seq, LANES), lambda p, i: (0, 2 * pairs + p)), blk, blk, const, const],
        out_specs=[blk, any_spec, any_spec],
        out_shape=[jax.ShapeDtypeStruct((seq, W_SB), F32), jax.ShapeDtypeStruct((pairs, seq, LANES), F32),
                   jax.ShapeDtypeStruct((pairs, seq, LANES), F32)],
        scratch_shapes=[acc, acc, acc, acc, pltpu.VMEM((seq, LANES), F32), pltpu.VMEM((seq, LANES), F32),
                        pltpu.SemaphoreType.DMA((2,))],
        compiler_params=_params(56),
    )(proj_bf, proj_bf, proj_bf, d_oa, oa, tri_a, tri_b)


def _dil_bwd(qkn, qkv_raw, d_od, lse, delta, r):
    seq = qkn.shape[0]
    tile, edge, per = _dil_tiling(seq, r)
    ntile = seq // tile
    pairs = W_DIL // LANES

    def body(qc, doc, lsc, dlc, kc, kp, vc, vp, dq_ref, dk_ref, dv_ref, dk_carry, dv_carry):
        n = pl.program_id(1)

        @pl.when(n == 0)
        def _():
            dk_carry[...] = jnp.zeros_like(dk_carry)
            dv_carry[...] = jnp.zeros_like(dv_carry)

        dk_ref[...] = dk_carry[...]
        dv_ref[...] = dv_carry[...]

        @pl.when(n < ntile)
        def _():
            lo = _lane_lo((BLOCK, LANES))
            band, col = _band_mask(2 * BLOCK, 2 * BLOCK)
            first_band = jnp.logical_and(band, jnp.logical_or(col >= BLOCK, n > 0))

            def stacked_cols(b):
                return jnp.concatenate([_col_of(b, lo, 0), _col_of(b, lo, 1)], axis=0)

            def run(blocks):
                rows = [_class_rows(c, a * BLOCK, BLOCK, r) for c, a, _ in blocks]
                keys = _dil_keys(blocks, r, kc, kp)
                vals = _dil_keys(blocks, r, vc, vp)
                qss = [_stack_heads(qc[rw, :], lo).astype(BF16) for rw in rows]
                doss = [_stack_heads(doc[rw, :], lo).astype(BF16) for rw in rows]
                scores = [_dot_nt(qs, kb) for qs, kb in zip(qss, keys)]
                dps = [_dot_nt(dos, vb) for dos, vb in zip(doss, vals)]
                pws, dss = [], []
                for rw, s, dp, (_, _, first) in zip(rows, scores, dps, blocks):
                    pw = jnp.where(first_band if first else band, jnp.exp(s - stacked_cols(lsc[rw, :])), 0.0)
                    pws.append(pw.astype(BF16))
                    dss.append((pw * (dp - stacked_cols(dlc[rw, :]))).astype(BF16))
                dqs = [_dot(ds, kb) for ds, kb in zip(dss, keys)]
                dks = [_dot_tn(ds, qs) for ds, qs in zip(dss, qss)]
                dvs = [_dot_tn(pw, dos) for pw, dos in zip(pws, doss)]
                for (c, a, first), rw, dq, dk, dv in zip(blocks, rows, dqs, dks, dvs):
                    dq_ref[rw, :] = _unstack_heads(dq, lo)
                    if first:
                        last = _class_rows(c, (per - 1) * BLOCK, BLOCK, r)
                        dk_ref[last, :] += dk[:BLOCK]
                        dv_ref[last, :] += dv[:BLOCK]
                    else:
                        prev = _class_rows(c, (a - 1) * BLOCK, BLOCK, r)
                        dk_carry[prev, :] += dk[:BLOCK]
                        dv_carry[prev, :] += dv[:BLOCK]
                    dk_carry[rw, :] = dk[BLOCK:]
                    dv_carry[rw, :] = dv[BLOCK:]

            _for_each_group(r, per, run)

    per_edge = tile // edge
    here = lambda n: jnp.minimum(n, ntile - 1)
    cur = lambda off: pl.BlockSpec((tile, LANES), lambda p, n: (here(n), off + p))
    before = lambda off: pl.BlockSpec((edge, LANES), lambda p, n: (jnp.maximum(here(n) * per_edge - 1, 0), off + p))
    lagged = pl.BlockSpec((tile, LANES), lambda p, n: (jnp.maximum(n - 1, 0), p))
    carry = pltpu.VMEM((tile, LANES), F32)
    return pl.pallas_call(
        body, name=f"dil_bwd_r{r}", grid=(pairs, ntile + 1),
        in_specs=[cur(0)] * 4 + [cur(pairs), before(pairs), cur(2 * pairs), before(2 * pairs)],
        out_specs=[cur(0), lagged, lagged],
        out_shape=[jax.ShapeDtypeStruct((seq, W_DIL), F32)] * 3,
        scratch_shapes=[carry, carry],
        compiler_params=_params(40),
    )(qkn, d_od, lse, delta, qkn, qkn, qkv_raw, qkv_raw)


def _assemble_dproj(d_qa, d_ka, d_va, d_gates, dq_p, dk_p, dv_p, qk_raw, gains, tables, bd):
    seq = qk_raw.shape[0]
    tm = min(256, seq)
    chunks = W_DIL // (2 * LANES)

    def body(dqa, dka0, dka1, dva0, dva1, dg, q0, q1, q2, k0, k1, k2, v0, v1, v2, x_ref, g_ref, c_ref, sn_ref, sp_ref,
             bd_ref, dp_ref, gpart_ref):
        @pl.when(pl.program_id(0) == 0)
        def _():
            gpart_ref[...] = jnp.zeros_like(gpart_ref)

        def put(first_col, v):
            dp_ref[:, first_col:first_col + v.shape[1]] = v.astype(BF16)

        put(0, dqa[...])
        put(W_SB, dka0[...])
        put(W_SB + LANES, dka1[...])
        put(2 * W_SB, dva0[...])
        put(2 * W_SB + LANES, dva1[...])
        put(3 * W_SB, dg[:, :W_SB])
        put(4 * W_SB + 2 * W_DIL, v0[...] + v1[...] + v2[...])
        put(4 * W_SB + 3 * W_DIL, dg[:, W_SB:])
        c, sn, sp, bdm = c_ref[...], sn_ref[...], sp_ref[...], bd_ref[...]
        for which, parts in enumerate(((q0, q1, q2), (k0, k1, k2))):
            scale = QK_SCALE if which == 0 else 1.0
            for k in range(chunks):
                sl = slice(k * 2 * LANES, (k + 1) * 2 * LANES)
                dyv = (parts[0][:, sl] + parts[1][:, sl] + parts[2][:, sl]) * scale
                dxn = _rope_t(dyv, c, sn, sp)
                x = x_ref[:, which * W_DIL + k * 2 * LANES:which * W_DIL + (k + 1) * 2 * LANES]
                rs = lax.rsqrt(_head_sums(x * x, bdm) * (1.0 / HEAD_DIM) + EPS)
                xhat = x * rs
                gpart_ref[which] += jnp.sum((dxn * xhat).reshape(tm // 8, 8, 2 * LANES), axis=0)
                dxhat = dxn * g_ref[which]
                mean = _head_sums(dxhat * xhat, bdm) * (1.0 / HEAD_DIM)
                put(4 * W_SB + which * W_DIL + k * 2 * LANES, rs * (dxhat - xhat * mean))

    row = lambda w: pl.BlockSpec((tm, w), lambda i: (i, 0))
    pair = lambda p: pl.BlockSpec((None, tm, LANES), lambda i: (p, i, 0))
    const = lambda shape: pl.BlockSpec(shape, lambda i: tuple(0 for _ in shape))
    return pl.pallas_call(
        body, name="assemble_dproj", grid=(seq // tm,),
        in_specs=[row(W_SB), pair(0), pair(1), pair(0), pair(1), row(MIX)] + [row(W_DIL)] * 9 +
                 [row(2 * W_DIL), const((2, 1, 2 * LANES)), row(2 * LANES), row(2 * LANES), row(2 * LANES),
                  const((2 * LANES, 2 * LANES))],
        out_specs=[row(IN_COLS), const((2, 8, 2 * LANES))],
        out_shape=[jax.ShapeDtypeStruct((seq, IN_COLS), BF16), jax.ShapeDtypeStruct((2, 8, 2 * LANES), F32)],
        compiler_params=_params(48),
    )(d_qa, d_ka, d_ka, d_va, d_va, d_gates, *dq_p, *dk_p, *dv_p, qk_raw, gains, *tables, bd)


def _dw_in(h, dproj, layer, dwin_acc):
    seq, d = h.shape
    tm = min(2048, seq)

    def body(h_ref, dp_ref, acc_in, dw_ref):
        del acc_in

        @pl.when(pl.program_id(1) == 0)
        def _():
            dw_ref[...] = jnp.zeros_like(dw_ref)

        dw_ref[...] += _dot_tn(h_ref[...], dp_ref[...])

    slab = pl.BlockSpec((None, None, d, COLS_PER_DEV), lambda n, i: (n, layer, 0, 0))
    return pl.pallas_call(
        body, name="dw_in", grid=(N_DEV, seq // tm),
        in_specs=[pl.BlockSpec((tm, d), lambda n, i: (i, 0)), pl.BlockSpec((tm, COLS_PER_DEV), lambda n, i: (i, n)),
                  pl.BlockSpec(memory_space=pl.ANY)],
        out_specs=slab, out_shape=jax.ShapeDtypeStruct(dwin_acc.shape, F32),
        input_output_aliases={2: 0},
        compiler_params=_params(40),
    )(h, dproj, dwin_acc)


def _dx_norm(dproj, w_all, layer, x, g, dx_next):
    seq, d = x.shape
    tm = min(256, seq)

    def body(dp_ref, w_ref, x_ref, g_ref, dn_ref, dx_ref, gpart_ref):
        @pl.when(pl.program_id(0) == 0)
        def _():
            gpart_ref[...] = jnp.zeros_like(gpart_ref)

        dh = jnp.zeros((tm, d), F32)
        for n in range(N_DEV):
            dh = dh + _dot_nt(dp_ref[:, n * COLS_PER_DEV:(n + 1) * COLS_PER_DEV], w_ref[n])
        xf = x_ref[...]
        rs = lax.rsqrt(jnp.mean(xf * xf, axis=-1, keepdims=True) + EPS)
        xhat = xf * rs
        gpart_ref[...] += jnp.sum((dh * xhat).reshape(tm // 8, 8, d), axis=0)
        dxhat = dh * g_ref[...]
        mean = jnp.mean(dxhat * xhat, axis=-1, keepdims=True)
        dx_ref[...] = rs * (dxhat - xhat * mean) + dn_ref[...]

    row = lambda w: pl.BlockSpec((tm, w), lambda i: (i, 0))
    return pl.pallas_call(
        body, name="dx_norm", grid=(seq // tm,),
        in_specs=[row(IN_COLS), pl.BlockSpec((N_DEV, None, d, COLS_PER_DEV), lambda i: (0, layer, 0, 0)), row(d),
                  pl.BlockSpec((1, d), lambda i: (0, 0)), row(d)],
        out_specs=[row(d), pl.BlockSpec((8, d), lambda i: (0, 0))],
        out_shape=[jax.ShapeDtypeStruct((seq, d), F32), jax.ShapeDtypeStruct((8, d), F32)],
        compiler_params=_params(48),
    )(dproj, w_all, x, g, dx_next)


def _exchange_grads(dwin, dwout, small):
    flips = _flips()

    def body(dwin_ref, dwout_ref, small_ref, rin_ref, rout_ref, rsmall_ref, send_sems, recv_sems, local_sems):
        x, y, c, me = _place()
        local = [pltpu.make_async_copy(dwin_ref.at[me], rin_ref.at[me], local_sems.at[0]),
                 pltpu.make_async_copy(dwout_ref.at[me], rout_ref.at[me], local_sems.at[1]),
                 pltpu.make_async_copy(small_ref, rsmall_ref.at[me], local_sems.at[2])]
        for cp in local:
            cp.start()
        copies = []
        for k, flip in enumerate(flips):
            px, py, pc = _peer(x, y, c, flip)
            to = 4 * px + 2 * py + pc
            for a, (src, dst) in enumerate(((dwin_ref.at[to], rin_ref), (dwout_ref.at[to], rout_ref), (small_ref, rsmall_ref))):
                cp = pltpu.make_async_remote_copy(
                    src_ref=src, dst_ref=dst.at[me], send_sem=send_sems.at[3 * k + a], recv_sem=recv_sems.at[3 * k + a],
                    device_id=(px, py, pc), device_id_type=MESH_ID)
                cp.start()
                copies.append(cp)
        for cp in copies:
            cp.wait()
        for cp in local:
            cp.wait()

    any_spec = pl.BlockSpec(memory_space=pl.ANY)
    vmem = pl.BlockSpec(memory_space=pltpu.VMEM)
    n = 3 * len(flips)
    return pl.pallas_call(
        body, name="exchange_grads",
        in_specs=[any_spec, any_spec, vmem], out_specs=[any_spec, any_spec, vmem],
        out_shape=[jax.ShapeDtypeStruct(dwin.shape, F32), jax.ShapeDtypeStruct(dwout.shape, F32),
                   jax.ShapeDtypeStruct((N_DEV,) + small.shape, F32)],
        scratch_shapes=[pltpu.SemaphoreType.DMA((n,)), pltpu.SemaphoreType.DMA((n,)), pltpu.SemaphoreType.DMA((3,))],
    )(dwin, dwout, small)


def _adamw_math(g, w, m, v):
    m = ADAM_B1 * m + (1.0 - ADAM_B1) * g
    v = ADAM_B2 * v + (1.0 - ADAM_B2) * (g * g)
    m_hat = m / (1.0 - ADAM_B1 ** ADAM_STEP)
    v_hat = v / (1.0 - ADAM_B2 ** ADAM_STEP)
    delta = -ADAM_LR * (m_hat / (jnp.sqrt(v_hat) + ADAM_EPS) + ADAM_WD * w)
    return delta, m, v


def _adamw(parts, w, m, v, name):
    nl, r, c = w.shape
    tr = min(r, (256 * 512) // c)

    def body(p_ref, w_ref, m_ref, v_ref, g_ref, d_ref, nm_ref, nv_ref):
        g = p_ref[0]
        for s in range(1, N_DEV):
            g = g + p_ref[s]
        g_ref[...] = g
        d_ref[...], nm_ref[...], nv_ref[...] = _adamw_math(g, w_ref[...], m_ref[...], v_ref[...])

    blk = pl.BlockSpec((None, tr, c), lambda l, i: (l, i, 0))
    return pl.pallas_call(
        body, name=name, grid=(nl, r // tr),
        in_specs=[pl.BlockSpec((N_DEV, None, tr, c), lambda l, i: (0, l, i, 0)), blk, blk, blk],
        out_specs=[blk] * 4, out_shape=[jax.ShapeDtypeStruct(w.shape, F32)] * 4,
        compiler_params=_params(32),
    )(parts, w, m, v)


def _adamw_small(parts, w, m, v):
    def body(p_ref, w_ref, m_ref, v_ref, g_ref, d_ref, nm_ref, nv_ref):
        g = p_ref[0]
        for s in range(1, N_DEV):
            g = g + p_ref[s]
        g_ref[...] = g
        d_ref[...], nm_ref[...], nv_ref[...] = _adamw_math(g, w_ref[...], m_ref[...], v_ref[...])

    vmem = pl.BlockSpec(memory_space=pltpu.VMEM)
    return pl.pallas_call(
        body, name="adamw_small", in_specs=[vmem] * 4, out_specs=[vmem] * 4,
        out_shape=[jax.ShapeDtypeStruct(w.shape, F32)] * 4,
    )(parts, w, m, v)


def _pack_small(a, b, c):
    pad = jnp.zeros((a.shape[0], SMALL_W - a.shape[1] - b.shape[1] - c.shape[1]), F32)
    return jnp.concatenate([a, b, c, pad], axis=1)


def _unpack_small(t, d):
    return t[:, :d], t[:, d:d + HEAD_DIM], t[:, d + HEAD_DIM:d + 2 * HEAD_DIM]


def _local_step(x0, target, norm_g, q_norm_g, k_norm_g, win_all, wout_all):
    seq, d = x0.shape
    depth = norm_g.shape[0]
    tri_a, tri_b, bd = _tri_constants()
    tables = _rope_tables(seq)
    rep = (2 * LANES) // HEAD_DIM

    saved = []
    xl = x0
    for layer in range(depth):
        gains = jnp.stack([jnp.tile(q_norm_g[layer], rep), jnp.tile(k_norm_g[layer], rep)])[:, None, :]
        proj_bf, gates, qk_raw, h = _norm_proj(xl, norm_g[layer][None, :], win_all, layer)
        qkn = _qk_prep(qk_raw, gains, tables, bd)
        oa = _sb_fwd(proj_bf, tri_a)
        o_p, ld_p = zip(*[_dil_fwd(qkn, qk_raw, r) for _, r in DIL_PATTERNS])
        x_next, cat, od, lse = _out_proj(xl, oa, o_p, ld_p, gates, wout_all, layer)
        saved.append((xl, gains, proj_bf, gates, qk_raw, h, qkn, oa, cat, od, lse))
        xl = x_next

    loss_part, dx = _loss_head(xl, target)

    dwin = jnp.zeros((N_DEV, depth, d, COLS_PER_DEV), F32)
    dwout = jnp.zeros((N_DEV, depth, ROWS_PER_DEV, d), F32)
    g_norm, g_q, g_k = [None] * depth, [None] * depth, [None] * depth
    for layer in reversed(range(depth)):
        xl, gains, proj_bf, gates, qk_raw, h, qkn, oa, cat, od, lse = saved[layer]
        d_oa, d_od, delta, d_gates, dwout = _out_proj_bwd(dx, wout_all, layer, cat, gates, oa, od, bd, dwout)
        d_qa, d_ka, d_va = _sb_bwd(proj_bf, d_oa, oa, tri_a, tri_b)
        dq_p, dk_p, dv_p = zip(*[_dil_bwd(qkn, qk_raw, d_od, lse, delta, r) for _, r in DIL_PATTERNS])
        dproj, gqk = _assemble_dproj(d_qa, d_ka, d_va, d_gates, dq_p, dk_p, dv_p, qk_raw, gains, tables, bd)
        dwin = _dw_in(h, dproj, layer, dwin)
        dx, gn = _dx_norm(dproj, win_all, layer, xl, norm_g[layer][None, :], dx)
        g_norm[layer] = jnp.sum(gn, axis=0)
        gqk = jnp.sum(gqk, axis=1).reshape(2, rep, HEAD_DIM).sum(axis=1)
        g_q[layer], g_k[layer] = gqk[0], gqk[1]

    small = _pack_small(jnp.stack(g_norm), jnp.stack(g_q), jnp.stack(g_k))
    return loss_part, dx, dwin, dwout, small


def kernel(x, norm_g, w_in, q_norm_g, k_norm_g, w_out, loss_target, m_norm_g, m_w_in, m_q_norm_g, m_k_norm_g, m_w_out,
           v_norm_g, v_w_in, v_q_norm_g, v_k_norm_g, v_w_out):
    d = w_in.shape[1]
    seq = x.shape[1]
    win_all, wout_all = _all_gather_weights(_cast_bf16(w_in, "cast_w_in"), _cast_bf16(w_out, "cast_w_out"))
    loss_part, dx, dwin, dwout, small = _local_step(
        x.reshape(seq, d), loss_target.reshape(seq, d), norm_g, q_norm_g, k_norm_g, win_all, wout_all)
    loss = lax.psum(jnp.sum(loss_part), ("x", "y", "c"))
    rin, rout, rsmall = _exchange_grads(dwin, dwout, small)

    g_in, d_in, nm_in, nv_in = _adamw(rin, w_in, m_w_in, v_w_in, "adamw_w_in")
    g_out, d_out, nm_out, nv_out = _adamw(rout, w_out, m_w_out, v_w_out, "adamw_w_out")
    small_out = _adamw_small(rsmall, _pack_small(norm_g, q_norm_g, k_norm_g), _pack_small(m_norm_g, m_q_norm_g, m_k_norm_g),
                             _pack_small(v_norm_g, v_q_norm_g, v_k_norm_g))
    (g_n, g_qn, g_kn), (d_n, d_qn, d_kn), (nm_n, nm_qn, nm_kn), (nv_n, nv_qn, nv_kn) = (_unpack_small(t, d) for t in small_out)

    return (loss, dx.reshape(x.shape), g_n, g_in, g_qn, g_kn, g_out, d_n, d_in, d_qn, d_kn, d_out,
            nm_n, nm_in, nm_qn, nm_kn, nm_out, nv_n, nv_in, nv_qn, nv_kn, nv_out)
```

```python
import math

import jax
import jax.numpy as jnp
from jax import lax
from jax.experimental import pallas as pl
from jax.experimental.pallas import tpu as pltpu

F32 = jnp.float32
BF16 = jnp.bfloat16

EPS = 1e-6
HEAD_DIM = 64
BLOCK = 128
LANES = 128
W_SB = 256
W_DIL = 768
MIX = W_SB + W_DIL
IN_COLS = 4 * W_SB + 4 * W_DIL
N_DEV = 8
COLS_PER_DEV = IN_COLS // N_DEV
ROWS_PER_DEV = MIX // N_DEV
QK_SCALE = 1.0 / math.sqrt(HEAD_DIM)
DIL_PATTERNS = ((128, 1), (512, 4), (2048, 16))
DIL_TILE = 2048
DIL_GROUP = 4
ROPE_THETA = 500000.0
ROPE_DIM = HEAD_DIM // 4
ROPE_HALF = ROPE_DIM // 2
DEAD_LOG = -110.0
SB_PEEL = 3
NEG_BIG = -1e30

ADAM_LR = 0.001
ADAM_B1 = 0.9
ADAM_B2 = 0.999
ADAM_EPS = 1e-08
ADAM_WD = 0.01
ADAM_STEP = 10

SMALL_W = 1280
MESH_ID = pl.DeviceIdType.MESH
MIB = 1 << 20


def _params(vmem_mib):
    return pltpu.CompilerParams(vmem_limit_bytes=vmem_mib * MIB)


def _dot(a, b):
    return jnp.dot(a, b, preferred_element_type=F32)


def _dot_nt(a, b):
    return lax.dot_general(a, b, (((1,), (1,)), ((), ())), preferred_element_type=F32)


def _dot_tn(a, b):
    return lax.dot_general(a, b, (((0,), (0,)), ((), ())), preferred_element_type=F32)


def _dot_exact(x, m01):
    hi = x.astype(BF16)
    r1 = x - hi.astype(F32)
    mid = r1.astype(BF16)
    lo = (r1 - mid.astype(F32)).astype(BF16)
    return _dot(hi, m01) + _dot(mid, m01) + _dot(lo, m01)


def _lane_lo(shape):
    return lax.broadcasted_iota(jnp.int32, shape, 1) < HEAD_DIM


def _col_of(b, lo, h):
    keep = lo if h == 0 else jnp.logical_not(lo)
    return jnp.max(jnp.where(keep, b, NEG_BIG), axis=1, keepdims=True)


def _tri_constants():
    j = jnp.arange(BLOCK)
    ones = jnp.ones((BLOCK, BLOCK), F32)
    excl = (j[:, None] > j[None, :]).astype(F32)
    incl = (j[:, None] >= j[None, :]).astype(F32)
    tri_a = jnp.concatenate([excl, ones], axis=1).astype(BF16)
    tri_b = jnp.concatenate([incl, ones], axis=1).astype(BF16)
    d = jnp.arange(2 * LANES)
    bd = (d[:, None] // HEAD_DIM == d[None, :] // HEAD_DIM).astype(BF16)
    return tri_a, tri_b, bd


def _rope_tables(seq):
    inv_freq = 1.0 / (ROPE_THETA ** (jnp.arange(ROPE_HALF, dtype=F32) * 2.0 / ROPE_DIM))
    ang = jnp.arange(seq).astype(F32)[:, None] * inv_freq[None, :]
    cos, sin = jnp.cos(ang), jnp.sin(ang)
    one = jnp.ones((seq, HEAD_DIM - ROPE_DIM), F32)
    zero8 = jnp.zeros((seq, ROPE_HALF), F32)
    zero_rest = jnp.zeros((seq, HEAD_DIM - ROPE_DIM), F32)
    c = jnp.concatenate([cos, cos, one], axis=1)
    s_next = jnp.concatenate([-sin, zero8, zero_rest], axis=1)
    s_prev = jnp.concatenate([zero8, sin, zero_rest], axis=1)
    rep = (2 * LANES) // HEAD_DIM
    return tuple(jnp.tile(t, (1, rep)) for t in (c, s_next, s_prev))


def _roll_lanes(x, shift):
    return jnp.concatenate([pltpu.roll(x[:, :LANES], shift, 1), pltpu.roll(x[:, LANES:], shift, 1)], axis=1)


def _rope(x, c, s_next, s_prev):
    return x * c + _roll_lanes(x, LANES - ROPE_HALF) * s_next + _roll_lanes(x, ROPE_HALF) * s_prev


def _rope_t(dy, c, s_next, s_prev):
    return dy * c + _roll_lanes(dy * s_next, ROPE_HALF) + _roll_lanes(dy * s_prev, LANES - ROPE_HALF)


def _cast_bf16(w, name):
    nl, r, c = w.shape

    def body(w_ref, o_ref):
        o_ref[...] = w_ref[...].astype(BF16)

    return pl.pallas_call(
        body, name=name, grid=(nl,),
        in_specs=[pl.BlockSpec((None, r, c), lambda l: (l, 0, 0))],
        out_specs=pl.BlockSpec((None, r, c), lambda l: (l, 0, 0)),
        out_shape=jax.ShapeDtypeStruct(w.shape, BF16),
        compiler_params=_params(24),
    )(w)


def _flips():
    return [(dx, dy, dc) for dx in (0, 1) for dy in (0, 1) for dc in (0, 1) if (dx, dy, dc) != (0, 0, 0)]


def _place():
    x, y, c = lax.axis_index("x"), lax.axis_index("y"), lax.axis_index("c")
    return x, y, c, 4 * x + 2 * y + c


def _peer(x, y, c, flip):
    dx, dy, dc = flip
    return (1 - x if dx else x, 1 - y if dy else y, 1 - c if dc else c)


def _to_every_device(srcs_for, dsts_at, sems):
    send_sems, recv_sems, local_sems = sems
    x, y, c, me = _place()
    dsts = dsts_at(me)
    n = len(dsts)
    copies = [pltpu.make_async_copy(src, dst, local_sems.at[a]) for a, (src, dst) in enumerate(zip(srcs_for(me), dsts))]
    for k, flip in enumerate(_flips()):
        px, py, pc = _peer(x, y, c, flip)
        for a, (src, dst) in enumerate(zip(srcs_for(4 * px + 2 * py + pc), dsts)):
            copies.append(pltpu.make_async_remote_copy(
                src_ref=src, dst_ref=dst, send_sem=send_sems.at[n * k + a], recv_sem=recv_sems.at[n * k + a],
                device_id=(px, py, pc), device_id_type=MESH_ID))
    return copies


def _copy_sems(n):
    remote = n * (N_DEV - 1)
    return [pltpu.SemaphoreType.DMA((remote,)), pltpu.SemaphoreType.DMA((remote,)), pltpu.SemaphoreType.DMA((n,))]


def _weight_copies(win_ref, wout_ref, layer, oin_ref, oout_ref, sems):
    return _to_every_device(lambda to: (win_ref.at[layer], wout_ref.at[layer]), lambda me: (oin_ref.at[me], oout_ref.at[me]), sems)


def _gathered_shapes(win_bf, wout_bf):
    return [jax.ShapeDtypeStruct((N_DEV,) + win_bf.shape[1:], BF16), jax.ShapeDtypeStruct((N_DEV,) + wout_bf.shape[1:], BF16)]


def _gather_weights(win_bf, wout_bf, layer):
    def body(win_ref, wout_ref, oin_ref, oout_ref, *sems):
        copies = _weight_copies(win_ref, wout_ref, layer, oin_ref, oout_ref, sems)
        for cp in copies:
            cp.start()
        for cp in copies:
            cp.wait()

    any_spec = pl.BlockSpec(memory_space=pl.ANY)
    return pl.pallas_call(
        body, name="gather_weights", in_specs=[any_spec, any_spec], out_specs=[any_spec, any_spec],
        out_shape=_gathered_shapes(win_bf, wout_bf), scratch_shapes=_copy_sems(2),
    )(win_bf, wout_bf)


_F32_ROUTES = {1: ((256, 256, 0, 0),), 2: ((0, 512, 1, 0),), 3: ((0, 512, 1, 512),), 4: ((0, 512, 1, 1024),),
               5: ((0, 512, 1, 1536),), 6: ((0, 256, 1, 2048), (256, 256, 0, 256)), 7: ((0, 512, 0, 512),)}


def _norm_proj(x, g, w_l, prefetch=None):
    seq, d = x.shape
    tm = min(256, seq)
    steps = seq // tm

    def body(x_ref, g_ref, w_ref, *rest):
        if prefetch is None:
            pbf_ref, gates_ref, qk_ref, h_ref = rest
        else:
            win_ref, wout_ref, pbf_ref, gates_ref, qk_ref, h_ref, oin_ref, oout_ref, *sems = rest
            copies = lambda: _weight_copies(win_ref, wout_ref, prefetch[2], oin_ref, oout_ref, sems)

            @pl.when(pl.program_id(0) == 0)
            def _():
                for cp in copies():
                    cp.start()

        xf = x_ref[...]
        rs = lax.rsqrt(jnp.mean(xf * xf, axis=-1, keepdims=True) + EPS)
        h = (xf * rs * g_ref[...]).astype(BF16)
        h_ref[...] = h
        targets = (gates_ref, qk_ref)
        for n in range(N_DEV):
            acc = _dot(h, w_ref[n])
            pbf_ref[:, n * COLS_PER_DEV:(n + 1) * COLS_PER_DEV] = acc.astype(BF16)
            for lo, width, tgt, dst in _F32_ROUTES.get(n, ()):
                targets[tgt][:, dst:dst + width] = acc[:, lo:lo + width]

        if prefetch is not None:
            @pl.when(pl.program_id(0) == steps - 1)
            def _():
                for cp in copies():
                    cp.wait()

    row = lambda w: pl.BlockSpec((tm, w), lambda i: (i, 0))
    any_spec = pl.BlockSpec(memory_space=pl.ANY)
    in_specs = [row(d), pl.BlockSpec((1, d), lambda i: (0, 0)), pl.BlockSpec((N_DEV, d, COLS_PER_DEV), lambda i: (0, 0, 0))]
    out_specs = [row(IN_COLS), row(MIX), row(3 * W_DIL), row(d)]
    out_shape = [jax.ShapeDtypeStruct((seq, IN_COLS), BF16), jax.ShapeDtypeStruct((seq, MIX), F32),
                 jax.ShapeDtypeStruct((seq, 3 * W_DIL), F32), jax.ShapeDtypeStruct((seq, d), BF16)]
    if prefetch is None:
        return pl.pallas_call(body, name="norm_proj", grid=(steps,), in_specs=in_specs, out_specs=out_specs,
                              out_shape=out_shape, compiler_params=_params(48))(x, g, w_l)
    return pl.pallas_call(
        body, name="norm_proj_gather", grid=(steps,), in_specs=in_specs + [any_spec, any_spec],
        out_specs=out_specs + [any_spec, any_spec], out_shape=out_shape + _gathered_shapes(*prefetch[:2]),
        scratch_shapes=_copy_sems(2), compiler_params=_params(48),
    )(x, g, w_l, *prefetch[:2])


def _head_sums(v, bd):
    hi = v.astype(BF16)
    lo = (v - hi.astype(F32)).astype(BF16)
    return _dot(hi, bd) + _dot(lo, bd)


def _qk_prep(qk_raw, gains, tables, bd):
    seq = qk_raw.shape[0]
    tm = min(1024, seq)
    chunks = W_DIL // (2 * LANES)

    def body(x_ref, g_ref, c_ref, sn_ref, sp_ref, bd_ref, o_ref):
        j = pl.program_id(1)
        x = x_ref[...]
        rs = lax.rsqrt(_head_sums(x * x, bd_ref[...]) * (1.0 / HEAD_DIM) + EPS)
        y = _rope(x * rs * g_ref[...], c_ref[...], sn_ref[...], sp_ref[...])
        o_ref[...] = y * jnp.where(j < chunks, QK_SCALE, 1.0)

    tab = pl.BlockSpec((tm, 2 * LANES), lambda i, j: (i, 0))
    return pl.pallas_call(
        body, name="qk_prep", grid=(seq // tm, 2 * chunks),
        in_specs=[pl.BlockSpec((tm, 2 * LANES), lambda i, j: (i, j)),
                  pl.BlockSpec((None, 1, 2 * LANES), lambda i, j: (j // chunks, 0, 0)),
                  tab, tab, tab, pl.BlockSpec((2 * LANES, 2 * LANES), lambda i, j: (0, 0))],
        out_specs=pl.BlockSpec((tm, 2 * LANES), lambda i, j: (i, j)),
        out_shape=jax.ShapeDtypeStruct((seq, 2 * W_DIL), F32),
        compiler_params=_params(32),
    )(qk_raw, gains, *tables, bd)


def _stack_heads(x, lo):
    return jnp.concatenate([jnp.where(lo, x, 0.0), jnp.where(lo, 0.0, x)], axis=0)


def _unstack_heads(y, lo):
    return jnp.where(lo, y[:BLOCK], y[BLOCK:])


def _stacked_causal():
    row = lax.broadcasted_iota(jnp.int32, (2 * BLOCK, BLOCK), 0) & (BLOCK - 1)
    return lax.broadcasted_iota(jnp.int32, (2 * BLOCK, BLOCK), 1) < row


def _keep(x, *conds):
    for cond in conds:
        if cond is not None:
            x = jnp.where(cond, x, 0.0)
    return x


def _sb_weights(qs, kbs, tri, r, masks, lives):
    zs = [_dot_nt(qs, kb) for kb in kbs]
    lss = [jnp.minimum(z, 0.0) - jnp.log1p(jnp.exp(-jnp.abs(z))) for z in zs]
    cts = [_dot_exact(_keep(ls - z, mask, live), tri) for ls, z, mask, live in zip(lss, zs, masks, lives)]
    weights = []
    for ls, ct, mask, live in zip(lss, cts, masks, lives):
        weights.append(_keep(jnp.exp(ls + ct[:, :BLOCK] + r), mask, live))
        r = r + ct[:, BLOCK:]
    return lss, weights, r


def _sb_fwd(proj_bf, tri_a):
    seq = proj_bf.shape[0]
    pairs = W_SB // LANES

    def body(q_ref, k_ref, v_ref, tri_ref, o_ref, r_ref, acc_ref):
        i = pl.program_id(1)
        lo = _lane_lo((BLOCK, LANES))
        qs = _stack_heads(q_ref[...].astype(F32) * QK_SCALE, lo).astype(BF16)
        causal = _stacked_causal()
        tri = tri_ref[...]

        def blocks(js, r, masks, lives):
            rows = [pl.ds(pl.multiple_of(j * BLOCK, BLOCK), BLOCK) for j in js]
            _, weights, r = _sb_weights(qs, [k_ref[rw, :] for rw in rows], tri, r, masks, lives)
            out = 0.0
            for a, rw in zip(weights, rows):
                a_hi = a.astype(BF16)
                a_lo = (a - a_hi.astype(F32)).astype(BF16)
                vb = v_ref[rw, :]
                out = out + _dot(a_hi, vb) + _dot(a_lo, vb)
            return out, r

        acc_ref[...], r_ref[...] = blocks(
            [jnp.maximum(i - k, 0) for k in range(SB_PEEL)], jnp.zeros((2 * BLOCK, LANES), F32),
            [causal] + [None] * (SB_PEEL - 1), [None] + [i >= k for k in range(1, SB_PEEL)])

        def alive():
            return (jnp.max(r_ref[...]) > DEAD_LOG).astype(jnp.int32)

        def step(carry):
            out, r_ref[...] = blocks([carry[0]], r_ref[...], [None], [None])
            acc_ref[...] += out
            return carry[0] - 1, alive()

        lax.while_loop(lambda carry: jnp.logical_and(carry[0] >= 0, carry[1] > 0), step, (i - SB_PEEL, alive()))
        o_ref[...] = _unstack_heads(acc_ref[...], lo)

    return pl.pallas_call(
        body, name="sb_fwd", grid=(pairs, seq // BLOCK),
        in_specs=[pl.BlockSpec((BLOCK, LANES), lambda p, i: (i, p)),
                  pl.BlockSpec((seq, LANES), lambda p, i: (0, pairs + p)),
                  pl.BlockSpec((seq, LANES), lambda p, i: (0, 2 * pairs + p)),
                  pl.BlockSpec((BLOCK, 2 * BLOCK), lambda p, i: (0, 0))],
        out_specs=pl.BlockSpec((BLOCK, LANES), lambda p, i: (i, p)),
        out_shape=jax.ShapeDtypeStruct((seq, W_SB), F32),
        scratch_shapes=[pltpu.VMEM((2 * BLOCK, LANES), F32), pltpu.VMEM((2 * BLOCK, LANES), F32)],
        compiler_params=_params(40),
    )(proj_bf, proj_bf, proj_bf, tri_a)


def _class_rows(c, first, count, r):
    start = c + first * r
    return pl.ds(start, count) if r == 1 else pl.ds(start, count, stride=r)


def _dil_tiling(seq, r):
    tile = min(DIL_TILE, seq)
    edge = BLOCK * r
    return tile, edge, tile // edge


def _band_mask(n_rows, n_keys):
    row = lax.broadcasted_iota(jnp.int32, (n_rows, n_keys), 0) & (BLOCK - 1)
    col = lax.broadcasted_iota(jnp.int32, (n_rows, n_keys), 1)
    return jnp.logical_and(col >= row, col <= row + BLOCK), col


def _for_each_group(r, per, run):
    g = DIL_GROUP
    assert per == 1 and r % g == 0 or per % g == 0
    loop = lambda lo, hi, fn: lax.fori_loop(lo, hi, lambda t, carry: (fn(t), carry)[1], 0)
    if per == 1:
        loop(0, r // g, lambda t: run([(t * g + u, 0, True) for u in range(g)]))
        return

    def one_class(c):
        run([(c, 0, True)] + [(c, a, False) for a in range(1, g)])
        if per > g:
            loop(1, per // g, lambda t: run([(c, t * g + u, False) for u in range(g)]))

    if r == 1:
        one_class(0)
    else:
        loop(0, r, one_class)


def _dil_keys(blocks, r, cur_ref, before_ref):
    out = []
    for c, a, first in blocks:
        if first:
            rows = _class_rows(c, 0, BLOCK, r)
            out.append(jnp.concatenate([before_ref[rows, :], cur_ref[rows, :]], axis=0).astype(BF16))
        else:
            out.append(cur_ref[_class_rows(c, (a - 1) * BLOCK, 2 * BLOCK, r), :].astype(BF16))
    return out


def _dil_fwd(qkn, qkv_raw, r):
    seq = qkn.shape[0]
    tile, edge, per = _dil_tiling(seq, r)
    pairs = W_DIL // LANES

    def body(q_ref, kc_ref, kp_ref, vc_ref, vp_ref, o_ref, ld_ref):
        n = pl.program_id(1)
        lo = _lane_lo((BLOCK, LANES))
        band, col = _band_mask(2 * BLOCK, 2 * BLOCK)
        first_band = jnp.logical_and(band, jnp.logical_or(col >= BLOCK, n > 0))

        def run(blocks):
            rows = [_class_rows(c, a * BLOCK, BLOCK, r) for c, a, _ in blocks]
            keys = _dil_keys(blocks, r, kc_ref, kp_ref)
            vals = _dil_keys(blocks, r, vc_ref, vp_ref)
            scores = [_dot_nt(_stack_heads(q_ref[rw, :], lo).astype(BF16), kb) for rw, kb in zip(rows, keys)]
            probs, sums, lds = [], [], []
            for s, (_, _, first) in zip(scores, blocks):
                s = jnp.where(first_band if first else band, s, NEG_BIG)
                m = jnp.max(s, axis=1, keepdims=True)
                p = jnp.exp(s - m)
                l = jnp.sum(p, axis=1, keepdims=True)
                probs.append(p.astype(BF16))
                sums.append(l)
                lds.append(m + jnp.log(l))
            outs = [_dot(p, vb) / l for p, vb, l in zip(probs, vals, sums)]
            for rw, o, ld in zip(rows, outs, lds):
                o_ref[rw, :] = _unstack_heads(o, lo)
                ld_ref[rw, :] = _unstack_heads(jnp.broadcast_to(ld, (2 * BLOCK, LANES)), lo)

        _for_each_group(r, per, run)

    per_edge = tile // edge
    cur = lambda off: pl.BlockSpec((tile, LANES), lambda p, n: (n, off + p))
    before = lambda off: pl.BlockSpec((edge, LANES), lambda p, n: (jnp.maximum(n * per_edge - 1, 0), off + p))
    return pl.pallas_call(
        body, name=f"dil_fwd_r{r}", grid=(pairs, seq // tile),
        in_specs=[cur(0), cur(pairs), before(pairs), cur(2 * pairs), before(2 * pairs)],
        out_specs=[cur(0), cur(0)],
        out_shape=[jax.ShapeDtypeStruct((seq, W_DIL), F32), jax.ShapeDtypeStruct((seq, W_DIL), F32)],
        compiler_params=_params(32),
    )(qkn, qkn, qkn, qkv_raw, qkv_raw)


def _silu_parts(g):
    sig = jax.nn.sigmoid(g)
    return g * sig, sig * (1.0 + g * (1.0 - sig))


def _out_proj(x, oa, o_p, ld_p, gates, wout_l):
    seq, d = x.shape
    tm = min(256, seq)

    def body(x_ref, oa_ref, o0, o1, o2, l0, l1, l2, g_ref, w_ref, xn_ref, cat_ref, od_ref, lse_ref):
        lds = (l0[...], l1[...], l2[...])
        m = jnp.maximum(jnp.maximum(lds[0], lds[1]), lds[2])
        es = [jnp.exp(v - m) for v in lds]
        tot = es[0] + es[1] + es[2]
        lse_ref[...] = m + jnp.log(tot)
        inv = 1.0 / tot
        od = (es[0] * inv) * o0[...] + (es[1] * inv) * o1[...] + (es[2] * inv) * o2[...]
        od_ref[...] = od
        silu, _ = _silu_parts(g_ref[...])
        cat_ref[:, :W_SB] = (oa_ref[...] * silu[:, :W_SB]).astype(BF16)
        cat_ref[:, W_SB:] = (od * silu[:, W_SB:]).astype(BF16)
        y = x_ref[...]
        for b in range(N_DEV):
            y = y + _dot(cat_ref[:, b * ROWS_PER_DEV:(b + 1) * ROWS_PER_DEV], w_ref[b])
        xn_ref[...] = y

    row = lambda w: pl.BlockSpec((tm, w), lambda i: (i, 0))
    return pl.pallas_call(
        body, name="out_proj", grid=(seq // tm,),
        in_specs=[row(d), row(W_SB)] + [row(W_DIL)] * 6 + [row(MIX),
                  pl.BlockSpec((N_DEV, ROWS_PER_DEV, d), lambda i: (0, 0, 0))],
        out_specs=[row(d), row(MIX), row(W_DIL), row(W_DIL)],
        out_shape=[jax.ShapeDtypeStruct((seq, d), F32), jax.ShapeDtypeStruct((seq, MIX), BF16),
                   jax.ShapeDtypeStruct((seq, W_DIL), F32), jax.ShapeDtypeStruct((seq, W_DIL), F32)],
        compiler_params=_params(48),
    )(x, oa, *o_p, *ld_p, gates, wout_l)


def _loss_head(y, target):
    seq, d = y.shape
    tm = min(512, seq)

    def body(y_ref, t_ref, part_ref, dy_ref):
        @pl.when(pl.program_id(0) == 0)
        def _():
            part_ref[...] = jnp.zeros_like(part_ref)

        diff = y_ref[...] - t_ref[...]
        dy_ref[...] = diff * (1.0 / d)
        part_ref[...] += jnp.sum((diff * diff).reshape(tm // 8, 8, d), axis=0) * (0.5 / d)

    row = pl.BlockSpec((tm, d), lambda i: (i, 0))
    return pl.pallas_call(
        body, name="loss_head", grid=(seq // tm,),
        in_specs=[row, row], out_specs=[pl.BlockSpec((8, d), lambda i: (0, 0)), row],
        out_shape=[jax.ShapeDtypeStruct((8, d), F32), jax.ShapeDtypeStruct((seq, d), F32)],
        compiler_params=_params(32),
    )(y, target)


def _out_proj_bwd(dy, wout_l, cat, gates, oa, od, bd):
    seq, d = dy.shape
    tm = min(256, seq)

    def body(dy_ref, w_ref, cat_ref, g_ref, oa_ref, od_ref, bd_ref, doa_ref, dod_ref, delta_ref, dg_ref, dw_ref, dcat):
        @pl.when(pl.program_id(0) == 0)
        def _():
            dw_ref[...] = jnp.zeros_like(dw_ref)

        dyb = dy_ref[...].astype(BF16)
        dw = _dot_tn(cat_ref[...], dyb)
        for b in range(N_DEV):
            dw_ref[b] += dw[b * ROWS_PER_DEV:(b + 1) * ROWS_PER_DEV, :]
            dcat[:, b * ROWS_PER_DEV:(b + 1) * ROWS_PER_DEV] = _dot_nt(dyb, w_ref[b])
        silu, dsilu = _silu_parts(g_ref[...])
        dc = dcat[...]
        dmix = dc * silu
        oa_v, od_v = oa_ref[...], od_ref[...]
        dg_ref[:, :W_SB] = dc[:, :W_SB] * oa_v * dsilu[:, :W_SB]
        dg_ref[:, W_SB:] = dc[:, W_SB:] * od_v * dsilu[:, W_SB:]
        doa_ref[...] = dmix[:, :W_SB]
        dod = dmix[:, W_SB:]
        dod_ref[...] = dod
        prod = dod * od_v
        for k in range(W_DIL // (2 * LANES)):
            sl = slice(k * 2 * LANES, (k + 1) * 2 * LANES)
            delta_ref[:, sl] = _head_sums(prod[:, sl], bd_ref[...])

    row = lambda w: pl.BlockSpec((tm, w), lambda i: (i, 0))
    slab = pl.BlockSpec((N_DEV, ROWS_PER_DEV, d), lambda i: (0, 0, 0))
    return pl.pallas_call(
        body, name="out_proj_bwd", grid=(seq // tm,),
        in_specs=[row(d), slab, row(MIX), row(MIX), row(W_SB), row(W_DIL),
                  pl.BlockSpec((2 * LANES, 2 * LANES), lambda i: (0, 0))],
        out_specs=[row(W_SB), row(W_DIL), row(W_DIL), row(MIX), slab],
        out_shape=[jax.ShapeDtypeStruct((seq, W_SB), F32), jax.ShapeDtypeStruct((seq, W_DIL), F32),
                   jax.ShapeDtypeStruct((seq, W_DIL), F32), jax.ShapeDtypeStruct((seq, MIX), F32),
                   jax.ShapeDtypeStruct((N_DEV, ROWS_PER_DEV, d), F32)],
        scratch_shapes=[pltpu.VMEM((tm, MIX), F32)],
        compiler_params=_params(48),
    )(dy, wout_l, cat, gates, oa, od, bd)


def _sb_bwd(proj_bf, d_oa, oa, tri_a, tri_b):
    seq = proj_bf.shape[0]
    pairs = W_SB // LANES
    nq = seq // BLOCK

    def body(q_ref, k_ref, v_ref, do_ref, o_ref, tria_ref, trib_ref, dq_ref, dk_hbm, dv_hbm,
             r_ref, sfx_ref, dtot_ref, dq_acc, dk_acc, dv_acc, sems):
        p, i = pl.program_id(0), pl.program_id(1)

        @pl.when(i == 0)
        def _():
            dk_acc[...] = jnp.zeros_like(dk_acc)
            dv_acc[...] = jnp.zeros_like(dv_acc)

        lo = _lane_lo((BLOCK, LANES))
        qs = _stack_heads(q_ref[...].astype(F32) * QK_SCALE, lo).astype(BF16)
        dos = _stack_heads(do_ref[...], lo).astype(BF16)
        o2 = o_ref[...]
        tri, trib = tria_ref[...], trib_ref[...]
        dtot_ref[...] = _dot_exact(dos.astype(F32) * jnp.concatenate([o2, o2], axis=0), tri[:, BLOCK:])
        causal = _stacked_causal()

        def blocks(js, r, sfx, masks, lives):
            rows = [pl.ds(pl.multiple_of(j * BLOCK, BLOCK), BLOCK) for j in js]
            kbs = [k_ref[rw, :] for rw in rows]
            dovs = [_dot_nt(dos, v_ref[rw, :]) for rw in rows]
            lss, weights, r = _sb_weights(qs, kbs, tri, r, masks, lives)
            pws = [a * dov for a, dov in zip(weights, dovs)]
            cps = [_dot_exact(pw, trib) for pw in pws]
            dzs = []
            for ls, pw, cp, mask, live in zip(lss, pws, cps, masks, lives):
                beta = jnp.exp(ls)
                before = dtot_ref[...] - sfx - cp[:, :BLOCK]
                dzs.append(_keep(pw * (1.0 - beta) - before * beta, mask, live).astype(BF16))
                sfx = sfx + cp[:, BLOCK:]
            dq = 0.0
            for dzb, kb in zip(dzs, kbs):
                dq = dq + _dot(dzb, kb)
            for dzb, a, rw in zip(dzs, weights, rows):
                dk_acc[rw, :] += _dot_tn(dzb, qs)
                dv_acc[rw, :] += _dot_tn(a.astype(BF16), dos)
            return dq, r, sfx

        zero = jnp.zeros((2 * BLOCK, LANES), F32)
        dq_acc[...], r_ref[...], sfx_ref[...] = blocks(
            [jnp.maximum(i - k, 0) for k in range(SB_PEEL)], zero, zero,
            [causal] + [None] * (SB_PEEL - 1), [None] + [i >= k for k in range(1, SB_PEEL)])

        def alive():
            return (jnp.max(r_ref[...]) > DEAD_LOG).astype(jnp.int32)

        def step(carry):
            dq, r_ref[...], sfx_ref[...] = blocks([carry[0]], r_ref[...], sfx_ref[...], [None], [None])
            dq_acc[...] += dq
            return carry[0] - 1, alive()

        lax.while_loop(lambda carry: jnp.logical_and(carry[0] >= 0, carry[1] > 0), step, (i - SB_PEEL, alive()))
        dq_ref[...] = _unstack_heads(dq_acc[...], lo) * QK_SCALE

        @pl.when(i == nq - 1)
        def _():
            outs = [pltpu.make_async_copy(dk_acc, dk_hbm.at[p], sems.at[0]),
                    pltpu.make_async_copy(dv_acc, dv_hbm.at[p], sems.at[1])]
            for cp in outs:
                cp.start()
            for cp in outs:
                cp.wait()

    blk = pl.BlockSpec((BLOCK, LANES), lambda p, i: (i, p))
    const = pl.BlockSpec((BLOCK, 2 * BLOCK), lambda p, i: (0, 0))
    any_spec = pl.BlockSpec(memory_space=pl.ANY)
    acc = pltpu.VMEM((2 * BLOCK, LANES), F32)
    return pl.pallas_call(
        body, name="sb_bwd", grid=(pairs, nq),
        in_specs=[blk, pl.BlockSpec((seq, LANES), lambda p, i: (0, pairs + p)),
                  pl.BlockSpec((seq, LANES), lambda p, i: (0, 2 * pairs + p)), blk, blk, const, const],
        out_specs=[blk, any_spec, any_spec],
        out_shape=[jax.ShapeDtypeStruct((seq, W_SB), F32), jax.ShapeDtypeStruct((pairs, seq, LANES), F32),
                   jax.ShapeDtypeStruct((pairs, seq, LANES), F32)],
        scratch_shapes=[acc, acc, acc, acc, pltpu.VMEM((seq, LANES), F32), pltpu.VMEM((seq, LANES), F32),
                        pltpu.SemaphoreType.DMA((2,))],
        compiler_params=_params(56),
    )(proj_bf, proj_bf, proj_bf, d_oa, oa, tri_a, tri_b)


def _dil_bwd(qkn, qkv_raw, d_od, lse, delta, r):
    seq = qkn.shape[0]
    tile, edge, per = _dil_tiling(seq, r)
    ntile = seq // tile
    pairs = W_DIL // LANES

    def body(qc, doc, lsc, dlc, kc, kp, vc, vp, dq_ref, dk_ref, dv_ref, dk_carry, dv_carry):
        n = pl.program_id(1)

        @pl.when(n == 0)
        def _():
            dk_carry[...] = jnp.zeros_like(dk_carry)
            dv_carry[...] = jnp.zeros_like(dv_carry)

        dk_ref[...] = dk_carry[...]
        dv_ref[...] = dv_carry[...]

        @pl.when(n < ntile)
        def _():
            lo = _lane_lo((BLOCK, LANES))
            band, col = _band_mask(2 * BLOCK, 2 * BLOCK)
            first_band = jnp.logical_and(band, jnp.logical_or(col >= BLOCK, n > 0))

            def stacked_cols(b):
                return jnp.concatenate([_col_of(b, lo, 0), _col_of(b, lo, 1)], axis=0)

            def run(blocks):
                rows = [_class_rows(c, a * BLOCK, BLOCK, r) for c, a, _ in blocks]
                keys = _dil_keys(blocks, r, kc, kp)
                vals = _dil_keys(blocks, r, vc, vp)
                qss = [_stack_heads(qc[rw, :], lo).astype(BF16) for rw in rows]
                doss = [_stack_heads(doc[rw, :], lo).astype(BF16) for rw in rows]
                scores = [_dot_nt(qs, kb) for qs, kb in zip(qss, keys)]
                dps = [_dot_nt(dos, vb) for dos, vb in zip(doss, vals)]
                pws, dss = [], []
                for rw, s, dp, (_, _, first) in zip(rows, scores, dps, blocks):
                    pw = jnp.where(first_band if first else band, jnp.exp(s - stacked_cols(lsc[rw, :])), 0.0)
                    pws.append(pw.astype(BF16))
                    dss.append((pw * (dp - stacked_cols(dlc[rw, :]))).astype(BF16))
                dqs = [_dot(ds, kb) for ds, kb in zip(dss, keys)]
                dks = [_dot_tn(ds, qs) for ds, qs in zip(dss, qss)]
                dvs = [_dot_tn(pw, dos) for pw, dos in zip(pws, doss)]
                for (c, a, first), rw, dq, dk, dv in zip(blocks, rows, dqs, dks, dvs):
                    dq_ref[rw, :] = _unstack_heads(dq, lo)
                    if first:
                        last = _class_rows(c, (per - 1) * BLOCK, BLOCK, r)
                        dk_ref[last, :] += dk[:BLOCK]
                        dv_ref[last, :] += dv[:BLOCK]
                    else:
                        prev = _class_rows(c, (a - 1) * BLOCK, BLOCK, r)
                        dk_carry[prev, :] += dk[:BLOCK]
                        dv_carry[prev, :] += dv[:BLOCK]
                    dk_carry[rw, :] = dk[BLOCK:]
                    dv_carry[rw, :] = dv[BLOCK:]

            _for_each_group(r, per, run)

    per_edge = tile // edge
    here = lambda n: jnp.minimum(n, ntile - 1)
    cur = lambda off: pl.BlockSpec((tile, LANES), lambda p, n: (here(n), off + p))
    before = lambda off: pl.BlockSpec((edge, LANES), lambda p, n: (jnp.maximum(here(n) * per_edge - 1, 0), off + p))
    lagged = pl.BlockSpec((tile, LANES), lambda p, n: (jnp.maximum(n - 1, 0), p))
    carry = pltpu.VMEM((tile, LANES), F32)
    return pl.pallas_call(
        body, name=f"dil_bwd_r{r}", grid=(pairs, ntile + 1),
        in_specs=[cur(0)] * 4 + [cur(pairs), before(pairs), cur(2 * pairs), before(2 * pairs)],
        out_specs=[cur(0), lagged, lagged],
        out_shape=[jax.ShapeDtypeStruct((seq, W_DIL), F32)] * 3,
        scratch_shapes=[carry, carry],
        compiler_params=_params(40),
    )(qkn, d_od, lse, delta, qkn, qkn, qkv_raw, qkv_raw)


def _assemble_dproj(d_qa, d_ka, d_va, d_gates, dq_p, dk_p, dv_p, qk_raw, gains, tables, bd):
    seq = qk_raw.shape[0]
    tm = min(256, seq)
    chunks = W_DIL // (2 * LANES)

    def body(dqa, dka0, dka1, dva0, dva1, dg, q0, q1, q2, k0, k1, k2, v0, v1, v2, x_ref, g_ref, c_ref, sn_ref, sp_ref,
             bd_ref, dp_ref, gpart_ref):
        @pl.when(pl.program_id(0) == 0)
        def _():
            gpart_ref[...] = jnp.zeros_like(gpart_ref)

        def put(first_col, v):
            dp_ref[:, first_col:first_col + v.shape[1]] = v.astype(BF16)

        put(0, dqa[...])
        put(W_SB, dka0[...])
        put(W_SB + LANES, dka1[...])
        put(2 * W_SB, dva0[...])
        put(2 * W_SB + LANES, dva1[...])
        put(3 * W_SB, dg[:, :W_SB])
        put(4 * W_SB + 2 * W_DIL, v0[...] + v1[...] + v2[...])
        put(4 * W_SB + 3 * W_DIL, dg[:, W_SB:])
        c, sn, sp, bdm = c_ref[...], sn_ref[...], sp_ref[...], bd_ref[...]
        for which, parts in enumerate(((q0, q1, q2), (k0, k1, k2))):
            scale = QK_SCALE if which == 0 else 1.0
            for k in range(chunks):
                sl = slice(k * 2 * LANES, (k + 1) * 2 * LANES)
                dyv = (parts[0][:, sl] + parts[1][:, sl] + parts[2][:, sl]) * scale
                dxn = _rope_t(dyv, c, sn, sp)
                x = x_ref[:, which * W_DIL + k * 2 * LANES:which * W_DIL + (k + 1) * 2 * LANES]
                rs = lax.rsqrt(_head_sums(x * x, bdm) * (1.0 / HEAD_DIM) + EPS)
                xhat = x * rs
                gpart_ref[which] += jnp.sum((dxn * xhat).reshape(tm // 8, 8, 2 * LANES), axis=0)
                dxhat = dxn * g_ref[which]
                mean = _head_sums(dxhat * xhat, bdm) * (1.0 / HEAD_DIM)
                put(4 * W_SB + which * W_DIL + k * 2 * LANES, rs * (dxhat - xhat * mean))

    row = lambda w: pl.BlockSpec((tm, w), lambda i: (i, 0))
    pair = lambda p: pl.BlockSpec((None, tm, LANES), lambda i: (p, i, 0))
    const = lambda shape: pl.BlockSpec(shape, lambda i: tuple(0 for _ in shape))
    return pl.pallas_call(
        body, name="assemble_dproj", grid=(seq // tm,),
        in_specs=[row(W_SB), pair(0), pair(1), pair(0), pair(1), row(MIX)] + [row(W_DIL)] * 9 +
                 [row(2 * W_DIL), const((2, 1, 2 * LANES)), row(2 * LANES), row(2 * LANES), row(2 * LANES),
                  const((2 * LANES, 2 * LANES))],
        out_specs=[row(IN_COLS), const((2, 8, 2 * LANES))],
        out_shape=[jax.ShapeDtypeStruct((seq, IN_COLS), BF16), jax.ShapeDtypeStruct((2, 8, 2 * LANES), F32)],
        compiler_params=_params(48),
    )(d_qa, d_ka, d_ka, d_va, d_va, d_gates, *dq_p, *dk_p, *dv_p, qk_raw, gains, *tables, bd)


def _dw_in(h, dproj):
    seq, d = h.shape
    tm = min(2048, seq)

    def body(h_ref, dp_ref, dw_ref):
        @pl.when(pl.program_id(1) == 0)
        def _():
            dw_ref[...] = jnp.zeros_like(dw_ref)

        dw_ref[...] += _dot_tn(h_ref[...], dp_ref[...])

    return pl.pallas_call(
        body, name="dw_in", grid=(N_DEV, seq // tm),
        in_specs=[pl.BlockSpec((tm, d), lambda n, i: (i, 0)), pl.BlockSpec((tm, COLS_PER_DEV), lambda n, i: (i, n))],
        out_specs=pl.BlockSpec((None, d, COLS_PER_DEV), lambda n, i: (n, 0, 0)),
        out_shape=jax.ShapeDtypeStruct((N_DEV, d, COLS_PER_DEV), F32),
        compiler_params=_params(40),
    )(h, dproj)


def _dx_norm(dproj, w_l, x, g, dx_next, dwin_l, dwout_l, rin, rout, layer):
    seq, d = x.shape
    tm = min(256, seq)
    steps = seq // tm

    def body(dp_ref, w_ref, x_ref, g_ref, dn_ref, dwin_ref, dwout_ref, rin_in, rout_in, dx_ref, gpart_ref, rin_ref, rout_ref, *sems):
        del rin_in, rout_in
        copies = lambda: _to_every_device(lambda to: (dwin_ref.at[to], dwout_ref.at[to]),
                                          lambda me: (rin_ref.at[me, layer], rout_ref.at[me, layer]), sems)

        @pl.when(pl.program_id(0) == 0)
        def _():
            gpart_ref[...] = jnp.zeros_like(gpart_ref)
            for cp in copies():
                cp.start()

        dh = jnp.zeros((tm, d), F32)
        for n in range(N_DEV):
            dh = dh + _dot_nt(dp_ref[:, n * COLS_PER_DEV:(n + 1) * COLS_PER_DEV], w_ref[n])
        xf = x_ref[...]
        rs = lax.rsqrt(jnp.mean(xf * xf, axis=-1, keepdims=True) + EPS)
        xhat = xf * rs
        gpart_ref[...] += jnp.sum((dh * xhat).reshape(tm // 8, 8, d), axis=0)
        dxhat = dh * g_ref[...]
        mean = jnp.mean(dxhat * xhat, axis=-1, keepdims=True)
        dx_ref[...] = rs * (dxhat - xhat * mean) + dn_ref[...]

        @pl.when(pl.program_id(0) == steps - 1)
        def _():
            for cp in copies():
                cp.wait()

    row = lambda w: pl.BlockSpec((tm, w), lambda i: (i, 0))
    any_spec = pl.BlockSpec(memory_space=pl.ANY)
    return pl.pallas_call(
        body, name="dx_norm_exchange", grid=(steps,),
        in_specs=[row(IN_COLS), pl.BlockSpec((N_DEV, d, COLS_PER_DEV), lambda i: (0, 0, 0)), row(d),
                  pl.BlockSpec((1, d), lambda i: (0, 0)), row(d), any_spec, any_spec, any_spec, any_spec],
        out_specs=[row(d), pl.BlockSpec((8, d), lambda i: (0, 0)), any_spec, any_spec],
        out_shape=[jax.ShapeDtypeStruct((seq, d), F32), jax.ShapeDtypeStruct((8, d), F32),
                   jax.ShapeDtypeStruct(rin.shape, F32), jax.ShapeDtypeStruct(rout.shape, F32)],
        scratch_shapes=_copy_sems(2), input_output_aliases={7: 2, 8: 3},
        compiler_params=_params(48),
    )(dproj, w_l, x, g, dx_next, dwin_l, dwout_l, rin, rout)


def _exchange_small(small):
    def body(small_ref, out_ref, *sems):
        copies = _to_every_device(lambda to: (small_ref,), lambda me: (out_ref.at[me],), sems)
        for cp in copies:
            cp.start()
        for cp in copies:
            cp.wait()

    vmem = pl.BlockSpec(memory_space=pltpu.VMEM)
    return pl.pallas_call(
        body, name="exchange_small", in_specs=[vmem], out_specs=vmem,
        out_shape=jax.ShapeDtypeStruct((N_DEV,) + small.shape, F32), scratch_shapes=_copy_sems(1),
    )(small)


def _adamw_math(g, w, m, v):
    m = ADAM_B1 * m + (1.0 - ADAM_B1) * g
    v = ADAM_B2 * v + (1.0 - ADAM_B2) * (g * g)
    m_hat = m / (1.0 - ADAM_B1 ** ADAM_STEP)
    v_hat = v / (1.0 - ADAM_B2 ** ADAM_STEP)
    delta = -ADAM_LR * (m_hat / (jnp.sqrt(v_hat) + ADAM_EPS) + ADAM_WD * w)
    return delta, m, v


def _adamw(parts, w, m, v, name):
    nl, r, c = w.shape
    tr = min(r, (256 * 512) // c)

    def body(p_ref, w_ref, m_ref, v_ref, g_ref, d_ref, nm_ref, nv_ref):
        g = p_ref[0]
        for s in range(1, N_DEV):
            g = g + p_ref[s]
        g_ref[...] = g
        d_ref[...], nm_ref[...], nv_ref[...] = _adamw_math(g, w_ref[...], m_ref[...], v_ref[...])

    blk = pl.BlockSpec((None, tr, c), lambda l, i: (l, i, 0))
    return pl.pallas_call(
        body, name=name, grid=(nl, r // tr),
        in_specs=[pl.BlockSpec((N_DEV, None, tr, c), lambda l, i: (0, l, i, 0)), blk, blk, blk],
        out_specs=[blk] * 4, out_shape=[jax.ShapeDtypeStruct(w.shape, F32)] * 4,
        compiler_params=_params(32),
    )(parts, w, m, v)


def _adamw_small(parts, w, m, v):
    def body(p_ref, w_ref, m_ref, v_ref, g_ref, d_ref, nm_ref, nv_ref):
        g = p_ref[0]
        for s in range(1, N_DEV):
            g = g + p_ref[s]
        g_ref[...] = g
        d_ref[...], nm_ref[...], nv_ref[...] = _adamw_math(g, w_ref[...], m_ref[...], v_ref[...])

    vmem = pl.BlockSpec(memory_space=pltpu.VMEM)
    return pl.pallas_call(
        body, name="adamw_small", in_specs=[vmem] * 4, out_specs=[vmem] * 4,
        out_shape=[jax.ShapeDtypeStruct(w.shape, F32)] * 4,
    )(parts, w, m, v)


def _pack_small(a, b, c):
    pad = jnp.zeros((a.shape[0], SMALL_W - a.shape[1] - b.shape[1] - c.shape[1]), F32)
    return jnp.concatenate([a, b, c, pad], axis=1)


def _unpack_small(t, d):
    return t[:, :d], t[:, d:d + HEAD_DIM], t[:, d + HEAD_DIM:d + 2 * HEAD_DIM]


def kernel(x, norm_g, w_in, q_norm_g, k_norm_g, w_out, loss_target, m_norm_g, m_w_in, m_q_norm_g, m_k_norm_g, m_w_out,
           v_norm_g, v_w_in, v_q_norm_g, v_k_norm_g, v_w_out):
    depth, d, _ = w_in.shape
    seq = x.shape[1]
    tri_a, tri_b, bd = _tri_constants()
    tables = _rope_tables(seq)
    rep = (2 * LANES) // HEAD_DIM

    win_bf, wout_bf = _cast_bf16(w_in, "cast_w_in"), _cast_bf16(w_out, "cast_w_out")
    win_l, wout_l = _gather_weights(win_bf, wout_bf, 0)
    saved = []
    xl = x.reshape(seq, d)
    for layer in range(depth):
        gains = jnp.stack([jnp.tile(q_norm_g[layer], rep), jnp.tile(k_norm_g[layer], rep)])[:, None, :]
        if layer + 1 < depth:
            proj_bf, gates, qk_raw, h, win_next, wout_next = _norm_proj(
                xl, norm_g[layer][None, :], win_l, prefetch=(win_bf, wout_bf, layer + 1))
        else:
            proj_bf, gates, qk_raw, h = _norm_proj(xl, norm_g[layer][None, :], win_l)
            win_next = wout_next = None
        qkn = _qk_prep(qk_raw, gains, tables, bd)
        oa = _sb_fwd(proj_bf, tri_a)
        o_p, ld_p = zip(*[_dil_fwd(qkn, qk_raw, r) for _, r in DIL_PATTERNS])
        x_next, cat, od, lse = _out_proj(xl, oa, o_p, ld_p, gates, wout_l)
        saved.append((xl, gains, proj_bf, gates, qk_raw, h, qkn, oa, cat, od, lse, win_l, wout_l))
        xl, win_l, wout_l = x_next, win_next, wout_next

    loss_part, dx = _loss_head(xl, loss_target.reshape(seq, d))
    loss = lax.psum(jnp.sum(loss_part), ("x", "y", "c"))

    rin = jnp.zeros((N_DEV, depth, d, COLS_PER_DEV), F32)
    rout = jnp.zeros((N_DEV, depth, ROWS_PER_DEV, d), F32)
    g_norm, g_q, g_k = [None] * depth, [None] * depth, [None] * depth
    for layer in reversed(range(depth)):
        xl, gains, proj_bf, gates, qk_raw, h, qkn, oa, cat, od, lse, win_l, wout_l = saved[layer]
        d_oa, d_od, delta, d_gates, dwout_l = _out_proj_bwd(dx, wout_l, cat, gates, oa, od, bd)
        d_qa, d_ka, d_va = _sb_bwd(proj_bf, d_oa, oa, tri_a, tri_b)
        dq_p, dk_p, dv_p = zip(*[_dil_bwd(qkn, qk_raw, d_od, lse, delta, r) for _, r in DIL_PATTERNS])
        dproj, gqk = _assemble_dproj(d_qa, d_ka, d_va, d_gates, dq_p, dk_p, dv_p, qk_raw, gains, tables, bd)
        dwin_l = _dw_in(h, dproj)
        dx, gn, rin, rout = _dx_norm(dproj, win_l, xl, norm_g[layer][None, :], dx, dwin_l, dwout_l, rin, rout, layer)
        g_norm[layer] = jnp.sum(gn, axis=0)
        gqk = jnp.sum(gqk, axis=1).reshape(2, rep, HEAD_DIM).sum(axis=1)
        g_q[layer], g_k[layer] = gqk[0], gqk[1]
    rsmall = _exchange_small(_pack_small(jnp.stack(g_norm), jnp.stack(g_q), jnp.stack(g_k)))

    g_in, d_in, nm_in, nv_in = _adamw(rin, w_in, m_w_in, v_w_in, "adamw_w_in")
    g_out, d_out, nm_out, nv_out = _adamw(rout, w_out, m_w_out, v_w_out, "adamw_w_out")
    small_out = _adamw_small(rsmall, _pack_small(norm_g, q_norm_g, k_norm_g), _pack_small(m_norm_g, m_q_norm_g, m_k_norm_g),
                             _pack_small(v_norm_g, v_q_norm_g, v_k_norm_g))
    (g_n, g_qn, g_kn), (d_n, d_qn, d_kn), (nm_n, nm_qn, nm_kn), (nv_n, nv_qn, nv_kn) = (_unpack_small(t, d) for t in small_out)

    return (loss, dx.reshape(x.shape), g_n, g_in, g_qn, g_kn, g_out, d_n, d_in, d_qn, d_kn, d_out,
            nm_n, nm_in, nm_qn, nm_kn, nm_out, nv_n, nv_in, nv_qn, nv_kn, nv_out)
```

```python
import math

import jax
import jax.numpy as jnp
from jax import lax
from jax.experimental import pallas as pl
from jax.experimental.pallas import tpu as pltpu

F32 = jnp.float32
BF16 = jnp.bfloat16

EPS = 1e-6
HEAD_DIM = 64
BLOCK = 128
LANES = 128
W_SB = 256
W_DIL = 768
MIX = W_SB + W_DIL
IN_COLS = 4 * W_SB + 4 * W_DIL
N_DEV = 8
COLS_PER_DEV = IN_COLS // N_DEV
ROWS_PER_DEV = MIX // N_DEV
QK_SCALE = 1.0 / math.sqrt(HEAD_DIM)
DIL_PATTERNS = ((128, 1), (512, 4), (2048, 16))
DIL_TILE = 2048
DIL_GROUP_FWD = 4
DIL_GROUP_BWD = 4
ROPE_THETA = 500000.0
ROPE_DIM = HEAD_DIM // 4
ROPE_HALF = ROPE_DIM // 2
DEAD_LOG = -110.0
SB_PEEL = 3
SB_QBLOCKS = 2
SB_SUM_TERMS = 2
NEG_BIG = -1e30

ADAM_LR = 0.001
ADAM_B1 = 0.9
ADAM_B2 = 0.999
ADAM_EPS = 1e-08
ADAM_WD = 0.01
ADAM_STEP = 10

SMALL_W = 1280
MESH_ID = pl.DeviceIdType.MESH
MIB = 1 << 20


def _params(vmem_mib):
    return pltpu.CompilerParams(vmem_limit_bytes=vmem_mib * MIB)


def _dot(a, b):
    return jnp.dot(a, b, preferred_element_type=F32)


def _dot_nt(a, b):
    return lax.dot_general(a, b, (((1,), (1,)), ((), ())), preferred_element_type=F32)


def _dot_tn(a, b):
    return lax.dot_general(a, b, (((0,), (0,)), ((), ())), preferred_element_type=F32)


def _dot_exact(x, m01, terms=3):
    out = 0.0
    for _ in range(terms):
        part = x.astype(BF16)
        out = out + _dot(part, m01)
        x = x - part.astype(F32)
    return out


def _lane_lo(shape):
    return lax.broadcasted_iota(jnp.int32, shape, 1) < HEAD_DIM


def _col_of(b, lo, h):
    keep = lo if h == 0 else jnp.logical_not(lo)
    return jnp.max(jnp.where(keep, b, NEG_BIG), axis=1, keepdims=True)


def _tri_constants():
    j = jnp.arange(BLOCK)
    ones = jnp.ones((BLOCK, BLOCK), F32)
    excl = (j[:, None] > j[None, :]).astype(F32)
    incl = (j[:, None] >= j[None, :]).astype(F32)
    tri_a = jnp.concatenate([excl, ones], axis=1).astype(BF16)
    tri_b = jnp.concatenate([incl, ones], axis=1).astype(BF16)
    d = jnp.arange(2 * LANES)
    bd = (d[:, None] // HEAD_DIM == d[None, :] // HEAD_DIM).astype(BF16)
    return tri_a, tri_b, bd


def _rope_tables(seq):
    inv_freq = 1.0 / (ROPE_THETA ** (jnp.arange(ROPE_HALF, dtype=F32) * 2.0 / ROPE_DIM))
    ang = jnp.arange(seq).astype(F32)[:, None] * inv_freq[None, :]
    cos, sin = jnp.cos(ang), jnp.sin(ang)
    one = jnp.ones((seq, HEAD_DIM - ROPE_DIM), F32)
    zero8 = jnp.zeros((seq, ROPE_HALF), F32)
    zero_rest = jnp.zeros((seq, HEAD_DIM - ROPE_DIM), F32)
    c = jnp.concatenate([cos, cos, one], axis=1)
    s_next = jnp.concatenate([-sin, zero8, zero_rest], axis=1)
    s_prev = jnp.concatenate([zero8, sin, zero_rest], axis=1)
    rep = (2 * LANES) // HEAD_DIM
    return tuple(jnp.tile(t, (1, rep)) for t in (c, s_next, s_prev))


def _roll_lanes(x, shift):
    return jnp.concatenate([pltpu.roll(x[:, :LANES], shift, 1), pltpu.roll(x[:, LANES:], shift, 1)], axis=1)


def _rope(x, c, s_next, s_prev):
    return x * c + _roll_lanes(x, LANES - ROPE_HALF) * s_next + _roll_lanes(x, ROPE_HALF) * s_prev


def _rope_t(dy, c, s_next, s_prev):
    return dy * c + _roll_lanes(dy * s_next, ROPE_HALF) + _roll_lanes(dy * s_prev, LANES - ROPE_HALF)


def _cast_bf16(w, name):
    nl, r, c = w.shape

    def body(w_ref, o_ref):
        o_ref[...] = w_ref[...].astype(BF16)

    return pl.pallas_call(
        body, name=name, grid=(nl,),
        in_specs=[pl.BlockSpec((None, r, c), lambda l: (l, 0, 0))],
        out_specs=pl.BlockSpec((None, r, c), lambda l: (l, 0, 0)),
        out_shape=jax.ShapeDtypeStruct(w.shape, BF16),
        compiler_params=_params(24),
    )(w)


def _flips():
    return [(dx, dy, dc) for dx in (0, 1) for dy in (0, 1) for dc in (0, 1) if (dx, dy, dc) != (0, 0, 0)]


def _place():
    x, y, c = lax.axis_index("x"), lax.axis_index("y"), lax.axis_index("c")
    return x, y, c, 4 * x + 2 * y + c


def _peer(x, y, c, flip):
    dx, dy, dc = flip
    return (1 - x if dx else x, 1 - y if dy else y, 1 - c if dc else c)


def _to_every_device(srcs_for, dsts_at, sems):
    send_sems, recv_sems, local_sems = sems
    x, y, c, me = _place()
    dsts = dsts_at(me)
    n = len(dsts)
    copies = [pltpu.make_async_copy(src, dst, local_sems.at[a]) for a, (src, dst) in enumerate(zip(srcs_for(me), dsts))]
    for k, flip in enumerate(_flips()):
        px, py, pc = _peer(x, y, c, flip)
        for a, (src, dst) in enumerate(zip(srcs_for(4 * px + 2 * py + pc), dsts)):
            copies.append(pltpu.make_async_remote_copy(
                src_ref=src, dst_ref=dst, send_sem=send_sems.at[n * k + a], recv_sem=recv_sems.at[n * k + a],
                device_id=(px, py, pc), device_id_type=MESH_ID))
    return copies


def _copy_sems(n):
    remote = n * (N_DEV - 1)
    return [pltpu.SemaphoreType.DMA((remote,)), pltpu.SemaphoreType.DMA((remote,)), pltpu.SemaphoreType.DMA((n,))]


def _weight_copies(win_ref, wout_ref, layer, oin_ref, oout_ref, sems):
    return _to_every_device(lambda to: (win_ref.at[layer], wout_ref.at[layer]), lambda me: (oin_ref.at[me], oout_ref.at[me]), sems)


def _gathered_shapes(win_bf, wout_bf):
    return [jax.ShapeDtypeStruct((N_DEV,) + win_bf.shape[1:], BF16), jax.ShapeDtypeStruct((N_DEV,) + wout_bf.shape[1:], BF16)]


def _gather_weights(win_bf, wout_bf, layer):
    def body(win_ref, wout_ref, oin_ref, oout_ref, *sems):
        copies = _weight_copies(win_ref, wout_ref, layer, oin_ref, oout_ref, sems)
        for cp in copies:
            cp.start()
        for cp in copies:
            cp.wait()

    any_spec = pl.BlockSpec(memory_space=pl.ANY)
    return pl.pallas_call(
        body, name="gather_weights", in_specs=[any_spec, any_spec], out_specs=[any_spec, any_spec],
        out_shape=_gathered_shapes(win_bf, wout_bf), scratch_shapes=_copy_sems(2),
    )(win_bf, wout_bf)


_F32_ROUTES = {1: ((256, 256, 0, 0),), 2: ((0, 512, 1, 0),), 3: ((0, 512, 1, 512),), 4: ((0, 512, 1, 1024),),
               5: ((0, 512, 1, 1536),), 6: ((0, 256, 1, 2048), (256, 256, 0, 256)), 7: ((0, 512, 0, 512),)}


def _norm_proj(x, g, w_l, prefetch=None):
    seq, d = x.shape
    tm = min(256, seq)
    steps = seq // tm

    def body(x_ref, g_ref, w_ref, *rest):
        if prefetch is None:
            pbf_ref, gates_ref, qk_ref, h_ref = rest
        else:
            win_ref, wout_ref, pbf_ref, gates_ref, qk_ref, h_ref, oin_ref, oout_ref, *sems = rest
            copies = lambda: _weight_copies(win_ref, wout_ref, prefetch[2], oin_ref, oout_ref, sems)

            @pl.when(pl.program_id(0) == 0)
            def _():
                for cp in copies():
                    cp.start()

        xf = x_ref[...]
        rs = lax.rsqrt(jnp.mean(xf * xf, axis=-1, keepdims=True) + EPS)
        h = (xf * rs * g_ref[...]).astype(BF16)
        h_ref[...] = h
        targets = (gates_ref, qk_ref)
        for n in range(N_DEV):
            acc = _dot(h, w_ref[n])
            pbf_ref[:, n * COLS_PER_DEV:(n + 1) * COLS_PER_DEV] = acc.astype(BF16)
            for lo, width, tgt, dst in _F32_ROUTES.get(n, ()):
                targets[tgt][:, dst:dst + width] = acc[:, lo:lo + width]

        if prefetch is not None:
            @pl.when(pl.program_id(0) == steps - 1)
            def _():
                for cp in copies():
                    cp.wait()

    row = lambda w: pl.BlockSpec((tm, w), lambda i: (i, 0))
    any_spec = pl.BlockSpec(memory_space=pl.ANY)
    in_specs = [row(d), pl.BlockSpec((1, d), lambda i: (0, 0)), pl.BlockSpec((N_DEV, d, COLS_PER_DEV), lambda i: (0, 0, 0))]
    out_specs = [row(IN_COLS), row(MIX), row(3 * W_DIL), row(d)]
    out_shape = [jax.ShapeDtypeStruct((seq, IN_COLS), BF16), jax.ShapeDtypeStruct((seq, MIX), F32),
                 jax.ShapeDtypeStruct((seq, 3 * W_DIL), F32), jax.ShapeDtypeStruct((seq, d), BF16)]
    if prefetch is None:
        return pl.pallas_call(body, name="norm_proj", grid=(steps,), in_specs=in_specs, out_specs=out_specs,
                              out_shape=out_shape, compiler_params=_params(48))(x, g, w_l)
    return pl.pallas_call(
        body, name="norm_proj_gather", grid=(steps,), in_specs=in_specs + [any_spec, any_spec],
        out_specs=out_specs + [any_spec, any_spec], out_shape=out_shape + _gathered_shapes(*prefetch[:2]),
        scratch_shapes=_copy_sems(2), compiler_params=_params(48),
    )(x, g, w_l, *prefetch[:2])


def _head_sums(v, bd):
    hi = v.astype(BF16)
    lo = (v - hi.astype(F32)).astype(BF16)
    return _dot(hi, bd) + _dot(lo, bd)


def _qk_prep(qk_raw, gains, tables, bd):
    seq = qk_raw.shape[0]
    tm = min(1024, seq)
    chunks = W_DIL // (2 * LANES)

    def body(x_ref, g_ref, c_ref, sn_ref, sp_ref, bd_ref, o_ref):
        j = pl.program_id(1)
        x = x_ref[...]
        rs = lax.rsqrt(_head_sums(x * x, bd_ref[...]) * (1.0 / HEAD_DIM) + EPS)
        y = _rope(x * rs * g_ref[...], c_ref[...], sn_ref[...], sp_ref[...])
        o_ref[...] = y * jnp.where(j < chunks, QK_SCALE, 1.0)

    tab = pl.BlockSpec((tm, 2 * LANES), lambda i, j: (i, 0))
    return pl.pallas_call(
        body, name="qk_prep", grid=(seq // tm, 2 * chunks),
        in_specs=[pl.BlockSpec((tm, 2 * LANES), lambda i, j: (i, j)),
                  pl.BlockSpec((None, 1, 2 * LANES), lambda i, j: (j // chunks, 0, 0)),
                  tab, tab, tab, pl.BlockSpec((2 * LANES, 2 * LANES), lambda i, j: (0, 0))],
        out_specs=pl.BlockSpec((tm, 2 * LANES), lambda i, j: (i, j)),
        out_shape=jax.ShapeDtypeStruct((seq, 2 * W_DIL), F32),
        compiler_params=_params(32),
    )(qk_raw, gains, *tables, bd)


def _stack_heads(x, lo):
    return jnp.concatenate([jnp.where(lo, x, 0.0), jnp.where(lo, 0.0, x)], axis=0)


def _unstack_heads(y, lo):
    return jnp.where(lo, y[:BLOCK], y[BLOCK:])


def _stacked_causal():
    row = lax.broadcasted_iota(jnp.int32, (2 * BLOCK, BLOCK), 0) & (BLOCK - 1)
    return lax.broadcasted_iota(jnp.int32, (2 * BLOCK, BLOCK), 1) < row


def _keep(x, *conds):
    for cond in conds:
        if cond is not None:
            x = jnp.where(cond, x, 0.0)
    return x


def _sb_weights(chains, tri):
    zs = [[_dot_nt(qs, kb) for kb in kbs] for qs, kbs, _, _, _ in chains]
    lss = [[jnp.minimum(z, 0.0) - jnp.log1p(jnp.exp(-jnp.abs(z))) for z in zc] for zc in zs]
    cts = [[_dot_exact(_keep(ls - z, mask, live), tri, SB_SUM_TERMS) for ls, z, mask, live in zip(lsc, zc, masks, lives)]
           for lsc, zc, (_, _, _, masks, lives) in zip(lss, zs, chains)]
    out = []
    for lsc, ctc, (_, _, r, masks, lives) in zip(lss, cts, chains):
        weights = []
        for ls, ct, mask, live in zip(lsc, ctc, masks, lives):
            weights.append(_keep(jnp.exp(ls + ct[:, :BLOCK] + r), mask, live))
            r = r + ct[:, BLOCK:]
        out.append((lsc, weights, r))
    return out


def _sb_peel(i, causal):
    return ([jnp.maximum(i - k, 0) for k in range(SB_PEEL)], [causal] + [None] * (SB_PEEL - 1),
            [None] + [i >= k for k in range(1, SB_PEEL)])


def _block_rows(j):
    return pl.ds(pl.multiple_of(j * BLOCK, BLOCK), BLOCK)


def _sb_fwd(proj_bf, tri_a):
    seq = proj_bf.shape[0]
    pairs = W_SB // LANES

    def body(q_ref, k_ref, v_ref, tri_ref, o_ref, r_ref, acc_ref):
        t = pl.program_id(1)
        lo = _lane_lo((BLOCK, LANES))
        causal = _stacked_causal()
        tri = tri_ref[...]
        qblocks = [t * SB_QBLOCKS + c for c in range(SB_QBLOCKS)]
        qss = [_stack_heads(q_ref[c * BLOCK:(c + 1) * BLOCK, :].astype(F32) * QK_SCALE, lo).astype(BF16)
               for c in range(SB_QBLOCKS)]

        def blocks(specs):
            rows = [[_block_rows(j) for j in js] for _, js, _, _, _ in specs]
            res = _sb_weights([(qs, [k_ref[rw, :] for rw in rws], r, masks, lives)
                               for (qs, _, r, masks, lives), rws in zip(specs, rows)], tri)
            outs = []
            for (_, weights, r), rws in zip(res, rows):
                out = 0.0
                for a, rw in zip(weights, rws):
                    a_hi = a.astype(BF16)
                    a_lo = (a - a_hi.astype(F32)).astype(BF16)
                    vb = v_ref[rw, :]
                    out = out + _dot(a_hi, vb) + _dot(a_lo, vb)
                outs.append((out, r))
            return outs

        def peel(qs, i):
            js, masks, lives = _sb_peel(i, causal)
            return qs, js, jnp.zeros((2 * BLOCK, LANES), F32), masks, lives

        for c, (out, r) in enumerate(blocks([peel(qs, i) for qs, i in zip(qss, qblocks)])):
            acc_ref[c], r_ref[c] = out, r

        for c in range(SB_QBLOCKS):
            def alive(c=c):
                return (jnp.max(r_ref[c]) > DEAD_LOG).astype(jnp.int32)

            def step(carry, c=c):
                (out, r_ref[c]), = blocks([(qss[c], [carry[0]], r_ref[c], [None], [None])])
                acc_ref[c] += out
                return carry[0] - 1, alive(c)

            lax.while_loop(lambda carry: jnp.logical_and(carry[0] >= 0, carry[1] > 0), step, (qblocks[c] - SB_PEEL, alive()))
            o_ref[c * BLOCK:(c + 1) * BLOCK, :] = _unstack_heads(acc_ref[c], lo)

    qtile = SB_QBLOCKS * BLOCK
    state = pltpu.VMEM((SB_QBLOCKS, 2 * BLOCK, LANES), F32)
    return pl.pallas_call(
        body, name="sb_fwd", grid=(pairs, seq // qtile),
        in_specs=[pl.BlockSpec((qtile, LANES), lambda p, t: (t, p)),
                  pl.BlockSpec((seq, LANES), lambda p, t: (0, pairs + p)),
                  pl.BlockSpec((seq, LANES), lambda p, t: (0, 2 * pairs + p)),
                  pl.BlockSpec((BLOCK, 2 * BLOCK), lambda p, t: (0, 0))],
        out_specs=pl.BlockSpec((qtile, LANES), lambda p, t: (t, p)),
        out_shape=jax.ShapeDtypeStruct((seq, W_SB), F32),
        scratch_shapes=[state, state],
        compiler_params=_params(40),
    )(proj_bf, proj_bf, proj_bf, tri_a)


def _class_rows(c, first, count, r):
    start = c + first * r
    return pl.ds(start, count) if r == 1 else pl.ds(start, count, stride=r)


def _dil_tiling(seq, r):
    tile = min(DIL_TILE, seq)
    edge = BLOCK * r
    return tile, edge, tile // edge


def _band_mask(n_rows, n_keys):
    row = lax.broadcasted_iota(jnp.int32, (n_rows, n_keys), 0) & (BLOCK - 1)
    col = lax.broadcasted_iota(jnp.int32, (n_rows, n_keys), 1)
    return jnp.logical_and(col >= row, col <= row + BLOCK), col


def _for_each_group(r, per, g, run):
    loop = lambda lo, hi, fn: lax.fori_loop(lo, hi, lambda t, carry: (fn(t), carry)[1], 0)
    if per <= g:
        n_cls = g // per
        assert g % per == 0 and r % n_cls == 0
        group = lambda t: [(t * n_cls + u, a, a == 0) for u in range(n_cls) for a in range(per)]
        if r == n_cls:
            run(group(0))
        else:
            loop(0, r // n_cls, lambda t: run(group(t)))
        return

    assert per % g == 0

    def one_class(c):
        run([(c, 0, True)] + [(c, a, False) for a in range(1, g)])
        loop(1, per // g, lambda t: run([(c, t * g + u, False) for u in range(g)]))

    if r == 1:
        one_class(0)
    else:
        loop(0, r, one_class)


def _dil_keys(blocks, r, cur_ref, before_ref):
    out = []
    for c, a, first in blocks:
        if first:
            rows = _class_rows(c, 0, BLOCK, r)
            out.append(jnp.concatenate([before_ref[rows, :], cur_ref[rows, :]], axis=0).astype(BF16))
        else:
            out.append(cur_ref[_class_rows(c, (a - 1) * BLOCK, 2 * BLOCK, r), :].astype(BF16))
    return out


def _dil_fwd(qkn, qkv_raw, r):
    seq = qkn.shape[0]
    tile, edge, per = _dil_tiling(seq, r)
    pairs = W_DIL // LANES

    def body(q_ref, kc_ref, kp_ref, vc_ref, vp_ref, o_ref, ld_ref):
        n = pl.program_id(1)
        lo = _lane_lo((BLOCK, LANES))
        band, col = _band_mask(2 * BLOCK, 2 * BLOCK)
        first_band = jnp.logical_and(band, jnp.logical_or(col >= BLOCK, n > 0))

        def run(blocks):
            rows = [_class_rows(c, a * BLOCK, BLOCK, r) for c, a, _ in blocks]
            keys = _dil_keys(blocks, r, kc_ref, kp_ref)
            vals = _dil_keys(blocks, r, vc_ref, vp_ref)
            scores = [_dot_nt(_stack_heads(q_ref[rw, :], lo).astype(BF16), kb) for rw, kb in zip(rows, keys)]
            probs, sums, lds = [], [], []
            for s, (_, _, first) in zip(scores, blocks):
                s = jnp.where(first_band if first else band, s, NEG_BIG)
                m = jnp.max(s, axis=1, keepdims=True)
                p = jnp.exp(s - m)
                l = jnp.sum(p, axis=1, keepdims=True)
                probs.append(p.astype(BF16))
                sums.append(l)
                lds.append(m + jnp.log(l))
            outs = [_dot(p, vb) / l for p, vb, l in zip(probs, vals, sums)]
            for rw, o, ld in zip(rows, outs, lds):
                o_ref[rw, :] = _unstack_heads(o, lo)
                ld_ref[rw, :] = _unstack_heads(jnp.broadcast_to(ld, (2 * BLOCK, LANES)), lo)

        _for_each_group(r, per, DIL_GROUP_FWD, run)

    per_edge = tile // edge
    cur = lambda off: pl.BlockSpec((tile, LANES), lambda p, n: (n, off + p))
    before = lambda off: pl.BlockSpec((edge, LANES), lambda p, n: (jnp.maximum(n * per_edge - 1, 0), off + p))
    return pl.pallas_call(
        body, name=f"dil_fwd_r{r}", grid=(pairs, seq // tile),
        in_specs=[cur(0), cur(pairs), before(pairs), cur(2 * pairs), before(2 * pairs)],
        out_specs=[cur(0), cur(0)],
        out_shape=[jax.ShapeDtypeStruct((seq, W_DIL), F32), jax.ShapeDtypeStruct((seq, W_DIL), F32)],
        compiler_params=_params(32),
    )(qkn, qkn, qkn, qkv_raw, qkv_raw)


def _silu_parts(g):
    sig = jax.nn.sigmoid(g)
    return g * sig, sig * (1.0 + g * (1.0 - sig))


def _out_proj(x, oa, o_p, ld_p, gates, wout_l):
    seq, d = x.shape
    tm = min(256, seq)

    def body(x_ref, oa_ref, o0, o1, o2, l0, l1, l2, g_ref, w_ref, xn_ref, cat_ref, od_ref, lse_ref):
        lds = (l0[...], l1[...], l2[...])
        m = jnp.maximum(jnp.maximum(lds[0], lds[1]), lds[2])
        es = [jnp.exp(v - m) for v in lds]
        tot = es[0] + es[1] + es[2]
        lse_ref[...] = m + jnp.log(tot)
        inv = 1.0 / tot
        od = (es[0] * inv) * o0[...] + (es[1] * inv) * o1[...] + (es[2] * inv) * o2[...]
        od_ref[...] = od
        silu, _ = _silu_parts(g_ref[...])
        cat_ref[:, :W_SB] = (oa_ref[...] * silu[:, :W_SB]).astype(BF16)
        cat_ref[:, W_SB:] = (od * silu[:, W_SB:]).astype(BF16)
        y = x_ref[...]
        for b in range(N_DEV):
            y = y + _dot(cat_ref[:, b * ROWS_PER_DEV:(b + 1) * ROWS_PER_DEV], w_ref[b])
        xn_ref[...] = y

    row = lambda w: pl.BlockSpec((tm, w), lambda i: (i, 0))
    return pl.pallas_call(
        body, name="out_proj", grid=(seq // tm,),
        in_specs=[row(d), row(W_SB)] + [row(W_DIL)] * 6 + [row(MIX),
                  pl.BlockSpec((N_DEV, ROWS_PER_DEV, d), lambda i: (0, 0, 0))],
        out_specs=[row(d), row(MIX), row(W_DIL), row(W_DIL)],
        out_shape=[jax.ShapeDtypeStruct((seq, d), F32), jax.ShapeDtypeStruct((seq, MIX), BF16),
                   jax.ShapeDtypeStruct((seq, W_DIL), F32), jax.ShapeDtypeStruct((seq, W_DIL), F32)],
        compiler_params=_params(48),
    )(x, oa, *o_p, *ld_p, gates, wout_l)


def _loss_head(y, target):
    seq, d = y.shape
    tm = min(512, seq)

    def body(y_ref, t_ref, part_ref, dy_ref):
        @pl.when(pl.program_id(0) == 0)
        def _():
            part_ref[...] = jnp.zeros_like(part_ref)

        diff = y_ref[...] - t_ref[...]
        dy_ref[...] = diff * (1.0 / d)
        part_ref[...] += jnp.sum((diff * diff).reshape(tm // 8, 8, d), axis=0) * (0.5 / d)

    row = pl.BlockSpec((tm, d), lambda i: (i, 0))
    return pl.pallas_call(
        body, name="loss_head", grid=(seq // tm,),
        in_specs=[row, row], out_specs=[pl.BlockSpec((8, d), lambda i: (0, 0)), row],
        out_shape=[jax.ShapeDtypeStruct((8, d), F32), jax.ShapeDtypeStruct((seq, d), F32)],
        compiler_params=_params(32),
    )(y, target)


def _out_proj_bwd(dy, wout_l, cat, gates, oa, od, bd):
    seq, d = dy.shape
    tm = min(256, seq)

    def body(dy_ref, w_ref, cat_ref, g_ref, oa_ref, od_ref, bd_ref, doa_ref, dod_ref, delta_ref, dg_ref, dw_ref, dcat):
        @pl.when(pl.program_id(0) == 0)
        def _():
            dw_ref[...] = jnp.zeros_like(dw_ref)

        dyb = dy_ref[...].astype(BF16)
        dw = _dot_tn(cat_ref[...], dyb)
        for b in range(N_DEV):
            dw_ref[b] += dw[b * ROWS_PER_DEV:(b + 1) * ROWS_PER_DEV, :]
            dcat[:, b * ROWS_PER_DEV:(b + 1) * ROWS_PER_DEV] = _dot_nt(dyb, w_ref[b])
        silu, dsilu = _silu_parts(g_ref[...])
        dc = dcat[...]
        dmix = dc * silu
        oa_v, od_v = oa_ref[...], od_ref[...]
        dg_ref[:, :W_SB] = dc[:, :W_SB] * oa_v * dsilu[:, :W_SB]
        dg_ref[:, W_SB:] = dc[:, W_SB:] * od_v * dsilu[:, W_SB:]
        doa_ref[...] = dmix[:, :W_SB]
        dod = dmix[:, W_SB:]
        dod_ref[...] = dod
        prod = dod * od_v
        for k in range(W_DIL // (2 * LANES)):
            sl = slice(k * 2 * LANES, (k + 1) * 2 * LANES)
            delta_ref[:, sl] = _head_sums(prod[:, sl], bd_ref[...])

    row = lambda w: pl.BlockSpec((tm, w), lambda i: (i, 0))
    slab = pl.BlockSpec((N_DEV, ROWS_PER_DEV, d), lambda i: (0, 0, 0))
    return pl.pallas_call(
        body, name="out_proj_bwd", grid=(seq // tm,),
        in_specs=[row(d), slab, row(MIX), row(MIX), row(W_SB), row(W_DIL),
                  pl.BlockSpec((2 * LANES, 2 * LANES), lambda i: (0, 0))],
        out_specs=[row(W_SB), row(W_DIL), row(W_DIL), row(MIX), slab],
        out_shape=[jax.ShapeDtypeStruct((seq, W_SB), F32), jax.ShapeDtypeStruct((seq, W_DIL), F32),
                   jax.ShapeDtypeStruct((seq, W_DIL), F32), jax.ShapeDtypeStruct((seq, MIX), F32),
                   jax.ShapeDtypeStruct((N_DEV, ROWS_PER_DEV, d), F32)],
        scratch_shapes=[pltpu.VMEM((tm, MIX), F32)],
        compiler_params=_params(48),
    )(dy, wout_l, cat, gates, oa, od, bd)


def _sb_bwd(proj_bf, d_oa, oa, tri_a, tri_b):
    seq = proj_bf.shape[0]
    pairs = W_SB // LANES
    nq = seq // BLOCK

    def body(q_ref, k_ref, v_ref, do_ref, o_ref, tria_ref, trib_ref, dq_ref, dk_hbm, dv_hbm,
             r_ref, sfx_ref, dtot_ref, dq_acc, dk_acc, dv_acc, sems):
        p, t = pl.program_id(0), pl.program_id(1)

        @pl.when(t == 0)
        def _():
            dk_acc[...] = jnp.zeros_like(dk_acc)
            dv_acc[...] = jnp.zeros_like(dv_acc)

        lo = _lane_lo((BLOCK, LANES))
        tri, trib = tria_ref[...], trib_ref[...]
        causal = _stacked_causal()
        qblocks = [t * SB_QBLOCKS + c for c in range(SB_QBLOCKS)]
        qss, doss = [], []
        for c in range(SB_QBLOCKS):
            sl = slice(c * BLOCK, (c + 1) * BLOCK)
            qss.append(_stack_heads(q_ref[sl, :].astype(F32) * QK_SCALE, lo).astype(BF16))
            doss.append(_stack_heads(do_ref[sl, :], lo).astype(BF16))
            o2 = o_ref[sl, :]
            dtot_ref[c] = _dot_exact(doss[c].astype(F32) * jnp.concatenate([o2, o2], axis=0), tri[:, BLOCK:])

        def blocks(specs):
            rows = [[_block_rows(j) for j in js] for _, js, _, _, _, _ in specs]
            kbs = [[k_ref[rw, :] for rw in rws] for rws in rows]
            dovs = [[_dot_nt(doss[c], v_ref[rw, :]) for rw in rws] for (c, *_), rws in zip(specs, rows)]
            res = _sb_weights([(qss[c], kbc, r, masks, lives) for (c, _, r, _, masks, lives), kbc in zip(specs, kbs)], tri)
            pws = [[a * dov for a, dov in zip(weights, dovc)] for (_, weights, _), dovc in zip(res, dovs)]
            cps = [[_dot_exact(pw, trib, SB_SUM_TERMS) for pw in pwc] for pwc in pws]
            dzs, sfxs = [], []
            for (c, _, _, sfx, masks, lives), (lsc, _, _), pwc, cpc in zip(specs, res, pws, cps):
                dzc = []
                for ls, pw, cp, mask, live in zip(lsc, pwc, cpc, masks, lives):
                    beta = jnp.exp(ls)
                    before = dtot_ref[c] - sfx - cp[:, :BLOCK]
                    dzc.append(_keep(pw * (1.0 - beta) - before * beta, mask, live).astype(BF16))
                    sfx = sfx + cp[:, BLOCK:]
                dzs.append(dzc)
                sfxs.append(sfx)
            dqs = []
            for dzc, kbc in zip(dzs, kbs):
                dq = 0.0
                for dzb, kb in zip(dzc, kbc):
                    dq = dq + _dot(dzb, kb)
                dqs.append(dq)
            for (c, *_), dzc, (_, weights, _), rws in zip(specs, dzs, res, rows):
                for dzb, a, rw in zip(dzc, weights, rws):
                    dk_acc[rw, :] += _dot_tn(dzb, qss[c])
                    dv_acc[rw, :] += _dot_tn(a.astype(BF16), doss[c])
            return [(dq, r, sfx) for dq, (_, _, r), sfx in zip(dqs, res, sfxs)]

        def peel(c, i):
            js, masks, lives = _sb_peel(i, causal)
            zero = jnp.zeros((2 * BLOCK, LANES), F32)
            return c, js, zero, zero, masks, lives

        for c, (dq, r, sfx) in enumerate(blocks([peel(c, i) for c, i in enumerate(qblocks)])):
            dq_acc[c], r_ref[c], sfx_ref[c] = dq, r, sfx

        for c in range(SB_QBLOCKS):
            def alive(c=c):
                return (jnp.max(r_ref[c]) > DEAD_LOG).astype(jnp.int32)

            def step(carry, c=c):
                (dq, r_ref[c], sfx_ref[c]), = blocks([(c, [carry[0]], r_ref[c], sfx_ref[c], [None], [None])])
                dq_acc[c] += dq
                return carry[0] - 1, alive(c)

            lax.while_loop(lambda carry: jnp.logical_and(carry[0] >= 0, carry[1] > 0), step, (qblocks[c] - SB_PEEL, alive()))
            dq_ref[c * BLOCK:(c + 1) * BLOCK, :] = _unstack_heads(dq_acc[c], lo) * QK_SCALE

        @pl.when(t == nq // SB_QBLOCKS - 1)
        def _():
            outs = [pltpu.make_async_copy(dk_acc, dk_hbm.at[p], sems.at[0]),
                    pltpu.make_async_copy(dv_acc, dv_hbm.at[p], sems.at[1])]
            for cp in outs:
                cp.start()
            for cp in outs:
                cp.wait()

    qtile = SB_QBLOCKS * BLOCK
    blk = pl.BlockSpec((qtile, LANES), lambda p, t: (t, p))
    const = pl.BlockSpec((BLOCK, 2 * BLOCK), lambda p, t: (0, 0))
    any_spec = pl.BlockSpec(memory_space=pl.ANY)
    state = pltpu.VMEM((SB_QBLOCKS, 2 * BLOCK, LANES), F32)
    return pl.pallas_call(
        body, name="sb_bwd", grid=(pairs, seq // qtile),
        in_specs=[blk, pl.BlockSpec((seq, LANES), lambda p, t: (0, pairs + p)),
                  pl.BlockSpec((seq, LANES), lambda p, t: (0, 2 * pairs + p)), blk, blk, const, const],
        out_specs=[blk, any_spec, any_spec],
        out_shape=[jax.ShapeDtypeStruct((seq, W_SB), F32), jax.ShapeDtypeStruct((pairs, seq, LANES), F32),
                   jax.ShapeDtypeStruct((pairs, seq, LANES), F32)],
        scratch_shapes=[state, state, state, state, pltpu.VMEM((seq, LANES), F32), pltpu.VMEM((seq, LANES), F32),
                        pltpu.SemaphoreType.DMA((2,))],
        compiler_params=_params(56),
    )(proj_bf, proj_bf, proj_bf, d_oa, oa, tri_a, tri_b)


def _dil_bwd(qkn, qkv_raw, d_od, lse, delta, r):
    seq = qkn.shape[0]
    tile, edge, per = _dil_tiling(seq, r)
    ntile = seq // tile
    pairs = W_DIL // LANES

    def body(qc, doc, lsc, dlc, kc, kp, vc, vp, dq_ref, dk_ref, dv_ref, dk_carry, dv_carry):
        n = pl.program_id(1)

        @pl.when(n == 0)
        def _():
            dk_carry[...] = jnp.zeros_like(dk_carry)
            dv_carry[...] = jnp.zeros_like(dv_carry)

        dk_ref[...] = dk_carry[...]
        dv_ref[...] = dv_carry[...]

        @pl.when(n < ntile)
        def _():
            lo = _lane_lo((BLOCK, LANES))
            band, col = _band_mask(2 * BLOCK, 2 * BLOCK)
            first_band = jnp.logical_and(band, jnp.logical_or(col >= BLOCK, n > 0))

            def stacked_cols(b):
                return jnp.concatenate([_col_of(b, lo, 0), _col_of(b, lo, 1)], axis=0)

            def run(blocks):
                rows = [_class_rows(c, a * BLOCK, BLOCK, r) for c, a, _ in blocks]
                keys = _dil_keys(blocks, r, kc, kp)
                vals = _dil_keys(blocks, r, vc, vp)
                qss = [_stack_heads(qc[rw, :], lo).astype(BF16) for rw in rows]
                doss = [_stack_heads(doc[rw, :], lo).astype(BF16) for rw in rows]
                scores = [_dot_nt(qs, kb) for qs, kb in zip(qss, keys)]
                dps = [_dot_nt(dos, vb) for dos, vb in zip(doss, vals)]
                pws, dss = [], []
                for rw, s, dp, (_, _, first) in zip(rows, scores, dps, blocks):
                    pw = jnp.where(first_band if first else band, jnp.exp(s - stacked_cols(lsc[rw, :])), 0.0)
                    pws.append(pw.astype(BF16))
                    dss.append((pw * (dp - stacked_cols(dlc[rw, :]))).astype(BF16))
                dqs = [_dot(ds, kb) for ds, kb in zip(dss, keys)]
                dks = [_dot_tn(ds, qs) for ds, qs in zip(dss, qss)]
                dvs = [_dot_tn(pw, dos) for pw, dos in zip(pws, doss)]
                for (c, a, first), rw, dq, dk, dv in zip(blocks, rows, dqs, dks, dvs):
                    dq_ref[rw, :] = _unstack_heads(dq, lo)
                    if first:
                        last = _class_rows(c, (per - 1) * BLOCK, BLOCK, r)
                        dk_ref[last, :] += dk[:BLOCK]
                        dv_ref[last, :] += dv[:BLOCK]
                    else:
                        prev = _class_rows(c, (a - 1) * BLOCK, BLOCK, r)
                        dk_carry[prev, :] += dk[:BLOCK]
                        dv_carry[prev, :] += dv[:BLOCK]
                    dk_carry[rw, :] = dk[BLOCK:]
                    dv_carry[rw, :] = dv[BLOCK:]

            _for_each_group(r, per, DIL_GROUP_BWD, run)

    per_edge = tile // edge
    here = lambda n: jnp.minimum(n, ntile - 1)
    cur = lambda off: pl.BlockSpec((tile, LANES), lambda p, n: (here(n), off + p))
    before = lambda off: pl.BlockSpec((edge, LANES), lambda p, n: (jnp.maximum(here(n) * per_edge - 1, 0), off + p))
    lagged = pl.BlockSpec((tile, LANES), lambda p, n: (jnp.maximum(n - 1, 0), p))
    carry = pltpu.VMEM((tile, LANES), F32)
    return pl.pallas_call(
        body, name=f"dil_bwd_r{r}", grid=(pairs, ntile + 1),
        in_specs=[cur(0)] * 4 + [cur(pairs), before(pairs), cur(2 * pairs), before(2 * pairs)],
        out_specs=[cur(0), lagged, lagged],
        out_shape=[jax.ShapeDtypeStruct((seq, W_DIL), F32)] * 3,
        scratch_shapes=[carry, carry],
        compiler_params=_params(40),
    )(qkn, d_od, lse, delta, qkn, qkn, qkv_raw, qkv_raw)


def _assemble_dproj(d_qa, d_ka, d_va, d_gates, dq_p, dk_p, dv_p, qk_raw, gains, tables, bd):
    seq = qk_raw.shape[0]
    tm = min(256, seq)
    chunks = W_DIL // (2 * LANES)

    def body(dqa, dka0, dka1, dva0, dva1, dg, q0, q1, q2, k0, k1, k2, v0, v1, v2, x_ref, g_ref, c_ref, sn_ref, sp_ref,
             bd_ref, dp_ref, gpart_ref):
        @pl.when(pl.program_id(0) == 0)
        def _():
            gpart_ref[...] = jnp.zeros_like(gpart_ref)

        def put(first_col, v):
            dp_ref[:, first_col:first_col + v.shape[1]] = v.astype(BF16)

        put(0, dqa[...])
        put(W_SB, dka0[...])
        put(W_SB + LANES, dka1[...])
        put(2 * W_SB, dva0[...])
        put(2 * W_SB + LANES, dva1[...])
        put(3 * W_SB, dg[:, :W_SB])
        put(4 * W_SB + 2 * W_DIL, v0[...] + v1[...] + v2[...])
        put(4 * W_SB + 3 * W_DIL, dg[:, W_SB:])
        c, sn, sp, bdm = c_ref[...], sn_ref[...], sp_ref[...], bd_ref[...]
        for which, parts in enumerate(((q0, q1, q2), (k0, k1, k2))):
            scale = QK_SCALE if which == 0 else 1.0
            for k in range(chunks):
                sl = slice(k * 2 * LANES, (k + 1) * 2 * LANES)
                dyv = (parts[0][:, sl] + parts[1][:, sl] + parts[2][:, sl]) * scale
                dxn = _rope_t(dyv, c, sn, sp)
                x = x_ref[:, which * W_DIL + k * 2 * LANES:which * W_DIL + (k + 1) * 2 * LANES]
                rs = lax.rsqrt(_head_sums(x * x, bdm) * (1.0 / HEAD_DIM) + EPS)
                xhat = x * rs
                gpart_ref[which] += jnp.sum((dxn * xhat).reshape(tm // 8, 8, 2 * LANES), axis=0)
                dxhat = dxn * g_ref[which]
                mean = _head_sums(dxhat * xhat, bdm) * (1.0 / HEAD_DIM)
                put(4 * W_SB + which * W_DIL + k * 2 * LANES, rs * (dxhat - xhat * mean))

    row = lambda w: pl.BlockSpec((tm, w), lambda i: (i, 0))
    pair = lambda p: pl.BlockSpec((None, tm, LANES), lambda i: (p, i, 0))
    const = lambda shape: pl.BlockSpec(shape, lambda i: tuple(0 for _ in shape))
    return pl.pallas_call(
        body, name="assemble_dproj", grid=(seq // tm,),
        in_specs=[row(W_SB), pair(0), pair(1), pair(0), pair(1), row(MIX)] + [row(W_DIL)] * 9 +
                 [row(2 * W_DIL), const((2, 1, 2 * LANES)), row(2 * LANES), row(2 * LANES), row(2 * LANES),
                  const((2 * LANES, 2 * LANES))],
        out_specs=[row(IN_COLS), const((2, 8, 2 * LANES))],
        out_shape=[jax.ShapeDtypeStruct((seq, IN_COLS), BF16), jax.ShapeDtypeStruct((2, 8, 2 * LANES), F32)],
        compiler_params=_params(48),
    )(d_qa, d_ka, d_ka, d_va, d_va, d_gates, *dq_p, *dk_p, *dv_p, qk_raw, gains, *tables, bd)


def _dw_in(h, dproj):
    seq, d = h.shape
    tm = min(2048, seq)

    def body(h_ref, dp_ref, dw_ref):
        @pl.when(pl.program_id(1) == 0)
        def _():
            dw_ref[...] = jnp.zeros_like(dw_ref)

        dw_ref[...] += _dot_tn(h_ref[...], dp_ref[...])

    return pl.pallas_call(
        body, name="dw_in", grid=(N_DEV, seq // tm),
        in_specs=[pl.BlockSpec((tm, d), lambda n, i: (i, 0)), pl.BlockSpec((tm, COLS_PER_DEV), lambda n, i: (i, n))],
        out_specs=pl.BlockSpec((None, d, COLS_PER_DEV), lambda n, i: (n, 0, 0)),
        out_shape=jax.ShapeDtypeStruct((N_DEV, d, COLS_PER_DEV), F32),
        compiler_params=_params(40),
    )(h, dproj)


def _dx_norm(dproj, w_l, x, g, dx_next, dwin_l, dwout_l, rin, rout, layer):
    seq, d = x.shape
    tm = min(256, seq)
    steps = seq // tm

    def body(dp_ref, w_ref, x_ref, g_ref, dn_ref, dwin_ref, dwout_ref, rin_in, rout_in, dx_ref, gpart_ref, rin_ref, rout_ref, *sems):
        del rin_in, rout_in
        copies = lambda: _to_every_device(lambda to: (dwin_ref.at[to], dwout_ref.at[to]),
                                          lambda me: (rin_ref.at[me, layer], rout_ref.at[me, layer]), sems)

        @pl.when(pl.program_id(0) == 0)
        def _():
            gpart_ref[...] = jnp.zeros_like(gpart_ref)
            for cp in copies():
                cp.start()

        dh = jnp.zeros((tm, d), F32)
        for n in range(N_DEV):
            dh = dh + _dot_nt(dp_ref[:, n * COLS_PER_DEV:(n + 1) * COLS_PER_DEV], w_ref[n])
        xf = x_ref[...]
        rs = lax.rsqrt(jnp.mean(xf * xf, axis=-1, keepdims=True) + EPS)
        xhat = xf * rs
        gpart_ref[...] += jnp.sum((dh * xhat).reshape(tm // 8, 8, d), axis=0)
        dxhat = dh * g_ref[...]
        mean = jnp.mean(dxhat * xhat, axis=-1, keepdims=True)
        dx_ref[...] = rs * (dxhat - xhat * mean) + dn_ref[...]

        @pl.when(pl.program_id(0) == steps - 1)
        def _():
            for cp in copies():
                cp.wait()

    row = lambda w: pl.BlockSpec((tm, w), lambda i: (i, 0))
    any_spec = pl.BlockSpec(memory_space=pl.ANY)
    return pl.pallas_call(
        body, name="dx_norm_exchange", grid=(steps,),
        in_specs=[row(IN_COLS), pl.BlockSpec((N_DEV, d, COLS_PER_DEV), lambda i: (0, 0, 0)), row(d),
                  pl.BlockSpec((1, d), lambda i: (0, 0)), row(d), any_spec, any_spec, any_spec, any_spec],
        out_specs=[row(d), pl.BlockSpec((8, d), lambda i: (0, 0)), any_spec, any_spec],
        out_shape=[jax.ShapeDtypeStruct((seq, d), F32), jax.ShapeDtypeStruct((8, d), F32),
                   jax.ShapeDtypeStruct(rin.shape, F32), jax.ShapeDtypeStruct(rout.shape, F32)],
        scratch_shapes=_copy_sems(2), input_output_aliases={7: 2, 8: 3},
        compiler_params=_params(48),
    )(dproj, w_l, x, g, dx_next, dwin_l, dwout_l, rin, rout)


def _exchange_small(small):
    def body(small_ref, out_ref, *sems):
        copies = _to_every_device(lambda to: (small_ref,), lambda me: (out_ref.at[me],), sems)
        for cp in copies:
            cp.start()
        for cp in copies:
            cp.wait()

    vmem = pl.BlockSpec(memory_space=pltpu.VMEM)
    return pl.pallas_call(
        body, name="exchange_small", in_specs=[vmem], out_specs=vmem,
        out_shape=jax.ShapeDtypeStruct((N_DEV,) + small.shape, F32), scratch_shapes=_copy_sems(1),
    )(small)


def _adamw_math(g, w, m, v):
    m = ADAM_B1 * m + (1.0 - ADAM_B1) * g
    v = ADAM_B2 * v + (1.0 - ADAM_B2) * (g * g)
    m_hat = m / (1.0 - ADAM_B1 ** ADAM_STEP)
    v_hat = v / (1.0 - ADAM_B2 ** ADAM_STEP)
    delta = -ADAM_LR * (m_hat / (jnp.sqrt(v_hat) + ADAM_EPS) + ADAM_WD * w)
    return delta, m, v


def _adamw(parts, w, m, v, name):
    nl, r, c = w.shape
    tr = min(r, (256 * 512) // c)

    def body(p_ref, w_ref, m_ref, v_ref, g_ref, d_ref, nm_ref, nv_ref):
        g = p_ref[0]
        for s in range(1, N_DEV):
            g = g + p_ref[s]
        g_ref[...] = g
        d_ref[...], nm_ref[...], nv_ref[...] = _adamw_math(g, w_ref[...], m_ref[...], v_ref[...])

    blk = pl.BlockSpec((None, tr, c), lambda l, i: (l, i, 0))
    return pl.pallas_call(
        body, name=name, grid=(nl, r // tr),
        in_specs=[pl.BlockSpec((N_DEV, None, tr, c), lambda l, i: (0, l, i, 0)), blk, blk, blk],
        out_specs=[blk] * 4, out_shape=[jax.ShapeDtypeStruct(w.shape, F32)] * 4,
        compiler_params=_params(32),
    )(parts, w, m, v)


def _adamw_small(parts, w, m, v):
    def body(p_ref, w_ref, m_ref, v_ref, g_ref, d_ref, nm_ref, nv_ref):
        g = p_ref[0]
        for s in range(1, N_DEV):
            g = g + p_ref[s]
        g_ref[...] = g
        d_ref[...], nm_ref[...], nv_ref[...] = _adamw_math(g, w_ref[...], m_ref[...], v_ref[...])

    vmem = pl.BlockSpec(memory_space=pltpu.VMEM)
    return pl.pallas_call(
        body, name="adamw_small", in_specs=[vmem] * 4, out_specs=[vmem] * 4,
        out_shape=[jax.ShapeDtypeStruct(w.shape, F32)] * 4,
    )(parts, w, m, v)


def _pack_small(a, b, c):
    pad = jnp.zeros((a.shape[0], SMALL_W - a.shape[1] - b.shape[1] - c.shape[1]), F32)
    return jnp.concatenate([a, b, c, pad], axis=1)


def _unpack_small(t, d):
    return t[:, :d], t[:, d:d + HEAD_DIM], t[:, d + HEAD_DIM:d + 2 * HEAD_DIM]


def kernel(x, norm_g, w_in, q_norm_g, k_norm_g, w_out, loss_target, m_norm_g, m_w_in, m_q_norm_g, m_k_norm_g, m_w_out,
           v_norm_g, v_w_in, v_q_norm_g, v_k_norm_g, v_w_out):
    depth, d, _ = w_in.shape
    seq = x.shape[1]
    tri_a, tri_b, bd = _tri_constants()
    tables = _rope_tables(seq)
    rep = (2 * LANES) // HEAD_DIM

    win_bf, wout_bf = _cast_bf16(w_in, "cast_w_in"), _cast_bf16(w_out, "cast_w_out")
    win_l, wout_l = _gather_weights(win_bf, wout_bf, 0)
    saved = []
    xl = x.reshape(seq, d)
    for layer in range(depth):
        gains = jnp.stack([jnp.tile(q_norm_g[layer], rep), jnp.tile(k_norm_g[layer], rep)])[:, None, :]
        if layer + 1 < depth:
            proj_bf, gates, qk_raw, h, win_next, wout_next = _norm_proj(
                xl, norm_g[layer][None, :], win_l, prefetch=(win_bf, wout_bf, layer + 1))
        else:
            proj_bf, gates, qk_raw, h = _norm_proj(xl, norm_g[layer][None, :], win_l)
            win_next = wout_next = None
        qkn = _qk_prep(qk_raw, gains, tables, bd)
        oa = _sb_fwd(proj_bf, tri_a)
        o_p, ld_p = zip(*[_dil_fwd(qkn, qk_raw, r) for _, r in DIL_PATTERNS])
        x_next, cat, od, lse = _out_proj(xl, oa, o_p, ld_p, gates, wout_l)
        saved.append((xl, gains, proj_bf, gates, qk_raw, h, qkn, oa, cat, od, lse, win_l, wout_l))
        xl, win_l, wout_l = x_next, win_next, wout_next

    loss_part, dx = _loss_head(xl, loss_target.reshape(seq, d))
    loss = lax.psum(jnp.sum(loss_part), ("x", "y", "c"))

    rin = jnp.zeros((N_DEV, depth, d, COLS_PER_DEV), F32)
    rout = jnp.zeros((N_DEV, depth, ROWS_PER_DEV, d), F32)
    g_norm, g_q, g_k = [None] * depth, [None] * depth, [None] * depth
    for layer in reversed(range(depth)):
        xl, gains, proj_bf, gates, qk_raw, h, qkn, oa, cat, od, lse, win_l, wout_l = saved[layer]
        d_oa, d_od, delta, d_gates, dwout_l = _out_proj_bwd(dx, wout_l, cat, gates, oa, od, bd)
        d_qa, d_ka, d_va = _sb_bwd(proj_bf, d_oa, oa, tri_a, tri_b)
        dq_p, dk_p, dv_p = zip(*[_dil_bwd(qkn, qk_raw, d_od, lse, delta, r) for _, r in DIL_PATTERNS])
        dproj, gqk = _assemble_dproj(d_qa, d_ka, d_va, d_gates, dq_p, dk_p, dv_p, qk_raw, gains, tables, bd)
        dwin_l = _dw_in(h, dproj)
        dx, gn, rin, rout = _dx_norm(dproj, win_l, xl, norm_g[layer][None, :], dx, dwin_l, dwout_l, rin, rout, layer)
        g_norm[layer] = jnp.sum(gn, axis=0)
        gqk = jnp.sum(gqk, axis=1).reshape(2, rep, HEAD_DIM).sum(axis=1)
        g_q[layer], g_k[layer] = gqk[0], gqk[1]
    rsmall = _exchange_small(_pack_small(jnp.stack(g_norm), jnp.stack(g_q), jnp.stack(g_k)))

    g_in, d_in, nm_in, nv_in = _adamw(rin, w_in, m_w_in, v_w_in, "adamw_w_in")
    g_out, d_out, nm_out, nv_out = _adamw(rout, w_out, m_w_out, v_w_out, "adamw_w_out")
    small_out = _adamw_small(rsmall, _pack_small(norm_g, q_norm_g, k_norm_g), _pack_small(m_norm_g, m_q_norm_g, m_k_norm_g),
                             _pack_small(v_norm_g, v_q_norm_g, v_k_norm_g))
    (g_n, g_qn, g_kn), (d_n, d_qn, d_kn), (nm_n, nm_qn, nm_kn), (nv_n, nv_qn, nv_kn) = (_unpack_small(t, d) for t in small_out)

    return (loss, dx.reshape(x.shape), g_n, g_in, g_qn, g_kn, g_out, d_n, d_in, d_qn, d_kn, d_out,
            nm_n, nm_in, nm_qn, nm_kn, nm_out, nv_n, nv_in, nv_qn, nv_kn, nv_out)
```

```python
import math

import jax
import jax.numpy as jnp
from jax import lax
from jax.experimental import pallas as pl
from jax.experimental.pallas import tpu as pltpu

F32 = jnp.float32
BF16 = jnp.bfloat16

EPS = 1e-6
HEAD_DIM = 64
BLOCK = 128
LANES = 128
W_SB = 256
W_DIL = 768
MIX = W_SB + W_DIL
IN_COLS = 4 * W_SB + 4 * W_DIL
N_DEV = 8
COLS_PER_DEV = IN_COLS // N_DEV
ROWS_PER_DEV = MIX // N_DEV
QK_SCALE = 1.0 / math.sqrt(HEAD_DIM)
DIL_PATTERNS = ((128, 1), (512, 4), (2048, 16))
DIL_TILE = 2048
DIL_GROUP_FWD = 4
DIL_GROUP_BWD = 4
ROPE_THETA = 500000.0
ROPE_DIM = HEAD_DIM // 4
ROPE_HALF = ROPE_DIM // 2
DEAD_LOG = -110.0
SB_PEEL = 3
SB_QBLOCKS = 2
SB_SUM_TERMS = 2
NEG_BIG = -1e30

ADAM_LR = 0.001
ADAM_B1 = 0.9
ADAM_B2 = 0.999
ADAM_EPS = 1e-08
ADAM_WD = 0.01
ADAM_STEP = 10

SMALL_W = 1280
MESH_ID = pl.DeviceIdType.MESH
MIB = 1 << 20


def _params(vmem_mib):
    return pltpu.CompilerParams(vmem_limit_bytes=vmem_mib * MIB)


def _dot(a, b):
    return jnp.dot(a, b, preferred_element_type=F32)


def _dot_nt(a, b):
    return lax.dot_general(a, b, (((1,), (1,)), ((), ())), preferred_element_type=F32)


def _dot_tn(a, b):
    return lax.dot_general(a, b, (((0,), (0,)), ((), ())), preferred_element_type=F32)


def _dot_exact(x, m01, terms=3):
    out = 0.0
    for _ in range(terms):
        part = x.astype(BF16)
        out = out + _dot(part, m01)
        x = x - part.astype(F32)
    return out


def _lane_lo(shape):
    return lax.broadcasted_iota(jnp.int32, shape, 1) < HEAD_DIM


def _tri_constants():
    j = jnp.arange(BLOCK)
    ones = jnp.ones((BLOCK, BLOCK), F32)
    excl = (j[:, None] > j[None, :]).astype(F32)
    incl = (j[:, None] >= j[None, :]).astype(F32)
    tri_a = jnp.concatenate([excl, ones], axis=1).astype(BF16)
    tri_b = jnp.concatenate([incl, ones], axis=1).astype(BF16)
    d = jnp.arange(2 * LANES)
    bd = (d[:, None] // HEAD_DIM == d[None, :] // HEAD_DIM).astype(BF16)
    return tri_a, tri_b, bd


def _rope_tables(seq):
    inv_freq = 1.0 / (ROPE_THETA ** (jnp.arange(ROPE_HALF, dtype=F32) * 2.0 / ROPE_DIM))
    ang = jnp.arange(seq).astype(F32)[:, None] * inv_freq[None, :]
    cos, sin = jnp.cos(ang), jnp.sin(ang)
    one = jnp.ones((seq, HEAD_DIM - ROPE_DIM), F32)
    zero8 = jnp.zeros((seq, ROPE_HALF), F32)
    zero_rest = jnp.zeros((seq, HEAD_DIM - ROPE_DIM), F32)
    c = jnp.concatenate([cos, cos, one], axis=1)
    s_next = jnp.concatenate([-sin, zero8, zero_rest], axis=1)
    s_prev = jnp.concatenate([zero8, sin, zero_rest], axis=1)
    rep = (2 * LANES) // HEAD_DIM
    return tuple(jnp.tile(t, (1, rep)) for t in (c, s_next, s_prev))


def _roll_lanes(x, shift):
    return jnp.concatenate([pltpu.roll(x[:, :LANES], shift, 1), pltpu.roll(x[:, LANES:], shift, 1)], axis=1)


def _rope(x, c, s_next, s_prev):
    return x * c + _roll_lanes(x, LANES - ROPE_HALF) * s_next + _roll_lanes(x, ROPE_HALF) * s_prev


def _rope_t(dy, c, s_next, s_prev):
    return dy * c + _roll_lanes(dy * s_next, ROPE_HALF) + _roll_lanes(dy * s_prev, LANES - ROPE_HALF)


def _cast_bf16(w, name):
    nl, r, c = w.shape

    def body(w_ref, o_ref):
        o_ref[...] = w_ref[...].astype(BF16)

    return pl.pallas_call(
        body, name=name, grid=(nl,),
        in_specs=[pl.BlockSpec((None, r, c), lambda l: (l, 0, 0))],
        out_specs=pl.BlockSpec((None, r, c), lambda l: (l, 0, 0)),
        out_shape=jax.ShapeDtypeStruct(w.shape, BF16),
        compiler_params=_params(24),
    )(w)


def _flips():
    return [(dx, dy, dc) for dx in (0, 1) for dy in (0, 1) for dc in (0, 1) if (dx, dy, dc) != (0, 0, 0)]


def _place():
    x, y, c = lax.axis_index("x"), lax.axis_index("y"), lax.axis_index("c")
    return x, y, c, 4 * x + 2 * y + c


def _peer(x, y, c, flip):
    dx, dy, dc = flip
    return (1 - x if dx else x, 1 - y if dy else y, 1 - c if dc else c)


def _to_every_device(srcs_for, dsts_at, sems):
    send_sems, recv_sems, local_sems = sems
    x, y, c, me = _place()
    dsts = dsts_at(me)
    n = len(dsts)
    copies = [pltpu.make_async_copy(src, dst, local_sems.at[a]) for a, (src, dst) in enumerate(zip(srcs_for(me), dsts))]
    for k, flip in enumerate(_flips()):
        px, py, pc = _peer(x, y, c, flip)
        for a, (src, dst) in enumerate(zip(srcs_for(4 * px + 2 * py + pc), dsts)):
            copies.append(pltpu.make_async_remote_copy(
                src_ref=src, dst_ref=dst, send_sem=send_sems.at[n * k + a], recv_sem=recv_sems.at[n * k + a],
                device_id=(px, py, pc), device_id_type=MESH_ID))
    return copies


def _copy_sems(n):
    remote = n * (N_DEV - 1)
    return [pltpu.SemaphoreType.DMA((remote,)), pltpu.SemaphoreType.DMA((remote,)), pltpu.SemaphoreType.DMA((n,))]


def _weight_copies(win_ref, wout_ref, layer, oin_ref, oout_ref, sems):
    return _to_every_device(lambda to: (win_ref.at[layer], wout_ref.at[layer]), lambda me: (oin_ref.at[me], oout_ref.at[me]), sems)


def _gathered_shapes(win_bf, wout_bf):
    return [jax.ShapeDtypeStruct((N_DEV,) + win_bf.shape[1:], BF16), jax.ShapeDtypeStruct((N_DEV,) + wout_bf.shape[1:], BF16)]


def _gather_weights(win_bf, wout_bf, layer):
    def body(win_ref, wout_ref, oin_ref, oout_ref, *sems):
        copies = _weight_copies(win_ref, wout_ref, layer, oin_ref, oout_ref, sems)
        for cp in copies:
            cp.start()
        for cp in copies:
            cp.wait()

    any_spec = pl.BlockSpec(memory_space=pl.ANY)
    return pl.pallas_call(
        body, name="gather_weights", in_specs=[any_spec, any_spec], out_specs=[any_spec, any_spec],
        out_shape=_gathered_shapes(win_bf, wout_bf), scratch_shapes=_copy_sems(2),
    )(win_bf, wout_bf)


_F32_ROUTES = {1: ((256, 256, 0, 0),), 2: ((0, 512, 1, 0),), 3: ((0, 512, 1, 512),), 4: ((0, 512, 1, 1024),),
               5: ((0, 512, 1, 1536),), 6: ((0, 256, 1, 2048), (256, 256, 0, 256)), 7: ((0, 512, 0, 512),)}


def _norm_proj(x, g, w_l, prefetch=None):
    seq, d = x.shape
    tm = min(256, seq)
    steps = seq // tm

    def body(x_ref, g_ref, w_ref, *rest):
        if prefetch is None:
            pbf_ref, gates_ref, qk_ref, h_ref = rest
        else:
            win_ref, wout_ref, pbf_ref, gates_ref, qk_ref, h_ref, oin_ref, oout_ref, *sems = rest
            copies = lambda: _weight_copies(win_ref, wout_ref, prefetch[2], oin_ref, oout_ref, sems)

            @pl.when(pl.program_id(0) == 0)
            def _():
                for cp in copies():
                    cp.start()

        xf = x_ref[...]
        rs = lax.rsqrt(jnp.mean(xf * xf, axis=-1, keepdims=True) + EPS)
        h = (xf * rs * g_ref[...]).astype(BF16)
        h_ref[...] = h
        targets = (gates_ref, qk_ref)
        for n in range(N_DEV):
            acc = _dot(h, w_ref[n])
            pbf_ref[:, n * COLS_PER_DEV:(n + 1) * COLS_PER_DEV] = acc.astype(BF16)
            for lo, width, tgt, dst in _F32_ROUTES.get(n, ()):
                targets[tgt][:, dst:dst + width] = acc[:, lo:lo + width]

        if prefetch is not None:
            @pl.when(pl.program_id(0) == steps - 1)
            def _():
                for cp in copies():
                    cp.wait()

    row = lambda w: pl.BlockSpec((tm, w), lambda i: (i, 0))
    any_spec = pl.BlockSpec(memory_space=pl.ANY)
    in_specs = [row(d), pl.BlockSpec((1, d), lambda i: (0, 0)), pl.BlockSpec((N_DEV, d, COLS_PER_DEV), lambda i: (0, 0, 0))]
    out_specs = [row(IN_COLS), row(MIX), row(3 * W_DIL), row(d)]
    out_shape = [jax.ShapeDtypeStruct((seq, IN_COLS), BF16), jax.ShapeDtypeStruct((seq, MIX), F32),
                 jax.ShapeDtypeStruct((seq, 3 * W_DIL), F32), jax.ShapeDtypeStruct((seq, d), BF16)]
    if prefetch is None:
        return pl.pallas_call(body, name="norm_proj", grid=(steps,), in_specs=in_specs, out_specs=out_specs,
                              out_shape=out_shape, compiler_params=_params(48))(x, g, w_l)
    return pl.pallas_call(
        body, name="norm_proj_gather", grid=(steps,), in_specs=in_specs + [any_spec, any_spec],
        out_specs=out_specs + [any_spec, any_spec], out_shape=out_shape + _gathered_shapes(*prefetch[:2]),
        scratch_shapes=_copy_sems(2), compiler_params=_params(48),
    )(x, g, w_l, *prefetch[:2])


def _head_sums(v, bd):
    hi = v.astype(BF16)
    lo = (v - hi.astype(F32)).astype(BF16)
    return _dot(hi, bd) + _dot(lo, bd)


def _qk_prep(qk_raw, gains, tables, bd):
    seq = qk_raw.shape[0]
    tm = min(1024, seq)
    chunks = W_DIL // (2 * LANES)

    def body(x_ref, g_ref, c_ref, sn_ref, sp_ref, bd_ref, o_ref):
        j = pl.program_id(1)
        x = x_ref[...]
        rs = lax.rsqrt(_head_sums(x * x, bd_ref[...]) * (1.0 / HEAD_DIM) + EPS)
        y = _rope(x * rs * g_ref[...], c_ref[...], sn_ref[...], sp_ref[...])
        o_ref[...] = y * jnp.where(j < chunks, QK_SCALE, 1.0)

    tab = pl.BlockSpec((tm, 2 * LANES), lambda i, j: (i, 0))
    return pl.pallas_call(
        body, name="qk_prep", grid=(seq // tm, 2 * chunks),
        in_specs=[pl.BlockSpec((tm, 2 * LANES), lambda i, j: (i, j)),
                  pl.BlockSpec((None, 1, 2 * LANES), lambda i, j: (j // chunks, 0, 0)),
                  tab, tab, tab, pl.BlockSpec((2 * LANES, 2 * LANES), lambda i, j: (0, 0))],
        out_specs=pl.BlockSpec((tm, 2 * LANES), lambda i, j: (i, j)),
        out_shape=jax.ShapeDtypeStruct((seq, 2 * W_DIL), F32),
        compiler_params=_params(32),
    )(qk_raw, gains, *tables, bd)


def _stack_heads(x, lo):
    return jnp.concatenate([jnp.where(lo, x, 0.0), jnp.where(lo, 0.0, x)], axis=0)


def _unstack_heads(y, lo):
    return jnp.where(lo, y[:BLOCK], y[BLOCK:])


def _stacked_causal():
    row = lax.broadcasted_iota(jnp.int32, (2 * BLOCK, BLOCK), 0) & (BLOCK - 1)
    return lax.broadcasted_iota(jnp.int32, (2 * BLOCK, BLOCK), 1) < row


def _keep(x, *conds):
    for cond in conds:
        if cond is not None:
            x = jnp.where(cond, x, 0.0)
    return x


def _sb_weights(chains, tri):
    zs = [[_dot_nt(qs, kb) for kb in kbs] for qs, kbs, _, _, _ in chains]
    lss = [[jnp.minimum(z, 0.0) - jnp.log1p(jnp.exp(-jnp.abs(z))) for z in zc] for zc in zs]
    cts = [[_dot_exact(_keep(ls - z, mask, live), tri, SB_SUM_TERMS) for ls, z, mask, live in zip(lsc, zc, masks, lives)]
           for lsc, zc, (_, _, _, masks, lives) in zip(lss, zs, chains)]
    out = []
    for lsc, ctc, (_, _, r, masks, lives) in zip(lss, cts, chains):
        weights = []
        for ls, ct, mask, live in zip(lsc, ctc, masks, lives):
            weights.append(_keep(jnp.exp(ls + ct[:, :BLOCK] + r), mask, live))
            r = r + ct[:, BLOCK:]
        out.append((lsc, weights, r))
    return out


def _sb_peel(i, causal):
    return ([jnp.maximum(i - k, 0) for k in range(SB_PEEL)], [causal] + [None] * (SB_PEEL - 1),
            [None] + [i >= k for k in range(1, SB_PEEL)])


def _block_rows(j):
    return pl.ds(pl.multiple_of(j * BLOCK, BLOCK), BLOCK)


def _sb_fwd(proj_bf, tri_a):
    seq = proj_bf.shape[0]
    pairs = W_SB // LANES

    def body(q_ref, k_ref, v_ref, tri_ref, o_ref, r_ref, acc_ref):
        t = pl.program_id(1)
        lo = _lane_lo((BLOCK, LANES))
        causal = _stacked_causal()
        tri = tri_ref[...]
        qblocks = [t * SB_QBLOCKS + c for c in range(SB_QBLOCKS)]
        qss = [_stack_heads(q_ref[c * BLOCK:(c + 1) * BLOCK, :].astype(F32) * QK_SCALE, lo).astype(BF16)
               for c in range(SB_QBLOCKS)]

        def blocks(specs):
            rows = [[_block_rows(j) for j in js] for _, js, _, _, _ in specs]
            res = _sb_weights([(qs, [k_ref[rw, :] for rw in rws], r, masks, lives)
                               for (qs, _, r, masks, lives), rws in zip(specs, rows)], tri)
            outs = []
            for (_, weights, r), rws in zip(res, rows):
                out = 0.0
                for a, rw in zip(weights, rws):
                    a_hi = a.astype(BF16)
                    a_lo = (a - a_hi.astype(F32)).astype(BF16)
                    vb = v_ref[rw, :]
                    out = out + _dot(a_hi, vb) + _dot(a_lo, vb)
                outs.append((out, r))
            return outs

        def peel(qs, i):
            js, masks, lives = _sb_peel(i, causal)
            return qs, js, jnp.zeros((2 * BLOCK, LANES), F32), masks, lives

        for c, (out, r) in enumerate(blocks([peel(qs, i) for qs, i in zip(qss, qblocks)])):
            acc_ref[c], r_ref[c] = out, r

        for c in range(SB_QBLOCKS):
            def alive(c=c):
                return (jnp.max(r_ref[c]) > DEAD_LOG).astype(jnp.int32)

            def step(carry, c=c):
                (out, r_ref[c]), = blocks([(qss[c], [carry[0]], r_ref[c], [None], [None])])
                acc_ref[c] += out
                return carry[0] - 1, alive(c)

            lax.while_loop(lambda carry: jnp.logical_and(carry[0] >= 0, carry[1] > 0), step, (qblocks[c] - SB_PEEL, alive()))
            o_ref[c * BLOCK:(c + 1) * BLOCK, :] = _unstack_heads(acc_ref[c], lo)

    qtile = SB_QBLOCKS * BLOCK
    state = pltpu.VMEM((SB_QBLOCKS, 2 * BLOCK, LANES), F32)
    return pl.pallas_call(
        body, name="sb_fwd", grid=(pairs, seq // qtile),
        in_specs=[pl.BlockSpec((qtile, LANES), lambda p, t: (t, p)),
                  pl.BlockSpec((seq, LANES), lambda p, t: (0, pairs + p)),
                  pl.BlockSpec((seq, LANES), lambda p, t: (0, 2 * pairs + p)),
                  pl.BlockSpec((BLOCK, 2 * BLOCK), lambda p, t: (0, 0))],
        out_specs=pl.BlockSpec((qtile, LANES), lambda p, t: (t, p)),
        out_shape=jax.ShapeDtypeStruct((seq, W_SB), F32),
        scratch_shapes=[state, state],
        compiler_params=_params(40),
    )(proj_bf, proj_bf, proj_bf, tri_a)


def _class_rows(c, first, count, r):
    start = c + first * r
    return pl.ds(start, count) if r == 1 else pl.ds(start, count, stride=r)


def _dil_tiling(seq, r):
    tile = min(DIL_TILE, seq)
    edge = BLOCK * r
    return tile, edge, tile // edge


def _band_mask(n_rows, n_keys):
    row = lax.broadcasted_iota(jnp.int32, (n_rows, n_keys), 0) & (BLOCK - 1)
    col = lax.broadcasted_iota(jnp.int32, (n_rows, n_keys), 1)
    return jnp.logical_and(col >= row, col <= row + BLOCK), col


def _for_each_group(r, per, g, run):
    loop = lambda lo, hi, fn: lax.fori_loop(lo, hi, lambda t, carry: (fn(t), carry)[1], 0)
    if per <= g:
        n_cls = g // per
        assert g % per == 0 and r % n_cls == 0
        group = lambda t: [(t * n_cls + u, a, a == 0) for u in range(n_cls) for a in range(per)]
        if r == n_cls:
            run(group(0))
        else:
            loop(0, r // n_cls, lambda t: run(group(t)))
        return

    assert per % g == 0

    def one_class(c):
        run([(c, 0, True)] + [(c, a, False) for a in range(1, g)])
        loop(1, per // g, lambda t: run([(c, t * g + u, False) for u in range(g)]))

    if r == 1:
        one_class(0)
    else:
        loop(0, r, one_class)


def _dil_keys(blocks, r, cur_ref, before_ref):
    out = []
    for c, a, first in blocks:
        if first:
            rows = _class_rows(c, 0, BLOCK, r)
            out.append(jnp.concatenate([before_ref[rows, :], cur_ref[rows, :]], axis=0).astype(BF16))
        else:
            out.append(cur_ref[_class_rows(c, (a - 1) * BLOCK, 2 * BLOCK, r), :].astype(BF16))
    return out


def _dil_fwd(qkn, qkv_raw, r):
    seq = qkn.shape[0]
    tile, edge, per = _dil_tiling(seq, r)
    pairs = W_DIL // LANES

    def body(q_ref, kc_ref, kp_ref, vc_ref, vp_ref, o_ref, ld_ref):
        n = pl.program_id(1)
        lo = _lane_lo((BLOCK, LANES))
        band, col = _band_mask(2 * BLOCK, 2 * BLOCK)
        first_band = jnp.logical_and(band, jnp.logical_or(col >= BLOCK, n > 0))

        def run(blocks):
            rows = [_class_rows(c, a * BLOCK, BLOCK, r) for c, a, _ in blocks]
            keys = _dil_keys(blocks, r, kc_ref, kp_ref)
            vals = _dil_keys(blocks, r, vc_ref, vp_ref)
            scores = [_dot_nt(_stack_heads(q_ref[rw, :], lo).astype(BF16), kb) for rw, kb in zip(rows, keys)]
            probs, sums, lds = [], [], []
            for s, (_, _, first) in zip(scores, blocks):
                s = jnp.where(first_band if first else band, s, NEG_BIG)
                m = jnp.max(s, axis=1, keepdims=True)
                p = jnp.exp(s - m)
                l = jnp.sum(p, axis=1, keepdims=True)
                probs.append(p.astype(BF16))
                sums.append(l)
                lds.append(m + jnp.log(l))
            outs = [_dot(p, vb) / l for p, vb, l in zip(probs, vals, sums)]
            for rw, o, ld in zip(rows, outs, lds):
                o_ref[rw, :] = _unstack_heads(o, lo)
                ld_ref[rw, :] = _unstack_heads(jnp.broadcast_to(ld, (2 * BLOCK, LANES)), lo)

        _for_each_group(r, per, DIL_GROUP_FWD, run)

    per_edge = tile // edge
    cur = lambda off: pl.BlockSpec((tile, LANES), lambda p, n: (n, off + p))
    before = lambda off: pl.BlockSpec((edge, LANES), lambda p, n: (jnp.maximum(n * per_edge - 1, 0), off + p))
    return pl.pallas_call(
        body, name=f"dil_fwd_r{r}", grid=(pairs, seq // tile),
        in_specs=[cur(0), cur(pairs), before(pairs), cur(2 * pairs), before(2 * pairs)],
        out_specs=[cur(0), cur(0)],
        out_shape=[jax.ShapeDtypeStruct((seq, W_DIL), F32), jax.ShapeDtypeStruct((seq, W_DIL), F32)],
        compiler_params=_params(32),
    )(qkn, qkn, qkn, qkv_raw, qkv_raw)


def _silu_parts(g):
    sig = jax.nn.sigmoid(g)
    return g * sig, sig * (1.0 + g * (1.0 - sig))


def _out_proj(x, oa, o_p, ld_p, gates, wout_l):
    seq, d = x.shape
    tm = min(256, seq)

    def body(x_ref, oa_ref, o0, o1, o2, l0, l1, l2, g_ref, w_ref, xn_ref, cat_ref, od_ref, lse_ref):
        lds = (l0[...], l1[...], l2[...])
        m = jnp.maximum(jnp.maximum(lds[0], lds[1]), lds[2])
        es = [jnp.exp(v - m) for v in lds]
        tot = es[0] + es[1] + es[2]
        lse_ref[...] = m + jnp.log(tot)
        inv = 1.0 / tot
        od = (es[0] * inv) * o0[...] + (es[1] * inv) * o1[...] + (es[2] * inv) * o2[...]
        od_ref[...] = od
        silu, _ = _silu_parts(g_ref[...])
        cat_ref[:, :W_SB] = (oa_ref[...] * silu[:, :W_SB]).astype(BF16)
        cat_ref[:, W_SB:] = (od * silu[:, W_SB:]).astype(BF16)
        y = x_ref[...]
        for b in range(N_DEV):
            y = y + _dot(cat_ref[:, b * ROWS_PER_DEV:(b + 1) * ROWS_PER_DEV], w_ref[b])
        xn_ref[...] = y

    row = lambda w: pl.BlockSpec((tm, w), lambda i: (i, 0))
    return pl.pallas_call(
        body, name="out_proj", grid=(seq // tm,),
        in_specs=[row(d), row(W_SB)] + [row(W_DIL)] * 6 + [row(MIX),
                  pl.BlockSpec((N_DEV, ROWS_PER_DEV, d), lambda i: (0, 0, 0))],
        out_specs=[row(d), row(MIX), row(W_DIL), row(W_DIL)],
        out_shape=[jax.ShapeDtypeStruct((seq, d), F32), jax.ShapeDtypeStruct((seq, MIX), BF16),
                   jax.ShapeDtypeStruct((seq, W_DIL), F32), jax.ShapeDtypeStruct((seq, W_DIL), F32)],
        compiler_params=_params(48),
    )(x, oa, *o_p, *ld_p, gates, wout_l)


def _loss_head(y, target):
    seq, d = y.shape
    tm = min(512, seq)

    def body(y_ref, t_ref, part_ref, dy_ref):
        @pl.when(pl.program_id(0) == 0)
        def _():
            part_ref[...] = jnp.zeros_like(part_ref)

        diff = y_ref[...] - t_ref[...]
        dy_ref[...] = diff * (1.0 / d)
        part_ref[...] += jnp.sum((diff * diff).reshape(tm // 8, 8, d), axis=0) * (0.5 / d)

    row = pl.BlockSpec((tm, d), lambda i: (i, 0))
    return pl.pallas_call(
        body, name="loss_head", grid=(seq // tm,),
        in_specs=[row, row], out_specs=[pl.BlockSpec((8, d), lambda i: (0, 0)), row],
        out_shape=[jax.ShapeDtypeStruct((8, d), F32), jax.ShapeDtypeStruct((seq, d), F32)],
        compiler_params=_params(32),
    )(y, target)


def _out_proj_bwd(dy, wout_l, cat, gates, oa, od, bd):
    seq, d = dy.shape
    tm = min(256, seq)

    def body(dy_ref, w_ref, cat_ref, g_ref, oa_ref, od_ref, bd_ref, doa_ref, dod_ref, delta_ref, dg_ref, dw_ref, dcat):
        @pl.when(pl.program_id(0) == 0)
        def _():
            dw_ref[...] = jnp.zeros_like(dw_ref)

        dyb = dy_ref[...].astype(BF16)
        dw = _dot_tn(cat_ref[...], dyb)
        for b in range(N_DEV):
            dw_ref[b] += dw[b * ROWS_PER_DEV:(b + 1) * ROWS_PER_DEV, :]
            dcat[:, b * ROWS_PER_DEV:(b + 1) * ROWS_PER_DEV] = _dot_nt(dyb, w_ref[b])
        silu, dsilu = _silu_parts(g_ref[...])
        dc = dcat[...]
        dmix = dc * silu
        oa_v, od_v = oa_ref[...], od_ref[...]
        dg_ref[:, :W_SB] = dc[:, :W_SB] * oa_v * dsilu[:, :W_SB]
        dg_ref[:, W_SB:] = dc[:, W_SB:] * od_v * dsilu[:, W_SB:]
        doa_ref[...] = dmix[:, :W_SB]
        dod = dmix[:, W_SB:]
        dod_ref[...] = dod
        prod = dod * od_v
        for k in range(W_DIL // (2 * LANES)):
            sl = slice(k * 2 * LANES, (k + 1) * 2 * LANES)
            delta_ref[:, sl] = _head_sums(prod[:, sl], bd_ref[...])

    row = lambda w: pl.BlockSpec((tm, w), lambda i: (i, 0))
    slab = pl.BlockSpec((N_DEV, ROWS_PER_DEV, d), lambda i: (0, 0, 0))
    return pl.pallas_call(
        body, name="out_proj_bwd", grid=(seq // tm,),
        in_specs=[row(d), slab, row(MIX), row(MIX), row(W_SB), row(W_DIL),
                  pl.BlockSpec((2 * LANES, 2 * LANES), lambda i: (0, 0))],
        out_specs=[row(W_SB), row(W_DIL), row(W_DIL), row(MIX), slab],
        out_shape=[jax.ShapeDtypeStruct((seq, W_SB), F32), jax.ShapeDtypeStruct((seq, W_DIL), F32),
                   jax.ShapeDtypeStruct((seq, W_DIL), F32), jax.ShapeDtypeStruct((seq, MIX), F32),
                   jax.ShapeDtypeStruct((N_DEV, ROWS_PER_DEV, d), F32)],
        scratch_shapes=[pltpu.VMEM((tm, MIX), F32)],
        compiler_params=_params(48),
    )(dy, wout_l, cat, gates, oa, od, bd)


def _sb_bwd(proj_bf, d_oa, oa, tri_a, tri_b):
    seq = proj_bf.shape[0]
    pairs = W_SB // LANES
    nq = seq // BLOCK

    def body(q_ref, k_ref, v_ref, do_ref, o_ref, tria_ref, trib_ref, dq_ref, dk_hbm, dv_hbm,
             r_ref, sfx_ref, dtot_ref, dq_acc, dk_acc, dv_acc, sems):
        p, t = pl.program_id(0), pl.program_id(1)

        @pl.when(t == 0)
        def _():
            dk_acc[...] = jnp.zeros_like(dk_acc)
            dv_acc[...] = jnp.zeros_like(dv_acc)

        lo = _lane_lo((BLOCK, LANES))
        tri, trib = tria_ref[...], trib_ref[...]
        causal = _stacked_causal()
        qblocks = [t * SB_QBLOCKS + c for c in range(SB_QBLOCKS)]
        qss, doss = [], []
        for c in range(SB_QBLOCKS):
            sl = slice(c * BLOCK, (c + 1) * BLOCK)
            qss.append(_stack_heads(q_ref[sl, :].astype(F32) * QK_SCALE, lo).astype(BF16))
            doss.append(_stack_heads(do_ref[sl, :], lo).astype(BF16))
            o2 = o_ref[sl, :]
            dtot_ref[c] = _dot_exact(doss[c].astype(F32) * jnp.concatenate([o2, o2], axis=0), tri[:, BLOCK:])

        def blocks(specs):
            rows = [[_block_rows(j) for j in js] for _, js, _, _, _, _ in specs]
            kbs = [[k_ref[rw, :] for rw in rws] for rws in rows]
            dovs = [[_dot_nt(doss[c], v_ref[rw, :]) for rw in rws] for (c, *_), rws in zip(specs, rows)]
            res = _sb_weights([(qss[c], kbc, r, masks, lives) for (c, _, r, _, masks, lives), kbc in zip(specs, kbs)], tri)
            pws = [[a * dov for a, dov in zip(weights, dovc)] for (_, weights, _), dovc in zip(res, dovs)]
            cps = [[_dot_exact(pw, trib, SB_SUM_TERMS) for pw in pwc] for pwc in pws]
            dzs, sfxs = [], []
            for (c, _, _, sfx, masks, lives), (lsc, _, _), pwc, cpc in zip(specs, res, pws, cps):
                dzc = []
                for ls, pw, cp, mask, live in zip(lsc, pwc, cpc, masks, lives):
                    beta = jnp.exp(ls)
                    before = dtot_ref[c] - sfx - cp[:, :BLOCK]
                    dzc.append(_keep(pw * (1.0 - beta) - before * beta, mask, live).astype(BF16))
                    sfx = sfx + cp[:, BLOCK:]
                dzs.append(dzc)
                sfxs.append(sfx)
            dqs = []
            for dzc, kbc in zip(dzs, kbs):
                dq = 0.0
                for dzb, kb in zip(dzc, kbc):
                    dq = dq + _dot(dzb, kb)
                dqs.append(dq)
            for (c, *_), dzc, (_, weights, _), rws in zip(specs, dzs, res, rows):
                for dzb, a, rw in zip(dzc, weights, rws):
                    dk_acc[rw, :] += _dot_tn(dzb, qss[c])
                    dv_acc[rw, :] += _dot_tn(a.astype(BF16), doss[c])
            return [(dq, r, sfx) for dq, (_, _, r), sfx in zip(dqs, res, sfxs)]

        def peel(c, i):
            js, masks, lives = _sb_peel(i, causal)
            zero = jnp.zeros((2 * BLOCK, LANES), F32)
            return c, js, zero, zero, masks, lives

        for c, (dq, r, sfx) in enumerate(blocks([peel(c, i) for c, i in enumerate(qblocks)])):
            dq_acc[c], r_ref[c], sfx_ref[c] = dq, r, sfx

        for c in range(SB_QBLOCKS):
            def alive(c=c):
                return (jnp.max(r_ref[c]) > DEAD_LOG).astype(jnp.int32)

            def step(carry, c=c):
                (dq, r_ref[c], sfx_ref[c]), = blocks([(c, [carry[0]], r_ref[c], sfx_ref[c], [None], [None])])
                dq_acc[c] += dq
                return carry[0] - 1, alive(c)

            lax.while_loop(lambda carry: jnp.logical_and(carry[0] >= 0, carry[1] > 0), step, (qblocks[c] - SB_PEEL, alive()))
            dq_ref[c * BLOCK:(c + 1) * BLOCK, :] = _unstack_heads(dq_acc[c], lo) * QK_SCALE

        @pl.when(t == nq // SB_QBLOCKS - 1)
        def _():
            outs = [pltpu.make_async_copy(dk_acc, dk_hbm.at[p], sems.at[0]),
                    pltpu.make_async_copy(dv_acc, dv_hbm.at[p], sems.at[1])]
            for cp in outs:
                cp.start()
            for cp in outs:
                cp.wait()

    qtile = SB_QBLOCKS * BLOCK
    blk = pl.BlockSpec((qtile, LANES), lambda p, t: (t, p))
    const = pl.BlockSpec((BLOCK, 2 * BLOCK), lambda p, t: (0, 0))
    any_spec = pl.BlockSpec(memory_space=pl.ANY)
    state = pltpu.VMEM((SB_QBLOCKS, 2 * BLOCK, LANES), F32)
    return pl.pallas_call(
        body, name="sb_bwd", grid=(pairs, seq // qtile),
        in_specs=[blk, pl.BlockSpec((seq, LANES), lambda p, t: (0, pairs + p)),
                  pl.BlockSpec((seq, LANES), lambda p, t: (0, 2 * pairs + p)), blk, blk, const, const],
        out_specs=[blk, any_spec, any_spec],
        out_shape=[jax.ShapeDtypeStruct((seq, W_SB), F32), jax.ShapeDtypeStruct((pairs, seq, LANES), F32),
                   jax.ShapeDtypeStruct((pairs, seq, LANES), F32)],
        scratch_shapes=[state, state, state, state, pltpu.VMEM((seq, LANES), F32), pltpu.VMEM((seq, LANES), F32),
                        pltpu.SemaphoreType.DMA((2,))],
        compiler_params=_params(56),
    )(proj_bf, proj_bf, proj_bf, d_oa, oa, tri_a, tri_b)


def _dil_bwd(qkn, qkv_raw, d_od, lse, delta, r, others=None):
    seq = qkn.shape[0]
    tile, edge, per = _dil_tiling(seq, r)
    ntile = seq // tile
    pairs = W_DIL // LANES

    def body(qc, doc, lsc, dlc, kc, kp, vc, vp, *rest):
        if others is None:
            dq_ref, dk_ref, dv_ref, dk_carry, dv_carry = rest
        else:
            dq_in, dk_in, dv_in, dq_ref, dk_ref, dv_ref, dk_carry, dv_carry = rest
        n = pl.program_id(1)

        @pl.when(n == 0)
        def _():
            dk_carry[...] = jnp.zeros_like(dk_carry)
            dv_carry[...] = jnp.zeros_like(dv_carry)

        if others is None:
            dk_ref[...] = dk_carry[...]
            dv_ref[...] = dv_carry[...]
        else:
            dk_ref[...] = dk_carry[...] + dk_in[...]
            dv_ref[...] = dv_carry[...] + dv_in[...]

        @pl.when(n < ntile)
        def _():
            lo = _lane_lo((BLOCK, LANES))
            band, col = _band_mask(2 * BLOCK, 2 * BLOCK)
            first_band = jnp.logical_and(band, jnp.logical_or(col >= BLOCK, n > 0))

            def stacked_cols(b):
                other = pltpu.roll(b, HEAD_DIM, 1)
                rows = jnp.concatenate([jnp.where(lo, b, other), jnp.where(lo, other, b)], axis=0)
                return jnp.concatenate([rows, rows], axis=1)

            def run(blocks):
                rows = [_class_rows(c, a * BLOCK, BLOCK, r) for c, a, _ in blocks]
                keys = _dil_keys(blocks, r, kc, kp)
                vals = _dil_keys(blocks, r, vc, vp)
                qss = [_stack_heads(qc[rw, :], lo).astype(BF16) for rw in rows]
                doss = [_stack_heads(doc[rw, :], lo).astype(BF16) for rw in rows]
                scores = [_dot_nt(qs, kb) for qs, kb in zip(qss, keys)]
                dps = [_dot_nt(dos, vb) for dos, vb in zip(doss, vals)]
                pws, dss = [], []
                for rw, s, dp, (_, _, first) in zip(rows, scores, dps, blocks):
                    pw = jnp.where(first_band if first else band, jnp.exp(s - stacked_cols(lsc[rw, :])), 0.0)
                    pws.append(pw.astype(BF16))
                    dss.append((pw * (dp - stacked_cols(dlc[rw, :]))).astype(BF16))
                dqs = [_dot(ds, kb) for ds, kb in zip(dss, keys)]
                dks = [_dot_tn(ds, qs) for ds, qs in zip(dss, qss)]
                dvs = [_dot_tn(pw, dos) for pw, dos in zip(pws, doss)]
                for (c, a, first), rw, dq, dk, dv in zip(blocks, rows, dqs, dks, dvs):
                    dq_ref[rw, :] = _unstack_heads(dq, lo)
                    if first:
                        last = _class_rows(c, (per - 1) * BLOCK, BLOCK, r)
                        dk_ref[last, :] += dk[:BLOCK]
                        dv_ref[last, :] += dv[:BLOCK]
                    else:
                        prev = _class_rows(c, (a - 1) * BLOCK, BLOCK, r)
                        dk_carry[prev, :] += dk[:BLOCK]
                        dv_carry[prev, :] += dv[:BLOCK]
                    dk_carry[rw, :] = dk[BLOCK:]
                    dv_carry[rw, :] = dv[BLOCK:]

            _for_each_group(r, per, DIL_GROUP_BWD, run)
            if others is not None:
                dq_ref[...] += dq_in[...]

    per_edge = tile // edge
    here = lambda n: jnp.minimum(n, ntile - 1)
    cur = lambda off: pl.BlockSpec((tile, LANES), lambda p, n: (here(n), off + p))
    before = lambda off: pl.BlockSpec((edge, LANES), lambda p, n: (jnp.maximum(here(n) * per_edge - 1, 0), off + p))
    lagged = pl.BlockSpec((tile, LANES), lambda p, n: (jnp.maximum(n - 1, 0), p))
    carry = pltpu.VMEM((tile, LANES), F32)
    return pl.pallas_call(
        body, name=f"dil_bwd_r{r}", grid=(pairs, ntile + 1),
        in_specs=[cur(0)] * 4 + [cur(pairs), before(pairs), cur(2 * pairs), before(2 * pairs)] +
                 ([] if others is None else [cur(0), lagged, lagged]),
        out_specs=[cur(0), lagged, lagged],
        out_shape=[jax.ShapeDtypeStruct((seq, W_DIL), F32)] * 3,
        scratch_shapes=[carry, carry],
        compiler_params=_params(48),
    )(qkn, d_od, lse, delta, qkn, qkn, qkv_raw, qkv_raw, *(others or ()))


def _assemble_dproj(d_qa, d_ka, d_va, d_gates, d_qd, d_kd, d_vd, qk_raw, gains, tables, bd):
    seq = qk_raw.shape[0]
    tm = min(256, seq)
    chunks = W_DIL // (2 * LANES)

    def body(dqa, dka0, dka1, dva0, dva1, dg, dqd, dkd, dvd, x_ref, g_ref, c_ref, sn_ref, sp_ref, bd_ref, dp_ref, gpart_ref):
        @pl.when(pl.program_id(0) == 0)
        def _():
            gpart_ref[...] = jnp.zeros_like(gpart_ref)

        def put(first_col, v):
            dp_ref[:, first_col:first_col + v.shape[1]] = v.astype(BF16)

        put(0, dqa[...])
        put(W_SB, dka0[...])
        put(W_SB + LANES, dka1[...])
        put(2 * W_SB, dva0[...])
        put(2 * W_SB + LANES, dva1[...])
        put(3 * W_SB, dg[:, :W_SB])
        put(4 * W_SB + 2 * W_DIL, dvd[...])
        put(4 * W_SB + 3 * W_DIL, dg[:, W_SB:])
        c, sn, sp, bdm = c_ref[...], sn_ref[...], sp_ref[...], bd_ref[...]
        for which, part in enumerate((dqd, dkd)):
            scale = QK_SCALE if which == 0 else 1.0
            for k in range(chunks):
                sl = slice(k * 2 * LANES, (k + 1) * 2 * LANES)
                dyv = part[:, sl] * scale
                dxn = _rope_t(dyv, c, sn, sp)
                x = x_ref[:, which * W_DIL + k * 2 * LANES:which * W_DIL + (k + 1) * 2 * LANES]
                rs = lax.rsqrt(_head_sums(x * x, bdm) * (1.0 / HEAD_DIM) + EPS)
                xhat = x * rs
                gpart_ref[which] += jnp.sum((dxn * xhat).reshape(tm // 8, 8, 2 * LANES), axis=0)
                dxhat = dxn * g_ref[which]
                mean = _head_sums(dxhat * xhat, bdm) * (1.0 / HEAD_DIM)
                put(4 * W_SB + which * W_DIL + k * 2 * LANES, rs * (dxhat - xhat * mean))

    row = lambda w: pl.BlockSpec((tm, w), lambda i: (i, 0))
    pair = lambda p: pl.BlockSpec((None, tm, LANES), lambda i: (p, i, 0))
    const = lambda shape: pl.BlockSpec(shape, lambda i: tuple(0 for _ in shape))
    return pl.pallas_call(
        body, name="assemble_dproj", grid=(seq // tm,),
        in_specs=[row(W_SB), pair(0), pair(1), pair(0), pair(1), row(MIX)] + [row(W_DIL)] * 3 +
                 [row(2 * W_DIL), const((2, 1, 2 * LANES)), row(2 * LANES), row(2 * LANES), row(2 * LANES),
                  const((2 * LANES, 2 * LANES))],
        out_specs=[row(IN_COLS), const((2, 8, 2 * LANES))],
        out_shape=[jax.ShapeDtypeStruct((seq, IN_COLS), BF16), jax.ShapeDtypeStruct((2, 8, 2 * LANES), F32)],
        compiler_params=_params(48),
    )(d_qa, d_ka, d_ka, d_va, d_va, d_gates, d_qd, d_kd, d_vd, qk_raw, gains, *tables, bd)


def _dw_in(h, dproj):
    seq, d = h.shape
    tm = min(2048, seq)

    def body(h_ref, dp_ref, dw_ref):
        @pl.when(pl.program_id(1) == 0)
        def _():
            dw_ref[...] = jnp.zeros_like(dw_ref)

        dw_ref[...] += _dot_tn(h_ref[...], dp_ref[...])

    return pl.pallas_call(
        body, name="dw_in", grid=(N_DEV, seq // tm),
        in_specs=[pl.BlockSpec((tm, d), lambda n, i: (i, 0)), pl.BlockSpec((tm, COLS_PER_DEV), lambda n, i: (i, n))],
        out_specs=pl.BlockSpec((None, d, COLS_PER_DEV), lambda n, i: (n, 0, 0)),
        out_shape=jax.ShapeDtypeStruct((N_DEV, d, COLS_PER_DEV), F32),
        compiler_params=_params(40),
    )(h, dproj)


def _dx_norm(dproj, w_l, x, g, dx_next, dwin_l, dwout_l, rin, rout, layer):
    seq, d = x.shape
    tm = min(256, seq)
    steps = seq // tm

    def body(dp_ref, w_ref, x_ref, g_ref, dn_ref, dwin_ref, dwout_ref, rin_in, rout_in, dx_ref, gpart_ref, rin_ref, rout_ref, *sems):
        del rin_in, rout_in
        copies = lambda: _to_every_device(lambda to: (dwin_ref.at[to], dwout_ref.at[to]),
                                          lambda me: (rin_ref.at[me, layer], rout_ref.at[me, layer]), sems)

        @pl.when(pl.program_id(0) == 0)
        def _():
            gpart_ref[...] = jnp.zeros_like(gpart_ref)
            for cp in copies():
                cp.start()

        dh = jnp.zeros((tm, d), F32)
        for n in range(N_DEV):
            dh = dh + _dot_nt(dp_ref[:, n * COLS_PER_DEV:(n + 1) * COLS_PER_DEV], w_ref[n])
        xf = x_ref[...]
        rs = lax.rsqrt(jnp.mean(xf * xf, axis=-1, keepdims=True) + EPS)
        xhat = xf * rs
        gpart_ref[...] += jnp.sum((dh * xhat).reshape(tm // 8, 8, d), axis=0)
        dxhat = dh * g_ref[...]
        mean = jnp.mean(dxhat * xhat, axis=-1, keepdims=True)
        dx_ref[...] = rs * (dxhat - xhat * mean) + dn_ref[...]

        @pl.when(pl.program_id(0) == steps - 1)
        def _():
            for cp in copies():
                cp.wait()

    row = lambda w: pl.BlockSpec((tm, w), lambda i: (i, 0))
    any_spec = pl.BlockSpec(memory_space=pl.ANY)
    return pl.pallas_call(
        body, name="dx_norm_exchange", grid=(steps,),
        in_specs=[row(IN_COLS), pl.BlockSpec((N_DEV, d, COLS_PER_DEV), lambda i: (0, 0, 0)), row(d),
                  pl.BlockSpec((1, d), lambda i: (0, 0)), row(d), any_spec, any_spec, any_spec, any_spec],
        out_specs=[row(d), pl.BlockSpec((8, d), lambda i: (0, 0)), any_spec, any_spec],
        out_shape=[jax.ShapeDtypeStruct((seq, d), F32), jax.ShapeDtypeStruct((8, d), F32),
                   jax.ShapeDtypeStruct(rin.shape, F32), jax.ShapeDtypeStruct(rout.shape, F32)],
        scratch_shapes=_copy_sems(2), input_output_aliases={7: 2, 8: 3},
        compiler_params=_params(48),
    )(dproj, w_l, x, g, dx_next, dwin_l, dwout_l, rin, rout)


def _exchange_small(small):
    def body(small_ref, out_ref, *sems):
        copies = _to_every_device(lambda to: (small_ref,), lambda me: (out_ref.at[me],), sems)
        for cp in copies:
            cp.start()
        for cp in copies:
            cp.wait()

    vmem = pl.BlockSpec(memory_space=pltpu.VMEM)
    return pl.pallas_call(
        body, name="exchange_small", in_specs=[vmem], out_specs=vmem,
        out_shape=jax.ShapeDtypeStruct((N_DEV,) + small.shape, F32), scratch_shapes=_copy_sems(1),
    )(small)


def _adamw_math(g, w, m, v):
    m = ADAM_B1 * m + (1.0 - ADAM_B1) * g
    v = ADAM_B2 * v + (1.0 - ADAM_B2) * (g * g)
    m_hat = m / (1.0 - ADAM_B1 ** ADAM_STEP)
    v_hat = v / (1.0 - ADAM_B2 ** ADAM_STEP)
    delta = -ADAM_LR * (m_hat / (jnp.sqrt(v_hat) + ADAM_EPS) + ADAM_WD * w)
    return delta, m, v


def _adamw(parts, w, m, v, name):
    nl, r, c = w.shape
    tr = min(r, (256 * 512) // c)

    def body(p_ref, w_ref, m_ref, v_ref, g_ref, d_ref, nm_ref, nv_ref):
        g = p_ref[0]
        for s in range(1, N_DEV):
            g = g + p_ref[s]
        g_ref[...] = g
        d_ref[...], nm_ref[...], nv_ref[...] = _adamw_math(g, w_ref[...], m_ref[...], v_ref[...])

    blk = pl.BlockSpec((None, tr, c), lambda l, i: (l, i, 0))
    return pl.pallas_call(
        body, name=name, grid=(nl, r // tr),
        in_specs=[pl.BlockSpec((N_DEV, None, tr, c), lambda l, i: (0, l, i, 0)), blk, blk, blk],
        out_specs=[blk] * 4, out_shape=[jax.ShapeDtypeStruct(w.shape, F32)] * 4,
        compiler_params=_params(32),
    )(parts, w, m, v)


def _adamw_small(parts, w, m, v):
    def body(p_ref, w_ref, m_ref, v_ref, g_ref, d_ref, nm_ref, nv_ref):
        g = p_ref[0]
        for s in range(1, N_DEV):
            g = g + p_ref[s]
        g_ref[...] = g
        d_ref[...], nm_ref[...], nv_ref[...] = _adamw_math(g, w_ref[...], m_ref[...], v_ref[...])

    vmem = pl.BlockSpec(memory_space=pltpu.VMEM)
    return pl.pallas_call(
        body, name="adamw_small", in_specs=[vmem] * 4, out_specs=[vmem] * 4,
        out_shape=[jax.ShapeDtypeStruct(w.shape, F32)] * 4,
    )(parts, w, m, v)


def _pack_small(a, b, c):
    pad = jnp.zeros((a.shape[0], SMALL_W - a.shape[1] - b.shape[1] - c.shape[1]), F32)
    return jnp.concatenate([a, b, c, pad], axis=1)


def _unpack_small(t, d):
    return t[:, :d], t[:, d:d + HEAD_DIM], t[:, d + HEAD_DIM:d + 2 * HEAD_DIM]


def kernel(x, norm_g, w_in, q_norm_g, k_norm_g, w_out, loss_target, m_norm_g, m_w_in, m_q_norm_g, m_k_norm_g, m_w_out,
           v_norm_g, v_w_in, v_q_norm_g, v_k_norm_g, v_w_out):
    depth, d, _ = w_in.shape
    seq = x.shape[1]
    tri_a, tri_b, bd = _tri_constants()
    tables = _rope_tables(seq)
    rep = (2 * LANES) // HEAD_DIM

    win_bf, wout_bf = _cast_bf16(w_in, "cast_w_in"), _cast_bf16(w_out, "cast_w_out")
    win_l, wout_l = _gather_weights(win_bf, wout_bf, 0)
    saved = []
    xl = x.reshape(seq, d)
    for layer in range(depth):
        gains = jnp.stack([jnp.tile(q_norm_g[layer], rep), jnp.tile(k_norm_g[layer], rep)])[:, None, :]
        if layer + 1 < depth:
            proj_bf, gates, qk_raw, h, win_next, wout_next = _norm_proj(
                xl, norm_g[layer][None, :], win_l, prefetch=(win_bf, wout_bf, layer + 1))
        else:
            proj_bf, gates, qk_raw, h = _norm_proj(xl, norm_g[layer][None, :], win_l)
            win_next = wout_next = None
        qkn = _qk_prep(qk_raw, gains, tables, bd)
        oa = _sb_fwd(proj_bf, tri_a)
        o_p, ld_p = zip(*[_dil_fwd(qkn, qk_raw, r) for _, r in DIL_PATTERNS])
        x_next, cat, od, lse = _out_proj(xl, oa, o_p, ld_p, gates, wout_l)
        saved.append((xl, gains, proj_bf, gates, qk_raw, h, qkn, oa, cat, od, lse, win_l, wout_l))
        xl, win_l, wout_l = x_next, win_next, wout_next

    loss_part, dx = _loss_head(xl, loss_target.reshape(seq, d))
    loss = lax.psum(jnp.sum(loss_part), ("x", "y", "c"))

    rin = jnp.zeros((N_DEV, depth, d, COLS_PER_DEV), F32)
    rout = jnp.zeros((N_DEV, depth, ROWS_PER_DEV, d), F32)
    g_norm, g_q, g_k = [None] * depth, [None] * depth, [None] * depth
    for layer in reversed(range(depth)):
        xl, gains, proj_bf, gates, qk_raw, h, qkn, oa, cat, od, lse, win_l, wout_l = saved[layer]
        d_oa, d_od, delta, d_gates, dwout_l = _out_proj_bwd(dx, wout_l, cat, gates, oa, od, bd)
        d_qa, d_ka, d_va = _sb_bwd(proj_bf, d_oa, oa, tri_a, tri_b)
        d_dil = None
        for _, r in DIL_PATTERNS:
            d_dil = _dil_bwd(qkn, qk_raw, d_od, lse, delta, r, others=d_dil)
        dproj, gqk = _assemble_dproj(d_qa, d_ka, d_va, d_gates, *d_dil, qk_raw, gains, tables, bd)
        dwin_l = _dw_in(h, dproj)
        dx, gn, rin, rout = _dx_norm(dproj, win_l, xl, norm_g[layer][None, :], dx, dwin_l, dwout_l, rin, rout, layer)
        g_norm[layer] = jnp.sum(gn, axis=0)
        gqk = jnp.sum(gqk, axis=1).reshape(2, rep, HEAD_DIM).sum(axis=1)
        g_q[layer], g_k[layer] = gqk[0], gqk[1]
    rsmall = _exchange_small(_pack_small(jnp.stack(g_norm), jnp.stack(g_q), jnp.stack(g_k)))

    g_in, d_in, nm_in, nv_in = _adamw(rin, w_in, m_w_in, v_w_in, "adamw_w_in")
    g_out, d_out, nm_out, nv_out = _adamw(rout, w_out, m_w_out, v_w_out, "adamw_w_out")
    small_out = _adamw_small(rsmall, _pack_small(norm_g, q_norm_g, k_norm_g), _pack_small(m_norm_g, m_q_norm_g, m_k_norm_g),
                             _pack_small(v_norm_g, v_q_norm_g, v_k_norm_g))
    (g_n, g_qn, g_kn), (d_n, d_qn, d_kn), (nm_n, nm_qn, nm_kn), (nv_n, nv_qn, nv_kn) = (_unpack_small(t, d) for t in small_out)

    return (loss, dx.reshape(x.shape), g_n, g_in, g_qn, g_kn, g_out, d_n, d_in, d_qn, d_kn, d_out,
            nm_n, nm_in, nm_qn, nm_kn, nm_out, nv_n, nv_in, nv_qn, nv_kn, nv_out)
```

```python
import math

import jax
import jax.numpy as jnp
from jax import lax
from jax.experimental import pallas as pl
from jax.experimental.pallas import tpu as pltpu

F32 = jnp.float32
BF16 = jnp.bfloat16

EPS = 1e-6
HEAD_DIM = 64
BLOCK = 128
LANES = 128
W_SB = 256
W_DIL = 768
MIX = W_SB + W_DIL
IN_COLS = 4 * W_SB + 4 * W_DIL
N_DEV = 8
COLS_PER_DEV = IN_COLS // N_DEV
ROWS_PER_DEV = MIX // N_DEV
QK_SCALE = 1.0 / math.sqrt(HEAD_DIM)
DIL_PATTERNS = ((128, 1), (512, 4), (2048, 16))
DIL_TILE = 2048
DIL_GROUP_FWD = 4
DIL_GROUP_BWD = 4
SPLIT = 4
ROPE_THETA = 500000.0
ROPE_DIM = HEAD_DIM // 4
ROPE_HALF = ROPE_DIM // 2
DEAD_LOG = -110.0
SB_PEEL = 3
SB_QBLOCKS = 2
SB_SUM_TERMS = 2
NEG_BIG = -1e30

ADAM_LR = 0.001
ADAM_B1 = 0.9
ADAM_B2 = 0.999
ADAM_EPS = 1e-08
ADAM_WD = 0.01
ADAM_STEP = 10

SMALL_W = 1280
MESH_ID = pl.DeviceIdType.MESH
MIB = 1 << 20


def _params(vmem_mib):
    return pltpu.CompilerParams(vmem_limit_bytes=vmem_mib * MIB)


def _dot(a, b):
    return jnp.dot(a, b, preferred_element_type=F32)


def _dot_nt(a, b):
    return lax.dot_general(a, b, (((1,), (1,)), ((), ())), preferred_element_type=F32)


def _dot_tn(a, b):
    return lax.dot_general(a, b, (((0,), (0,)), ((), ())), preferred_element_type=F32)


def _dot_exact(x, m01, terms=3):
    out = 0.0
    for _ in range(terms):
        part = x.astype(BF16)
        out = out + _dot(part, m01)
        x = x - part.astype(F32)
    return out


def _lane_lo(shape):
    return lax.broadcasted_iota(jnp.int32, shape, 1) < HEAD_DIM


def _tri_constants():
    j = jnp.arange(BLOCK)
    ones = jnp.ones((BLOCK, BLOCK), F32)
    excl = (j[:, None] > j[None, :]).astype(F32)
    incl = (j[:, None] >= j[None, :]).astype(F32)
    tri_a = jnp.concatenate([excl, ones], axis=1).astype(BF16)
    tri_b = jnp.concatenate([incl, ones], axis=1).astype(BF16)
    d = jnp.arange(2 * LANES)
    bd = (d[:, None] // HEAD_DIM == d[None, :] // HEAD_DIM).astype(BF16)
    return tri_a, tri_b, bd


def _rope_tables(seq):
    d = jnp.arange(2 * LANES) % HEAD_DIM
    inv_freq = 1.0 / (ROPE_THETA ** ((d % ROPE_HALF).astype(F32) * 2.0 / ROPE_DIM))
    ang = jnp.arange(seq).astype(F32)[:, None] * inv_freq[None, :]
    cos, sin = jnp.cos(ang), jnp.sin(ang)
    c = jnp.where(d < ROPE_DIM, cos, 1.0)
    s_next = jnp.where(d < ROPE_HALF, -sin, 0.0)
    s_prev = jnp.where((d >= ROPE_HALF) & (d < ROPE_DIM), sin, 0.0)
    return c, s_next, s_prev


def _roll_lanes(x, shift):
    return jnp.concatenate([pltpu.roll(x[:, :LANES], shift, 1), pltpu.roll(x[:, LANES:], shift, 1)], axis=1)


def _rope(x, c, s_next, s_prev):
    return x * c + _roll_lanes(x, LANES - ROPE_HALF) * s_next + _roll_lanes(x, ROPE_HALF) * s_prev


def _rope_t(dy, c, s_next, s_prev):
    return dy * c + _roll_lanes(dy * s_next, ROPE_HALF) + _roll_lanes(dy * s_prev, LANES - ROPE_HALF)


def _cast_bf16(w, name):
    nl, r, c = w.shape

    def body(w_ref, o_ref):
        o_ref[...] = w_ref[...].astype(BF16)

    return pl.pallas_call(
        body, name=name, grid=(nl,),
        in_specs=[pl.BlockSpec((None, r, c), lambda l: (l, 0, 0))],
        out_specs=pl.BlockSpec((None, r, c), lambda l: (l, 0, 0)),
        out_shape=jax.ShapeDtypeStruct(w.shape, BF16),
        compiler_params=_params(24),
    )(w)


def _flips():
    return [(dx, dy, dc) for dx in (0, 1) for dy in (0, 1) for dc in (0, 1) if (dx, dy, dc) != (0, 0, 0)]


def _place():
    x, y, c = lax.axis_index("x"), lax.axis_index("y"), lax.axis_index("c")
    return x, y, c, 4 * x + 2 * y + c


def _peer(x, y, c, flip):
    dx, dy, dc = flip
    return (1 - x if dx else x, 1 - y if dy else y, 1 - c if dc else c)


def _to_every_device(srcs_for, dsts_at, sems):
    send_sems, recv_sems, local_sems = sems
    x, y, c, me = _place()
    dsts = dsts_at(me)
    n = len(dsts)
    copies = [pltpu.make_async_copy(src, dst, local_sems.at[a]) for a, (src, dst) in enumerate(zip(srcs_for(me), dsts))]
    for k, flip in enumerate(_flips()):
        px, py, pc = _peer(x, y, c, flip)
        for a, (src, dst) in enumerate(zip(srcs_for(4 * px + 2 * py + pc), dsts)):
            copies.append(pltpu.make_async_remote_copy(
                src_ref=src, dst_ref=dst, send_sem=send_sems.at[n * k + a], recv_sem=recv_sems.at[n * k + a],
                device_id=(px, py, pc), device_id_type=MESH_ID))
    return copies


def _copy_sems(n):
    remote = n * (N_DEV - 1)
    return [pltpu.SemaphoreType.DMA((remote,)), pltpu.SemaphoreType.DMA((remote,)), pltpu.SemaphoreType.DMA((n,))]


def _weight_copies(win_ref, wout_ref, layer, oin_ref, oout_ref, sems):
    return _to_every_device(lambda to: (win_ref.at[layer], wout_ref.at[layer]), lambda me: (oin_ref.at[me], oout_ref.at[me]), sems)


def _gathered_shapes(win_bf, wout_bf):
    return [jax.ShapeDtypeStruct((N_DEV,) + win_bf.shape[1:], BF16), jax.ShapeDtypeStruct((N_DEV,) + wout_bf.shape[1:], BF16)]


def _gather_weights(win_bf, wout_bf, layer):
    def body(win_ref, wout_ref, oin_ref, oout_ref, *sems):
        copies = _weight_copies(win_ref, wout_ref, layer, oin_ref, oout_ref, sems)
        for cp in copies:
            cp.start()
        for cp in copies:
            cp.wait()

    any_spec = pl.BlockSpec(memory_space=pl.ANY)
    return pl.pallas_call(
        body, name="gather_weights", in_specs=[any_spec, any_spec], out_specs=[any_spec, any_spec],
        out_shape=_gathered_shapes(win_bf, wout_bf), scratch_shapes=_copy_sems(2),
    )(win_bf, wout_bf)


_F32_ROUTES = {1: ((256, 256, 0, 0),), 2: ((0, 512, 1, 0),), 3: ((0, 512, 1, 512),), 4: ((0, 512, 1, 1024),),
               5: ((0, 512, 1, 1536),), 6: ((0, 256, 1, 2048), (256, 256, 0, 256)), 7: ((0, 512, 0, 512),)}


def _norm_proj(x, g, w_l, prefetch=None):
    seq, d = x.shape
    tm = min(256, seq)
    steps = seq // tm

    def body(x_ref, g_ref, w_ref, *rest):
        if prefetch is None:
            pbf_ref, gates_ref, qk_ref, h_ref = rest
        else:
            win_ref, wout_ref, pbf_ref, gates_ref, qk_ref, h_ref, oin_ref, oout_ref, *sems = rest
            copies = lambda: _weight_copies(win_ref, wout_ref, prefetch[2], oin_ref, oout_ref, sems)

            @pl.when(pl.program_id(0) == 0)
            def _():
                for cp in copies():
                    cp.start()

        xf = x_ref[...]
        rs = lax.rsqrt(jnp.mean(xf * xf, axis=-1, keepdims=True) + EPS)
        h = (xf * rs * g_ref[...]).astype(BF16)
        h_ref[...] = h
        targets = (gates_ref, qk_ref)
        for n in range(N_DEV):
            acc = _dot(h, w_ref[n])
            pbf_ref[:, n * COLS_PER_DEV:(n + 1) * COLS_PER_DEV] = acc.astype(BF16)
            for lo, width, tgt, dst in _F32_ROUTES.get(n, ()):
                targets[tgt][:, dst:dst + width] = acc[:, lo:lo + width]

        if prefetch is not None:
            @pl.when(pl.program_id(0) == steps - 1)
            def _():
                for cp in copies():
                    cp.wait()

    row = lambda w: pl.BlockSpec((tm, w), lambda i: (i, 0))
    any_spec = pl.BlockSpec(memory_space=pl.ANY)
    in_specs = [row(d), pl.BlockSpec((1, d), lambda i: (0, 0)), pl.BlockSpec((N_DEV, d, COLS_PER_DEV), lambda i: (0, 0, 0))]
    out_specs = [row(IN_COLS), row(MIX), row(3 * W_DIL), row(d)]
    out_shape = [jax.ShapeDtypeStruct((seq, IN_COLS), BF16), jax.ShapeDtypeStruct((seq, MIX), F32),
                 jax.ShapeDtypeStruct((seq, 3 * W_DIL), F32), jax.ShapeDtypeStruct((seq, d), BF16)]
    if prefetch is None:
        return pl.pallas_call(body, name="norm_proj", grid=(steps,), in_specs=in_specs, out_specs=out_specs,
                              out_shape=out_shape, compiler_params=_params(48))(x, g, w_l)
    return pl.pallas_call(
        body, name="norm_proj_gather", grid=(steps,), in_specs=in_specs + [any_spec, any_spec],
        out_specs=out_specs + [any_spec, any_spec], out_shape=out_shape + _gathered_shapes(*prefetch[:2]),
        scratch_shapes=_copy_sems(2), compiler_params=_params(48),
    )(x, g, w_l, *prefetch[:2])


def _head_sums(v, bd):
    hi = v.astype(BF16)
    lo = (v - hi.astype(F32)).astype(BF16)
    return _dot(hi, bd) + _dot(lo, bd)


def _qk_prep(qk_raw, gains, tables, bd):
    seq = qk_raw.shape[0]
    tm = min(1024, seq)
    chunks = W_DIL // (2 * LANES)

    def body(x_ref, g_ref, c_ref, sn_ref, sp_ref, bd_ref, o_ref):
        j = pl.program_id(1)
        x = x_ref[...]
        rs = lax.rsqrt(_head_sums(x * x, bd_ref[...]) * (1.0 / HEAD_DIM) + EPS)
        y = _rope(x * rs * g_ref[...], c_ref[...], sn_ref[...], sp_ref[...])
        o_ref[...] = y * jnp.where(j < chunks, QK_SCALE, 1.0)

    tab = pl.BlockSpec((tm, 2 * LANES), lambda i, j: (i, 0))
    return pl.pallas_call(
        body, name="qk_prep", grid=(seq // tm, 2 * chunks),
        in_specs=[pl.BlockSpec((tm, 2 * LANES), lambda i, j: (i, j)),
                  pl.BlockSpec((None, 1, 2 * LANES), lambda i, j: (j // chunks, 0, 0)),
                  tab, tab, tab, pl.BlockSpec((2 * LANES, 2 * LANES), lambda i, j: (0, 0))],
        out_specs=pl.BlockSpec((tm, 2 * LANES), lambda i, j: (i, j)),
        out_shape=jax.ShapeDtypeStruct((seq, 2 * W_DIL), F32),
        compiler_params=_params(32),
    )(qk_raw, gains, *tables, bd)


def _stack_heads(x, lo):
    return jnp.concatenate([jnp.where(lo, x, 0.0), jnp.where(lo, 0.0, x)], axis=0)


def _unstack_heads(y, lo):
    return jnp.where(lo, y[:BLOCK], y[BLOCK:])


def _stacked_causal():
    row = lax.broadcasted_iota(jnp.int32, (2 * BLOCK, BLOCK), 0) & (BLOCK - 1)
    return lax.broadcasted_iota(jnp.int32, (2 * BLOCK, BLOCK), 1) < row


def _keep(x, *conds):
    for cond in conds:
        if cond is not None:
            x = jnp.where(cond, x, 0.0)
    return x


def _sb_weights(chains, tri):
    zs = [[_dot_nt(qs, kb) for kb in kbs] for qs, kbs, _, _, _ in chains]
    lss = [[jnp.minimum(z, 0.0) - jnp.log1p(jnp.exp(-jnp.abs(z))) for z in zc] for zc in zs]
    cts = [[_dot_exact(_keep(ls - z, mask, live), tri, SB_SUM_TERMS) for ls, z, mask, live in zip(lsc, zc, masks, lives)]
           for lsc, zc, (_, _, _, masks, lives) in zip(lss, zs, chains)]
    out = []
    for lsc, ctc, (_, _, r, masks, lives) in zip(lss, cts, chains):
        weights = []
        for ls, ct, mask, live in zip(lsc, ctc, masks, lives):
            weights.append(_keep(jnp.exp(ls + ct[:, :BLOCK] + r), mask, live))
            r = r + ct[:, BLOCK:]
        out.append((lsc, weights, r))
    return out


def _sb_peel(i, causal):
    return ([jnp.maximum(i - k, 0) for k in range(SB_PEEL)], [causal] + [None] * (SB_PEEL - 1),
            [None] + [i >= k for k in range(1, SB_PEEL)])


def _block_rows(j):
    return pl.ds(pl.multiple_of(j * BLOCK, BLOCK), BLOCK)


def _sb_fwd(proj_bf, tri_a):
    seq = proj_bf.shape[0]
    pairs = W_SB // LANES

    def body(q_ref, k_ref, v_ref, tri_ref, o_ref, r_ref, acc_ref):
        t = pl.program_id(1)
        lo = _lane_lo((BLOCK, LANES))
        causal = _stacked_causal()
        tri = tri_ref[...]
        qblocks = [t * SB_QBLOCKS + c for c in range(SB_QBLOCKS)]
        qss = [_stack_heads(q_ref[c * BLOCK:(c + 1) * BLOCK, :].astype(F32) * QK_SCALE, lo).astype(BF16)
               for c in range(SB_QBLOCKS)]

        def blocks(specs):
            rows = [[_block_rows(j) for j in js] for _, js, _, _, _ in specs]
            res = _sb_weights([(qs, [k_ref[rw, :] for rw in rws], r, masks, lives)
                               for (qs, _, r, masks, lives), rws in zip(specs, rows)], tri)
            outs = []
            for (_, weights, r), rws in zip(res, rows):
                out = 0.0
                for a, rw in zip(weights, rws):
                    a_hi = a.astype(BF16)
                    a_lo = (a - a_hi.astype(F32)).astype(BF16)
                    vb = v_ref[rw, :]
                    out = out + _dot(a_hi, vb) + _dot(a_lo, vb)
                outs.append((out, r))
            return outs

        def peel(qs, i):
            js, masks, lives = _sb_peel(i, causal)
            return qs, js, jnp.zeros((2 * BLOCK, LANES), F32), masks, lives

        for c, (out, r) in enumerate(blocks([peel(qs, i) for qs, i in zip(qss, qblocks)])):
            acc_ref[c], r_ref[c] = out, r

        for c in range(SB_QBLOCKS):
            def alive(c=c):
                return (jnp.max(r_ref[c]) > DEAD_LOG).astype(jnp.int32)

            def step(carry, c=c):
                (out, r_ref[c]), = blocks([(qss[c], [carry[0]], r_ref[c], [None], [None])])
                acc_ref[c] += out
                return carry[0] - 1, alive(c)

            lax.while_loop(lambda carry: jnp.logical_and(carry[0] >= 0, carry[1] > 0), step, (qblocks[c] - SB_PEEL, alive()))
            o_ref[c * BLOCK:(c + 1) * BLOCK, :] = _unstack_heads(acc_ref[c], lo)

    qtile = SB_QBLOCKS * BLOCK
    state = pltpu.VMEM((SB_QBLOCKS, 2 * BLOCK, LANES), F32)
    return pl.pallas_call(
        body, name="sb_fwd", grid=(pairs, seq // qtile),
        in_specs=[pl.BlockSpec((qtile, LANES), lambda p, t: (t, p)),
                  pl.BlockSpec((seq, LANES), lambda p, t: (0, pairs + p)),
                  pl.BlockSpec((seq, LANES), lambda p, t: (0, 2 * pairs + p)),
                  pl.BlockSpec((BLOCK, 2 * BLOCK), lambda p, t: (0, 0))],
        out_specs=pl.BlockSpec((qtile, LANES), lambda p, t: (t, p)),
        out_shape=jax.ShapeDtypeStruct((seq, W_SB), F32),
        scratch_shapes=[state, state],
        compiler_params=_params(40),
    )(proj_bf, proj_bf, proj_bf, tri_a)


def _class_rows(cls, first, count, r):
    c = cls[0] + SPLIT * cls[1] if isinstance(cls, tuple) else cls
    start = c + first * r
    return pl.ds(start, count) if r == 1 else pl.ds(start, count, stride=r)


def _class_reader(r, refs, scratches):
    if r != SPLIT * SPLIT:
        return lambda ref, cls, first, count: ref[_class_rows(cls, first, count, r), :]
    slabs = {}
    for ref, scr in zip(refs, scratches):
        for lo in range(SPLIT):
            scr[lo] = ref[pl.ds(lo, ref.shape[0] // SPLIT, stride=SPLIT), :]
        slabs[id(ref)] = scr

    def take(ref, cls, first, count):
        lo, hi = cls
        return slabs[id(ref)][lo, pl.ds(hi + SPLIT * first, count, stride=SPLIT), :]

    return take


def _class_scratches(r, shapes):
    return [pltpu.VMEM((SPLIT, rows // SPLIT, LANES), F32) for rows in shapes] if r == SPLIT * SPLIT else []


def _dil_tiling(seq, r):
    tile = min(DIL_TILE, seq)
    edge = BLOCK * r
    return tile, edge, tile // edge


def _band_mask(n_rows, n_keys):
    row = lax.broadcasted_iota(jnp.int32, (n_rows, n_keys), 0) & (BLOCK - 1)
    col = lax.broadcasted_iota(jnp.int32, (n_rows, n_keys), 1)
    return jnp.logical_and(col >= row, col <= row + BLOCK), col


def _for_each_group(r, per, g, run):
    loop = lambda lo, hi, fn: lax.fori_loop(lo, hi, lambda t, carry: (fn(t), carry)[1], 0)
    if per <= g:
        n_cls = g // per
        assert g % per == 0 and r % n_cls == 0 and (r != SPLIT * SPLIT or n_cls == SPLIT)
        cls = (lambda t, u: (u, t)) if r == SPLIT * SPLIT else (lambda t, u: t * n_cls + u)
        group = lambda t: [(cls(t, u), a, a == 0) for u in range(n_cls) for a in range(per)]
        if r == n_cls:
            run(group(0))
        else:
            loop(0, r // n_cls, lambda t: run(group(t)))
        return

    assert per % g == 0

    def one_class(c):
        run([(c, 0, True)] + [(c, a, False) for a in range(1, g)])
        loop(1, per // g, lambda t: run([(c, t * g + u, False) for u in range(g)]))

    if r == 1:
        one_class(0)
    else:
        loop(0, r, one_class)


def _dil_keys(blocks, take, cur_ref, before_ref):
    out = []
    for cls, a, first in blocks:
        if first:
            both = jnp.concatenate([take(before_ref, cls, 0, BLOCK), take(cur_ref, cls, 0, BLOCK)], axis=0)
        else:
            both = take(cur_ref, cls, (a - 1) * BLOCK, 2 * BLOCK)
        out.append(both.astype(BF16))
    return out


def _dil_fwd(qkn, qkv_raw, r):
    seq = qkn.shape[0]
    tile, edge, per = _dil_tiling(seq, r)
    pairs = W_DIL // LANES

    def body(q_ref, kc_ref, kp_ref, vc_ref, vp_ref, o_ref, ld_ref, *scratches):
        n = pl.program_id(1)
        lo = _lane_lo((BLOCK, LANES))
        band, col = _band_mask(2 * BLOCK, 2 * BLOCK)
        first_band = jnp.logical_and(band, jnp.logical_or(col >= BLOCK, n > 0))
        take = _class_reader(r, (q_ref, kc_ref, kp_ref, vc_ref, vp_ref), scratches)

        def run(blocks):
            rows = [_class_rows(c, a * BLOCK, BLOCK, r) for c, a, _ in blocks]
            keys = _dil_keys(blocks, take, kc_ref, kp_ref)
            vals = _dil_keys(blocks, take, vc_ref, vp_ref)
            scores = [_dot_nt(_stack_heads(take(q_ref, c, a * BLOCK, BLOCK), lo).astype(BF16), kb)
                      for (c, a, _), kb in zip(blocks, keys)]
            probs, sums, lds = [], [], []
            for s, (_, _, first) in zip(scores, blocks):
                s = jnp.where(first_band if first else band, s, NEG_BIG)
                m = jnp.max(s, axis=1, keepdims=True)
                p = jnp.exp(s - m)
                l = jnp.sum(p, axis=1, keepdims=True)
                probs.append(p.astype(BF16))
                sums.append(l)
                lds.append(m + jnp.log(l))
            outs = [_dot(p, vb) / l for p, vb, l in zip(probs, vals, sums)]
            for rw, o, ld in zip(rows, outs, lds):
                o_ref[rw, :] = _unstack_heads(o, lo)
                ld_ref[rw, :] = _unstack_heads(jnp.broadcast_to(ld, (2 * BLOCK, LANES)), lo)

        _for_each_group(r, per, DIL_GROUP_FWD, run)

    per_edge = tile // edge
    cur = lambda off: pl.BlockSpec((tile, LANES), lambda p, n: (n, off + p))
    before = lambda off: pl.BlockSpec((edge, LANES), lambda p, n: (jnp.maximum(n * per_edge - 1, 0), off + p))
    return pl.pallas_call(
        body, name=f"dil_fwd_r{r}", grid=(pairs, seq // tile),
        in_specs=[cur(0), cur(pairs), before(pairs), cur(2 * pairs), before(2 * pairs)],
        out_specs=[cur(0), cur(0)],
        out_shape=[jax.ShapeDtypeStruct((seq, W_DIL), F32), jax.ShapeDtypeStruct((seq, W_DIL), F32)],
        scratch_shapes=_class_scratches(r, (tile, tile, edge, tile, edge)),
        compiler_params=_params(32),
    )(qkn, qkn, qkn, qkv_raw, qkv_raw)


def _silu_parts(g):
    sig = jax.nn.sigmoid(g)
    return g * sig, sig * (1.0 + g * (1.0 - sig))


def _out_proj(x, oa, o_p, ld_p, gates, wout_l):
    seq, d = x.shape
    tm = min(256, seq)

    def body(x_ref, oa_ref, o0, o1, o2, l0, l1, l2, g_ref, w_ref, xn_ref, cat_ref, od_ref, lse_ref):
        lds = (l0[...], l1[...], l2[...])
        m = jnp.maximum(jnp.maximum(lds[0], lds[1]), lds[2])
        es = [jnp.exp(v - m) for v in lds]
        tot = es[0] + es[1] + es[2]
        lse_ref[...] = m + jnp.log(tot)
        inv = 1.0 / tot
        od = (es[0] * inv) * o0[...] + (es[1] * inv) * o1[...] + (es[2] * inv) * o2[...]
        od_ref[...] = od
        silu, _ = _silu_parts(g_ref[...])
        cat_ref[:, :W_SB] = (oa_ref[...] * silu[:, :W_SB]).astype(BF16)
        cat_ref[:, W_SB:] = (od * silu[:, W_SB:]).astype(BF16)
        y = x_ref[...]
        for b in range(N_DEV):
            y = y + _dot(cat_ref[:, b * ROWS_PER_DEV:(b + 1) * ROWS_PER_DEV], w_ref[b])
        xn_ref[...] = y

    row = lambda w: pl.BlockSpec((tm, w), lambda i: (i, 0))
    return pl.pallas_call(
        body, name="out_proj", grid=(seq // tm,),
        in_specs=[row(d), row(W_SB)] + [row(W_DIL)] * 6 + [row(MIX),
                  pl.BlockSpec((N_DEV, ROWS_PER_DEV, d), lambda i: (0, 0, 0))],
        out_specs=[row(d), row(MIX), row(W_DIL), row(W_DIL)],
        out_shape=[jax.ShapeDtypeStruct((seq, d), F32), jax.ShapeDtypeStruct((seq, MIX), BF16),
                   jax.ShapeDtypeStruct((seq, W_DIL), F32), jax.ShapeDtypeStruct((seq, W_DIL), F32)],
        compiler_params=_params(48),
    )(x, oa, *o_p, *ld_p, gates, wout_l)


def _loss_head(y, target):
    seq, d = y.shape
    tm = min(512, seq)

    def body(y_ref, t_ref, part_ref, dy_ref):
        @pl.when(pl.program_id(0) == 0)
        def _():
            part_ref[...] = jnp.zeros_like(part_ref)

        diff = y_ref[...] - t_ref[...]
        dy_ref[...] = diff * (1.0 / d)
        part_ref[...] += jnp.sum((diff * diff).reshape(tm // 8, 8, d), axis=0) * (0.5 / d)

    row = pl.BlockSpec((tm, d), lambda i: (i, 0))
    return pl.pallas_call(
        body, name="loss_head", grid=(seq // tm,),
        in_specs=[row, row], out_specs=[pl.BlockSpec((8, d), lambda i: (0, 0)), row],
        out_shape=[jax.ShapeDtypeStruct((8, d), F32), jax.ShapeDtypeStruct((seq, d), F32)],
        compiler_params=_params(32),
    )(y, target)


def _out_proj_bwd(dy, wout_l, cat, gates, oa, od, bd):
    seq, d = dy.shape
    tm = min(256, seq)

    def body(dy_ref, w_ref, cat_ref, g_ref, oa_ref, od_ref, bd_ref, doa_ref, dod_ref, delta_ref, dg_ref, dw_ref, dcat):
        @pl.when(pl.program_id(0) == 0)
        def _():
            dw_ref[...] = jnp.zeros_like(dw_ref)

        dyb = dy_ref[...].astype(BF16)
        dw = _dot_tn(cat_ref[...], dyb)
        for b in range(N_DEV):
            dw_ref[b] += dw[b * ROWS_PER_DEV:(b + 1) * ROWS_PER_DEV, :]
            dcat[:, b * ROWS_PER_DEV:(b + 1) * ROWS_PER_DEV] = _dot_nt(dyb, w_ref[b])
        silu, dsilu = _silu_parts(g_ref[...])
        dc = dcat[...]
        dmix = dc * silu
        oa_v, od_v = oa_ref[...], od_ref[...]
        dg_ref[:, :W_SB] = dc[:, :W_SB] * oa_v * dsilu[:, :W_SB]
        dg_ref[:, W_SB:] = dc[:, W_SB:] * od_v * dsilu[:, W_SB:]
        doa_ref[...] = dmix[:, :W_SB]
        dod = dmix[:, W_SB:]
        dod_ref[...] = dod
        prod = dod * od_v
        for k in range(W_DIL // (2 * LANES)):
            sl = slice(k * 2 * LANES, (k + 1) * 2 * LANES)
            delta_ref[:, sl] = _head_sums(prod[:, sl], bd_ref[...])

    row = lambda w: pl.BlockSpec((tm, w), lambda i: (i, 0))
    slab = pl.BlockSpec((N_DEV, ROWS_PER_DEV, d), lambda i: (0, 0, 0))
    return pl.pallas_call(
        body, name="out_proj_bwd", grid=(seq // tm,),
        in_specs=[row(d), slab, row(MIX), row(MIX), row(W_SB), row(W_DIL),
                  pl.BlockSpec((2 * LANES, 2 * LANES), lambda i: (0, 0))],
        out_specs=[row(W_SB), row(W_DIL), row(W_DIL), row(MIX), slab],
        out_shape=[jax.ShapeDtypeStruct((seq, W_SB), F32), jax.ShapeDtypeStruct((seq, W_DIL), F32),
                   jax.ShapeDtypeStruct((seq, W_DIL), F32), jax.ShapeDtypeStruct((seq, MIX), F32),
                   jax.ShapeDtypeStruct((N_DEV, ROWS_PER_DEV, d), F32)],
        scratch_shapes=[pltpu.VMEM((tm, MIX), F32)],
        compiler_params=_params(48),
    )(dy, wout_l, cat, gates, oa, od, bd)


def _sb_bwd(proj_bf, d_oa, oa, tri_a, tri_b):
    seq = proj_bf.shape[0]
    pairs = W_SB // LANES
    nq = seq // BLOCK

    def body(q_ref, k_ref, v_ref, do_ref, o_ref, tria_ref, trib_ref, dq_ref, dk_hbm, dv_hbm,
             r_ref, sfx_ref, dtot_ref, dq_acc, dk_acc, dv_acc, sems):
        p, t = pl.program_id(0), pl.program_id(1)

        @pl.when(t == 0)
        def _():
            dk_acc[...] = jnp.zeros_like(dk_acc)
            dv_acc[...] = jnp.zeros_like(dv_acc)

        lo = _lane_lo((BLOCK, LANES))
        tri, trib = tria_ref[...], trib_ref[...]
        causal = _stacked_causal()
        qblocks = [t * SB_QBLOCKS + c for c in range(SB_QBLOCKS)]
        qss, doss = [], []
        for c in range(SB_QBLOCKS):
            sl = slice(c * BLOCK, (c + 1) * BLOCK)
            qss.append(_stack_heads(q_ref[sl, :].astype(F32) * QK_SCALE, lo).astype(BF16))
            doss.append(_stack_heads(do_ref[sl, :], lo).astype(BF16))
            o2 = o_ref[sl, :]
            dtot_ref[c] = _dot_exact(doss[c].astype(F32) * jnp.concatenate([o2, o2], axis=0), tri[:, BLOCK:])

        def blocks(specs):
            rows = [[_block_rows(j) for j in js] for _, js, _, _, _, _ in specs]
            kbs = [[k_ref[rw, :] for rw in rws] for rws in rows]
            dovs = [[_dot_nt(doss[c], v_ref[rw, :]) for rw in rws] for (c, *_), rws in zip(specs, rows)]
            res = _sb_weights([(qss[c], kbc, r, masks, lives) for (c, _, r, _, masks, lives), kbc in zip(specs, kbs)], tri)
            pws = [[a * dov for a, dov in zip(weights, dovc)] for (_, weights, _), dovc in zip(res, dovs)]
            cps = [[_dot_exact(pw, trib, SB_SUM_TERMS) for pw in pwc] for pwc in pws]
            dzs, sfxs = [], []
            for (c, _, _, sfx, masks, lives), (lsc, _, _), pwc, cpc in zip(specs, res, pws, cps):
                dzc = []
                for ls, pw, cp, mask, live in zip(lsc, pwc, cpc, masks, lives):
                    beta = jnp.exp(ls)
                    before = dtot_ref[c] - sfx - cp[:, :BLOCK]
                    dzc.append(_keep(pw * (1.0 - beta) - before * beta, mask, live).astype(BF16))
                    sfx = sfx + cp[:, BLOCK:]
                dzs.append(dzc)
                sfxs.append(sfx)
            dqs = []
            for dzc, kbc in zip(dzs, kbs):
                dq = 0.0
                for dzb, kb in zip(dzc, kbc):
                    dq = dq + _dot(dzb, kb)
                dqs.append(dq)
            for (c, *_), dzc, (_, weights, _), rws in zip(specs, dzs, res, rows):
                for dzb, a, rw in zip(dzc, weights, rws):
                    dk_acc[rw, :] += _dot_tn(dzb, qss[c])
                    dv_acc[rw, :] += _dot_tn(a.astype(BF16), doss[c])
            return [(dq, r, sfx) for dq, (_, _, r), sfx in zip(dqs, res, sfxs)]

        def peel(c, i):
            js, masks, lives = _sb_peel(i, causal)
            zero = jnp.zeros((2 * BLOCK, LANES), F32)
            return c, js, zero, zero, masks, lives

        for c, (dq, r, sfx) in enumerate(blocks([peel(c, i) for c, i in enumerate(qblocks)])):
            dq_acc[c], r_ref[c], sfx_ref[c] = dq, r, sfx

        for c in range(SB_QBLOCKS):
            def alive(c=c):
                return (jnp.max(r_ref[c]) > DEAD_LOG).astype(jnp.int32)

            def step(carry, c=c):
                (dq, r_ref[c], sfx_ref[c]), = blocks([(c, [carry[0]], r_ref[c], sfx_ref[c], [None], [None])])
                dq_acc[c] += dq
                return carry[0] - 1, alive(c)

            lax.while_loop(lambda carry: jnp.logical_and(carry[0] >= 0, carry[1] > 0), step, (qblocks[c] - SB_PEEL, alive()))
            dq_ref[c * BLOCK:(c + 1) * BLOCK, :] = _unstack_heads(dq_acc[c], lo) * QK_SCALE

        @pl.when(t == nq // SB_QBLOCKS - 1)
        def _():
            outs = [pltpu.make_async_copy(dk_acc, dk_hbm.at[p], sems.at[0]),
                    pltpu.make_async_copy(dv_acc, dv_hbm.at[p], sems.at[1])]
            for cp in outs:
                cp.start()
            for cp in outs:
                cp.wait()

    qtile = SB_QBLOCKS * BLOCK
    blk = pl.BlockSpec((qtile, LANES), lambda p, t: (t, p))
    const = pl.BlockSpec((BLOCK, 2 * BLOCK), lambda p, t: (0, 0))
    any_spec = pl.BlockSpec(memory_space=pl.ANY)
    state = pltpu.VMEM((SB_QBLOCKS, 2 * BLOCK, LANES), F32)
    return pl.pallas_call(
        body, name="sb_bwd", grid=(pairs, seq // qtile),
        in_specs=[blk, pl.BlockSpec((seq, LANES), lambda p, t: (0, pairs + p)),
                  pl.BlockSpec((seq, LANES), lambda p, t: (0, 2 * pairs + p)), blk, blk, const, const],
        out_specs=[blk, any_spec, any_spec],
        out_shape=[jax.ShapeDtypeStruct((seq, W_SB), F32), jax.ShapeDtypeStruct((pairs, seq, LANES), F32),
                   jax.ShapeDtypeStruct((pairs, seq, LANES), F32)],
        scratch_shapes=[state, state, state, state, pltpu.VMEM((seq, LANES), F32), pltpu.VMEM((seq, LANES), F32),
                        pltpu.SemaphoreType.DMA((2,))],
        compiler_params=_params(56),
    )(proj_bf, proj_bf, proj_bf, d_oa, oa, tri_a, tri_b)


def _dil_bwd(qkn, qkv_raw, d_od, lse, delta, r, others=None):
    seq = qkn.shape[0]
    tile, edge, per = _dil_tiling(seq, r)
    ntile = seq // tile
    pairs = W_DIL // LANES

    def body(qc, doc, lsc, dlc, kc, kp, vc, vp, *rest):
        if others is None:
            dq_ref, dk_ref, dv_ref, dk_carry, dv_carry, *scratches = rest
        else:
            dq_in, dk_in, dv_in, dq_ref, dk_ref, dv_ref, dk_carry, dv_carry, *scratches = rest
        n = pl.program_id(1)

        @pl.when(n == 0)
        def _():
            dk_carry[...] = jnp.zeros_like(dk_carry)
            dv_carry[...] = jnp.zeros_like(dv_carry)

        if others is None:
            dk_ref[...] = dk_carry[...]
            dv_ref[...] = dv_carry[...]
        else:
            dk_ref[...] = dk_carry[...] + dk_in[...]
            dv_ref[...] = dv_carry[...] + dv_in[...]

        @pl.when(n < ntile)
        def _():
            lo = _lane_lo((BLOCK, LANES))
            band, col = _band_mask(2 * BLOCK, 2 * BLOCK)
            first_band = jnp.logical_and(band, jnp.logical_or(col >= BLOCK, n > 0))
            take = _class_reader(r, (qc, doc, lsc, dlc, kc, kp, vc, vp), scratches)

            def stacked_cols(b):
                other = pltpu.roll(b, HEAD_DIM, 1)
                rows = jnp.concatenate([jnp.where(lo, b, other), jnp.where(lo, other, b)], axis=0)
                return jnp.concatenate([rows, rows], axis=1)

            def run(blocks):
                rows = [_class_rows(c, a * BLOCK, BLOCK, r) for c, a, _ in blocks]
                own = lambda ref: [take(ref, c, a * BLOCK, BLOCK) for c, a, _ in blocks]
                keys = _dil_keys(blocks, take, kc, kp)
                vals = _dil_keys(blocks, take, vc, vp)
                qss = [_stack_heads(x, lo).astype(BF16) for x in own(qc)]
                doss = [_stack_heads(x, lo).astype(BF16) for x in own(doc)]
                scores = [_dot_nt(qs, kb) for qs, kb in zip(qss, keys)]
                dps = [_dot_nt(dos, vb) for dos, vb in zip(doss, vals)]
                pws, dss = [], []
                for lsb, dlb, s, dp, (_, _, first) in zip(own(lsc), own(dlc), scores, dps, blocks):
                    pw = jnp.where(first_band if first else band, jnp.exp(s - stacked_cols(lsb)), 0.0)
                    pws.append(pw.astype(BF16))
                    dss.append((pw * (dp - stacked_cols(dlb))).astype(BF16))
                dqs = [_dot(ds, kb) for ds, kb in zip(dss, keys)]
                dks = [_dot_tn(ds, qs) for ds, qs in zip(dss, qss)]
                dvs = [_dot_tn(pw, dos) for pw, dos in zip(pws, doss)]
                for (c, a, first), rw, dq, dk, dv in zip(blocks, rows, dqs, dks, dvs):
                    dq_ref[rw, :] = _unstack_heads(dq, lo)
                    if first:
                        last = _class_rows(c, (per - 1) * BLOCK, BLOCK, r)
                        dk_ref[last, :] += dk[:BLOCK]
                        dv_ref[last, :] += dv[:BLOCK]
                    else:
                        prev = _class_rows(c, (a - 1) * BLOCK, BLOCK, r)
                        dk_carry[prev, :] += dk[:BLOCK]
                        dv_carry[prev, :] += dv[:BLOCK]
                    dk_carry[rw, :] = dk[BLOCK:]
                    dv_carry[rw, :] = dv[BLOCK:]

            _for_each_group(r, per, DIL_GROUP_BWD, run)
            if others is not None:
                dq_ref[...] += dq_in[...]

    per_edge = tile // edge
    here = lambda n: jnp.minimum(n, ntile - 1)
    cur = lambda off: pl.BlockSpec((tile, LANES), lambda p, n: (here(n), off + p))
    before = lambda off: pl.BlockSpec((edge, LANES), lambda p, n: (jnp.maximum(here(n) * per_edge - 1, 0), off + p))
    lagged = pl.BlockSpec((tile, LANES), lambda p, n: (jnp.maximum(n - 1, 0), p))
    carry = pltpu.VMEM((tile, LANES), F32)
    return pl.pallas_call(
        body, name=f"dil_bwd_r{r}", grid=(pairs, ntile + 1),
        in_specs=[cur(0)] * 4 + [cur(pairs), before(pairs), cur(2 * pairs), before(2 * pairs)] +
                 ([] if others is None else [cur(0), lagged, lagged]),
        out_specs=[cur(0), lagged, lagged],
        out_shape=[jax.ShapeDtypeStruct((seq, W_DIL), F32)] * 3,
        scratch_shapes=[carry, carry] + _class_scratches(r, (tile, tile, tile, tile, tile, edge, tile, edge)),
        compiler_params=_params(48),
    )(qkn, d_od, lse, delta, qkn, qkn, qkv_raw, qkv_raw, *(others or ()))


def _assemble_dproj(d_qa, d_ka, d_va, d_gates, d_qd, d_kd, d_vd, qk_raw, gains, tables, bd):
    seq = qk_raw.shape[0]
    tm = min(256, seq)
    chunks = W_DIL // (2 * LANES)

    def body(dqa, dka0, dka1, dva0, dva1, dg, dqd, dkd, dvd, x_ref, g_ref, c_ref, sn_ref, sp_ref, bd_ref, dp_ref, gpart_ref):
        @pl.when(pl.program_id(0) == 0)
        def _():
            gpart_ref[...] = jnp.zeros_like(gpart_ref)

        def put(first_col, v):
            dp_ref[:, first_col:first_col + v.shape[1]] = v.astype(BF16)

        put(0, dqa[...])
        put(W_SB, dka0[...])
        put(W_SB + LANES, dka1[...])
        put(2 * W_SB, dva0[...])
        put(2 * W_SB + LANES, dva1[...])
        put(3 * W_SB, dg[:, :W_SB])
        put(4 * W_SB + 2 * W_DIL, dvd[...])
        put(4 * W_SB + 3 * W_DIL, dg[:, W_SB:])
        c, sn, sp, bdm = c_ref[...], sn_ref[...], sp_ref[...], bd_ref[...]
        for which, part in enumerate((dqd, dkd)):
            scale = QK_SCALE if which == 0 else 1.0
            for k in range(chunks):
                sl = slice(k * 2 * LANES, (k + 1) * 2 * LANES)
                dyv = part[:, sl] * scale
                dxn = _rope_t(dyv, c, sn, sp)
                x = x_ref[:, which * W_DIL + k * 2 * LANES:which * W_DIL + (k + 1) * 2 * LANES]
                rs = lax.rsqrt(_head_sums(x * x, bdm) * (1.0 / HEAD_DIM) + EPS)
                xhat = x * rs
                gpart_ref[which] += jnp.sum((dxn * xhat).reshape(tm // 8, 8, 2 * LANES), axis=0)
                dxhat = dxn * g_ref[which]
                mean = _head_sums(dxhat * xhat, bdm) * (1.0 / HEAD_DIM)
                put(4 * W_SB + which * W_DIL + k * 2 * LANES, rs * (dxhat - xhat * mean))

    row = lambda w: pl.BlockSpec((tm, w), lambda i: (i, 0))
    pair = lambda p: pl.BlockSpec((None, tm, LANES), lambda i: (p, i, 0))
    const = lambda shape: pl.BlockSpec(shape, lambda i: tuple(0 for _ in shape))
    return pl.pallas_call(
        body, name="assemble_dproj", grid=(seq // tm,),
        in_specs=[row(W_SB), pair(0), pair(1), pair(0), pair(1), row(MIX)] + [row(W_DIL)] * 3 +
                 [row(2 * W_DIL), const((2, 1, 2 * LANES)), row(2 * LANES), row(2 * LANES), row(2 * LANES),
                  const((2 * LANES, 2 * LANES))],
        out_specs=[row(IN_COLS), const((2, 8, 2 * LANES))],
        out_shape=[jax.ShapeDtypeStruct((seq, IN_COLS), BF16), jax.ShapeDtypeStruct((2, 8, 2 * LANES), F32)],
        compiler_params=_params(48),
    )(d_qa, d_ka, d_ka, d_va, d_va, d_gates, d_qd, d_kd, d_vd, qk_raw, gains, *tables, bd)


def _dw_in(h, dproj):
    seq, d = h.shape
    tm = min(2048, seq)

    def body(h_ref, dp_ref, dw_ref):
        @pl.when(pl.program_id(1) == 0)
        def _():
            dw_ref[...] = jnp.zeros_like(dw_ref)

        dw_ref[...] += _dot_tn(h_ref[...], dp_ref[...])

    return pl.pallas_call(
        body, name="dw_in", grid=(N_DEV, seq // tm),
        in_specs=[pl.BlockSpec((tm, d), lambda n, i: (i, 0)), pl.BlockSpec((tm, COLS_PER_DEV), lambda n, i: (i, n))],
        out_specs=pl.BlockSpec((None, d, COLS_PER_DEV), lambda n, i: (n, 0, 0)),
        out_shape=jax.ShapeDtypeStruct((N_DEV, d, COLS_PER_DEV), F32),
        compiler_params=_params(40),
    )(h, dproj)


def _dx_norm(dproj, w_l, x, g, dx_next, dwin_l, dwout_l, rin, rout, layer):
    seq, d = x.shape
    tm = min(256, seq)
    steps = seq // tm

    def body(dp_ref, w_ref, x_ref, g_ref, dn_ref, dwin_ref, dwout_ref, rin_in, rout_in, dx_ref, gpart_ref, rin_ref, rout_ref, *sems):
        del rin_in, rout_in
        copies = lambda: _to_every_device(lambda to: (dwin_ref.at[to], dwout_ref.at[to]),
                                          lambda me: (rin_ref.at[me, layer], rout_ref.at[me, layer]), sems)

        @pl.when(pl.program_id(0) == 0)
        def _():
            gpart_ref[...] = jnp.zeros_like(gpart_ref)
            for cp in copies():
                cp.start()

        dh = jnp.zeros((tm, d), F32)
        for n in range(N_DEV):
            dh = dh + _dot_nt(dp_ref[:, n * COLS_PER_DEV:(n + 1) * COLS_PER_DEV], w_ref[n])
        xf = x_ref[...]
        rs = lax.rsqrt(jnp.mean(xf * xf, axis=-1, keepdims=True) + EPS)
        xhat = xf * rs
        gpart_ref[...] += jnp.sum((dh * xhat).reshape(tm // 8, 8, d), axis=0)
        dxhat = dh * g_ref[...]
        mean = jnp.mean(dxhat * xhat, axis=-1, keepdims=True)
        dx_ref[...] = rs * (dxhat - xhat * mean) + dn_ref[...]

        @pl.when(pl.program_id(0) == steps - 1)
        def _():
            for cp in copies():
                cp.wait()

    row = lambda w: pl.BlockSpec((tm, w), lambda i: (i, 0))
    any_spec = pl.BlockSpec(memory_space=pl.ANY)
    return pl.pallas_call(
        body, name="dx_norm_exchange", grid=(steps,),
        in_specs=[row(IN_COLS), pl.BlockSpec((N_DEV, d, COLS_PER_DEV), lambda i: (0, 0, 0)), row(d),
                  pl.BlockSpec((1, d), lambda i: (0, 0)), row(d), any_spec, any_spec, any_spec, any_spec],
        out_specs=[row(d), pl.BlockSpec((8, d), lambda i: (0, 0)), any_spec, any_spec],
        out_shape=[jax.ShapeDtypeStruct((seq, d), F32), jax.ShapeDtypeStruct((8, d), F32),
                   jax.ShapeDtypeStruct(rin.shape, F32), jax.ShapeDtypeStruct(rout.shape, F32)],
        scratch_shapes=_copy_sems(2), input_output_aliases={7: 2, 8: 3},
        compiler_params=_params(48),
    )(dproj, w_l, x, g, dx_next, dwin_l, dwout_l, rin, rout)


def _exchange_small(small):
    def body(small_ref, out_ref, *sems):
        copies = _to_every_device(lambda to: (small_ref,), lambda me: (out_ref.at[me],), sems)
        for cp in copies:
            cp.start()
        for cp in copies:
            cp.wait()

    vmem = pl.BlockSpec(memory_space=pltpu.VMEM)
    return pl.pallas_call(
        body, name="exchange_small", in_specs=[vmem], out_specs=vmem,
        out_shape=jax.ShapeDtypeStruct((N_DEV,) + small.shape, F32), scratch_shapes=_copy_sems(1),
    )(small)


def _adamw_math(g, w, m, v):
    m = ADAM_B1 * m + (1.0 - ADAM_B1) * g
    v = ADAM_B2 * v + (1.0 - ADAM_B2) * (g * g)
    m_hat = m / (1.0 - ADAM_B1 ** ADAM_STEP)
    v_hat = v / (1.0 - ADAM_B2 ** ADAM_STEP)
    delta = -ADAM_LR * (m_hat / (jnp.sqrt(v_hat) + ADAM_EPS) + ADAM_WD * w)
    return delta, m, v


def _adamw(parts, w, m, v, name):
    nl, r, c = w.shape
    tr = min(r, (256 * 512) // c)

    def body(p_ref, w_ref, m_ref, v_ref, g_ref, d_ref, nm_ref, nv_ref):
        g = p_ref[0]
        for s in range(1, N_DEV):
            g = g + p_ref[s]
        g_ref[...] = g
        d_ref[...], nm_ref[...], nv_ref[...] = _adamw_math(g, w_ref[...], m_ref[...], v_ref[...])

    blk = pl.BlockSpec((None, tr, c), lambda l, i: (l, i, 0))
    return pl.pallas_call(
        body, name=name, grid=(nl, r // tr),
        in_specs=[pl.BlockSpec((N_DEV, None, tr, c), lambda l, i: (0, l, i, 0)), blk, blk, blk],
        out_specs=[blk] * 4, out_shape=[jax.ShapeDtypeStruct(w.shape, F32)] * 4,
        compiler_params=_params(32),
    )(parts, w, m, v)


def _adamw_small(parts, w, m, v):
    def body(p_ref, w_ref, m_ref, v_ref, g_ref, d_ref, nm_ref, nv_ref):
        g = p_ref[0]
        for s in range(1, N_DEV):
            g = g + p_ref[s]
        g_ref[...] = g
        d_ref[...], nm_ref[...], nv_ref[...] = _adamw_math(g, w_ref[...], m_ref[...], v_ref[...])

    vmem = pl.BlockSpec(memory_space=pltpu.VMEM)
    return pl.pallas_call(
        body, name="adamw_small", in_specs=[vmem] * 4, out_specs=[vmem] * 4,
        out_shape=[jax.ShapeDtypeStruct(w.shape, F32)] * 4,
    )(parts, w, m, v)


def _pack_small(a, b, c):
    pad = jnp.zeros((a.shape[0], SMALL_W - a.shape[1] - b.shape[1] - c.shape[1]), F32)
    return jnp.concatenate([a, b, c, pad], axis=1)


def _unpack_small(t, d):
    return t[:, :d], t[:, d:d + HEAD_DIM], t[:, d + HEAD_DIM:d + 2 * HEAD_DIM]


def kernel(x, norm_g, w_in, q_norm_g, k_norm_g, w_out, loss_target, m_norm_g, m_w_in, m_q_norm_g, m_k_norm_g, m_w_out,
           v_norm_g, v_w_in, v_q_norm_g, v_k_norm_g, v_w_out):
    depth, d, _ = w_in.shape
    seq = x.shape[1]
    tri_a, tri_b, bd = _tri_constants()
    tables = _rope_tables(seq)
    rep = (2 * LANES) // HEAD_DIM

    win_bf, wout_bf = _cast_bf16(w_in, "cast_w_in"), _cast_bf16(w_out, "cast_w_out")
    win_l, wout_l = _gather_weights(win_bf, wout_bf, 0)
    saved = []
    xl = x.reshape(seq, d)
    for layer in range(depth):
        gains = jnp.stack([jnp.tile(q_norm_g[layer], rep), jnp.tile(k_norm_g[layer], rep)])[:, None, :]
        if layer + 1 < depth:
            proj_bf, gates, qk_raw, h, win_next, wout_next = _norm_proj(
                xl, norm_g[layer][None, :], win_l, prefetch=(win_bf, wout_bf, layer + 1))
        else:
            proj_bf, gates, qk_raw, h = _norm_proj(xl, norm_g[layer][None, :], win_l)
            win_next = wout_next = None
        qkn = _qk_prep(qk_raw, gains, tables, bd)
        oa = _sb_fwd(proj_bf, tri_a)
        o_p, ld_p = zip(*[_dil_fwd(qkn, qk_raw, r) for _, r in DIL_PATTERNS])
        x_next, cat, od, lse = _out_proj(xl, oa, o_p, ld_p, gates, wout_l)
        saved.append((xl, gains, proj_bf, gates, qk_raw, h, qkn, oa, cat, od, lse, win_l, wout_l))
        xl, win_l, wout_l = x_next, win_next, wout_next

    loss_part, dx = _loss_head(xl, loss_target.reshape(seq, d))
    loss = lax.psum(jnp.sum(loss_part), ("x", "y", "c"))

    rin = jnp.zeros((N_DEV, depth, d, COLS_PER_DEV), F32)
    rout = jnp.zeros((N_DEV, depth, ROWS_PER_DEV, d), F32)
    g_norm, g_q, g_k = [None] * depth, [None] * depth, [None] * depth
    for layer in reversed(range(depth)):
        xl, gains, proj_bf, gates, qk_raw, h, qkn, oa, cat, od, lse, win_l, wout_l = saved[layer]
        d_oa, d_od, delta, d_gates, dwout_l = _out_proj_bwd(dx, wout_l, cat, gates, oa, od, bd)
        d_qa, d_ka, d_va = _sb_bwd(proj_bf, d_oa, oa, tri_a, tri_b)
        d_dil = None
        for _, r in DIL_PATTERNS:
            d_dil = _dil_bwd(qkn, qk_raw, d_od, lse, delta, r, others=d_dil)
        dproj, gqk = _assemble_dproj(d_qa, d_ka, d_va, d_gates, *d_dil, qk_raw, gains, tables, bd)
        dwin_l = _dw_in(h, dproj)
        dx, gn, rin, rout = _dx_norm(dproj, win_l, xl, norm_g[layer][None, :], dx, dwin_l, dwout_l, rin, rout, layer)
        g_norm[layer] = jnp.sum(gn, axis=0)
        gqk = jnp.sum(gqk, axis=1).reshape(2, rep, HEAD_DIM).sum(axis=1)
        g_q[layer], g_k[layer] = gqk[0], gqk[1]
    rsmall = _exchange_small(_pack_small(jnp.stack(g_norm), jnp.stack(g_q), jnp.stack(g_k)))

    g_in, d_in, nm_in, nv_in = _adamw(rin, w_in, m_w_in, v_w_in, "adamw_w_in")
    g_out, d_out, nm_out, nv_out = _adamw(rout, w_out, m_w_out, v_w_out, "adamw_w_out")
    small_out = _adamw_small(rsmall, _pack_small(norm_g, q_norm_g, k_norm_g), _pack_small(m_norm_g, m_q_norm_g, m_k_norm_g),
                             _pack_small(v_norm_g, v_q_norm_g, v_k_norm_g))
    (g_n, g_qn, g_kn), (d_n, d_qn, d_kn), (nm_n, nm_qn, nm_kn), (nv_n, nv_qn, nv_kn) = (_unpack_small(t, d) for t in small_out)

    return (loss, dx.reshape(x.shape), g_n, g_in, g_qn, g_kn, g_out, d_n, d_in, d_qn, d_kn, d_out,
            nm_n, nm_in, nm_qn, nm_kn, nm_out, nv_n, nv_in, nv_qn, nv_kn, nv_out)
```

```python
import math

import jax
import jax.numpy as jnp
from jax import lax
from jax.experimental import pallas as pl
from jax.experimental.pallas import tpu as pltpu

F32 = jnp.float32
BF16 = jnp.bfloat16

EPS = 1e-6
HEAD_DIM = 64
BLOCK = 128
LANES = 128
W_SB = 256
W_DIL = 768
MIX = W_SB + W_DIL
IN_COLS = 4 * W_SB + 4 * W_DIL
N_DEV = 8
COLS_PER_DEV = IN_COLS // N_DEV
ROWS_PER_DEV = MIX // N_DEV
QK_SCALE = 1.0 / math.sqrt(HEAD_DIM)
DIL_PATTERNS = ((128, 1), (512, 4), (2048, 16))
DIL_TILE = 2048
DIL_GROUP_FWD = 4
DIL_GROUP_BWD = 4
SPLIT = 4
ROPE_THETA = 500000.0
ROPE_DIM = HEAD_DIM // 4
ROPE_HALF = ROPE_DIM // 2
DEAD_LOG = -110.0
SB_PEEL = 3
SB_QBLOCKS = 2
SB_SUM_TERMS = 2
NEG_BIG = -1e30

ADAM_LR = 0.001
ADAM_B1 = 0.9
ADAM_B2 = 0.999
ADAM_EPS = 1e-08
ADAM_WD = 0.01
ADAM_STEP = 10

SMALL_W = 1280
MESH_ID = pl.DeviceIdType.MESH
MIB = 1 << 20


def _params(vmem_mib):
    return pltpu.CompilerParams(vmem_limit_bytes=vmem_mib * MIB)


def _dot(a, b):
    return jnp.dot(a, b, preferred_element_type=F32)


def _dot_nt(a, b):
    return lax.dot_general(a, b, (((1,), (1,)), ((), ())), preferred_element_type=F32)


def _dot_tn(a, b):
    return lax.dot_general(a, b, (((0,), (0,)), ((), ())), preferred_element_type=F32)


def _dot_exact(x, m01, terms=3):
    out = 0.0
    for _ in range(terms):
        part = x.astype(BF16)
        out = out + _dot(part, m01)
        x = x - part.astype(F32)
    return out


def _lane_lo(shape):
    return lax.broadcasted_iota(jnp.int32, shape, 1) < HEAD_DIM


def _tri_constants():
    j = jnp.arange(BLOCK)
    ones = jnp.ones((BLOCK, BLOCK), F32)
    excl = (j[:, None] > j[None, :]).astype(F32)
    incl = (j[:, None] >= j[None, :]).astype(F32)
    tri_a = jnp.concatenate([excl, ones], axis=1).astype(BF16)
    tri_b = jnp.concatenate([incl, ones], axis=1).astype(BF16)
    d = jnp.arange(2 * LANES)
    bd = (d[:, None] // HEAD_DIM == d[None, :] // HEAD_DIM).astype(BF16)
    return tri_a, tri_b, bd


def _rope_tables(seq):
    d = jnp.arange(2 * LANES) % HEAD_DIM
    inv_freq = 1.0 / (ROPE_THETA ** ((d % ROPE_HALF).astype(F32) * 2.0 / ROPE_DIM))
    ang = jnp.arange(seq).astype(F32)[:, None] * inv_freq[None, :]
    cos, sin = jnp.cos(ang), jnp.sin(ang)
    c = jnp.where(d < ROPE_DIM, cos, 1.0)
    s_next = jnp.where(d < ROPE_HALF, -sin, 0.0)
    s_prev = jnp.where((d >= ROPE_HALF) & (d < ROPE_DIM), sin, 0.0)
    return c, s_next, s_prev


def _roll_lanes(x, shift):
    return jnp.concatenate([pltpu.roll(x[:, :LANES], shift, 1), pltpu.roll(x[:, LANES:], shift, 1)], axis=1)


def _rope(x, c, s_next, s_prev):
    return x * c + _roll_lanes(x, LANES - ROPE_HALF) * s_next + _roll_lanes(x, ROPE_HALF) * s_prev


def _rope_t(dy, c, s_next, s_prev):
    return dy * c + _roll_lanes(dy * s_next, ROPE_HALF) + _roll_lanes(dy * s_prev, LANES - ROPE_HALF)


def _cast_bf16(w, name):
    nl, r, c = w.shape

    def body(w_ref, o_ref):
        o_ref[...] = w_ref[...].astype(BF16)

    return pl.pallas_call(
        body, name=name, grid=(nl,),
        in_specs=[pl.BlockSpec((None, r, c), lambda l: (l, 0, 0))],
        out_specs=pl.BlockSpec((None, r, c), lambda l: (l, 0, 0)),
        out_shape=jax.ShapeDtypeStruct(w.shape, BF16),
        compiler_params=_params(24),
    )(w)


def _flips():
    return [(dx, dy, dc) for dx in (0, 1) for dy in (0, 1) for dc in (0, 1) if (dx, dy, dc) != (0, 0, 0)]


def _place():
    x, y, c = lax.axis_index("x"), lax.axis_index("y"), lax.axis_index("c")
    return x, y, c, 4 * x + 2 * y + c


def _peer(x, y, c, flip):
    dx, dy, dc = flip
    return (1 - x if dx else x, 1 - y if dy else y, 1 - c if dc else c)


def _to_every_device(srcs_for, dsts_at, sems):
    send_sems, recv_sems, local_sems = sems
    x, y, c, me = _place()
    dsts = dsts_at(me)
    n = len(dsts)
    copies = [pltpu.make_async_copy(src, dst, local_sems.at[a]) for a, (src, dst) in enumerate(zip(srcs_for(me), dsts))]
    for k, flip in enumerate(_flips()):
        px, py, pc = _peer(x, y, c, flip)
        for a, (src, dst) in enumerate(zip(srcs_for(4 * px + 2 * py + pc), dsts)):
            copies.append(pltpu.make_async_remote_copy(
                src_ref=src, dst_ref=dst, send_sem=send_sems.at[n * k + a], recv_sem=recv_sems.at[n * k + a],
                device_id=(px, py, pc), device_id_type=MESH_ID))
    return copies


def _copy_sems(n):
    remote = n * (N_DEV - 1)
    return [pltpu.SemaphoreType.DMA((remote,)), pltpu.SemaphoreType.DMA((remote,)), pltpu.SemaphoreType.DMA((n,))]


def _weight_copies(win_ref, wout_ref, layer, oin_ref, oout_ref, sems):
    return _to_every_device(lambda to: (win_ref.at[layer], wout_ref.at[layer]), lambda me: (oin_ref.at[me], oout_ref.at[me]), sems)


def _gathered_shapes(win_bf, wout_bf):
    return [jax.ShapeDtypeStruct((N_DEV,) + win_bf.shape[1:], BF16), jax.ShapeDtypeStruct((N_DEV,) + wout_bf.shape[1:], BF16)]


def _gather_weights(win_bf, wout_bf, layer):
    def body(win_ref, wout_ref, oin_ref, oout_ref, *sems):
        copies = _weight_copies(win_ref, wout_ref, layer, oin_ref, oout_ref, sems)
        for cp in copies:
            cp.start()
        for cp in copies:
            cp.wait()

    any_spec = pl.BlockSpec(memory_space=pl.ANY)
    return pl.pallas_call(
        body, name="gather_weights", in_specs=[any_spec, any_spec], out_specs=[any_spec, any_spec],
        out_shape=_gathered_shapes(win_bf, wout_bf), scratch_shapes=_copy_sems(2),
    )(win_bf, wout_bf)


_QK_BLOCKS = (2, 3, 4)
_F32_ROUTES = {1: ((256, 256, 0, 0),), 2: ((0, 512, 1, 0),), 3: ((0, 512, 1, 512),), 4: ((0, 512, 1, 1024),),
               5: ((0, 512, 1, 1536),), 6: ((0, 256, 1, 2048), (256, 256, 0, 256)), 7: ((0, 512, 0, 512),)}


def _norm_proj(x, g, w_l, gains, tables, bd, prefetch=None):
    seq, d = x.shape
    tm = min(256, seq)
    steps = seq // tm

    def body(x_ref, g_ref, w_ref, gains_ref, c_ref, sn_ref, sp_ref, bd_ref, *rest):
        if prefetch is None:
            pbf_ref, gates_ref, qk_ref, h_ref, qkn_ref = rest
        else:
            win_ref, wout_ref, pbf_ref, gates_ref, qk_ref, h_ref, qkn_ref, oin_ref, oout_ref, *sems = rest
            copies = lambda: _weight_copies(win_ref, wout_ref, prefetch[2], oin_ref, oout_ref, sems)

            @pl.when(pl.program_id(0) == 0)
            def _():
                for cp in copies():
                    cp.start()

        xf = x_ref[...]
        rs = lax.rsqrt(jnp.mean(xf * xf, axis=-1, keepdims=True) + EPS)
        h = (xf * rs * g_ref[...]).astype(BF16)
        h_ref[...] = h
        targets = (gates_ref, qk_ref)
        for n in range(N_DEV):
            acc = _dot(h, w_ref[n])
            pbf_ref[:, n * COLS_PER_DEV:(n + 1) * COLS_PER_DEV] = acc.astype(BF16)
            for lo, width, tgt, dst in _F32_ROUTES.get(n, ()):
                targets[tgt][:, dst:dst + width] = acc[:, lo:lo + width]
            for half in range(2) if n in _QK_BLOCKS else ():
                col = (_QK_BLOCKS.index(n) * 2 + half) * 2 * LANES
                is_q = col < W_DIL
                xq = acc[:, half * 2 * LANES:(half + 1) * 2 * LANES]
                rsq = lax.rsqrt(_head_sums(xq * xq, bd_ref[...]) * (1.0 / HEAD_DIM) + EPS)
                y = _rope(xq * rsq * gains_ref[0 if is_q else 1], c_ref[...], sn_ref[...], sp_ref[...])
                qkn_ref[:, col:col + 2 * LANES] = y * QK_SCALE if is_q else y

        if prefetch is not None:
            @pl.when(pl.program_id(0) == steps - 1)
            def _():
                for cp in copies():
                    cp.wait()

    row = lambda w: pl.BlockSpec((tm, w), lambda i: (i, 0))
    any_spec = pl.BlockSpec(memory_space=pl.ANY)
    const = lambda shape: pl.BlockSpec(shape, lambda i: tuple(0 for _ in shape))
    in_specs = [row(d), const((1, d)), const((N_DEV, d, COLS_PER_DEV)), const((2, 1, 2 * LANES)),
                row(2 * LANES), row(2 * LANES), row(2 * LANES), const((2 * LANES, 2 * LANES))]
    out_specs = [row(IN_COLS), row(MIX), row(3 * W_DIL), row(d), row(2 * W_DIL)]
    out_shape = [jax.ShapeDtypeStruct((seq, IN_COLS), BF16), jax.ShapeDtypeStruct((seq, MIX), F32),
                 jax.ShapeDtypeStruct((seq, 3 * W_DIL), F32), jax.ShapeDtypeStruct((seq, d), BF16),
                 jax.ShapeDtypeStruct((seq, 2 * W_DIL), F32)]
    operands = (x, g, w_l, gains, *tables, bd)
    if prefetch is None:
        return pl.pallas_call(body, name="norm_proj", grid=(steps,), in_specs=in_specs, out_specs=out_specs,
                              out_shape=out_shape, compiler_params=_params(48))(*operands)
    return pl.pallas_call(
        body, name="norm_proj_gather", grid=(steps,), in_specs=in_specs + [any_spec, any_spec],
        out_specs=out_specs + [any_spec, any_spec], out_shape=out_shape + _gathered_shapes(*prefetch[:2]),
        scratch_shapes=_copy_sems(2), compiler_params=_params(48),
    )(*operands, *prefetch[:2])


def _head_sums(v, bd):
    hi = v.astype(BF16)
    lo = (v - hi.astype(F32)).astype(BF16)
    return _dot(hi, bd) + _dot(lo, bd)


def _stack_heads(x, lo):
    return jnp.concatenate([jnp.where(lo, x, 0.0), jnp.where(lo, 0.0, x)], axis=0)


def _unstack_heads(y, lo):
    return jnp.where(lo, y[:BLOCK], y[BLOCK:])


def _stacked_causal():
    row = lax.broadcasted_iota(jnp.int32, (2 * BLOCK, BLOCK), 0) & (BLOCK - 1)
    return lax.broadcasted_iota(jnp.int32, (2 * BLOCK, BLOCK), 1) < row


def _keep(x, *conds):
    for cond in conds:
        if cond is not None:
            x = jnp.where(cond, x, 0.0)
    return x


def _sb_weights(chains, tri):
    zs = [[_dot_nt(qs, kb) for kb in kbs] for qs, kbs, _, _, _ in chains]
    lss = [[jnp.minimum(z, 0.0) - jnp.log1p(jnp.exp(-jnp.abs(z))) for z in zc] for zc in zs]
    cts = [[_dot_exact(_keep(ls - z, mask, live), tri, SB_SUM_TERMS) for ls, z, mask, live in zip(lsc, zc, masks, lives)]
           for lsc, zc, (_, _, _, masks, lives) in zip(lss, zs, chains)]
    out = []
    for lsc, ctc, (_, _, r, masks, lives) in zip(lss, cts, chains):
        weights = []
        for ls, ct, mask, live in zip(lsc, ctc, masks, lives):
            weights.append(_keep(jnp.exp(ls + ct[:, :BLOCK] + r), mask, live))
            r = r + ct[:, BLOCK:]
        out.append((lsc, weights, r))
    return out


def _sb_peel(i, causal):
    return ([jnp.maximum(i - k, 0) for k in range(SB_PEEL)], [causal] + [None] * (SB_PEEL - 1),
            [None] + [i >= k for k in range(1, SB_PEEL)])


def _block_rows(j):
    return pl.ds(pl.multiple_of(j * BLOCK, BLOCK), BLOCK)


def _sb_fwd(proj_bf, tri_a):
    seq = proj_bf.shape[0]
    pairs = W_SB // LANES

    def body(q_ref, k_ref, v_ref, tri_ref, o_ref, r_ref, acc_ref):
        t = pl.program_id(1)
        lo = _lane_lo((BLOCK, LANES))
        causal = _stacked_causal()
        tri = tri_ref[...]
        qblocks = [t * SB_QBLOCKS + c for c in range(SB_QBLOCKS)]
        qss = [_stack_heads(q_ref[c * BLOCK:(c + 1) * BLOCK, :].astype(F32) * QK_SCALE, lo).astype(BF16)
               for c in range(SB_QBLOCKS)]

        def blocks(specs):
            rows = [[_block_rows(j) for j in js] for _, js, _, _, _ in specs]
            res = _sb_weights([(qs, [k_ref[rw, :] for rw in rws], r, masks, lives)
                               for (qs, _, r, masks, lives), rws in zip(specs, rows)], tri)
            outs = []
            for (_, weights, r), rws in zip(res, rows):
                out = 0.0
                for a, rw in zip(weights, rws):
                    a_hi = a.astype(BF16)
                    a_lo = (a - a_hi.astype(F32)).astype(BF16)
                    vb = v_ref[rw, :]
                    out = out + _dot(a_hi, vb) + _dot(a_lo, vb)
                outs.append((out, r))
            return outs

        def peel(qs, i):
            js, masks, lives = _sb_peel(i, causal)
            return qs, js, jnp.zeros((2 * BLOCK, LANES), F32), masks, lives

        for c, (out, r) in enumerate(blocks([peel(qs, i) for qs, i in zip(qss, qblocks)])):
            acc_ref[c], r_ref[c] = out, r

        for c in range(SB_QBLOCKS):
            def alive(c=c):
                return (jnp.max(r_ref[c]) > DEAD_LOG).astype(jnp.int32)

            def step(carry, c=c):
                (out, r_ref[c]), = blocks([(qss[c], [carry[0]], r_ref[c], [None], [None])])
                acc_ref[c] += out
                return carry[0] - 1, alive(c)

            lax.while_loop(lambda carry: jnp.logical_and(carry[0] >= 0, carry[1] > 0), step, (qblocks[c] - SB_PEEL, alive()))
            o_ref[c * BLOCK:(c + 1) * BLOCK, :] = _unstack_heads(acc_ref[c], lo)

    qtile = SB_QBLOCKS * BLOCK
    state = pltpu.VMEM((SB_QBLOCKS, 2 * BLOCK, LANES), F32)
    return pl.pallas_call(
        body, name="sb_fwd", grid=(pairs, seq // qtile),
        in_specs=[pl.BlockSpec((qtile, LANES), lambda p, t: (t, p)),
                  pl.BlockSpec((seq, LANES), lambda p, t: (0, pairs + p)),
                  pl.BlockSpec((seq, LANES), lambda p, t: (0, 2 * pairs + p)),
                  pl.BlockSpec((BLOCK, 2 * BLOCK), lambda p, t: (0, 0))],
        out_specs=pl.BlockSpec((qtile, LANES), lambda p, t: (t, p)),
        out_shape=jax.ShapeDtypeStruct((seq, W_SB), F32),
        scratch_shapes=[state, state],
        compiler_params=_params(40),
    )(proj_bf, proj_bf, proj_bf, tri_a)


def _class_rows(cls, first, count, r):
    c = cls[0] + SPLIT * cls[1] if isinstance(cls, tuple) else cls
    start = c + first * r
    return pl.ds(start, count) if r == 1 else pl.ds(start, count, stride=r)


def _class_reader(r, refs, scratches):
    if r != SPLIT * SPLIT:
        return lambda ref, cls, first, count: ref[_class_rows(cls, first, count, r), :]
    slabs = {}
    for ref, scr in zip(refs, scratches):
        for lo in range(SPLIT):
            scr[lo] = ref[pl.ds(lo, ref.shape[0] // SPLIT, stride=SPLIT), :]
        slabs[id(ref)] = scr

    def take(ref, cls, first, count):
        lo, hi = cls
        return slabs[id(ref)][lo, pl.ds(hi + SPLIT * first, count, stride=SPLIT), :]

    return take


def _class_scratches(r, shapes):
    return [pltpu.VMEM((SPLIT, rows // SPLIT, LANES), F32) for rows in shapes] if r == SPLIT * SPLIT else []


def _dil_tiling(seq, r):
    tile = min(DIL_TILE, seq)
    edge = BLOCK * r
    return tile, edge, tile // edge


def _band_mask(n_rows, n_keys):
    row = lax.broadcasted_iota(jnp.int32, (n_rows, n_keys), 0) & (BLOCK - 1)
    col = lax.broadcasted_iota(jnp.int32, (n_rows, n_keys), 1)
    return jnp.logical_and(col >= row, col <= row + BLOCK), col


def _for_each_group(r, per, g, run):
    loop = lambda lo, hi, fn: lax.fori_loop(lo, hi, lambda t, carry: (fn(t), carry)[1], 0)
    if per <= g:
        n_cls = g // per
        assert g % per == 0 and r % n_cls == 0 and (r != SPLIT * SPLIT or n_cls == SPLIT)
        cls = (lambda t, u: (u, t)) if r == SPLIT * SPLIT else (lambda t, u: t * n_cls + u)
        group = lambda t: [(cls(t, u), a, a == 0) for u in range(n_cls) for a in range(per)]
        if r == n_cls:
            run(group(0))
        else:
            loop(0, r // n_cls, lambda t: run(group(t)))
        return

    assert per % g == 0

    def one_class(c):
        run([(c, 0, True)] + [(c, a, False) for a in range(1, g)])
        loop(1, per // g, lambda t: run([(c, t * g + u, False) for u in range(g)]))

    if r == 1:
        one_class(0)
    else:
        loop(0, r, one_class)


def _dil_keys(blocks, take, cur_ref, before_ref):
    out = []
    for cls, a, first in blocks:
        if first:
            both = jnp.concatenate([take(before_ref, cls, 0, BLOCK), take(cur_ref, cls, 0, BLOCK)], axis=0)
        else:
            both = take(cur_ref, cls, (a - 1) * BLOCK, 2 * BLOCK)
        out.append(both.astype(BF16))
    return out


def _dil_fwd(qkn, qkv_raw, r):
    seq = qkn.shape[0]
    tile, edge, per = _dil_tiling(seq, r)
    pairs = W_DIL // LANES

    def body(q_ref, kc_ref, kp_ref, vc_ref, vp_ref, o_ref, ld_ref, *scratches):
        n = pl.program_id(1)
        lo = _lane_lo((BLOCK, LANES))
        band, col = _band_mask(2 * BLOCK, 2 * BLOCK)
        first_band = jnp.logical_and(band, jnp.logical_or(col >= BLOCK, n > 0))
        take = _class_reader(r, (q_ref, kc_ref, kp_ref, vc_ref, vp_ref), scratches)

        def run(blocks):
            rows = [_class_rows(c, a * BLOCK, BLOCK, r) for c, a, _ in blocks]
            keys = _dil_keys(blocks, take, kc_ref, kp_ref)
            vals = _dil_keys(blocks, take, vc_ref, vp_ref)
            scores = [_dot_nt(_stack_heads(take(q_ref, c, a * BLOCK, BLOCK), lo).astype(BF16), kb)
                      for (c, a, _), kb in zip(blocks, keys)]
            probs, sums, lds = [], [], []
            for s, (_, _, first) in zip(scores, blocks):
                s = jnp.where(first_band if first else band, s, NEG_BIG)
                m = jnp.max(s, axis=1, keepdims=True)
                p = jnp.exp(s - m)
                l = jnp.sum(p, axis=1, keepdims=True)
                probs.append(p.astype(BF16))
                sums.append(l)
                lds.append(m + jnp.log(l))
            outs = [_dot(p, vb) / l for p, vb, l in zip(probs, vals, sums)]
            for rw, o, ld in zip(rows, outs, lds):
                o_ref[rw, :] = _unstack_heads(o, lo)
                ld_ref[rw, :] = _unstack_heads(jnp.broadcast_to(ld, (2 * BLOCK, LANES)), lo)

        _for_each_group(r, per, DIL_GROUP_FWD, run)

    per_edge = tile // edge
    cur = lambda off: pl.BlockSpec((tile, LANES), lambda p, n: (n, off + p))
    before = lambda off: pl.BlockSpec((edge, LANES), lambda p, n: (jnp.maximum(n * per_edge - 1, 0), off + p))
    return pl.pallas_call(
        body, name=f"dil_fwd_r{r}", grid=(pairs, seq // tile),
        in_specs=[cur(0), cur(pairs), before(pairs), cur(2 * pairs), before(2 * pairs)],
        out_specs=[cur(0), cur(0)],
        out_shape=[jax.ShapeDtypeStruct((seq, W_DIL), F32), jax.ShapeDtypeStruct((seq, W_DIL), F32)],
        scratch_shapes=_class_scratches(r, (tile, tile, edge, tile, edge)),
        compiler_params=_params(32),
    )(qkn, qkn, qkn, qkv_raw, qkv_raw)


def _silu_parts(g):
    sig = jax.nn.sigmoid(g)
    return g * sig, sig * (1.0 + g * (1.0 - sig))


def _out_proj(x, oa, o_p, ld_p, gates, wout_l):
    seq, d = x.shape
    tm = min(256, seq)

    def body(x_ref, oa_ref, o0, o1, o2, l0, l1, l2, g_ref, w_ref, xn_ref, cat_ref, od_ref, lse_ref):
        lds = (l0[...], l1[...], l2[...])
        m = jnp.maximum(jnp.maximum(lds[0], lds[1]), lds[2])
        es = [jnp.exp(v - m) for v in lds]
        tot = es[0] + es[1] + es[2]
        lse_ref[...] = m + jnp.log(tot)
        inv = 1.0 / tot
        od = (es[0] * inv) * o0[...] + (es[1] * inv) * o1[...] + (es[2] * inv) * o2[...]
        od_ref[...] = od
        silu, _ = _silu_parts(g_ref[...])
        cat_ref[:, :W_SB] = (oa_ref[...] * silu[:, :W_SB]).astype(BF16)
        cat_ref[:, W_SB:] = (od * silu[:, W_SB:]).astype(BF16)
        y = x_ref[...]
        for b in range(N_DEV):
            y = y + _dot(cat_ref[:, b * ROWS_PER_DEV:(b + 1) * ROWS_PER_DEV], w_ref[b])
        xn_ref[...] = y

    row = lambda w: pl.BlockSpec((tm, w), lambda i: (i, 0))
    return pl.pallas_call(
        body, name="out_proj", grid=(seq // tm,),
        in_specs=[row(d), row(W_SB)] + [row(W_DIL)] * 6 + [row(MIX),
                  pl.BlockSpec((N_DEV, ROWS_PER_DEV, d), lambda i: (0, 0, 0))],
        out_specs=[row(d), row(MIX), row(W_DIL), row(W_DIL)],
        out_shape=[jax.ShapeDtypeStruct((seq, d), F32), jax.ShapeDtypeStruct((seq, MIX), BF16),
                   jax.ShapeDtypeStruct((seq, W_DIL), F32), jax.ShapeDtypeStruct((seq, W_DIL), F32)],
        compiler_params=_params(48),
    )(x, oa, *o_p, *ld_p, gates, wout_l)


def _loss_head(y, target):
    seq, d = y.shape
    tm = min(512, seq)

    def body(y_ref, t_ref, part_ref, dy_ref):
        @pl.when(pl.program_id(0) == 0)
        def _():
            part_ref[...] = jnp.zeros_like(part_ref)

        diff = y_ref[...] - t_ref[...]
        dy_ref[...] = diff * (1.0 / d)
        part_ref[...] += jnp.sum((diff * diff).reshape(tm // 8, 8, d), axis=0) * (0.5 / d)

    row = pl.BlockSpec((tm, d), lambda i: (i, 0))
    return pl.pallas_call(
        body, name="loss_head", grid=(seq // tm,),
        in_specs=[row, row], out_specs=[pl.BlockSpec((8, d), lambda i: (0, 0)), row],
        out_shape=[jax.ShapeDtypeStruct((8, d), F32), jax.ShapeDtypeStruct((seq, d), F32)],
        compiler_params=_params(32),
    )(y, target)


def _out_proj_bwd(dy, wout_l, cat, gates, oa, od, bd):
    seq, d = dy.shape
    tm = min(256, seq)

    def body(dy_ref, w_ref, cat_ref, g_ref, oa_ref, od_ref, bd_ref, doa_ref, dod_ref, delta_ref, dg_ref, dw_ref, dcat):
        @pl.when(pl.program_id(0) == 0)
        def _():
            dw_ref[...] = jnp.zeros_like(dw_ref)

        dyb = dy_ref[...].astype(BF16)
        dw = _dot_tn(cat_ref[...], dyb)
        for b in range(N_DEV):
            dw_ref[b] += dw[b * ROWS_PER_DEV:(b + 1) * ROWS_PER_DEV, :]
            dcat[:, b * ROWS_PER_DEV:(b + 1) * ROWS_PER_DEV] = _dot_nt(dyb, w_ref[b])
        silu, dsilu = _silu_parts(g_ref[...])
        dc = dcat[...]
        dmix = dc * silu
        oa_v, od_v = oa_ref[...], od_ref[...]
        dg_ref[:, :W_SB] = dc[:, :W_SB] * oa_v * dsilu[:, :W_SB]
        dg_ref[:, W_SB:] = dc[:, W_SB:] * od_v * dsilu[:, W_SB:]
        doa_ref[...] = dmix[:, :W_SB]
        dod = dmix[:, W_SB:]
        dod_ref[...] = dod
        prod = dod * od_v
        for k in range(W_DIL // (2 * LANES)):
            sl = slice(k * 2 * LANES, (k + 1) * 2 * LANES)
            delta_ref[:, sl] = _head_sums(prod[:, sl], bd_ref[...])

    row = lambda w: pl.BlockSpec((tm, w), lambda i: (i, 0))
    slab = pl.BlockSpec((N_DEV, ROWS_PER_DEV, d), lambda i: (0, 0, 0))
    return pl.pallas_call(
        body, name="out_proj_bwd", grid=(seq // tm,),
        in_specs=[row(d), slab, row(MIX), row(MIX), row(W_SB), row(W_DIL),
                  pl.BlockSpec((2 * LANES, 2 * LANES), lambda i: (0, 0))],
        out_specs=[row(W_SB), row(W_DIL), row(W_DIL), row(MIX), slab],
        out_shape=[jax.ShapeDtypeStruct((seq, W_SB), F32), jax.ShapeDtypeStruct((seq, W_DIL), F32),
                   jax.ShapeDtypeStruct((seq, W_DIL), F32), jax.ShapeDtypeStruct((seq, MIX), F32),
                   jax.ShapeDtypeStruct((N_DEV, ROWS_PER_DEV, d), F32)],
        scratch_shapes=[pltpu.VMEM((tm, MIX), F32)],
        compiler_params=_params(48),
    )(dy, wout_l, cat, gates, oa, od, bd)


def _sb_bwd(proj_bf, d_oa, oa, tri_a, tri_b):
    seq = proj_bf.shape[0]
    pairs = W_SB // LANES
    nq = seq // BLOCK

    def body(q_ref, k_ref, v_ref, do_ref, o_ref, tria_ref, trib_ref, dq_ref, dk_hbm, dv_hbm,
             r_ref, sfx_ref, dtot_ref, dq_acc, dk_acc, dv_acc, sems):
        p, t = pl.program_id(0), pl.program_id(1)

        @pl.when(t == 0)
        def _():
            dk_acc[...] = jnp.zeros_like(dk_acc)
            dv_acc[...] = jnp.zeros_like(dv_acc)

        lo = _lane_lo((BLOCK, LANES))
        tri, trib = tria_ref[...], trib_ref[...]
        causal = _stacked_causal()
        qblocks = [t * SB_QBLOCKS + c for c in range(SB_QBLOCKS)]
        qss, doss = [], []
        for c in range(SB_QBLOCKS):
            sl = slice(c * BLOCK, (c + 1) * BLOCK)
            qss.append(_stack_heads(q_ref[sl, :].astype(F32) * QK_SCALE, lo).astype(BF16))
            doss.append(_stack_heads(do_ref[sl, :], lo).astype(BF16))
            o2 = o_ref[sl, :]
            dtot_ref[c] = _dot_exact(doss[c].astype(F32) * jnp.concatenate([o2, o2], axis=0), tri[:, BLOCK:])

        def blocks(specs):
            rows = [[_block_rows(j) for j in js] for _, js, _, _, _, _ in specs]
            kbs = [[k_ref[rw, :] for rw in rws] for rws in rows]
            dovs = [[_dot_nt(doss[c], v_ref[rw, :]) for rw in rws] for (c, *_), rws in zip(specs, rows)]
            res = _sb_weights([(qss[c], kbc, r, masks, lives) for (c, _, r, _, masks, lives), kbc in zip(specs, kbs)], tri)
            pws = [[a * dov for a, dov in zip(weights, dovc)] for (_, weights, _), dovc in zip(res, dovs)]
            cps = [[_dot_exact(pw, trib, SB_SUM_TERMS) for pw in pwc] for pwc in pws]
            dzs, sfxs = [], []
            for (c, _, _, sfx, masks, lives), (lsc, _, _), pwc, cpc in zip(specs, res, pws, cps):
                dzc = []
                for ls, pw, cp, mask, live in zip(lsc, pwc, cpc, masks, lives):
                    beta = jnp.exp(ls)
                    before = dtot_ref[c] - sfx - cp[:, :BLOCK]
                    dzc.append(_keep(pw * (1.0 - beta) - before * beta, mask, live).astype(BF16))
                    sfx = sfx + cp[:, BLOCK:]
                dzs.append(dzc)
                sfxs.append(sfx)
            dqs = []
            for dzc, kbc in zip(dzs, kbs):
                dq = 0.0
                for dzb, kb in zip(dzc, kbc):
                    dq = dq + _dot(dzb, kb)
                dqs.append(dq)
            for (c, *_), dzc, (_, weights, _), rws in zip(specs, dzs, res, rows):
                for dzb, a, rw in zip(dzc, weights, rws):
                    dk_acc[rw, :] += _dot_tn(dzb, qss[c])
                    dv_acc[rw, :] += _dot_tn(a.astype(BF16), doss[c])
            return [(dq, r, sfx) for dq, (_, _, r), sfx in zip(dqs, res, sfxs)]

        def peel(c, i):
            js, masks, lives = _sb_peel(i, causal)
            zero = jnp.zeros((2 * BLOCK, LANES), F32)
            return c, js, zero, zero, masks, lives

        for c, (dq, r, sfx) in enumerate(blocks([peel(c, i) for c, i in enumerate(qblocks)])):
            dq_acc[c], r_ref[c], sfx_ref[c] = dq, r, sfx

        for c in range(SB_QBLOCKS):
            def alive(c=c):
                return (jnp.max(r_ref[c]) > DEAD_LOG).astype(jnp.int32)

            def step(carry, c=c):
                (dq, r_ref[c], sfx_ref[c]), = blocks([(c, [carry[0]], r_ref[c], sfx_ref[c], [None], [None])])
                dq_acc[c] += dq
                return carry[0] - 1, alive(c)

            lax.while_loop(lambda carry: jnp.logical_and(carry[0] >= 0, carry[1] > 0), step, (qblocks[c] - SB_PEEL, alive()))
            dq_ref[c * BLOCK:(c + 1) * BLOCK, :] = _unstack_heads(dq_acc[c], lo) * QK_SCALE

        @pl.when(t == nq // SB_QBLOCKS - 1)
        def _():
            outs = [pltpu.make_async_copy(dk_acc, dk_hbm.at[p], sems.at[0]),
                    pltpu.make_async_copy(dv_acc, dv_hbm.at[p], sems.at[1])]
            for cp in outs:
                cp.start()
            for cp in outs:
                cp.wait()

    qtile = SB_QBLOCKS * BLOCK
    blk = pl.BlockSpec((qtile, LANES), lambda p, t: (t, p))
    const = pl.BlockSpec((BLOCK, 2 * BLOCK), lambda p, t: (0, 0))
    any_spec = pl.BlockSpec(memory_space=pl.ANY)
    state = pltpu.VMEM((SB_QBLOCKS, 2 * BLOCK, LANES), F32)
    return pl.pallas_call(
        body, name="sb_bwd", grid=(pairs, seq // qtile),
        in_specs=[blk, pl.BlockSpec((seq, LANES), lambda p, t: (0, pairs + p)),
                  pl.BlockSpec((seq, LANES), lambda p, t: (0, 2 * pairs + p)), blk, blk, const, const],
        out_specs=[blk, any_spec, any_spec],
        out_shape=[jax.ShapeDtypeStruct((seq, W_SB), F32), jax.ShapeDtypeStruct((pairs, seq, LANES), F32),
                   jax.ShapeDtypeStruct((pairs, seq, LANES), F32)],
        scratch_shapes=[state, state, state, state, pltpu.VMEM((seq, LANES), F32), pltpu.VMEM((seq, LANES), F32),
                        pltpu.SemaphoreType.DMA((2,))],
        compiler_params=_params(56),
    )(proj_bf, proj_bf, proj_bf, d_oa, oa, tri_a, tri_b)


def _dil_bwd(qkn, qkv_raw, d_od, lse, delta, r, others=None):
    seq = qkn.shape[0]
    tile, edge, per = _dil_tiling(seq, r)
    ntile = seq // tile
    pairs = W_DIL // LANES

    def body(qc, doc, lsc, dlc, kc, kp, vc, vp, *rest):
        if others is None:
            dq_ref, dk_ref, dv_ref, dk_carry, dv_carry, *scratches = rest
        else:
            dq_in, dk_in, dv_in, dq_ref, dk_ref, dv_ref, dk_carry, dv_carry, *scratches = rest
        n = pl.program_id(1)

        @pl.when(n == 0)
        def _():
            dk_carry[...] = jnp.zeros_like(dk_carry)
            dv_carry[...] = jnp.zeros_like(dv_carry)

        if others is None:
            dk_ref[...] = dk_carry[...]
            dv_ref[...] = dv_carry[...]
        else:
            dk_ref[...] = dk_carry[...] + dk_in[...]
            dv_ref[...] = dv_carry[...] + dv_in[...]

        @pl.when(n < ntile)
        def _():
            lo = _lane_lo((BLOCK, LANES))
            band, col = _band_mask(2 * BLOCK, 2 * BLOCK)
            first_band = jnp.logical_and(band, jnp.logical_or(col >= BLOCK, n > 0))
            take = _class_reader(r, (qc, doc, lsc, dlc, kc, kp, vc, vp), scratches)

            def stacked_cols(b):
                other = pltpu.roll(b, HEAD_DIM, 1)
                rows = jnp.concatenate([jnp.where(lo, b, other), jnp.where(lo, other, b)], axis=0)
                return jnp.concatenate([rows, rows], axis=1)

            def run(blocks):
                rows = [_class_rows(c, a * BLOCK, BLOCK, r) for c, a, _ in blocks]
                own = lambda ref: [take(ref, c, a * BLOCK, BLOCK) for c, a, _ in blocks]
                keys = _dil_keys(blocks, take, kc, kp)
                vals = _dil_keys(blocks, take, vc, vp)
                qss = [_stack_heads(x, lo).astype(BF16) for x in own(qc)]
                doss = [_stack_heads(x, lo).astype(BF16) for x in own(doc)]
                scores = [_dot_nt(qs, kb) for qs, kb in zip(qss, keys)]
                dps = [_dot_nt(dos, vb) for dos, vb in zip(doss, vals)]
                pws, dss = [], []
                for lsb, dlb, s, dp, (_, _, first) in zip(own(lsc), own(dlc), scores, dps, blocks):
                    pw = jnp.where(first_band if first else band, jnp.exp(s - stacked_cols(lsb)), 0.0)
                    pws.append(pw.astype(BF16))
                    dss.append((pw * (dp - stacked_cols(dlb))).astype(BF16))
                dqs = [_dot(ds, kb) for ds, kb in zip(dss, keys)]
                dks = [_dot_tn(ds, qs) for ds, qs in zip(dss, qss)]
                dvs = [_dot_tn(pw, dos) for pw, dos in zip(pws, doss)]
                for (c, a, first), rw, dq, dk, dv in zip(blocks, rows, dqs, dks, dvs):
                    dq_ref[rw, :] = _unstack_heads(dq, lo)
                    if first:
                        last = _class_rows(c, (per - 1) * BLOCK, BLOCK, r)
                        dk_ref[last, :] += dk[:BLOCK]
                        dv_ref[last, :] += dv[:BLOCK]
                    else:
                        prev = _class_rows(c, (a - 1) * BLOCK, BLOCK, r)
                        dk_carry[prev, :] += dk[:BLOCK]
                        dv_carry[prev, :] += dv[:BLOCK]
                    dk_carry[rw, :] = dk[BLOCK:]
                    dv_carry[rw, :] = dv[BLOCK:]

            _for_each_group(r, per, DIL_GROUP_BWD, run)
            if others is not None:
                dq_ref[...] += dq_in[...]

    per_edge = tile // edge
    here = lambda n: jnp.minimum(n, ntile - 1)
    cur = lambda off: pl.BlockSpec((tile, LANES), lambda p, n: (here(n), off + p))
    before = lambda off: pl.BlockSpec((edge, LANES), lambda p, n: (jnp.maximum(here(n) * per_edge - 1, 0), off + p))
    lagged = pl.BlockSpec((tile, LANES), lambda p, n: (jnp.maximum(n - 1, 0), p))
    carry = pltpu.VMEM((tile, LANES), F32)
    return pl.pallas_call(
        body, name=f"dil_bwd_r{r}", grid=(pairs, ntile + 1),
        in_specs=[cur(0)] * 4 + [cur(pairs), before(pairs), cur(2 * pairs), before(2 * pairs)] +
                 ([] if others is None else [cur(0), lagged, lagged]),
        out_specs=[cur(0), lagged, lagged],
        out_shape=[jax.ShapeDtypeStruct((seq, W_DIL), F32)] * 3,
        scratch_shapes=[carry, carry] + _class_scratches(r, (tile, tile, tile, tile, tile, edge, tile, edge)),
        compiler_params=_params(48),
    )(qkn, d_od, lse, delta, qkn, qkn, qkv_raw, qkv_raw, *(others or ()))


def _assemble_dproj(d_qa, d_ka, d_va, d_gates, d_qd, d_kd, d_vd, qk_raw, gains, tables, bd):
    seq = qk_raw.shape[0]
    tm = min(256, seq)
    chunks = W_DIL // (2 * LANES)

    def body(dqa, dka0, dka1, dva0, dva1, dg, dqd, dkd, dvd, x_ref, g_ref, c_ref, sn_ref, sp_ref, bd_ref, dp_ref, gpart_ref):
        @pl.when(pl.program_id(0) == 0)
        def _():
            gpart_ref[...] = jnp.zeros_like(gpart_ref)

        def put(first_col, v):
            dp_ref[:, first_col:first_col + v.shape[1]] = v.astype(BF16)

        put(0, dqa[...])
        put(W_SB, dka0[...])
        put(W_SB + LANES, dka1[...])
        put(2 * W_SB, dva0[...])
        put(2 * W_SB + LANES, dva1[...])
        put(3 * W_SB, dg[:, :W_SB])
        put(4 * W_SB + 2 * W_DIL, dvd[...])
        put(4 * W_SB + 3 * W_DIL, dg[:, W_SB:])
        c, sn, sp, bdm = c_ref[...], sn_ref[...], sp_ref[...], bd_ref[...]
        for which, part in enumerate((dqd, dkd)):
            scale = QK_SCALE if which == 0 else 1.0
            for k in range(chunks):
                sl = slice(k * 2 * LANES, (k + 1) * 2 * LANES)
                dyv = part[:, sl] * scale
                dxn = _rope_t(dyv, c, sn, sp)
                x = x_ref[:, which * W_DIL + k * 2 * LANES:which * W_DIL + (k + 1) * 2 * LANES]
                rs = lax.rsqrt(_head_sums(x * x, bdm) * (1.0 / HEAD_DIM) + EPS)
                xhat = x * rs
                gpart_ref[which] += jnp.sum((dxn * xhat).reshape(tm // 8, 8, 2 * LANES), axis=0)
                dxhat = dxn * g_ref[which]
                mean = _head_sums(dxhat * xhat, bdm) * (1.0 / HEAD_DIM)
                put(4 * W_SB + which * W_DIL + k * 2 * LANES, rs * (dxhat - xhat * mean))

    row = lambda w: pl.BlockSpec((tm, w), lambda i: (i, 0))
    pair = lambda p: pl.BlockSpec((None, tm, LANES), lambda i: (p, i, 0))
    const = lambda shape: pl.BlockSpec(shape, lambda i: tuple(0 for _ in shape))
    return pl.pallas_call(
        body, name="assemble_dproj", grid=(seq // tm,),
        in_specs=[row(W_SB), pair(0), pair(1), pair(0), pair(1), row(MIX)] + [row(W_DIL)] * 3 +
                 [row(2 * W_DIL), const((2, 1, 2 * LANES)), row(2 * LANES), row(2 * LANES), row(2 * LANES),
                  const((2 * LANES, 2 * LANES))],
        out_specs=[row(IN_COLS), const((2, 8, 2 * LANES))],
        out_shape=[jax.ShapeDtypeStruct((seq, IN_COLS), BF16), jax.ShapeDtypeStruct((2, 8, 2 * LANES), F32)],
        compiler_params=_params(48),
    )(d_qa, d_ka, d_ka, d_va, d_va, d_gates, d_qd, d_kd, d_vd, qk_raw, gains, *tables, bd)


def _dw_in(h, dproj):
    seq, d = h.shape
    tm = min(2048, seq)

    def body(h_ref, dp_ref, dw_ref):
        @pl.when(pl.program_id(1) == 0)
        def _():
            dw_ref[...] = jnp.zeros_like(dw_ref)

        dw_ref[...] += _dot_tn(h_ref[...], dp_ref[...])

    return pl.pallas_call(
        body, name="dw_in", grid=(N_DEV, seq // tm),
        in_specs=[pl.BlockSpec((tm, d), lambda n, i: (i, 0)), pl.BlockSpec((tm, COLS_PER_DEV), lambda n, i: (i, n))],
        out_specs=pl.BlockSpec((None, d, COLS_PER_DEV), lambda n, i: (n, 0, 0)),
        out_shape=jax.ShapeDtypeStruct((N_DEV, d, COLS_PER_DEV), F32),
        compiler_params=_params(40),
    )(h, dproj)


def _dx_norm(dproj, w_l, x, g, dx_next, dwin_l, dwout_l, rin, rout, layer):
    seq, d = x.shape
    tm = min(256, seq)
    steps = seq // tm

    def body(dp_ref, w_ref, x_ref, g_ref, dn_ref, dwin_ref, dwout_ref, rin_in, rout_in, dx_ref, gpart_ref, rin_ref, rout_ref, *sems):
        del rin_in, rout_in
        copies = lambda: _to_every_device(lambda to: (dwin_ref.at[to], dwout_ref.at[to]),
                                          lambda me: (rin_ref.at[me, layer], rout_ref.at[me, layer]), sems)

        @pl.when(pl.program_id(0) == 0)
        def _():
            gpart_ref[...] = jnp.zeros_like(gpart_ref)
            for cp in copies():
                cp.start()

        dh = jnp.zeros((tm, d), F32)
        for n in range(N_DEV):
            dh = dh + _dot_nt(dp_ref[:, n * COLS_PER_DEV:(n + 1) * COLS_PER_DEV], w_ref[n])
        xf = x_ref[...]
        rs = lax.rsqrt(jnp.mean(xf * xf, axis=-1, keepdims=True) + EPS)
        xhat = xf * rs
        gpart_ref[...] += jnp.sum((dh * xhat).reshape(tm // 8, 8, d), axis=0)
        dxhat = dh * g_ref[...]
        mean = jnp.mean(dxhat * xhat, axis=-1, keepdims=True)
        dx_ref[...] = rs * (dxhat - xhat * mean) + dn_ref[...]

        @pl.when(pl.program_id(0) == steps - 1)
        def _():
            for cp in copies():
                cp.wait()

    row = lambda w: pl.BlockSpec((tm, w), lambda i: (i, 0))
    any_spec = pl.BlockSpec(memory_space=pl.ANY)
    return pl.pallas_call(
        body, name="dx_norm_exchange", grid=(steps,),
        in_specs=[row(IN_COLS), pl.BlockSpec((N_DEV, d, COLS_PER_DEV), lambda i: (0, 0, 0)), row(d),
                  pl.BlockSpec((1, d), lambda i: (0, 0)), row(d), any_spec, any_spec, any_spec, any_spec],
        out_specs=[row(d), pl.BlockSpec((8, d), lambda i: (0, 0)), any_spec, any_spec],
        out_shape=[jax.ShapeDtypeStruct((seq, d), F32), jax.ShapeDtypeStruct((8, d), F32),
                   jax.ShapeDtypeStruct(rin.shape, F32), jax.ShapeDtypeStruct(rout.shape, F32)],
        scratch_shapes=_copy_sems(2), input_output_aliases={7: 2, 8: 3},
        compiler_params=_params(48),
    )(dproj, w_l, x, g, dx_next, dwin_l, dwout_l, rin, rout)


def _exchange_small(small):
    def body(small_ref, out_ref, *sems):
        copies = _to_every_device(lambda to: (small_ref,), lambda me: (out_ref.at[me],), sems)
        for cp in copies:
            cp.start()
        for cp in copies:
            cp.wait()

    vmem = pl.BlockSpec(memory_space=pltpu.VMEM)
    return pl.pallas_call(
        body, name="exchange_small", in_specs=[vmem], out_specs=vmem,
        out_shape=jax.ShapeDtypeStruct((N_DEV,) + small.shape, F32), scratch_shapes=_copy_sems(1),
    )(small)


def _adamw_math(g, w, m, v):
    m = ADAM_B1 * m + (1.0 - ADAM_B1) * g
    v = ADAM_B2 * v + (1.0 - ADAM_B2) * (g * g)
    m_hat = m / (1.0 - ADAM_B1 ** ADAM_STEP)
    v_hat = v / (1.0 - ADAM_B2 ** ADAM_STEP)
    delta = -ADAM_LR * (m_hat / (jnp.sqrt(v_hat) + ADAM_EPS) + ADAM_WD * w)
    return delta, m, v


def _adamw(parts, w, m, v, name):
    nl, r, c = w.shape
    tr = min(r, (256 * 512) // c)

    def body(p_ref, w_ref, m_ref, v_ref, g_ref, d_ref, nm_ref, nv_ref):
        g = p_ref[0]
        for s in range(1, N_DEV):
            g = g + p_ref[s]
        g_ref[...] = g
        d_ref[...], nm_ref[...], nv_ref[...] = _adamw_math(g, w_ref[...], m_ref[...], v_ref[...])

    blk = pl.BlockSpec((None, tr, c), lambda l, i: (l, i, 0))
    return pl.pallas_call(
        body, name=name, grid=(nl, r // tr),
        in_specs=[pl.BlockSpec((N_DEV, None, tr, c), lambda l, i: (0, l, i, 0)), blk, blk, blk],
        out_specs=[blk] * 4, out_shape=[jax.ShapeDtypeStruct(w.shape, F32)] * 4,
        compiler_params=_params(32),
    )(parts, w, m, v)


def _adamw_small(parts, w, m, v):
    def body(p_ref, w_ref, m_ref, v_ref, g_ref, d_ref, nm_ref, nv_ref):
        g = p_ref[0]
        for s in range(1, N_DEV):
            g = g + p_ref[s]
        g_ref[...] = g
        d_ref[...], nm_ref[...], nv_ref[...] = _adamw_math(g, w_ref[...], m_ref[...], v_ref[...])

    vmem = pl.BlockSpec(memory_space=pltpu.VMEM)
    return pl.pallas_call(
        body, name="adamw_small", in_specs=[vmem] * 4, out_specs=[vmem] * 4,
        out_shape=[jax.ShapeDtypeStruct(w.shape, F32)] * 4,
    )(parts, w, m, v)


def _pack_small(a, b, c):
    pad = jnp.zeros((a.shape[0], SMALL_W - a.shape[1] - b.shape[1] - c.shape[1]), F32)
    return jnp.concatenate([a, b, c, pad], axis=1)


def _unpack_small(t, d):
    return t[:, :d], t[:, d:d + HEAD_DIM], t[:, d + HEAD_DIM:d + 2 * HEAD_DIM]


def kernel(x, norm_g, w_in, q_norm_g, k_norm_g, w_out, loss_target, m_norm_g, m_w_in, m_q_norm_g, m_k_norm_g, m_w_out,
           v_norm_g, v_w_in, v_q_norm_g, v_k_norm_g, v_w_out):
    depth, d, _ = w_in.shape
    seq = x.shape[1]
    tri_a, tri_b, bd = _tri_constants()
    tables = _rope_tables(seq)
    rep = (2 * LANES) // HEAD_DIM

    win_bf, wout_bf = _cast_bf16(w_in, "cast_w_in"), _cast_bf16(w_out, "cast_w_out")
    win_l, wout_l = _gather_weights(win_bf, wout_bf, 0)
    saved = []
    xl = x.reshape(seq, d)
    for layer in range(depth):
        gains = jnp.stack([jnp.tile(q_norm_g[layer], rep), jnp.tile(k_norm_g[layer], rep)])[:, None, :]
        if layer + 1 < depth:
            proj_bf, gates, qk_raw, h, qkn, win_next, wout_next = _norm_proj(
                xl, norm_g[layer][None, :], win_l, gains, tables, bd, prefetch=(win_bf, wout_bf, layer + 1))
        else:
            proj_bf, gates, qk_raw, h, qkn = _norm_proj(xl, norm_g[layer][None, :], win_l, gains, tables, bd)
            win_next = wout_next = None
        oa = _sb_fwd(proj_bf, tri_a)
        o_p, ld_p = zip(*[_dil_fwd(qkn, qk_raw, r) for _, r in DIL_PATTERNS])
        x_next, cat, od, lse = _out_proj(xl, oa, o_p, ld_p, gates, wout_l)
        saved.append((xl, gains, proj_bf, gates, qk_raw, h, qkn, oa, cat, od, lse, win_l, wout_l))
        xl, win_l, wout_l = x_next, win_next, wout_next

    loss_part, dx = _loss_head(xl, loss_target.reshape(seq, d))
    loss = lax.psum(jnp.sum(loss_part), ("x", "y", "c"))

    rin = jnp.zeros((N_DEV, depth, d, COLS_PER_DEV), F32)
    rout = jnp.zeros((N_DEV, depth, ROWS_PER_DEV, d), F32)
    g_norm, g_q, g_k = [None] * depth, [None] * depth, [None] * depth
    for layer in reversed(range(depth)):
        xl, gains, proj_bf, gates, qk_raw, h, qkn, oa, cat, od, lse, win_l, wout_l = saved[layer]
        d_oa, d_od, delta, d_gates, dwout_l = _out_proj_bwd(dx, wout_l, cat, gates, oa, od, bd)
        d_qa, d_ka, d_va = _sb_bwd(proj_bf, d_oa, oa, tri_a, tri_b)
        d_dil = None
        for _, r in DIL_PATTERNS:
            d_dil = _dil_bwd(qkn, qk_raw, d_od, lse, delta, r, others=d_dil)
        dproj, gqk = _assemble_dproj(d_qa, d_ka, d_va, d_gates, *d_dil, qk_raw, gains, tables, bd)
        dwin_l = _dw_in(h, dproj)
        dx, gn, rin, rout = _dx_norm(dproj, win_l, xl, norm_g[layer][None, :], dx, dwin_l, dwout_l, rin, rout, layer)
        g_norm[layer] = jnp.sum(gn, axis=0)
        gqk = jnp.sum(gqk, axis=1).reshape(2, rep, HEAD_DIM).sum(axis=1)
        g_q[layer], g_k[layer] = gqk[0], gqk[1]
    rsmall = _exchange_small(_pack_small(jnp.stack(g_norm), jnp.stack(g_q), jnp.stack(g_k)))

    g_in, d_in, nm_in, nv_in = _adamw(rin, w_in, m_w_in, v_w_in, "adamw_w_in")
    g_out, d_out, nm_out, nv_out = _adamw(rout, w_out, m_w_out, v_w_out, "adamw_w_out")
    small_out = _adamw_small(rsmall, _pack_small(norm_g, q_norm_g, k_norm_g), _pack_small(m_norm_g, m_q_norm_g, m_k_norm_g),
                             _pack_small(v_norm_g, v_q_norm_g, v_k_norm_g))
    (g_n, g_qn, g_kn), (d_n, d_qn, d_kn), (nm_n, nm_qn, nm_kn), (nv_n, nv_qn, nv_kn) = (_unpack_small(t, d) for t in small_out)

    return (loss, dx.reshape(x.shape), g_n, g_in, g_qn, g_kn, g_out, d_n, d_in, d_qn, d_kn, d_out,
            nm_n, nm_in, nm_qn, nm_kn, nm_out, nv_n, nv_in, nv_qn, nv_kn, nv_out)
```

```python
import math

import jax
import jax.numpy as jnp
from jax import lax
from jax.experimental import pallas as pl
from jax.experimental.pallas import tpu as pltpu

F32 = jnp.float32
BF16 = jnp.bfloat16

EPS = 1e-6
HEAD_DIM = 64
BLOCK = 128
LANES = 128
W_SB = 256
W_DIL = 768
MIX = W_SB + W_DIL
IN_COLS = 4 * W_SB + 4 * W_DIL
N_DEV = 8
COLS_PER_DEV = IN_COLS // N_DEV
ROWS_PER_DEV = MIX // N_DEV
QK_SCALE = 1.0 / math.sqrt(HEAD_DIM)
DIL_PATTERNS = ((128, 1), (512, 4), (2048, 16))
DIL_TILE = 2048
DIL_GROUP_FWD = 4
DIL_GROUP_BWD = 4
SPLIT = 4
ROPE_THETA = 500000.0
ROPE_DIM = HEAD_DIM // 4
ROPE_HALF = ROPE_DIM // 2
DEAD_LOG = -110.0
SB_PEEL = 3
SB_QBLOCKS = 2
SB_SUM_TERMS = 2
NEG_BIG = -1e30

ADAM_LR = 0.001
ADAM_B1 = 0.9
ADAM_B2 = 0.999
ADAM_EPS = 1e-08
ADAM_WD = 0.01
ADAM_STEP = 10

SMALL_W = 1280
MESH_ID = pl.DeviceIdType.MESH
MIB = 1 << 20


def _params(vmem_mib):
    return pltpu.CompilerParams(vmem_limit_bytes=vmem_mib * MIB)


def _dot(a, b):
    return jnp.dot(a, b, preferred_element_type=F32)


def _dot_nt(a, b):
    return lax.dot_general(a, b, (((1,), (1,)), ((), ())), preferred_element_type=F32)


def _dot_tn(a, b):
    return lax.dot_general(a, b, (((0,), (0,)), ((), ())), preferred_element_type=F32)


def _dot_exact(x, m01, terms=3):
    out = 0.0
    for _ in range(terms):
        part = x.astype(BF16)
        out = out + _dot(part, m01)
        x = x - part.astype(F32)
    return out


def _lane_lo(shape):
    return lax.broadcasted_iota(jnp.int32, shape, 1) < HEAD_DIM


def _tri_constants():
    j = jnp.arange(BLOCK)
    ones = jnp.ones((BLOCK, BLOCK), F32)
    excl = (j[:, None] > j[None, :]).astype(F32)
    incl = (j[:, None] >= j[None, :]).astype(F32)
    tri_a = jnp.concatenate([excl, ones], axis=1).astype(BF16)
    tri_b = jnp.concatenate([incl, ones], axis=1).astype(BF16)
    d = jnp.arange(2 * LANES)
    bd = (d[:, None] // HEAD_DIM == d[None, :] // HEAD_DIM).astype(BF16)
    return tri_a, tri_b, bd


def _rope_tables(seq):
    d = jnp.arange(2 * LANES) % HEAD_DIM
    inv_freq = 1.0 / (ROPE_THETA ** ((d % ROPE_HALF).astype(F32) * 2.0 / ROPE_DIM))
    ang = jnp.arange(seq).astype(F32)[:, None] * inv_freq[None, :]
    cos, sin = jnp.cos(ang), jnp.sin(ang)
    c = jnp.where(d < ROPE_DIM, cos, 1.0)
    s_next = jnp.where(d < ROPE_HALF, -sin, 0.0)
    s_prev = jnp.where((d >= ROPE_HALF) & (d < ROPE_DIM), sin, 0.0)
    return c, s_next, s_prev


def _roll_lanes(x, shift):
    return jnp.concatenate([pltpu.roll(x[:, :LANES], shift, 1), pltpu.roll(x[:, LANES:], shift, 1)], axis=1)


def _rope(x, c, s_next, s_prev):
    return x * c + _roll_lanes(x, LANES - ROPE_HALF) * s_next + _roll_lanes(x, ROPE_HALF) * s_prev


def _rope_t(dy, c, s_next, s_prev):
    return dy * c + _roll_lanes(dy * s_next, ROPE_HALF) + _roll_lanes(dy * s_prev, LANES - ROPE_HALF)


def _cast_bf16(w, name):
    nl, r, c = w.shape

    def body(w_ref, o_ref):
        o_ref[...] = w_ref[...].astype(BF16)

    return pl.pallas_call(
        body, name=name, grid=(nl,),
        in_specs=[pl.BlockSpec((None, r, c), lambda l: (l, 0, 0))],
        out_specs=pl.BlockSpec((None, r, c), lambda l: (l, 0, 0)),
        out_shape=jax.ShapeDtypeStruct(w.shape, BF16),
        compiler_params=_params(24),
    )(w)


def _flips():
    return [(dx, dy, dc) for dx in (0, 1) for dy in (0, 1) for dc in (0, 1) if (dx, dy, dc) != (0, 0, 0)]


def _place():
    x, y, c = lax.axis_index("x"), lax.axis_index("y"), lax.axis_index("c")
    return x, y, c, 4 * x + 2 * y + c


def _peer(x, y, c, flip):
    dx, dy, dc = flip
    return (1 - x if dx else x, 1 - y if dy else y, 1 - c if dc else c)


def _to_every_device(srcs_for, dsts_at, sems):
    send_sems, recv_sems, local_sems = sems
    x, y, c, me = _place()
    dsts = dsts_at(me)
    n = len(dsts)
    copies = [pltpu.make_async_copy(src, dst, local_sems.at[a]) for a, (src, dst) in enumerate(zip(srcs_for(me), dsts))]
    for k, flip in enumerate(_flips()):
        px, py, pc = _peer(x, y, c, flip)
        for a, (src, dst) in enumerate(zip(srcs_for(4 * px + 2 * py + pc), dsts)):
            copies.append(pltpu.make_async_remote_copy(
                src_ref=src, dst_ref=dst, send_sem=send_sems.at[n * k + a], recv_sem=recv_sems.at[n * k + a],
                device_id=(px, py, pc), device_id_type=MESH_ID))
    return copies


def _copy_sems(n):
    remote = n * (N_DEV - 1)
    return [pltpu.SemaphoreType.DMA((remote,)), pltpu.SemaphoreType.DMA((remote,)), pltpu.SemaphoreType.DMA((n,))]


def _weight_copies(win_ref, wout_ref, layer, oin_ref, oout_ref, sems):
    return _to_every_device(lambda to: (win_ref.at[layer], wout_ref.at[layer]), lambda me: (oin_ref.at[me], oout_ref.at[me]), sems)


def _gathered_shapes(win_bf, wout_bf):
    return [jax.ShapeDtypeStruct((N_DEV,) + win_bf.shape[1:], BF16), jax.ShapeDtypeStruct((N_DEV,) + wout_bf.shape[1:], BF16)]


def _gather_weights(win_bf, wout_bf, layer):
    def body(win_ref, wout_ref, oin_ref, oout_ref, *sems):
        copies = _weight_copies(win_ref, wout_ref, layer, oin_ref, oout_ref, sems)
        for cp in copies:
            cp.start()
        for cp in copies:
            cp.wait()

    any_spec = pl.BlockSpec(memory_space=pl.ANY)
    return pl.pallas_call(
        body, name="gather_weights", in_specs=[any_spec, any_spec], out_specs=[any_spec, any_spec],
        out_shape=_gathered_shapes(win_bf, wout_bf), scratch_shapes=_copy_sems(2),
    )(win_bf, wout_bf)


_QK_BLOCKS = (2, 3, 4)
_F32_ROUTES = {1: ((256, 256, 0, 0),), 2: ((0, 512, 1, 0),), 3: ((0, 512, 1, 512),), 4: ((0, 512, 1, 1024),),
               5: ((0, 512, 1, 1536),), 6: ((0, 256, 1, 2048), (256, 256, 0, 256)), 7: ((0, 512, 0, 512),)}


def _norm_proj(x, g, w_l, gains, tables, bd, prefetch=None):
    seq, d = x.shape
    tm = min(256, seq)
    steps = seq // tm

    def body(x_ref, g_ref, w_ref, gains_ref, c_ref, sn_ref, sp_ref, bd_ref, *rest):
        if prefetch is None:
            pbf_ref, gates_ref, qk_ref, h_ref, qkn_ref = rest
        else:
            win_ref, wout_ref, pbf_ref, gates_ref, qk_ref, h_ref, qkn_ref, oin_ref, oout_ref, *sems = rest
            copies = lambda: _weight_copies(win_ref, wout_ref, prefetch[2], oin_ref, oout_ref, sems)

            @pl.when(pl.program_id(0) == 0)
            def _():
                for cp in copies():
                    cp.start()

        xf = x_ref[...]
        rs = lax.rsqrt(jnp.mean(xf * xf, axis=-1, keepdims=True) + EPS)
        h = (xf * rs * g_ref[...]).astype(BF16)
        h_ref[...] = h
        targets = (gates_ref, qk_ref)
        for n in range(N_DEV):
            acc = _dot(h, w_ref[n])
            pbf_ref[:, n * COLS_PER_DEV:(n + 1) * COLS_PER_DEV] = acc.astype(BF16)
            for lo, width, tgt, dst in _F32_ROUTES.get(n, ()):
                targets[tgt][:, dst:dst + width] = acc[:, lo:lo + width]
            for half in range(2) if n in _QK_BLOCKS else ():
                col = (_QK_BLOCKS.index(n) * 2 + half) * 2 * LANES
                is_q = col < W_DIL
                xq = acc[:, half * 2 * LANES:(half + 1) * 2 * LANES]
                rsq = lax.rsqrt(_head_sums(xq * xq, bd_ref[...]) * (1.0 / HEAD_DIM) + EPS)
                y = _rope(xq * rsq * gains_ref[0 if is_q else 1], c_ref[...], sn_ref[...], sp_ref[...])
                qkn_ref[:, col:col + 2 * LANES] = y * QK_SCALE if is_q else y

        if prefetch is not None:
            @pl.when(pl.program_id(0) == steps - 1)
            def _():
                for cp in copies():
                    cp.wait()

    row = lambda w: pl.BlockSpec((tm, w), lambda i: (i, 0))
    any_spec = pl.BlockSpec(memory_space=pl.ANY)
    const = lambda shape: pl.BlockSpec(shape, lambda i: tuple(0 for _ in shape))
    in_specs = [row(d), const((1, d)), const((N_DEV, d, COLS_PER_DEV)), const((2, 1, 2 * LANES)),
                row(2 * LANES), row(2 * LANES), row(2 * LANES), const((2 * LANES, 2 * LANES))]
    out_specs = [row(IN_COLS), row(MIX), row(3 * W_DIL), row(d), row(2 * W_DIL)]
    out_shape = [jax.ShapeDtypeStruct((seq, IN_COLS), BF16), jax.ShapeDtypeStruct((seq, MIX), F32),
                 jax.ShapeDtypeStruct((seq, 3 * W_DIL), F32), jax.ShapeDtypeStruct((seq, d), BF16),
                 jax.ShapeDtypeStruct((seq, 2 * W_DIL), F32)]
    operands = (x, g, w_l, gains, *tables, bd)
    if prefetch is None:
        return pl.pallas_call(body, name="norm_proj", grid=(steps,), in_specs=in_specs, out_specs=out_specs,
                              out_shape=out_shape, compiler_params=_params(48))(*operands)
    return pl.pallas_call(
        body, name="norm_proj_gather", grid=(steps,), in_specs=in_specs + [any_spec, any_spec],
        out_specs=out_specs + [any_spec, any_spec], out_shape=out_shape + _gathered_shapes(*prefetch[:2]),
        scratch_shapes=_copy_sems(2), compiler_params=_params(48),
    )(*operands, *prefetch[:2])


def _head_sums(v, bd):
    hi = v.astype(BF16)
    lo = (v - hi.astype(F32)).astype(BF16)
    return _dot(hi, bd) + _dot(lo, bd)


def _stack_heads(x, lo):
    return jnp.concatenate([jnp.where(lo, x, 0.0), jnp.where(lo, 0.0, x)], axis=0)


def _unstack_heads(y, lo):
    return jnp.where(lo, y[:BLOCK], y[BLOCK:])


def _stacked_causal():
    row = lax.broadcasted_iota(jnp.int32, (2 * BLOCK, BLOCK), 0) & (BLOCK - 1)
    return lax.broadcasted_iota(jnp.int32, (2 * BLOCK, BLOCK), 1) < row


def _keep(x, *conds):
    for cond in conds:
        if cond is not None:
            x = jnp.where(cond, x, 0.0)
    return x


def _sb_weights(chains, tri):
    zs = [[_dot_nt(qs, kb) for kb in kbs] for qs, kbs, _, _, _ in chains]
    lss = [[jnp.minimum(z, 0.0) - jnp.log1p(jnp.exp(-jnp.abs(z))) for z in zc] for zc in zs]
    cts = [[_dot_exact(_keep(ls - z, mask, live), tri, SB_SUM_TERMS) for ls, z, mask, live in zip(lsc, zc, masks, lives)]
           for lsc, zc, (_, _, _, masks, lives) in zip(lss, zs, chains)]
    out = []
    for lsc, ctc, (_, _, r, masks, lives) in zip(lss, cts, chains):
        weights = []
        for ls, ct, mask, live in zip(lsc, ctc, masks, lives):
            weights.append(_keep(jnp.exp(ls + ct[:, :BLOCK] + r), mask, live))
            r = r + ct[:, BLOCK:]
        out.append((lsc, weights, r))
    return out


def _sb_peel(i, causal):
    return ([jnp.maximum(i - k, 0) for k in range(SB_PEEL)], [causal] + [None] * (SB_PEEL - 1),
            [None] + [i >= k for k in range(1, SB_PEEL)])


def _block_rows(j):
    return pl.ds(pl.multiple_of(j * BLOCK, BLOCK), BLOCK)


def _sb_fwd(proj_bf, tri_a):
    seq = proj_bf.shape[0]
    pairs = W_SB // LANES

    def body(q_ref, k_ref, v_ref, tri_ref, o_ref, r_ref, acc_ref):
        t = pl.program_id(1)
        lo = _lane_lo((BLOCK, LANES))
        causal = _stacked_causal()
        tri = tri_ref[...]
        qblocks = [t * SB_QBLOCKS + c for c in range(SB_QBLOCKS)]
        qss = [_stack_heads(q_ref[c * BLOCK:(c + 1) * BLOCK, :].astype(F32) * QK_SCALE, lo).astype(BF16)
               for c in range(SB_QBLOCKS)]

        def blocks(specs):
            rows = [[_block_rows(j) for j in js] for _, js, _, _, _ in specs]
            res = _sb_weights([(qs, [k_ref[rw, :] for rw in rws], r, masks, lives)
                               for (qs, _, r, masks, lives), rws in zip(specs, rows)], tri)
            outs = []
            for (_, weights, r), rws in zip(res, rows):
                out = 0.0
                for a, rw in zip(weights, rws):
                    a_hi = a.astype(BF16)
                    a_lo = (a - a_hi.astype(F32)).astype(BF16)
                    vb = v_ref[rw, :]
                    out = out + _dot(a_hi, vb) + _dot(a_lo, vb)
                outs.append((out, r))
            return outs

        def peel(qs, i):
            js, masks, lives = _sb_peel(i, causal)
            return qs, js, jnp.zeros((2 * BLOCK, LANES), F32), masks, lives

        for c, (out, r) in enumerate(blocks([peel(qs, i) for qs, i in zip(qss, qblocks)])):
            acc_ref[c], r_ref[c] = out, r

        for c in range(SB_QBLOCKS):
            def alive(c=c):
                return (jnp.max(r_ref[c]) > DEAD_LOG).astype(jnp.int32)

            def step(carry, c=c):
                (out, r_ref[c]), = blocks([(qss[c], [carry[0]], r_ref[c], [None], [None])])
                acc_ref[c] += out
                return carry[0] - 1, alive(c)

            lax.while_loop(lambda carry: jnp.logical_and(carry[0] >= 0, carry[1] > 0), step, (qblocks[c] - SB_PEEL, alive()))
            o_ref[c * BLOCK:(c + 1) * BLOCK, :] = _unstack_heads(acc_ref[c], lo)

    qtile = SB_QBLOCKS * BLOCK
    state = pltpu.VMEM((SB_QBLOCKS, 2 * BLOCK, LANES), F32)
    return pl.pallas_call(
        body, name="sb_fwd", grid=(pairs, seq // qtile),
        in_specs=[pl.BlockSpec((qtile, LANES), lambda p, t: (t, p)),
                  pl.BlockSpec((seq, LANES), lambda p, t: (0, pairs + p)),
                  pl.BlockSpec((seq, LANES), lambda p, t: (0, 2 * pairs + p)),
                  pl.BlockSpec((BLOCK, 2 * BLOCK), lambda p, t: (0, 0))],
        out_specs=pl.BlockSpec((qtile, LANES), lambda p, t: (t, p)),
        out_shape=jax.ShapeDtypeStruct((seq, W_SB), F32),
        scratch_shapes=[state, state],
        compiler_params=_params(40),
    )(proj_bf, proj_bf, proj_bf, tri_a)


def _class_rows(cls, first, count, r):
    c = cls[0] + SPLIT * cls[1] if isinstance(cls, tuple) else cls
    start = c + first * r
    return pl.ds(start, count) if r == 1 else pl.ds(start, count, stride=r)


def _class_reader(r, refs, scratches):
    if r != SPLIT * SPLIT:
        return lambda ref, cls, first, count: ref[_class_rows(cls, first, count, r), :]
    slabs = {}
    for ref, scr in zip(refs, scratches):
        for lo in range(SPLIT):
            scr[lo] = ref[pl.ds(lo, ref.shape[0] // SPLIT, stride=SPLIT), :]
        slabs[id(ref)] = scr

    def take(ref, cls, first, count):
        lo, hi = cls
        return slabs[id(ref)][lo, pl.ds(hi + SPLIT * first, count, stride=SPLIT), :]

    return take


def _class_scratches(r, shapes):
    return [pltpu.VMEM((SPLIT, rows // SPLIT, LANES), F32) for rows in shapes] if r == SPLIT * SPLIT else []


def _dil_tiling(seq, r):
    tile = min(DIL_TILE, seq)
    edge = BLOCK * r
    return tile, edge, tile // edge


def _band_mask(n_rows, n_keys):
    row = lax.broadcasted_iota(jnp.int32, (n_rows, n_keys), 0) & (BLOCK - 1)
    col = lax.broadcasted_iota(jnp.int32, (n_rows, n_keys), 1)
    return jnp.logical_and(col >= row, col <= row + BLOCK), col


def _for_each_group(r, per, g, run):
    loop = lambda lo, hi, fn: lax.fori_loop(lo, hi, lambda t, carry: (fn(t), carry)[1], 0)
    if per <= g:
        n_cls = g // per
        assert g % per == 0 and r % n_cls == 0 and (r != SPLIT * SPLIT or n_cls == SPLIT)
        cls = (lambda t, u: (u, t)) if r == SPLIT * SPLIT else (lambda t, u: t * n_cls + u)
        group = lambda t: [(cls(t, u), a, a == 0) for u in range(n_cls) for a in range(per)]
        if r == n_cls:
            run(group(0))
        else:
            loop(0, r // n_cls, lambda t: run(group(t)))
        return

    assert per % g == 0

    def one_class(c):
        run([(c, 0, True)] + [(c, a, False) for a in range(1, g)])
        loop(1, per // g, lambda t: run([(c, t * g + u, False) for u in range(g)]))

    if r == 1:
        one_class(0)
    else:
        loop(0, r, one_class)


def _dil_keys(blocks, take, cur_ref, before_ref):
    out = []
    for cls, a, first in blocks:
        if first:
            both = jnp.concatenate([take(before_ref, cls, 0, BLOCK), take(cur_ref, cls, 0, BLOCK)], axis=0)
        else:
            both = take(cur_ref, cls, (a - 1) * BLOCK, 2 * BLOCK)
        out.append(both.astype(BF16))
    return out


def _dil_fwd(qkn, qkv_raw, r, others=()):
    seq = qkn.shape[0]
    tile, edge, per = _dil_tiling(seq, r)
    pairs = W_DIL // LANES

    def body(q_ref, kc_ref, kp_ref, vc_ref, vp_ref, *rest):
        other_refs, (o_ref, ld_ref, *scratches) = rest[:2 * len(others)], rest[2 * len(others):]
        n = pl.program_id(1)
        lo = _lane_lo((BLOCK, LANES))
        band, col = _band_mask(2 * BLOCK, 2 * BLOCK)
        first_band = jnp.logical_and(band, jnp.logical_or(col >= BLOCK, n > 0))
        take = _class_reader(r, (q_ref, kc_ref, kp_ref, vc_ref, vp_ref), scratches)

        def run(blocks):
            rows = [_class_rows(c, a * BLOCK, BLOCK, r) for c, a, _ in blocks]
            keys = _dil_keys(blocks, take, kc_ref, kp_ref)
            vals = _dil_keys(blocks, take, vc_ref, vp_ref)
            scores = [_dot_nt(_stack_heads(take(q_ref, c, a * BLOCK, BLOCK), lo).astype(BF16), kb)
                      for (c, a, _), kb in zip(blocks, keys)]
            probs, sums, lds = [], [], []
            for s, (_, _, first) in zip(scores, blocks):
                s = jnp.where(first_band if first else band, s, NEG_BIG)
                m = jnp.max(s, axis=1, keepdims=True)
                p = jnp.exp(s - m)
                l = jnp.sum(p, axis=1, keepdims=True)
                probs.append(p.astype(BF16))
                sums.append(l)
                lds.append(m + jnp.log(l))
            outs = [_dot(p, vb) / l for p, vb, l in zip(probs, vals, sums)]
            for rw, o, ld in zip(rows, outs, lds):
                o_ref[rw, :] = _unstack_heads(o, lo)
                ld_ref[rw, :] = _unstack_heads(jnp.broadcast_to(ld, (2 * BLOCK, LANES)), lo)

        _for_each_group(r, per, DIL_GROUP_FWD, run)
        if others:
            o_ref[...], ld_ref[...] = _mix_patterns([ref[...] for ref in other_refs[0::2]] + [o_ref[...]],
                                                    [ref[...] for ref in other_refs[1::2]] + [ld_ref[...]])

    per_edge = tile // edge
    cur = lambda off: pl.BlockSpec((tile, LANES), lambda p, n: (n, off + p))
    before = lambda off: pl.BlockSpec((edge, LANES), lambda p, n: (jnp.maximum(n * per_edge - 1, 0), off + p))
    return pl.pallas_call(
        body, name=f"dil_fwd_r{r}", grid=(pairs, seq // tile),
        in_specs=[cur(0), cur(pairs), before(pairs), cur(2 * pairs), before(2 * pairs)] + [cur(0)] * (2 * len(others)),
        out_specs=[cur(0), cur(0)],
        out_shape=[jax.ShapeDtypeStruct((seq, W_DIL), F32), jax.ShapeDtypeStruct((seq, W_DIL), F32)],
        scratch_shapes=_class_scratches(r, (tile, tile, edge, tile, edge)),
        compiler_params=_params(40),
    )(qkn, qkn, qkn, qkv_raw, qkv_raw, *[a for pair in others for a in pair])


def _silu_parts(g):
    sig = jax.nn.sigmoid(g)
    return g * sig, sig * (1.0 + g * (1.0 - sig))


def _mix_patterns(os_, lds):
    m = lds[0]
    for v in lds[1:]:
        m = jnp.maximum(m, v)
    es = [jnp.exp(v - m) for v in lds]
    tot = es[0]
    for e in es[1:]:
        tot = tot + e
    inv = 1.0 / tot
    od = (es[0] * inv) * os_[0]
    for e, o in zip(es[1:], os_[1:]):
        od = od + (e * inv) * o
    return od, m + jnp.log(tot)


def _out_proj(x, oa, od, gates, wout_l):
    seq, d = x.shape
    tm = min(256, seq)

    def body(x_ref, oa_ref, od_ref, g_ref, w_ref, xn_ref, cat_ref):
        silu, _ = _silu_parts(g_ref[...])
        cat_ref[:, :W_SB] = (oa_ref[...] * silu[:, :W_SB]).astype(BF16)
        cat_ref[:, W_SB:] = (od_ref[...] * silu[:, W_SB:]).astype(BF16)
        y = x_ref[...]
        for b in range(N_DEV):
            y = y + _dot(cat_ref[:, b * ROWS_PER_DEV:(b + 1) * ROWS_PER_DEV], w_ref[b])
        xn_ref[...] = y

    row = lambda w: pl.BlockSpec((tm, w), lambda i: (i, 0))
    return pl.pallas_call(
        body, name="out_proj", grid=(seq // tm,),
        in_specs=[row(d), row(W_SB), row(W_DIL), row(MIX), pl.BlockSpec((N_DEV, ROWS_PER_DEV, d), lambda i: (0, 0, 0))],
        out_specs=[row(d), row(MIX)],
        out_shape=[jax.ShapeDtypeStruct((seq, d), F32), jax.ShapeDtypeStruct((seq, MIX), BF16)],
        compiler_params=_params(40),
    )(x, oa, od, gates, wout_l)


def _loss_head(y, target):
    seq, d = y.shape
    tm = min(512, seq)

    def body(y_ref, t_ref, part_ref, dy_ref):
        @pl.when(pl.program_id(0) == 0)
        def _():
            part_ref[...] = jnp.zeros_like(part_ref)

        diff = y_ref[...] - t_ref[...]
        dy_ref[...] = diff * (1.0 / d)
        part_ref[...] += jnp.sum((diff * diff).reshape(tm // 8, 8, d), axis=0) * (0.5 / d)

    row = pl.BlockSpec((tm, d), lambda i: (i, 0))
    return pl.pallas_call(
        body, name="loss_head", grid=(seq // tm,),
        in_specs=[row, row], out_specs=[pl.BlockSpec((8, d), lambda i: (0, 0)), row],
        out_shape=[jax.ShapeDtypeStruct((8, d), F32), jax.ShapeDtypeStruct((seq, d), F32)],
        compiler_params=_params(32),
    )(y, target)


def _out_proj_bwd(dy, wout_l, cat, gates, oa, od, bd):
    seq, d = dy.shape
    tm = min(256, seq)

    def body(dy_ref, w_ref, cat_ref, g_ref, oa_ref, od_ref, bd_ref, doa_ref, dod_ref, delta_ref, dg_ref, dw_ref, dcat):
        @pl.when(pl.program_id(0) == 0)
        def _():
            dw_ref[...] = jnp.zeros_like(dw_ref)

        dyb = dy_ref[...].astype(BF16)
        dw = _dot_tn(cat_ref[...], dyb)
        for b in range(N_DEV):
            dw_ref[b] += dw[b * ROWS_PER_DEV:(b + 1) * ROWS_PER_DEV, :]
            dcat[:, b * ROWS_PER_DEV:(b + 1) * ROWS_PER_DEV] = _dot_nt(dyb, w_ref[b])
        silu, dsilu = _silu_parts(g_ref[...])
        dc = dcat[...]
        dmix = dc * silu
        oa_v, od_v = oa_ref[...], od_ref[...]
        dg_ref[:, :W_SB] = dc[:, :W_SB] * oa_v * dsilu[:, :W_SB]
        dg_ref[:, W_SB:] = dc[:, W_SB:] * od_v * dsilu[:, W_SB:]
        doa_ref[...] = dmix[:, :W_SB]
        dod = dmix[:, W_SB:]
        dod_ref[...] = dod
        prod = dod * od_v
        for k in range(W_DIL // (2 * LANES)):
            sl = slice(k * 2 * LANES, (k + 1) * 2 * LANES)
            delta_ref[:, sl] = _head_sums(prod[:, sl], bd_ref[...])

    row = lambda w: pl.BlockSpec((tm, w), lambda i: (i, 0))
    slab = pl.BlockSpec((N_DEV, ROWS_PER_DEV, d), lambda i: (0, 0, 0))
    return pl.pallas_call(
        body, name="out_proj_bwd", grid=(seq // tm,),
        in_specs=[row(d), slab, row(MIX), row(MIX), row(W_SB), row(W_DIL),
                  pl.BlockSpec((2 * LANES, 2 * LANES), lambda i: (0, 0))],
        out_specs=[row(W_SB), row(W_DIL), row(W_DIL), row(MIX), slab],
        out_shape=[jax.ShapeDtypeStruct((seq, W_SB), F32), jax.ShapeDtypeStruct((seq, W_DIL), F32),
                   jax.ShapeDtypeStruct((seq, W_DIL), F32), jax.ShapeDtypeStruct((seq, MIX), F32),
                   jax.ShapeDtypeStruct((N_DEV, ROWS_PER_DEV, d), F32)],
        scratch_shapes=[pltpu.VMEM((tm, MIX), F32)],
        compiler_params=_params(48),
    )(dy, wout_l, cat, gates, oa, od, bd)


def _sb_bwd(proj_bf, d_oa, oa, tri_a, tri_b):
    seq = proj_bf.shape[0]
    pairs = W_SB // LANES
    nq = seq // BLOCK

    def body(q_ref, k_ref, v_ref, do_ref, o_ref, tria_ref, trib_ref, dq_ref, dk_hbm, dv_hbm,
             r_ref, sfx_ref, dtot_ref, dq_acc, dk_acc, dv_acc, sems):
        p, t = pl.program_id(0), pl.program_id(1)

        @pl.when(t == 0)
        def _():
            dk_acc[...] = jnp.zeros_like(dk_acc)
            dv_acc[...] = jnp.zeros_like(dv_acc)

        lo = _lane_lo((BLOCK, LANES))
        tri, trib = tria_ref[...], trib_ref[...]
        causal = _stacked_causal()
        qblocks = [t * SB_QBLOCKS + c for c in range(SB_QBLOCKS)]
        qss, doss = [], []
        for c in range(SB_QBLOCKS):
            sl = slice(c * BLOCK, (c + 1) * BLOCK)
            qss.append(_stack_heads(q_ref[sl, :].astype(F32) * QK_SCALE, lo).astype(BF16))
            doss.append(_stack_heads(do_ref[sl, :], lo).astype(BF16))
            o2 = o_ref[sl, :]
            dtot_ref[c] = _dot_exact(doss[c].astype(F32) * jnp.concatenate([o2, o2], axis=0), tri[:, BLOCK:])

        def blocks(specs):
            rows = [[_block_rows(j) for j in js] for _, js, _, _, _, _ in specs]
            kbs = [[k_ref[rw, :] for rw in rws] for rws in rows]
            dovs = [[_dot_nt(doss[c], v_ref[rw, :]) for rw in rws] for (c, *_), rws in zip(specs, rows)]
            res = _sb_weights([(qss[c], kbc, r, masks, lives) for (c, _, r, _, masks, lives), kbc in zip(specs, kbs)], tri)
            pws = [[a * dov for a, dov in zip(weights, dovc)] for (_, weights, _), dovc in zip(res, dovs)]
            cps = [[_dot_exact(pw, trib, SB_SUM_TERMS) for pw in pwc] for pwc in pws]
            dzs, sfxs = [], []
            for (c, _, _, sfx, masks, lives), (lsc, _, _), pwc, cpc in zip(specs, res, pws, cps):
                dzc = []
                for ls, pw, cp, mask, live in zip(lsc, pwc, cpc, masks, lives):
                    beta = jnp.exp(ls)
                    before = dtot_ref[c] - sfx - cp[:, :BLOCK]
                    dzc.append(_keep(pw * (1.0 - beta) - before * beta, mask, live).astype(BF16))
                    sfx = sfx + cp[:, BLOCK:]
                dzs.append(dzc)
                sfxs.append(sfx)
            dqs = []
            for dzc, kbc in zip(dzs, kbs):
                dq = 0.0
                for dzb, kb in zip(dzc, kbc):
                    dq = dq + _dot(dzb, kb)
                dqs.append(dq)
            for (c, *_), dzc, (_, weights, _), rws in zip(specs, dzs, res, rows):
                for dzb, a, rw in zip(dzc, weights, rws):
                    dk_acc[rw, :] += _dot_tn(dzb, qss[c])
                    dv_acc[rw, :] += _dot_tn(a.astype(BF16), doss[c])
            return [(dq, r, sfx) for dq, (_, _, r), sfx in zip(dqs, res, sfxs)]

        def peel(c, i):
            js, masks, lives = _sb_peel(i, causal)
            zero = jnp.zeros((2 * BLOCK, LANES), F32)
            return c, js, zero, zero, masks, lives

        for c, (dq, r, sfx) in enumerate(blocks([peel(c, i) for c, i in enumerate(qblocks)])):
            dq_acc[c], r_ref[c], sfx_ref[c] = dq, r, sfx

        for c in range(SB_QBLOCKS):
            def alive(c=c):
                return (jnp.max(r_ref[c]) > DEAD_LOG).astype(jnp.int32)

            def step(carry, c=c):
                (dq, r_ref[c], sfx_ref[c]), = blocks([(c, [carry[0]], r_ref[c], sfx_ref[c], [None], [None])])
                dq_acc[c] += dq
                return carry[0] - 1, alive(c)

            lax.while_loop(lambda carry: jnp.logical_and(carry[0] >= 0, carry[1] > 0), step, (qblocks[c] - SB_PEEL, alive()))
            dq_ref[c * BLOCK:(c + 1) * BLOCK, :] = _unstack_heads(dq_acc[c], lo) * QK_SCALE

        @pl.when(t == nq // SB_QBLOCKS - 1)
        def _():
            outs = [pltpu.make_async_copy(dk_acc, dk_hbm.at[p], sems.at[0]),
                    pltpu.make_async_copy(dv_acc, dv_hbm.at[p], sems.at[1])]
            for cp in outs:
                cp.start()
            for cp in outs:
                cp.wait()

    qtile = SB_QBLOCKS * BLOCK
    blk = pl.BlockSpec((qtile, LANES), lambda p, t: (t, p))
    const = pl.BlockSpec((BLOCK, 2 * BLOCK), lambda p, t: (0, 0))
    any_spec = pl.BlockSpec(memory_space=pl.ANY)
    state = pltpu.VMEM((SB_QBLOCKS, 2 * BLOCK, LANES), F32)
    return pl.pallas_call(
        body, name="sb_bwd", grid=(pairs, seq // qtile),
        in_specs=[blk, pl.BlockSpec((seq, LANES), lambda p, t: (0, pairs + p)),
                  pl.BlockSpec((seq, LANES), lambda p, t: (0, 2 * pairs + p)), blk, blk, const, const],
        out_specs=[blk, any_spec, any_spec],
        out_shape=[jax.ShapeDtypeStruct((seq, W_SB), F32), jax.ShapeDtypeStruct((pairs, seq, LANES), F32),
                   jax.ShapeDtypeStruct((pairs, seq, LANES), F32)],
        scratch_shapes=[state, state, state, state, pltpu.VMEM((seq, LANES), F32), pltpu.VMEM((seq, LANES), F32),
                        pltpu.SemaphoreType.DMA((2,))],
        compiler_params=_params(56),
    )(proj_bf, proj_bf, proj_bf, d_oa, oa, tri_a, tri_b)


def _dil_bwd(qkn, qkv_raw, d_od, lse, delta, r, others=None):
    seq = qkn.shape[0]
    tile, edge, per = _dil_tiling(seq, r)
    ntile = seq // tile
    pairs = W_DIL // LANES

    def body(qc, doc, lsc, dlc, kc, kp, vc, vp, *rest):
        if others is None:
            dq_ref, dk_ref, dv_ref, dk_carry, dv_carry, *scratches = rest
        else:
            dq_in, dk_in, dv_in, dq_ref, dk_ref, dv_ref, dk_carry, dv_carry, *scratches = rest
        n = pl.program_id(1)

        @pl.when(n == 0)
        def _():
            dk_carry[...] = jnp.zeros_like(dk_carry)
            dv_carry[...] = jnp.zeros_like(dv_carry)

        if others is None:
            dk_ref[...] = dk_carry[...]
            dv_ref[...] = dv_carry[...]
        else:
            dk_ref[...] = dk_carry[...] + dk_in[...]
            dv_ref[...] = dv_carry[...] + dv_in[...]

        @pl.when(n < ntile)
        def _():
            lo = _lane_lo((BLOCK, LANES))
            band, col = _band_mask(2 * BLOCK, 2 * BLOCK)
            first_band = jnp.logical_and(band, jnp.logical_or(col >= BLOCK, n > 0))
            take = _class_reader(r, (qc, doc, lsc, dlc, kc, kp, vc, vp), scratches)

            def stacked_cols(b):
                other = pltpu.roll(b, HEAD_DIM, 1)
                rows = jnp.concatenate([jnp.where(lo, b, other), jnp.where(lo, other, b)], axis=0)
                return jnp.concatenate([rows, rows], axis=1)

            def run(blocks):
                rows = [_class_rows(c, a * BLOCK, BLOCK, r) for c, a, _ in blocks]
                own = lambda ref: [take(ref, c, a * BLOCK, BLOCK) for c, a, _ in blocks]
                keys = _dil_keys(blocks, take, kc, kp)
                vals = _dil_keys(blocks, take, vc, vp)
                qss = [_stack_heads(x, lo).astype(BF16) for x in own(qc)]
                doss = [_stack_heads(x, lo).astype(BF16) for x in own(doc)]
                scores = [_dot_nt(qs, kb) for qs, kb in zip(qss, keys)]
                dps = [_dot_nt(dos, vb) for dos, vb in zip(doss, vals)]
                pws, dss = [], []
                for lsb, dlb, s, dp, (_, _, first) in zip(own(lsc), own(dlc), scores, dps, blocks):
                    pw = jnp.where(first_band if first else band, jnp.exp(s - stacked_cols(lsb)), 0.0)
                    pws.append(pw.astype(BF16))
                    dss.append((pw * (dp - stacked_cols(dlb))).astype(BF16))
                dqs = [_dot(ds, kb) for ds, kb in zip(dss, keys)]
                dks = [_dot_tn(ds, qs) for ds, qs in zip(dss, qss)]
                dvs = [_dot_tn(pw, dos) for pw, dos in zip(pws, doss)]
                for (c, a, first), rw, dq, dk, dv in zip(blocks, rows, dqs, dks, dvs):
                    dq_ref[rw, :] = _unstack_heads(dq, lo)
                    if first:
                        last = _class_rows(c, (per - 1) * BLOCK, BLOCK, r)
                        dk_ref[last, :] += dk[:BLOCK]
                        dv_ref[last, :] += dv[:BLOCK]
                    else:
                        prev = _class_rows(c, (a - 1) * BLOCK, BLOCK, r)
                        dk_carry[prev, :] += dk[:BLOCK]
                        dv_carry[prev, :] += dv[:BLOCK]
                    dk_carry[rw, :] = dk[BLOCK:]
                    dv_carry[rw, :] = dv[BLOCK:]

            _for_each_group(r, per, DIL_GROUP_BWD, run)
            if others is not None:
                dq_ref[...] += dq_in[...]

    per_edge = tile // edge
    here = lambda n: jnp.minimum(n, ntile - 1)
    cur = lambda off: pl.BlockSpec((tile, LANES), lambda p, n: (here(n), off + p))
    before = lambda off: pl.BlockSpec((edge, LANES), lambda p, n: (jnp.maximum(here(n) * per_edge - 1, 0), off + p))
    lagged = pl.BlockSpec((tile, LANES), lambda p, n: (jnp.maximum(n - 1, 0), p))
    carry = pltpu.VMEM((tile, LANES), F32)
    return pl.pallas_call(
        body, name=f"dil_bwd_r{r}", grid=(pairs, ntile + 1),
        in_specs=[cur(0)] * 4 + [cur(pairs), before(pairs), cur(2 * pairs), before(2 * pairs)] +
                 ([] if others is None else [cur(0), lagged, lagged]),
        out_specs=[cur(0), lagged, lagged],
        out_shape=[jax.ShapeDtypeStruct((seq, W_DIL), F32)] * 3,
        scratch_shapes=[carry, carry] + _class_scratches(r, (tile, tile, tile, tile, tile, edge, tile, edge)),
        compiler_params=_params(48),
    )(qkn, d_od, lse, delta, qkn, qkn, qkv_raw, qkv_raw, *(others or ()))


def _assemble_dproj(d_qa, d_ka, d_va, d_gates, d_qd, d_kd, d_vd, qk_raw, gains, tables, bd):
    seq = qk_raw.shape[0]
    tm = min(256, seq)
    chunks = W_DIL // (2 * LANES)

    def body(dqa, dka0, dka1, dva0, dva1, dg, dqd, dkd, dvd, x_ref, g_ref, c_ref, sn_ref, sp_ref, bd_ref, dp_ref, gpart_ref):
        @pl.when(pl.program_id(0) == 0)
        def _():
            gpart_ref[...] = jnp.zeros_like(gpart_ref)

        def put(first_col, v):
            dp_ref[:, first_col:first_col + v.shape[1]] = v.astype(BF16)

        put(0, dqa[...])
        put(W_SB, dka0[...])
        put(W_SB + LANES, dka1[...])
        put(2 * W_SB, dva0[...])
        put(2 * W_SB + LANES, dva1[...])
        put(3 * W_SB, dg[:, :W_SB])
        put(4 * W_SB + 2 * W_DIL, dvd[...])
        put(4 * W_SB + 3 * W_DIL, dg[:, W_SB:])
        c, sn, sp, bdm = c_ref[...], sn_ref[...], sp_ref[...], bd_ref[...]
        for which, part in enumerate((dqd, dkd)):
            scale = QK_SCALE if which == 0 else 1.0
            for k in range(chunks):
                sl = slice(k * 2 * LANES, (k + 1) * 2 * LANES)
                dyv = part[:, sl] * scale
                dxn = _rope_t(dyv, c, sn, sp)
                x = x_ref[:, which * W_DIL + k * 2 * LANES:which * W_DIL + (k + 1) * 2 * LANES]
                rs = lax.rsqrt(_head_sums(x * x, bdm) * (1.0 / HEAD_DIM) + EPS)
                xhat = x * rs
                gpart_ref[which] += jnp.sum((dxn * xhat).reshape(tm // 8, 8, 2 * LANES), axis=0)
                dxhat = dxn * g_ref[which]
                mean = _head_sums(dxhat * xhat, bdm) * (1.0 / HEAD_DIM)
                put(4 * W_SB + which * W_DIL + k * 2 * LANES, rs * (dxhat - xhat * mean))

    row = lambda w: pl.BlockSpec((tm, w), lambda i: (i, 0))
    pair = lambda p: pl.BlockSpec((None, tm, LANES), lambda i: (p, i, 0))
    const = lambda shape: pl.BlockSpec(shape, lambda i: tuple(0 for _ in shape))
    return pl.pallas_call(
        body, name="assemble_dproj", grid=(seq // tm,),
        in_specs=[row(W_SB), pair(0), pair(1), pair(0), pair(1), row(MIX)] + [row(W_DIL)] * 3 +
                 [row(2 * W_DIL), const((2, 1, 2 * LANES)), row(2 * LANES), row(2 * LANES), row(2 * LANES),
                  const((2 * LANES, 2 * LANES))],
        out_specs=[row(IN_COLS), const((2, 8, 2 * LANES))],
        out_shape=[jax.ShapeDtypeStruct((seq, IN_COLS), BF16), jax.ShapeDtypeStruct((2, 8, 2 * LANES), F32)],
        compiler_params=_params(48),
    )(d_qa, d_ka, d_ka, d_va, d_va, d_gates, d_qd, d_kd, d_vd, qk_raw, gains, *tables, bd)


def _dw_in(h, dproj):
    seq, d = h.shape
    tm = min(2048, seq)

    def body(h_ref, dp_ref, dw_ref):
        @pl.when(pl.program_id(1) == 0)
        def _():
            dw_ref[...] = jnp.zeros_like(dw_ref)

        dw_ref[...] += _dot_tn(h_ref[...], dp_ref[...])

    return pl.pallas_call(
        body, name="dw_in", grid=(N_DEV, seq // tm),
        in_specs=[pl.BlockSpec((tm, d), lambda n, i: (i, 0)), pl.BlockSpec((tm, COLS_PER_DEV), lambda n, i: (i, n))],
        out_specs=pl.BlockSpec((None, d, COLS_PER_DEV), lambda n, i: (n, 0, 0)),
        out_shape=jax.ShapeDtypeStruct((N_DEV, d, COLS_PER_DEV), F32),
        compiler_params=_params(40),
    )(h, dproj)


def _dx_norm(dproj, w_l, x, g, dx_next, dwin_l, dwout_l, rin, rout, layer):
    seq, d = x.shape
    tm = min(256, seq)
    steps = seq // tm

    def body(dp_ref, w_ref, x_ref, g_ref, dn_ref, dwin_ref, dwout_ref, rin_in, rout_in, dx_ref, gpart_ref, rin_ref, rout_ref, *sems):
        del rin_in, rout_in
        copies = lambda: _to_every_device(lambda to: (dwin_ref.at[to], dwout_ref.at[to]),
                                          lambda me: (rin_ref.at[me, layer], rout_ref.at[me, layer]), sems)

        @pl.when(pl.program_id(0) == 0)
        def _():
            gpart_ref[...] = jnp.zeros_like(gpart_ref)
            for cp in copies():
                cp.start()

        dh = jnp.zeros((tm, d), F32)
        for n in range(N_DEV):
            dh = dh + _dot_nt(dp_ref[:, n * COLS_PER_DEV:(n + 1) * COLS_PER_DEV], w_ref[n])
        xf = x_ref[...]
        rs = lax.rsqrt(jnp.mean(xf * xf, axis=-1, keepdims=True) + EPS)
        xhat = xf * rs
        gpart_ref[...] += jnp.sum((dh * xhat).reshape(tm // 8, 8, d), axis=0)
        dxhat = dh * g_ref[...]
        mean = jnp.mean(dxhat * xhat, axis=-1, keepdims=True)
        dx_ref[...] = rs * (dxhat - xhat * mean) + dn_ref[...]

        @pl.when(pl.program_id(0) == steps - 1)
        def _():
            for cp in copies():
                cp.wait()

    row = lambda w: pl.BlockSpec((tm, w), lambda i: (i, 0))
    any_spec = pl.BlockSpec(memory_space=pl.ANY)
    return pl.pallas_call(
        body, name="dx_norm_exchange", grid=(steps,),
        in_specs=[row(IN_COLS), pl.BlockSpec((N_DEV, d, COLS_PER_DEV), lambda i: (0, 0, 0)), row(d),
                  pl.BlockSpec((1, d), lambda i: (0, 0)), row(d), any_spec, any_spec, any_spec, any_spec],
        out_specs=[row(d), pl.BlockSpec((8, d), lambda i: (0, 0)), any_spec, any_spec],
        out_shape=[jax.ShapeDtypeStruct((seq, d), F32), jax.ShapeDtypeStruct((8, d), F32),
                   jax.ShapeDtypeStruct(rin.shape, F32), jax.ShapeDtypeStruct(rout.shape, F32)],
        scratch_shapes=_copy_sems(2), input_output_aliases={7: 2, 8: 3},
        compiler_params=_params(48),
    )(dproj, w_l, x, g, dx_next, dwin_l, dwout_l, rin, rout)


def _exchange_small(small):
    def body(small_ref, out_ref, *sems):
        copies = _to_every_device(lambda to: (small_ref,), lambda me: (out_ref.at[me],), sems)
        for cp in copies:
            cp.start()
        for cp in copies:
            cp.wait()

    vmem = pl.BlockSpec(memory_space=pltpu.VMEM)
    return pl.pallas_call(
        body, name="exchange_small", in_specs=[vmem], out_specs=vmem,
        out_shape=jax.ShapeDtypeStruct((N_DEV,) + small.shape, F32), scratch_shapes=_copy_sems(1),
    )(small)


def _adamw_math(g, w, m, v):
    m = ADAM_B1 * m + (1.0 - ADAM_B1) * g
    v = ADAM_B2 * v + (1.0 - ADAM_B2) * (g * g)
    m_hat = m / (1.0 - ADAM_B1 ** ADAM_STEP)
    v_hat = v / (1.0 - ADAM_B2 ** ADAM_STEP)
    delta = -ADAM_LR * (m_hat / (jnp.sqrt(v_hat) + ADAM_EPS) + ADAM_WD * w)
    return delta, m, v


def _adamw(parts, w, m, v, name):
    nl, r, c = w.shape
    tr = min(r, (256 * 512) // c)

    def body(p_ref, w_ref, m_ref, v_ref, g_ref, d_ref, nm_ref, nv_ref):
        g = p_ref[0]
        for s in range(1, N_DEV):
            g = g + p_ref[s]
        g_ref[...] = g
        d_ref[...], nm_ref[...], nv_ref[...] = _adamw_math(g, w_ref[...], m_ref[...], v_ref[...])

    blk = pl.BlockSpec((None, tr, c), lambda l, i: (l, i, 0))
    return pl.pallas_call(
        body, name=name, grid=(nl, r // tr),
        in_specs=[pl.BlockSpec((N_DEV, None, tr, c), lambda l, i: (0, l, i, 0)), blk, blk, blk],
        out_specs=[blk] * 4, out_shape=[jax.ShapeDtypeStruct(w.shape, F32)] * 4,
        compiler_params=_params(32),
    )(parts, w, m, v)


def _adamw_small(parts, w, m, v):
    def body(p_ref, w_ref, m_ref, v_ref, g_ref, d_ref, nm_ref, nv_ref):
        g = p_ref[0]
        for s in range(1, N_DEV):
            g = g + p_ref[s]
        g_ref[...] = g
        d_ref[...], nm_ref[...], nv_ref[...] = _adamw_math(g, w_ref[...], m_ref[...], v_ref[...])

    vmem = pl.BlockSpec(memory_space=pltpu.VMEM)
    return pl.pallas_call(
        body, name="adamw_small", in_specs=[vmem] * 4, out_specs=[vmem] * 4,
        out_shape=[jax.ShapeDtypeStruct(w.shape, F32)] * 4,
    )(parts, w, m, v)


def _pack_small(a, b, c):
    pad = jnp.zeros((a.shape[0], SMALL_W - a.shape[1] - b.shape[1] - c.shape[1]), F32)
    return jnp.concatenate([a, b, c, pad], axis=1)


def _unpack_small(t, d):
    return t[:, :d], t[:, d:d + HEAD_DIM], t[:, d + HEAD_DIM:d + 2 * HEAD_DIM]


def kernel(x, norm_g, w_in, q_norm_g, k_norm_g, w_out, loss_target, m_norm_g, m_w_in, m_q_norm_g, m_k_norm_g, m_w_out,
           v_norm_g, v_w_in, v_q_norm_g, v_k_norm_g, v_w_out):
    depth, d, _ = w_in.shape
    seq = x.shape[1]
    tri_a, tri_b, bd = _tri_constants()
    tables = _rope_tables(seq)
    rep = (2 * LANES) // HEAD_DIM

    win_bf, wout_bf = _cast_bf16(w_in, "cast_w_in"), _cast_bf16(w_out, "cast_w_out")
    win_l, wout_l = _gather_weights(win_bf, wout_bf, 0)
    saved = []
    xl = x.reshape(seq, d)
    for layer in range(depth):
        gains = jnp.stack([jnp.tile(q_norm_g[layer], rep), jnp.tile(k_norm_g[layer], rep)])[:, None, :]
        if layer + 1 < depth:
            proj_bf, gates, qk_raw, h, qkn, win_next, wout_next = _norm_proj(
                xl, norm_g[layer][None, :], win_l, gains, tables, bd, prefetch=(win_bf, wout_bf, layer + 1))
        else:
            proj_bf, gates, qk_raw, h, qkn = _norm_proj(xl, norm_g[layer][None, :], win_l, gains, tables, bd)
            win_next = wout_next = None
        oa = _sb_fwd(proj_bf, tri_a)
        parts = [_dil_fwd(qkn, qk_raw, r) for _, r in DIL_PATTERNS[:-1]]
        od, lse = _dil_fwd(qkn, qk_raw, DIL_PATTERNS[-1][1], others=parts)
        x_next, cat = _out_proj(xl, oa, od, gates, wout_l)
        saved.append((xl, gains, proj_bf, gates, qk_raw, h, qkn, oa, cat, od, lse, win_l, wout_l))
        xl, win_l, wout_l = x_next, win_next, wout_next

    loss_part, dx = _loss_head(xl, loss_target.reshape(seq, d))
    loss = lax.psum(jnp.sum(loss_part), ("x", "y", "c"))

    rin = jnp.zeros((N_DEV, depth, d, COLS_PER_DEV), F32)
    rout = jnp.zeros((N_DEV, depth, ROWS_PER_DEV, d), F32)
    g_norm, g_q, g_k = [None] * depth, [None] * depth, [None] * depth
    for layer in reversed(range(depth)):
        xl, gains, proj_bf, gates, qk_raw, h, qkn, oa, cat, od, lse, win_l, wout_l = saved[layer]
        d_oa, d_od, delta, d_gates, dwout_l = _out_proj_bwd(dx, wout_l, cat, gates, oa, od, bd)
        d_qa, d_ka, d_va = _sb_bwd(proj_bf, d_oa, oa, tri_a, tri_b)
        d_dil = None
        for _, r in DIL_PATTERNS:
            d_dil = _dil_bwd(qkn, qk_raw, d_od, lse, delta, r, others=d_dil)
        dproj, gqk = _assemble_dproj(d_qa, d_ka, d_va, d_gates, *d_dil, qk_raw, gains, tables, bd)
        dwin_l = _dw_in(h, dproj)
        dx, gn, rin, rout = _dx_norm(dproj, win_l, xl, norm_g[layer][None, :], dx, dwin_l, dwout_l, rin, rout, layer)
        g_norm[layer] = jnp.sum(gn, axis=0)
        gqk = jnp.sum(gqk, axis=1).reshape(2, rep, HEAD_DIM).sum(axis=1)
        g_q[layer], g_k[layer] = gqk[0], gqk[1]
    rsmall = _exchange_small(_pack_small(jnp.stack(g_norm), jnp.stack(g_q), jnp.stack(g_k)))

    g_in, d_in, nm_in, nv_in = _adamw(rin, w_in, m_w_in, v_w_in, "adamw_w_in")
    g_out, d_out, nm_out, nv_out = _adamw(rout, w_out, m_w_out, v_w_out, "adamw_w_out")
    small_out = _adamw_small(rsmall, _pack_small(norm_g, q_norm_g, k_norm_g), _pack_small(m_norm_g, m_q_norm_g, m_k_norm_g),
                             _pack_small(v_norm_g, v_q_norm_g, v_k_norm_g))
    (g_n, g_qn, g_kn), (d_n, d_qn, d_kn), (nm_n, nm_qn, nm_kn), (nv_n, nv_qn, nv_kn) = (_unpack_small(t, d) for t in small_out)

    return (loss, dx.reshape(x.shape), g_n, g_in, g_qn, g_kn, g_out, d_n, d_in, d_qn, d_kn, d_out,
            nm_n, nm_in, nm_qn, nm_kn, nm_out, nv_n, nv_in, nv_qn, nv_kn, nv_out)
```

```python
import math

import jax
import jax.numpy as jnp
from jax import lax
from jax.experimental import pallas as pl
from jax.experimental.pallas import tpu as pltpu

F32 = jnp.float32
BF16 = jnp.bfloat16

EPS = 1e-6
HEAD_DIM = 64
BLOCK = 128
LANES = 128
W_SB = 256
W_DIL = 768
MIX = W_SB + W_DIL
IN_COLS = 4 * W_SB + 4 * W_DIL
N_DEV = 8
COLS_PER_DEV = IN_COLS // N_DEV
ROWS_PER_DEV = MIX // N_DEV
QK_SCALE = 1.0 / math.sqrt(HEAD_DIM)
DIL_PATTERNS = ((128, 1), (512, 4), (2048, 16))
DIL_TILE = 2048
DIL_GROUP_FWD = 4
DIL_GROUP_BWD = 4
SPLIT = 4
ROPE_THETA = 500000.0
ROPE_DIM = HEAD_DIM // 4
ROPE_HALF = ROPE_DIM // 2
DEAD_LOG = -110.0
SB_PEEL = 3
SB_QBLOCKS = 2
SB_SUM_TERMS = 2
NEG_BIG = -1e30

ADAM_LR = 0.001
ADAM_B1 = 0.9
ADAM_B2 = 0.999
ADAM_EPS = 1e-08
ADAM_WD = 0.01
ADAM_STEP = 10

SMALL_W = 1280
MESH_ID = pl.DeviceIdType.MESH
MIB = 1 << 20


def _params(vmem_mib):
    return pltpu.CompilerParams(vmem_limit_bytes=vmem_mib * MIB)


def _dot(a, b):
    return jnp.dot(a, b, preferred_element_type=F32)


def _dot_nt(a, b):
    return lax.dot_general(a, b, (((1,), (1,)), ((), ())), preferred_element_type=F32)


def _dot_tn(a, b):
    return lax.dot_general(a, b, (((0,), (0,)), ((), ())), preferred_element_type=F32)


def _dot_exact(x, m01, terms=3):
    out = 0.0
    for _ in range(terms):
        part = x.astype(BF16)
        out = out + _dot(part, m01)
        x = x - part.astype(F32)
    return out


def _lane_lo(shape):
    return lax.broadcasted_iota(jnp.int32, shape, 1) < HEAD_DIM


def _tri_constants():
    j = jnp.arange(BLOCK)
    ones = jnp.ones((BLOCK, BLOCK), F32)
    excl = (j[:, None] > j[None, :]).astype(F32)
    incl = (j[:, None] >= j[None, :]).astype(F32)
    tri_a = jnp.concatenate([excl, ones], axis=1).astype(BF16)
    tri_b = jnp.concatenate([incl, ones], axis=1).astype(BF16)
    d = jnp.arange(2 * LANES)
    bd = (d[:, None] // HEAD_DIM == d[None, :] // HEAD_DIM).astype(BF16)
    return tri_a, tri_b, bd


def _rope_tables(seq):
    d = jnp.arange(2 * LANES) % HEAD_DIM
    inv_freq = 1.0 / (ROPE_THETA ** ((d % ROPE_HALF).astype(F32) * 2.0 / ROPE_DIM))
    ang = jnp.arange(seq).astype(F32)[:, None] * inv_freq[None, :]
    cos, sin = jnp.cos(ang), jnp.sin(ang)
    c = jnp.where(d < ROPE_DIM, cos, 1.0)
    s_next = jnp.where(d < ROPE_HALF, -sin, 0.0)
    s_prev = jnp.where((d >= ROPE_HALF) & (d < ROPE_DIM), sin, 0.0)
    return c, s_next, s_prev


def _roll_lanes(x, shift):
    return jnp.concatenate([pltpu.roll(x[:, :LANES], shift, 1), pltpu.roll(x[:, LANES:], shift, 1)], axis=1)


def _rope(x, c, s_next, s_prev):
    return x * c + _roll_lanes(x, LANES - ROPE_HALF) * s_next + _roll_lanes(x, ROPE_HALF) * s_prev


def _rope_t(dy, c, s_next, s_prev):
    return dy * c + _roll_lanes(dy * s_next, ROPE_HALF) + _roll_lanes(dy * s_prev, LANES - ROPE_HALF)


def _cast_bf16(w, name):
    nl, r, c = w.shape

    def body(w_ref, o_ref):
        o_ref[...] = w_ref[...].astype(BF16)

    return pl.pallas_call(
        body, name=name, grid=(nl,),
        in_specs=[pl.BlockSpec((None, r, c), lambda l: (l, 0, 0))],
        out_specs=pl.BlockSpec((None, r, c), lambda l: (l, 0, 0)),
        out_shape=jax.ShapeDtypeStruct(w.shape, BF16),
        compiler_params=_params(24),
    )(w)


def _flips():
    return [(dx, dy, dc) for dx in (0, 1) for dy in (0, 1) for dc in (0, 1) if (dx, dy, dc) != (0, 0, 0)]


def _place():
    x, y, c = lax.axis_index("x"), lax.axis_index("y"), lax.axis_index("c")
    return x, y, c, 4 * x + 2 * y + c


def _peer(x, y, c, flip):
    dx, dy, dc = flip
    return (1 - x if dx else x, 1 - y if dy else y, 1 - c if dc else c)


def _to_every_device(srcs_for, dsts_at, sems):
    send_sems, recv_sems, local_sems = sems
    x, y, c, me = _place()
    dsts = dsts_at(me)
    n = len(dsts)
    copies = [pltpu.make_async_copy(src, dst, local_sems.at[a]) for a, (src, dst) in enumerate(zip(srcs_for(me), dsts))]
    for k, flip in enumerate(_flips()):
        px, py, pc = _peer(x, y, c, flip)
        for a, (src, dst) in enumerate(zip(srcs_for(4 * px + 2 * py + pc), dsts)):
            copies.append(pltpu.make_async_remote_copy(
                src_ref=src, dst_ref=dst, send_sem=send_sems.at[n * k + a], recv_sem=recv_sems.at[n * k + a],
                device_id=(px, py, pc), device_id_type=MESH_ID))
    return copies


def _copy_sems(n):
    remote = n * (N_DEV - 1)
    return [pltpu.SemaphoreType.DMA((remote,)), pltpu.SemaphoreType.DMA((remote,)), pltpu.SemaphoreType.DMA((n,))]


def _weight_copies(win_ref, wout_ref, layer, oin_ref, oout_ref, sems):
    return _to_every_device(lambda to: (win_ref.at[layer], wout_ref.at[layer]), lambda me: (oin_ref.at[me], oout_ref.at[me]), sems)


def _gathered_shapes(win_bf, wout_bf):
    return [jax.ShapeDtypeStruct((N_DEV,) + win_bf.shape[1:], BF16), jax.ShapeDtypeStruct((N_DEV,) + wout_bf.shape[1:], BF16)]


def _gather_weights(win_bf, wout_bf, layer):
    def body(win_ref, wout_ref, oin_ref, oout_ref, *sems):
        copies = _weight_copies(win_ref, wout_ref, layer, oin_ref, oout_ref, sems)
        for cp in copies:
            cp.start()
        for cp in copies:
            cp.wait()

    any_spec = pl.BlockSpec(memory_space=pl.ANY)
    return pl.pallas_call(
        body, name="gather_weights", in_specs=[any_spec, any_spec], out_specs=[any_spec, any_spec],
        out_shape=_gathered_shapes(win_bf, wout_bf), scratch_shapes=_copy_sems(2),
    )(win_bf, wout_bf)


_QK_BLOCKS = (2, 3, 4)
_F32_ROUTES = {1: ((256, 256, 0, 0),), 2: ((0, 512, 1, 0),), 3: ((0, 512, 1, 512),), 4: ((0, 512, 1, 1024),),
               5: ((0, 512, 1, 1536),), 6: ((0, 256, 1, 2048), (256, 256, 0, 256)), 7: ((0, 512, 0, 512),)}


def _norm_proj(x, g, w_l, gains, tables, bd, prefetch=None):
    seq, d = x.shape
    tm = min(256, seq)
    steps = seq // tm

    def body(x_ref, g_ref, w_ref, gains_ref, c_ref, sn_ref, sp_ref, bd_ref, *rest):
        if prefetch is None:
            pbf_ref, gates_ref, qk_ref, h_ref, qkn_ref = rest
        else:
            win_ref, wout_ref, pbf_ref, gates_ref, qk_ref, h_ref, qkn_ref, oin_ref, oout_ref, *sems = rest
            copies = lambda: _weight_copies(win_ref, wout_ref, prefetch[2], oin_ref, oout_ref, sems)

            @pl.when(pl.program_id(0) == 0)
            def _():
                for cp in copies():
                    cp.start()

        xf = x_ref[...]
        rs = lax.rsqrt(jnp.mean(xf * xf, axis=-1, keepdims=True) + EPS)
        h = (xf * rs * g_ref[...]).astype(BF16)
        h_ref[...] = h
        targets = (gates_ref, qk_ref)
        for n in range(N_DEV):
            acc = _dot(h, w_ref[n])
            pbf_ref[:, n * COLS_PER_DEV:(n + 1) * COLS_PER_DEV] = acc.astype(BF16)
            for lo, width, tgt, dst in _F32_ROUTES.get(n, ()):
                targets[tgt][:, dst:dst + width] = acc[:, lo:lo + width]
            for half in range(2) if n in _QK_BLOCKS else ():
                col = (_QK_BLOCKS.index(n) * 2 + half) * 2 * LANES
                is_q = col < W_DIL
                xq = acc[:, half * 2 * LANES:(half + 1) * 2 * LANES]
                rsq = lax.rsqrt(_head_sums(xq * xq, bd_ref[...]) * (1.0 / HEAD_DIM) + EPS)
                y = _rope(xq * rsq * gains_ref[0 if is_q else 1], c_ref[...], sn_ref[...], sp_ref[...])
                qkn_ref[:, col:col + 2 * LANES] = y * QK_SCALE if is_q else y

        if prefetch is not None:
            @pl.when(pl.program_id(0) == steps - 1)
            def _():
                for cp in copies():
                    cp.wait()

    row = lambda w: pl.BlockSpec((tm, w), lambda i: (i, 0))
    any_spec = pl.BlockSpec(memory_space=pl.ANY)
    const = lambda shape: pl.BlockSpec(shape, lambda i: tuple(0 for _ in shape))
    in_specs = [row(d), const((1, d)), const((N_DEV, d, COLS_PER_DEV)), const((2, 1, 2 * LANES)),
                row(2 * LANES), row(2 * LANES), row(2 * LANES), const((2 * LANES, 2 * LANES))]
    out_specs = [row(IN_COLS), row(MIX), row(3 * W_DIL), row(d), row(2 * W_DIL)]
    out_shape = [jax.ShapeDtypeStruct((seq, IN_COLS), BF16), jax.ShapeDtypeStruct((seq, MIX), F32),
                 jax.ShapeDtypeStruct((seq, 3 * W_DIL), F32), jax.ShapeDtypeStruct((seq, d), BF16),
                 jax.ShapeDtypeStruct((seq, 2 * W_DIL), F32)]
    operands = (x, g, w_l, gains, *tables, bd)
    if prefetch is None:
        return pl.pallas_call(body, name="norm_proj", grid=(steps,), in_specs=in_specs, out_specs=out_specs,
                              out_shape=out_shape, compiler_params=_params(48))(*operands)
    return pl.pallas_call(
        body, name="norm_proj_gather", grid=(steps,), in_specs=in_specs + [any_spec, any_spec],
        out_specs=out_specs + [any_spec, any_spec], out_shape=out_shape + _gathered_shapes(*prefetch[:2]),
        scratch_shapes=_copy_sems(2), compiler_params=_params(48),
    )(*operands, *prefetch[:2])


def _head_sums(v, bd):
    hi = v.astype(BF16)
    lo = (v - hi.astype(F32)).astype(BF16)
    return _dot(hi, bd) + _dot(lo, bd)


def _stack_heads(x, lo):
    return jnp.concatenate([jnp.where(lo, x, 0.0), jnp.where(lo, 0.0, x)], axis=0)


def _unstack_heads(y, lo):
    return jnp.where(lo, y[:BLOCK], y[BLOCK:])


def _stacked_causal():
    row = lax.broadcasted_iota(jnp.int32, (2 * BLOCK, BLOCK), 0) & (BLOCK - 1)
    return lax.broadcasted_iota(jnp.int32, (2 * BLOCK, BLOCK), 1) < row


def _keep(x, *conds):
    for cond in conds:
        if cond is not None:
            x = jnp.where(cond, x, 0.0)
    return x


def _sb_weights(chains, tri):
    zs = [[_dot_nt(qs, kb) for kb in kbs] for qs, kbs, _, _, _ in chains]
    lss = [[jnp.minimum(z, 0.0) - jnp.log1p(jnp.exp(-jnp.abs(z))) for z in zc] for zc in zs]
    cts = [[_dot_exact(_keep(ls - z, mask, live), tri, SB_SUM_TERMS) for ls, z, mask, live in zip(lsc, zc, masks, lives)]
           for lsc, zc, (_, _, _, masks, lives) in zip(lss, zs, chains)]
    out = []
    for lsc, ctc, (_, _, r, masks, lives) in zip(lss, cts, chains):
        weights = []
        for ls, ct, mask, live in zip(lsc, ctc, masks, lives):
            weights.append(_keep(jnp.exp(ls + ct[:, :BLOCK] + r), mask, live))
            r = r + ct[:, BLOCK:]
        out.append((lsc, weights, r))
    return out


def _sb_peel(i, causal):
    return ([jnp.maximum(i - k, 0) for k in range(SB_PEEL)], [causal] + [None] * (SB_PEEL - 1),
            [None] + [i >= k for k in range(1, SB_PEEL)])


def _block_rows(j):
    return pl.ds(pl.multiple_of(j * BLOCK, BLOCK), BLOCK)


def _sb_fwd(proj_bf, tri_a):
    seq = proj_bf.shape[0]
    pairs = W_SB // LANES

    def body(q_ref, k_ref, v_ref, tri_ref, o_ref, r_ref, acc_ref):
        t = pl.program_id(1)
        lo = _lane_lo((BLOCK, LANES))
        causal = _stacked_causal()
        tri = tri_ref[...]
        qblocks = [t * SB_QBLOCKS + c for c in range(SB_QBLOCKS)]
        qss = [_stack_heads(q_ref[c * BLOCK:(c + 1) * BLOCK, :].astype(F32) * QK_SCALE, lo).astype(BF16)
               for c in range(SB_QBLOCKS)]

        def blocks(specs):
            rows = [[_block_rows(j) for j in js] for _, js, _, _, _ in specs]
            res = _sb_weights([(qs, [k_ref[rw, :] for rw in rws], r, masks, lives)
                               for (qs, _, r, masks, lives), rws in zip(specs, rows)], tri)
            outs = []
            for (_, weights, r), rws in zip(res, rows):
                out = 0.0
                for a, rw in zip(weights, rws):
                    a_hi = a.astype(BF16)
                    a_lo = (a - a_hi.astype(F32)).astype(BF16)
                    vb = v_ref[rw, :]
                    out = out + _dot(a_hi, vb) + _dot(a_lo, vb)
                outs.append((out, r))
            return outs

        def peel(qs, i):
            js, masks, lives = _sb_peel(i, causal)
            return qs, js, jnp.zeros((2 * BLOCK, LANES), F32), masks, lives

        for c, (out, r) in enumerate(blocks([peel(qs, i) for qs, i in zip(qss, qblocks)])):
            acc_ref[c], r_ref[c] = out, r

        for c in range(SB_QBLOCKS):
            def alive(c=c):
                return (jnp.max(r_ref[c]) > DEAD_LOG).astype(jnp.int32)

            def step(carry, c=c):
                (out, r_ref[c]), = blocks([(qss[c], [carry[0]], r_ref[c], [None], [None])])
                acc_ref[c] += out
                return carry[0] - 1, alive(c)

            lax.while_loop(lambda carry: jnp.logical_and(carry[0] >= 0, carry[1] > 0), step, (qblocks[c] - SB_PEEL, alive()))
            o_ref[c * BLOCK:(c + 1) * BLOCK, :] = _unstack_heads(acc_ref[c], lo)

    qtile = SB_QBLOCKS * BLOCK
    state = pltpu.VMEM((SB_QBLOCKS, 2 * BLOCK, LANES), F32)
    return pl.pallas_call(
        body, name="sb_fwd", grid=(pairs, seq // qtile),
        in_specs=[pl.BlockSpec((qtile, LANES), lambda p, t: (t, p)),
                  pl.BlockSpec((seq, LANES), lambda p, t: (0, pairs + p)),
                  pl.BlockSpec((seq, LANES), lambda p, t: (0, 2 * pairs + p)),
                  pl.BlockSpec((BLOCK, 2 * BLOCK), lambda p, t: (0, 0))],
        out_specs=pl.BlockSpec((qtile, LANES), lambda p, t: (t, p)),
        out_shape=jax.ShapeDtypeStruct((seq, W_SB), F32),
        scratch_shapes=[state, state],
        compiler_params=_params(40),
    )(proj_bf, proj_bf, proj_bf, tri_a)


def _class_rows(cls, first, count, r):
    c = cls[0] + SPLIT * cls[1] if isinstance(cls, tuple) else cls
    start = c + first * r
    return pl.ds(start, count) if r == 1 else pl.ds(start, count, stride=r)


def _class_reader(r, refs, scratches):
    if r != SPLIT * SPLIT:
        return lambda ref, cls, first, count: ref[_class_rows(cls, first, count, r), :]
    slabs = {}
    for ref, scr in zip(refs, scratches):
        for lo in range(SPLIT):
            scr[lo] = ref[pl.ds(lo, ref.shape[0] // SPLIT, stride=SPLIT), :]
        slabs[id(ref)] = scr

    def take(ref, cls, first, count):
        lo, hi = cls
        return slabs[id(ref)][lo, pl.ds(hi + SPLIT * first, count, stride=SPLIT), :]

    return take


def _class_scratches(r, shapes):
    return [pltpu.VMEM((SPLIT, rows // SPLIT, LANES), F32) for rows in shapes] if r == SPLIT * SPLIT else []


def _dil_tiling(seq, r):
    tile = min(DIL_TILE, seq)
    edge = BLOCK * r
    return tile, edge, tile // edge


def _band_mask(n_rows, n_keys):
    row = lax.broadcasted_iota(jnp.int32, (n_rows, n_keys), 0) & (BLOCK - 1)
    col = lax.broadcasted_iota(jnp.int32, (n_rows, n_keys), 1)
    return jnp.logical_and(col >= row, col <= row + BLOCK), col


def _for_each_group(r, per, g, run):
    loop = lambda lo, hi, fn: lax.fori_loop(lo, hi, lambda t, carry: (fn(t), carry)[1], 0)
    if per <= g:
        n_cls = g // per
        assert g % per == 0 and r % n_cls == 0 and (r != SPLIT * SPLIT or n_cls == SPLIT)
        cls = (lambda t, u: (u, t)) if r == SPLIT * SPLIT else (lambda t, u: t * n_cls + u)
        group = lambda t: [(cls(t, u), a, a == 0) for u in range(n_cls) for a in range(per)]
        if r == n_cls:
            run(group(0))
        else:
            loop(0, r // n_cls, lambda t: run(group(t)))
        return

    assert per % g == 0

    def one_class(c):
        run([(c, 0, True)] + [(c, a, False) for a in range(1, g)])
        loop(1, per // g, lambda t: run([(c, t * g + u, False) for u in range(g)]))

    if r == 1:
        one_class(0)
    else:
        loop(0, r, one_class)


def _dil_keys(blocks, take, cur_ref, before_ref):
    out = []
    for cls, a, first in blocks:
        if first:
            both = jnp.concatenate([take(before_ref, cls, 0, BLOCK), take(cur_ref, cls, 0, BLOCK)], axis=0)
        else:
            both = take(cur_ref, cls, (a - 1) * BLOCK, 2 * BLOCK)
        out.append(both.astype(BF16))
    return out


def _dil_fwd(qkn, qkv_raw, r, others=()):
    seq = qkn.shape[0]
    tile, edge, per = _dil_tiling(seq, r)
    pairs = W_DIL // LANES

    def body(q_ref, kc_ref, kp_ref, vc_ref, vp_ref, *rest):
        other_refs, (o_ref, ld_ref, *scratches) = rest[:2 * len(others)], rest[2 * len(others):]
        n = pl.program_id(1)
        lo = _lane_lo((BLOCK, LANES))
        band, col = _band_mask(2 * BLOCK, 2 * BLOCK)
        first_band = jnp.logical_and(band, jnp.logical_or(col >= BLOCK, n > 0))
        take = _class_reader(r, (q_ref, kc_ref, kp_ref, vc_ref, vp_ref), scratches)

        def run(blocks):
            rows = [_class_rows(c, a * BLOCK, BLOCK, r) for c, a, _ in blocks]
            keys = _dil_keys(blocks, take, kc_ref, kp_ref)
            vals = _dil_keys(blocks, take, vc_ref, vp_ref)
            scores = [_dot_nt(_stack_heads(take(q_ref, c, a * BLOCK, BLOCK), lo).astype(BF16), kb)
                      for (c, a, _), kb in zip(blocks, keys)]
            probs, sums, lds = [], [], []
            for s, (_, _, first) in zip(scores, blocks):
                s = jnp.where(first_band if first else band, s, NEG_BIG)
                m = jnp.max(s, axis=1, keepdims=True)
                p = jnp.exp(s - m)
                l = jnp.sum(p, axis=1, keepdims=True)
                probs.append(p.astype(BF16))
                sums.append(l)
                lds.append(m + jnp.log(l))
            outs = [_dot(p, vb) / l for p, vb, l in zip(probs, vals, sums)]
            for rw, o, ld in zip(rows, outs, lds):
                o_ref[rw, :] = _unstack_heads(o, lo)
                ld_ref[rw, :] = _unstack_heads(jnp.broadcast_to(ld, (2 * BLOCK, LANES)), lo)

        _for_each_group(r, per, DIL_GROUP_FWD, run)
        if others:
            o_ref[...], ld_ref[...] = _mix_patterns([ref[...] for ref in other_refs[0::2]] + [o_ref[...]],
                                                    [ref[...] for ref in other_refs[1::2]] + [ld_ref[...]])

    per_edge = tile // edge
    cur = lambda off: pl.BlockSpec((tile, LANES), lambda p, n: (n, off + p))
    before = lambda off: pl.BlockSpec((edge, LANES), lambda p, n: (jnp.maximum(n * per_edge - 1, 0), off + p))
    return pl.pallas_call(
        body, name=f"dil_fwd_r{r}", grid=(pairs, seq // tile),
        in_specs=[cur(0), cur(pairs), before(pairs), cur(2 * pairs), before(2 * pairs)] + [cur(0)] * (2 * len(others)),
        out_specs=[cur(0), cur(0)],
        out_shape=[jax.ShapeDtypeStruct((seq, W_DIL), F32), jax.ShapeDtypeStruct((seq, W_DIL), F32)],
        scratch_shapes=_class_scratches(r, (tile, tile, edge, tile, edge)),
        compiler_params=_params(40),
    )(qkn, qkn, qkn, qkv_raw, qkv_raw, *[a for pair in others for a in pair])


def _silu_parts(g):
    sig = jax.nn.sigmoid(g)
    return g * sig, sig * (1.0 + g * (1.0 - sig))


def _mix_patterns(os_, lds):
    m = lds[0]
    for v in lds[1:]:
        m = jnp.maximum(m, v)
    es = [jnp.exp(v - m) for v in lds]
    tot = es[0]
    for e in es[1:]:
        tot = tot + e
    inv = 1.0 / tot
    od = (es[0] * inv) * os_[0]
    for e, o in zip(es[1:], os_[1:]):
        od = od + (e * inv) * o
    return od, m + jnp.log(tot)


def _out_proj(x, oa, od, gates, wout_l):
    seq, d = x.shape
    tm = min(256, seq)

    def body(x_ref, oa_ref, od_ref, g_ref, w_ref, xn_ref, cat_ref):
        silu, _ = _silu_parts(g_ref[...])
        cat_ref[:, :W_SB] = (oa_ref[...] * silu[:, :W_SB]).astype(BF16)
        cat_ref[:, W_SB:] = (od_ref[...] * silu[:, W_SB:]).astype(BF16)
        y = x_ref[...]
        for b in range(N_DEV):
            y = y + _dot(cat_ref[:, b * ROWS_PER_DEV:(b + 1) * ROWS_PER_DEV], w_ref[b])
        xn_ref[...] = y

    row = lambda w: pl.BlockSpec((tm, w), lambda i: (i, 0))
    return pl.pallas_call(
        body, name="out_proj", grid=(seq // tm,),
        in_specs=[row(d), row(W_SB), row(W_DIL), row(MIX), pl.BlockSpec((N_DEV, ROWS_PER_DEV, d), lambda i: (0, 0, 0))],
        out_specs=[row(d), row(MIX)],
        out_shape=[jax.ShapeDtypeStruct((seq, d), F32), jax.ShapeDtypeStruct((seq, MIX), BF16)],
        compiler_params=_params(40),
    )(x, oa, od, gates, wout_l)


def _loss_head(y, target):
    seq, d = y.shape
    tm = min(512, seq)

    def body(y_ref, t_ref, part_ref, dy_ref):
        @pl.when(pl.program_id(0) == 0)
        def _():
            part_ref[...] = jnp.zeros_like(part_ref)

        diff = y_ref[...] - t_ref[...]
        dy_ref[...] = diff * (1.0 / d)
        part_ref[...] += jnp.sum((diff * diff).reshape(tm // 8, 8, d), axis=0) * (0.5 / d)

    row = pl.BlockSpec((tm, d), lambda i: (i, 0))
    return pl.pallas_call(
        body, name="loss_head", grid=(seq // tm,),
        in_specs=[row, row], out_specs=[pl.BlockSpec((8, d), lambda i: (0, 0)), row],
        out_shape=[jax.ShapeDtypeStruct((8, d), F32), jax.ShapeDtypeStruct((seq, d), F32)],
        compiler_params=_params(32),
    )(y, target)


def _out_proj_bwd(dy, wout_l, cat, gates, oa, od, bd):
    seq, d = dy.shape
    tm = min(256, seq)

    def body(dy_ref, w_ref, cat_ref, g_ref, oa_ref, od_ref, bd_ref, doa_ref, dod_ref, delta_ref, dg_ref, dw_ref, dcat):
        @pl.when(pl.program_id(0) == 0)
        def _():
            dw_ref[...] = jnp.zeros_like(dw_ref)

        dyb = dy_ref[...].astype(BF16)
        dw = _dot_tn(cat_ref[...], dyb)
        for b in range(N_DEV):
            dw_ref[b] += dw[b * ROWS_PER_DEV:(b + 1) * ROWS_PER_DEV, :]
            dcat[:, b * ROWS_PER_DEV:(b + 1) * ROWS_PER_DEV] = _dot_nt(dyb, w_ref[b])
        silu, dsilu = _silu_parts(g_ref[...])
        dc = dcat[...]
        dmix = dc * silu
        oa_v, od_v = oa_ref[...], od_ref[...]
        dg_ref[:, :W_SB] = dc[:, :W_SB] * oa_v * dsilu[:, :W_SB]
        dg_ref[:, W_SB:] = dc[:, W_SB:] * od_v * dsilu[:, W_SB:]
        doa_ref[...] = dmix[:, :W_SB]
        dod = dmix[:, W_SB:]
        dod_ref[...] = dod
        prod = dod * od_v
        for k in range(W_DIL // (2 * LANES)):
            sl = slice(k * 2 * LANES, (k + 1) * 2 * LANES)
            delta_ref[:, sl] = _head_sums(prod[:, sl], bd_ref[...])

    row = lambda w: pl.BlockSpec((tm, w), lambda i: (i, 0))
    slab = pl.BlockSpec((N_DEV, ROWS_PER_DEV, d), lambda i: (0, 0, 0))
    return pl.pallas_call(
        body, name="out_proj_bwd", grid=(seq // tm,),
        in_specs=[row(d), slab, row(MIX), row(MIX), row(W_SB), row(W_DIL),
                  pl.BlockSpec((2 * LANES, 2 * LANES), lambda i: (0, 0))],
        out_specs=[row(W_SB), row(W_DIL), row(W_DIL), row(MIX), slab],
        out_shape=[jax.ShapeDtypeStruct((seq, W_SB), F32), jax.ShapeDtypeStruct((seq, W_DIL), F32),
                   jax.ShapeDtypeStruct((seq, W_DIL), F32), jax.ShapeDtypeStruct((seq, MIX), F32),
                   jax.ShapeDtypeStruct((N_DEV, ROWS_PER_DEV, d), F32)],
        scratch_shapes=[pltpu.VMEM((tm, MIX), F32)],
        compiler_params=_params(48),
    )(dy, wout_l, cat, gates, oa, od, bd)


def _sb_bwd(proj_bf, d_oa, oa, tri_a, tri_b):
    seq = proj_bf.shape[0]
    pairs = W_SB // LANES
    nq = seq // BLOCK

    def body(q_ref, k_ref, v_ref, do_ref, o_ref, tria_ref, trib_ref, dq_ref, dk_hbm, dv_hbm,
             r_ref, sfx_ref, dtot_ref, dq_acc, dk_acc, dv_acc, sems):
        p, t = pl.program_id(0), pl.program_id(1)

        @pl.when(t == 0)
        def _():
            dk_acc[...] = jnp.zeros_like(dk_acc)
            dv_acc[...] = jnp.zeros_like(dv_acc)

        lo = _lane_lo((BLOCK, LANES))
        tri, trib = tria_ref[...], trib_ref[...]
        causal = _stacked_causal()
        qblocks = [t * SB_QBLOCKS + c for c in range(SB_QBLOCKS)]
        qss, doss = [], []
        for c in range(SB_QBLOCKS):
            sl = slice(c * BLOCK, (c + 1) * BLOCK)
            qss.append(_stack_heads(q_ref[sl, :].astype(F32) * QK_SCALE, lo).astype(BF16))
            doss.append(_stack_heads(do_ref[sl, :], lo).astype(BF16))
            o2 = o_ref[sl, :]
            dtot_ref[c] = _dot_exact(doss[c].astype(F32) * jnp.concatenate([o2, o2], axis=0), tri[:, BLOCK:])

        def blocks(specs):
            rows = [[_block_rows(j) for j in js] for _, js, _, _, _, _ in specs]
            kbs = [[k_ref[rw, :] for rw in rws] for rws in rows]
            dovs = [[_dot_nt(doss[c], v_ref[rw, :]) for rw in rws] for (c, *_), rws in zip(specs, rows)]
            res = _sb_weights([(qss[c], kbc, r, masks, lives) for (c, _, r, _, masks, lives), kbc in zip(specs, kbs)], tri)
            pws = [[a * dov for a, dov in zip(weights, dovc)] for (_, weights, _), dovc in zip(res, dovs)]
            cps = [[_dot_exact(pw, trib, SB_SUM_TERMS) for pw in pwc] for pwc in pws]
            dzs, sfxs = [], []
            for (c, _, _, sfx, masks, lives), (lsc, _, _), pwc, cpc in zip(specs, res, pws, cps):
                dzc = []
                for ls, pw, cp, mask, live in zip(lsc, pwc, cpc, masks, lives):
                    beta = jnp.exp(ls)
                    before = dtot_ref[c] - sfx - cp[:, :BLOCK]
                    dzc.append(_keep(pw * (1.0 - beta) - before * beta, mask, live).astype(BF16))
                    sfx = sfx + cp[:, BLOCK:]
                dzs.append(dzc)
                sfxs.append(sfx)
            dqs = []
            for dzc, kbc in zip(dzs, kbs):
                dq = 0.0
                for dzb, kb in zip(dzc, kbc):
                    dq = dq + _dot(dzb, kb)
                dqs.append(dq)
            for (c, *_), dzc, (_, weights, _), rws in zip(specs, dzs, res, rows):
                for dzb, a, rw in zip(dzc, weights, rws):
                    dk_acc[rw, :] += _dot_tn(dzb, qss[c])
                    dv_acc[rw, :] += _dot_tn(a.astype(BF16), doss[c])
            return [(dq, r, sfx) for dq, (_, _, r), sfx in zip(dqs, res, sfxs)]

        def peel(c, i):
            js, masks, lives = _sb_peel(i, causal)
            zero = jnp.zeros((2 * BLOCK, LANES), F32)
            return c, js, zero, zero, masks, lives

        for c, (dq, r, sfx) in enumerate(blocks([peel(c, i) for c, i in enumerate(qblocks)])):
            dq_acc[c], r_ref[c], sfx_ref[c] = dq, r, sfx

        for c in range(SB_QBLOCKS):
            def alive(c=c):
                return (jnp.max(r_ref[c]) > DEAD_LOG).astype(jnp.int32)

            def step(carry, c=c):
                (dq, r_ref[c], sfx_ref[c]), = blocks([(c, [carry[0]], r_ref[c], sfx_ref[c], [None], [None])])
                dq_acc[c] += dq
                return carry[0] - 1, alive(c)

            lax.while_loop(lambda carry: jnp.logical_and(carry[0] >= 0, carry[1] > 0), step, (qblocks[c] - SB_PEEL, alive()))
            dq_ref[c * BLOCK:(c + 1) * BLOCK, :] = _unstack_heads(dq_acc[c], lo) * QK_SCALE

        @pl.when(t == nq // SB_QBLOCKS - 1)
        def _():
            outs = [pltpu.make_async_copy(dk_acc, dk_hbm.at[p], sems.at[0]),
                    pltpu.make_async_copy(dv_acc, dv_hbm.at[p], sems.at[1])]
            for cp in outs:
                cp.start()
            for cp in outs:
                cp.wait()

    qtile = SB_QBLOCKS * BLOCK
    blk = pl.BlockSpec((qtile, LANES), lambda p, t: (t, p))
    const = pl.BlockSpec((BLOCK, 2 * BLOCK), lambda p, t: (0, 0))
    any_spec = pl.BlockSpec(memory_space=pl.ANY)
    state = pltpu.VMEM((SB_QBLOCKS, 2 * BLOCK, LANES), F32)
    return pl.pallas_call(
        body, name="sb_bwd", grid=(pairs, seq // qtile),
        in_specs=[blk, pl.BlockSpec((seq, LANES), lambda p, t: (0, pairs + p)),
                  pl.BlockSpec((seq, LANES), lambda p, t: (0, 2 * pairs + p)), blk, blk, const, const],
        out_specs=[blk, any_spec, any_spec],
        out_shape=[jax.ShapeDtypeStruct((seq, W_SB), F32), jax.ShapeDtypeStruct((pairs, seq, LANES), F32),
                   jax.ShapeDtypeStruct((pairs, seq, LANES), F32)],
        scratch_shapes=[state, state, state, state, pltpu.VMEM((seq, LANES), F32), pltpu.VMEM((seq, LANES), F32),
                        pltpu.SemaphoreType.DMA((2,))],
        compiler_params=_params(56),
    )(proj_bf, proj_bf, proj_bf, d_oa, oa, tri_a, tri_b)


def _dil_bwd(qkn, qkv_raw, d_od, lse, delta, r, others=None):
    seq = qkn.shape[0]
    tile, edge, per = _dil_tiling(seq, r)
    ntile = seq // tile
    pairs = W_DIL // LANES

    def body(qc, doc, lsc, dlc, kc, kp, vc, vp, *rest):
        if others is None:
            dq_ref, dk_ref, dv_ref, dk_carry, dv_carry, *scratches = rest
        else:
            dq_in, dk_in, dv_in, dq_ref, dk_ref, dv_ref, dk_carry, dv_carry, *scratches = rest
        n = pl.program_id(1)

        @pl.when(n == 0)
        def _():
            dk_carry[...] = jnp.zeros_like(dk_carry)
            dv_carry[...] = jnp.zeros_like(dv_carry)

        if others is None:
            dk_ref[...] = dk_carry[...]
            dv_ref[...] = dv_carry[...]
        else:
            dk_ref[...] = dk_carry[...] + dk_in[...]
            dv_ref[...] = dv_carry[...] + dv_in[...]

        @pl.when(n < ntile)
        def _():
            lo = _lane_lo((BLOCK, LANES))
            band, col = _band_mask(2 * BLOCK, 2 * BLOCK)
            first_band = jnp.logical_and(band, jnp.logical_or(col >= BLOCK, n > 0))
            take = _class_reader(r, (qc, doc, lsc, dlc, kc, kp, vc, vp), scratches)

            def stacked_cols(b):
                other = pltpu.roll(b, HEAD_DIM, 1)
                rows = jnp.concatenate([jnp.where(lo, b, other), jnp.where(lo, other, b)], axis=0)
                return jnp.concatenate([rows, rows], axis=1)

            def run(blocks):
                rows = [_class_rows(c, a * BLOCK, BLOCK, r) for c, a, _ in blocks]
                own = lambda ref: [take(ref, c, a * BLOCK, BLOCK) for c, a, _ in blocks]
                keys = _dil_keys(blocks, take, kc, kp)
                vals = _dil_keys(blocks, take, vc, vp)
                qss = [_stack_heads(x, lo).astype(BF16) for x in own(qc)]
                doss = [_stack_heads(x, lo).astype(BF16) for x in own(doc)]
                scores = [_dot_nt(qs, kb) for qs, kb in zip(qss, keys)]
                dps = [_dot_nt(dos, vb) for dos, vb in zip(doss, vals)]
                pws, dss = [], []
                for lsb, dlb, s, dp, (_, _, first) in zip(own(lsc), own(dlc), scores, dps, blocks):
                    pw = jnp.where(first_band if first else band, jnp.exp(s - stacked_cols(lsb)), 0.0)
                    pws.append(pw.astype(BF16))
                    dss.append((pw * (dp - stacked_cols(dlb))).astype(BF16))
                dqs = [_dot(ds, kb) for ds, kb in zip(dss, keys)]
                dks = [_dot_tn(ds, qs) for ds, qs in zip(dss, qss)]
                dvs = [_dot_tn(pw, dos) for pw, dos in zip(pws, doss)]
                for (c, a, first), rw, dq, dk, dv in zip(blocks, rows, dqs, dks, dvs):
                    dq_ref[rw, :] = _unstack_heads(dq, lo)
                    if first:
                        last = _class_rows(c, (per - 1) * BLOCK, BLOCK, r)
                        dk_ref[last, :] += dk[:BLOCK]
                        dv_ref[last, :] += dv[:BLOCK]
                    else:
                        prev = _class_rows(c, (a - 1) * BLOCK, BLOCK, r)
                        dk_carry[prev, :] += dk[:BLOCK]
                        dv_carry[prev, :] += dv[:BLOCK]
                    dk_carry[rw, :] = dk[BLOCK:]
                    dv_carry[rw, :] = dv[BLOCK:]

            _for_each_group(r, per, DIL_GROUP_BWD, run)
            if others is not None:
                dq_ref[...] += dq_in[...]

    per_edge = tile // edge
    here = lambda n: jnp.minimum(n, ntile - 1)
    cur = lambda off: pl.BlockSpec((tile, LANES), lambda p, n: (here(n), off + p))
    before = lambda off: pl.BlockSpec((edge, LANES), lambda p, n: (jnp.maximum(here(n) * per_edge - 1, 0), off + p))
    lagged = pl.BlockSpec((tile, LANES), lambda p, n: (jnp.maximum(n - 1, 0), p))
    carry = pltpu.VMEM((tile, LANES), F32)
    return pl.pallas_call(
        body, name=f"dil_bwd_r{r}", grid=(pairs, ntile + 1),
        in_specs=[cur(0)] * 4 + [cur(pairs), before(pairs), cur(2 * pairs), before(2 * pairs)] +
                 ([] if others is None else [cur(0), lagged, lagged]),
        out_specs=[cur(0), lagged, lagged],
        out_shape=[jax.ShapeDtypeStruct((seq, W_DIL), F32)] * 3,
        scratch_shapes=[carry, carry] + _class_scratches(r, (tile, tile, tile, tile, tile, edge, tile, edge)),
        compiler_params=_params(48),
    )(qkn, d_od, lse, delta, qkn, qkn, qkv_raw, qkv_raw, *(others or ()))


def _assemble_dproj(d_qa, d_ka, d_va, d_gates, d_qd, d_kd, d_vd, qk_raw, gains, tables, bd):
    seq = qk_raw.shape[0]
    tm = min(256, seq)
    chunks = W_DIL // (2 * LANES)

    def body(dqa, dka0, dka1, dva0, dva1, dg, dqd, dkd, dvd, x_ref, g_ref, c_ref, sn_ref, sp_ref, bd_ref, dp_ref, gpart_ref):
        @pl.when(pl.program_id(0) == 0)
        def _():
            gpart_ref[...] = jnp.zeros_like(gpart_ref)

        def put(first_col, v):
            dp_ref[:, first_col:first_col + v.shape[1]] = v.astype(BF16)

        put(0, dqa[...])
        put(W_SB, dka0[...])
        put(W_SB + LANES, dka1[...])
        put(2 * W_SB, dva0[...])
        put(2 * W_SB + LANES, dva1[...])
        put(3 * W_SB, dg[:, :W_SB])
        put(4 * W_SB + 2 * W_DIL, dvd[...])
        put(4 * W_SB + 3 * W_DIL, dg[:, W_SB:])
        c, sn, sp, bdm = c_ref[...], sn_ref[...], sp_ref[...], bd_ref[...]
        for which, part in enumerate((dqd, dkd)):
            scale = QK_SCALE if which == 0 else 1.0
            for k in range(chunks):
                sl = slice(k * 2 * LANES, (k + 1) * 2 * LANES)
                dyv = part[:, sl] * scale
                dxn = _rope_t(dyv, c, sn, sp)
                x = x_ref[:, which * W_DIL + k * 2 * LANES:which * W_DIL + (k + 1) * 2 * LANES]
                rs = lax.rsqrt(_head_sums(x * x, bdm) * (1.0 / HEAD_DIM) + EPS)
                xhat = x * rs
                gpart_ref[which] += jnp.sum((dxn * xhat).reshape(tm // 8, 8, 2 * LANES), axis=0)
                dxhat = dxn * g_ref[which]
                mean = _head_sums(dxhat * xhat, bdm) * (1.0 / HEAD_DIM)
                put(4 * W_SB + which * W_DIL + k * 2 * LANES, rs * (dxhat - xhat * mean))

    row = lambda w: pl.BlockSpec((tm, w), lambda i: (i, 0))
    pair = lambda p: pl.BlockSpec((None, tm, LANES), lambda i: (p, i, 0))
    const = lambda shape: pl.BlockSpec(shape, lambda i: tuple(0 for _ in shape))
    return pl.pallas_call(
        body, name="assemble_dproj", grid=(seq // tm,),
        in_specs=[row(W_SB), pair(0), pair(1), pair(0), pair(1), row(MIX)] + [row(W_DIL)] * 3 +
                 [row(2 * W_DIL), const((2, 1, 2 * LANES)), row(2 * LANES), row(2 * LANES), row(2 * LANES),
                  const((2 * LANES, 2 * LANES))],
        out_specs=[row(IN_COLS), const((2, 8, 2 * LANES))],
        out_shape=[jax.ShapeDtypeStruct((seq, IN_COLS), BF16), jax.ShapeDtypeStruct((2, 8, 2 * LANES), F32)],
        compiler_params=_params(48),
    )(d_qa, d_ka, d_ka, d_va, d_va, d_gates, d_qd, d_kd, d_vd, qk_raw, gains, *tables, bd)


def _dw_in(h, dproj, dwout_l, rout, layer):
    seq, d = h.shape
    tm = min(2048, seq)
    steps = seq // tm

    def body(h_ref, dp_ref, dwout_ref, rout_in, dw_ref, rout_ref, *sems):
        del rout_in
        n, i = pl.program_id(0), pl.program_id(1)
        copies = lambda: _to_every_device(lambda to: (dwout_ref.at[to],), lambda me: (rout_ref.at[me, layer],), sems)

        @pl.when(jnp.logical_and(n == 0, i == 0))
        def _():
            for cp in copies():
                cp.start()

        @pl.when(i == 0)
        def _():
            dw_ref[...] = jnp.zeros_like(dw_ref)

        dw_ref[...] += _dot_tn(h_ref[...], dp_ref[...])

        @pl.when(jnp.logical_and(n == N_DEV - 1, i == steps - 1))
        def _():
            for cp in copies():
                cp.wait()

    any_spec = pl.BlockSpec(memory_space=pl.ANY)
    return pl.pallas_call(
        body, name="dw_in_exchange", grid=(N_DEV, steps),
        in_specs=[pl.BlockSpec((tm, d), lambda n, i: (i, 0)), pl.BlockSpec((tm, COLS_PER_DEV), lambda n, i: (i, n)),
                  any_spec, any_spec],
        out_specs=[pl.BlockSpec((None, d, COLS_PER_DEV), lambda n, i: (n, 0, 0)), any_spec],
        out_shape=[jax.ShapeDtypeStruct((N_DEV, d, COLS_PER_DEV), F32), jax.ShapeDtypeStruct(rout.shape, F32)],
        scratch_shapes=_copy_sems(1), input_output_aliases={3: 1},
        compiler_params=_params(40),
    )(h, dproj, dwout_l, rout)


def _dx_norm(dproj, w_l, x, g, dx_next, dwin_l, rin, layer):
    seq, d = x.shape
    tm = min(256, seq)
    steps = seq // tm

    def body(dp_ref, w_ref, x_ref, g_ref, dn_ref, dwin_ref, rin_in, dx_ref, gpart_ref, rin_ref, *sems):
        del rin_in
        copies = lambda: _to_every_device(lambda to: (dwin_ref.at[to],), lambda me: (rin_ref.at[me, layer],), sems)

        @pl.when(pl.program_id(0) == 0)
        def _():
            gpart_ref[...] = jnp.zeros_like(gpart_ref)
            for cp in copies():
                cp.start()

        dh = jnp.zeros((tm, d), F32)
        for n in range(N_DEV):
            dh = dh + _dot_nt(dp_ref[:, n * COLS_PER_DEV:(n + 1) * COLS_PER_DEV], w_ref[n])
        xf = x_ref[...]
        rs = lax.rsqrt(jnp.mean(xf * xf, axis=-1, keepdims=True) + EPS)
        xhat = xf * rs
        gpart_ref[...] += jnp.sum((dh * xhat).reshape(tm // 8, 8, d), axis=0)
        dxhat = dh * g_ref[...]
        mean = jnp.mean(dxhat * xhat, axis=-1, keepdims=True)
        dx_ref[...] = rs * (dxhat - xhat * mean) + dn_ref[...]

        @pl.when(pl.program_id(0) == steps - 1)
        def _():
            for cp in copies():
                cp.wait()

    row = lambda w: pl.BlockSpec((tm, w), lambda i: (i, 0))
    any_spec = pl.BlockSpec(memory_space=pl.ANY)
    return pl.pallas_call(
        body, name="dx_norm_exchange", grid=(steps,),
        in_specs=[row(IN_COLS), pl.BlockSpec((N_DEV, d, COLS_PER_DEV), lambda i: (0, 0, 0)), row(d),
                  pl.BlockSpec((1, d), lambda i: (0, 0)), row(d), any_spec, any_spec],
        out_specs=[row(d), pl.BlockSpec((8, d), lambda i: (0, 0)), any_spec],
        out_shape=[jax.ShapeDtypeStruct((seq, d), F32), jax.ShapeDtypeStruct((8, d), F32),
                   jax.ShapeDtypeStruct(rin.shape, F32)],
        scratch_shapes=_copy_sems(1), input_output_aliases={6: 2},
        compiler_params=_params(48),
    )(dproj, w_l, x, g, dx_next, dwin_l, rin)


def _exchange_small(small):
    def body(small_ref, out_ref, *sems):
        copies = _to_every_device(lambda to: (small_ref,), lambda me: (out_ref.at[me],), sems)
        for cp in copies:
            cp.start()
        for cp in copies:
            cp.wait()

    vmem = pl.BlockSpec(memory_space=pltpu.VMEM)
    return pl.pallas_call(
        body, name="exchange_small", in_specs=[vmem], out_specs=vmem,
        out_shape=jax.ShapeDtypeStruct((N_DEV,) + small.shape, F32), scratch_shapes=_copy_sems(1),
    )(small)


def _adamw_math(g, w, m, v):
    m = ADAM_B1 * m + (1.0 - ADAM_B1) * g
    v = ADAM_B2 * v + (1.0 - ADAM_B2) * (g * g)
    m_hat = m / (1.0 - ADAM_B1 ** ADAM_STEP)
    v_hat = v / (1.0 - ADAM_B2 ** ADAM_STEP)
    delta = -ADAM_LR * (m_hat / (jnp.sqrt(v_hat) + ADAM_EPS) + ADAM_WD * w)
    return delta, m, v


def _adamw(parts, w, m, v, name):
    nl, r, c = w.shape
    tr = min(r, (256 * 512) // c)

    def body(p_ref, w_ref, m_ref, v_ref, g_ref, d_ref, nm_ref, nv_ref):
        g = p_ref[0]
        for s in range(1, N_DEV):
            g = g + p_ref[s]
        g_ref[...] = g
        d_ref[...], nm_ref[...], nv_ref[...] = _adamw_math(g, w_ref[...], m_ref[...], v_ref[...])

    blk = pl.BlockSpec((None, tr, c), lambda l, i: (l, i, 0))
    return pl.pallas_call(
        body, name=name, grid=(nl, r // tr),
        in_specs=[pl.BlockSpec((N_DEV, None, tr, c), lambda l, i: (0, l, i, 0)), blk, blk, blk],
        out_specs=[blk] * 4, out_shape=[jax.ShapeDtypeStruct(w.shape, F32)] * 4,
        compiler_params=_params(32),
    )(parts, w, m, v)


def _adamw_small(parts, w, m, v):
    def body(p_ref, w_ref, m_ref, v_ref, g_ref, d_ref, nm_ref, nv_ref):
        g = p_ref[0]
        for s in range(1, N_DEV):
            g = g + p_ref[s]
        g_ref[...] = g
        d_ref[...], nm_ref[...], nv_ref[...] = _adamw_math(g, w_ref[...], m_ref[...], v_ref[...])

    vmem = pl.BlockSpec(memory_space=pltpu.VMEM)
    return pl.pallas_call(
        body, name="adamw_small", in_specs=[vmem] * 4, out_specs=[vmem] * 4,
        out_shape=[jax.ShapeDtypeStruct(w.shape, F32)] * 4,
    )(parts, w, m, v)


def _pack_small(a, b, c):
    pad = jnp.zeros((a.shape[0], SMALL_W - a.shape[1] - b.shape[1] - c.shape[1]), F32)
    return jnp.concatenate([a, b, c, pad], axis=1)


def _unpack_small(t, d):
    return t[:, :d], t[:, d:d + HEAD_DIM], t[:, d + HEAD_DIM:d + 2 * HEAD_DIM]


def kernel(x, norm_g, w_in, q_norm_g, k_norm_g, w_out, loss_target, m_norm_g, m_w_in, m_q_norm_g, m_k_norm_g, m_w_out,
           v_norm_g, v_w_in, v_q_norm_g, v_k_norm_g, v_w_out):
    depth, d, _ = w_in.shape
    seq = x.shape[1]
    tri_a, tri_b, bd = _tri_constants()
    tables = _rope_tables(seq)
    rep = (2 * LANES) // HEAD_DIM

    win_bf, wout_bf = _cast_bf16(w_in, "cast_w_in"), _cast_bf16(w_out, "cast_w_out")
    win_l, wout_l = _gather_weights(win_bf, wout_bf, 0)
    saved = []
    xl = x.reshape(seq, d)
    for layer in range(depth):
        gains = jnp.stack([jnp.tile(q_norm_g[layer], rep), jnp.tile(k_norm_g[layer], rep)])[:, None, :]
        if layer + 1 < depth:
            proj_bf, gates, qk_raw, h, qkn, win_next, wout_next = _norm_proj(
                xl, norm_g[layer][None, :], win_l, gains, tables, bd, prefetch=(win_bf, wout_bf, layer + 1))
        else:
            proj_bf, gates, qk_raw, h, qkn = _norm_proj(xl, norm_g[layer][None, :], win_l, gains, tables, bd)
            win_next = wout_next = None
        oa = _sb_fwd(proj_bf, tri_a)
        parts = [_dil_fwd(qkn, qk_raw, r) for _, r in DIL_PATTERNS[:-1]]
        od, lse = _dil_fwd(qkn, qk_raw, DIL_PATTERNS[-1][1], others=parts)
        x_next, cat = _out_proj(xl, oa, od, gates, wout_l)
        saved.append((xl, gains, proj_bf, gates, qk_raw, h, qkn, oa, cat, od, lse, win_l, wout_l))
        xl, win_l, wout_l = x_next, win_next, wout_next

    loss_part, dx = _loss_head(xl, loss_target.reshape(seq, d))
    loss = lax.psum(jnp.sum(loss_part), ("x", "y", "c"))

    rin = jnp.zeros((N_DEV, depth, d, COLS_PER_DEV), F32)
    rout = jnp.zeros((N_DEV, depth, ROWS_PER_DEV, d), F32)
    g_norm, g_q, g_k = [None] * depth, [None] * depth, [None] * depth
    for layer in reversed(range(depth)):
        xl, gains, proj_bf, gates, qk_raw, h, qkn, oa, cat, od, lse, win_l, wout_l = saved[layer]
        d_oa, d_od, delta, d_gates, dwout_l = _out_proj_bwd(dx, wout_l, cat, gates, oa, od, bd)
        d_qa, d_ka, d_va = _sb_bwd(proj_bf, d_oa, oa, tri_a, tri_b)
        d_dil = None
        for _, r in DIL_PATTERNS:
            d_dil = _dil_bwd(qkn, qk_raw, d_od, lse, delta, r, others=d_dil)
        dproj, gqk = _assemble_dproj(d_qa, d_ka, d_va, d_gates, *d_dil, qk_raw, gains, tables, bd)
        dwin_l, rout = _dw_in(h, dproj, dwout_l, rout, layer)
        dx, gn, rin = _dx_norm(dproj, win_l, xl, norm_g[layer][None, :], dx, dwin_l, rin, layer)
        g_norm[layer] = jnp.sum(gn, axis=0)
        gqk = jnp.sum(gqk, axis=1).reshape(2, rep, HEAD_DIM).sum(axis=1)
        g_q[layer], g_k[layer] = gqk[0], gqk[1]
    rsmall = _exchange_small(_pack_small(jnp.stack(g_norm), jnp.stack(g_q), jnp.stack(g_k)))

    g_in, d_in, nm_in, nv_in = _adamw(rin, w_in, m_w_in, v_w_in, "adamw_w_in")
    g_out, d_out, nm_out, nv_out = _adamw(rout, w_out, m_w_out, v_w_out, "adamw_w_out")
    small_out = _adamw_small(rsmall, _pack_small(norm_g, q_norm_g, k_norm_g), _pack_small(m_norm_g, m_q_norm_g, m_k_norm_g),
                             _pack_small(v_norm_g, v_q_norm_g, v_k_norm_g))
    (g_n, g_qn, g_kn), (d_n, d_qn, d_kn), (nm_n, nm_qn, nm_kn), (nv_n, nv_qn, nv_kn) = (_unpack_small(t, d) for t in small_out)

    return (loss, dx.reshape(x.shape), g_n, g_in, g_qn, g_kn, g_out, d_n, d_in, d_qn, d_kn, d_out,
            nm_n, nm_in, nm_qn, nm_kn, nm_out, nv_n, nv_in, nv_qn, nv_kn, nv_out)
```

```python
import math

import jax
import jax.numpy as jnp
from jax import lax
from jax.experimental import pallas as pl
from jax.experimental.pallas import tpu as pltpu

F32 = jnp.float32
BF16 = jnp.bfloat16

EPS = 1e-6
HEAD_DIM = 64
BLOCK = 128
LANES = 128
W_SB = 256
W_DIL = 768
MIX = W_SB + W_DIL
IN_COLS = 4 * W_SB + 4 * W_DIL
N_DEV = 8
COLS_PER_DEV = IN_COLS // N_DEV
ROWS_PER_DEV = MIX // N_DEV
QK_SCALE = 1.0 / math.sqrt(HEAD_DIM)
DIL_PATTERNS = ((128, 1), (512, 4), (2048, 16))
DIL_TILE = 2048
DIL_GROUP_FWD = 4
DIL_GROUP_BWD = 4
SPLIT = 4
ROPE_THETA = 500000.0
ROPE_DIM = HEAD_DIM // 4
ROPE_HALF = ROPE_DIM // 2
DEAD_LOG = -110.0
SB_PEEL = 3
SB_QBLOCKS = 2
SB_SUM_TERMS = 2
NEG_BIG = -1e30

ADAM_LR = 0.001
ADAM_B1 = 0.9
ADAM_B2 = 0.999
ADAM_EPS = 1e-08
ADAM_WD = 0.01
ADAM_STEP = 10

SMALL_W = 1280
MESH_ID = pl.DeviceIdType.MESH
MIB = 1 << 20


def _params(vmem_mib):
    return pltpu.CompilerParams(vmem_limit_bytes=vmem_mib * MIB)


def _dot(a, b):
    return jnp.dot(a, b, preferred_element_type=F32)


def _dot_nt(a, b):
    return lax.dot_general(a, b, (((1,), (1,)), ((), ())), preferred_element_type=F32)


def _dot_tn(a, b):
    return lax.dot_general(a, b, (((0,), (0,)), ((), ())), preferred_element_type=F32)


def _dot_exact(x, m01, terms=3):
    out = 0.0
    for _ in range(terms):
        part = x.astype(BF16)
        out = out + _dot(part, m01)
        x = x - part.astype(F32)
    return out


def _lane_lo(shape):
    return lax.broadcasted_iota(jnp.int32, shape, 1) < HEAD_DIM


def _tri_constants():
    j = jnp.arange(BLOCK)
    ones = jnp.ones((BLOCK, BLOCK), F32)
    excl = (j[:, None] > j[None, :]).astype(F32)
    incl = (j[:, None] >= j[None, :]).astype(F32)
    tri_a = jnp.concatenate([excl, ones], axis=1).astype(BF16)
    tri_b = jnp.concatenate([incl, ones], axis=1).astype(BF16)
    d = jnp.arange(2 * LANES)
    bd = (d[:, None] // HEAD_DIM == d[None, :] // HEAD_DIM).astype(BF16)
    return tri_a, tri_b, bd


def _rope_tables(seq):
    d = jnp.arange(2 * LANES) % HEAD_DIM
    inv_freq = 1.0 / (ROPE_THETA ** ((d % ROPE_HALF).astype(F32) * 2.0 / ROPE_DIM))
    ang = jnp.arange(seq).astype(F32)[:, None] * inv_freq[None, :]
    cos, sin = jnp.cos(ang), jnp.sin(ang)
    c = jnp.where(d < ROPE_DIM, cos, 1.0)
    s_next = jnp.where(d < ROPE_HALF, -sin, 0.0)
    s_prev = jnp.where((d >= ROPE_HALF) & (d < ROPE_DIM), sin, 0.0)
    return c, s_next, s_prev


def _roll_lanes(x, shift):
    return jnp.concatenate([pltpu.roll(x[:, :LANES], shift, 1), pltpu.roll(x[:, LANES:], shift, 1)], axis=1)


def _rope(x, c, s_next, s_prev):
    return x * c + _roll_lanes(x, LANES - ROPE_HALF) * s_next + _roll_lanes(x, ROPE_HALF) * s_prev


def _rope_t(dy, c, s_next, s_prev):
    return dy * c + _roll_lanes(dy * s_next, ROPE_HALF) + _roll_lanes(dy * s_prev, LANES - ROPE_HALF)


def _cast_bf16(w, name):
    nl, r, c = w.shape

    def body(w_ref, o_ref):
        o_ref[...] = w_ref[...].astype(BF16)

    return pl.pallas_call(
        body, name=name, grid=(nl,),
        in_specs=[pl.BlockSpec((None, r, c), lambda l: (l, 0, 0))],
        out_specs=pl.BlockSpec((None, r, c), lambda l: (l, 0, 0)),
        out_shape=jax.ShapeDtypeStruct(w.shape, BF16),
        compiler_params=_params(24),
    )(w)


def _flips():
    return [(dx, dy, dc) for dx in (0, 1) for dy in (0, 1) for dc in (0, 1) if (dx, dy, dc) != (0, 0, 0)]


def _place():
    x, y, c = lax.axis_index("x"), lax.axis_index("y"), lax.axis_index("c")
    return x, y, c, 4 * x + 2 * y + c


def _peer(x, y, c, flip):
    dx, dy, dc = flip
    return (1 - x if dx else x, 1 - y if dy else y, 1 - c if dc else c)


def _to_every_device(srcs_for, dsts_at, sems):
    send_sems, recv_sems, local_sems = sems
    x, y, c, me = _place()
    dsts = dsts_at(me)
    n = len(dsts)
    copies = [pltpu.make_async_copy(src, dst, local_sems.at[a]) for a, (src, dst) in enumerate(zip(srcs_for(me), dsts))]
    for k, flip in enumerate(_flips()):
        px, py, pc = _peer(x, y, c, flip)
        for a, (src, dst) in enumerate(zip(srcs_for(4 * px + 2 * py + pc), dsts)):
            copies.append(pltpu.make_async_remote_copy(
                src_ref=src, dst_ref=dst, send_sem=send_sems.at[n * k + a], recv_sem=recv_sems.at[n * k + a],
                device_id=(px, py, pc), device_id_type=MESH_ID))
    return copies


def _copy_sems(n):
    remote = n * (N_DEV - 1)
    return [pltpu.SemaphoreType.DMA((remote,)), pltpu.SemaphoreType.DMA((remote,)), pltpu.SemaphoreType.DMA((n,))]


def _weight_copies(win_ref, wout_ref, layer, oin_ref, oout_ref, sems):
    return _to_every_device(lambda to: (win_ref.at[layer], wout_ref.at[layer]), lambda me: (oin_ref.at[me], oout_ref.at[me]), sems)


def _gathered_shapes(win_bf, wout_bf):
    return [jax.ShapeDtypeStruct((N_DEV,) + win_bf.shape[1:], BF16), jax.ShapeDtypeStruct((N_DEV,) + wout_bf.shape[1:], BF16)]


def _gather_weights(win_bf, wout_bf, layer):
    def body(win_ref, wout_ref, oin_ref, oout_ref, *sems):
        copies = _weight_copies(win_ref, wout_ref, layer, oin_ref, oout_ref, sems)
        for cp in copies:
            cp.start()
        for cp in copies:
            cp.wait()

    any_spec = pl.BlockSpec(memory_space=pl.ANY)
    return pl.pallas_call(
        body, name="gather_weights", in_specs=[any_spec, any_spec], out_specs=[any_spec, any_spec],
        out_shape=_gathered_shapes(win_bf, wout_bf), scratch_shapes=_copy_sems(2),
    )(win_bf, wout_bf)


_QK_BLOCKS = (2, 3, 4)
_F32_ROUTES = {1: ((256, 256, 0, 0),), 2: ((0, 512, 1, 0),), 3: ((0, 512, 1, 512),), 4: ((0, 512, 1, 1024),),
               5: ((0, 512, 1, 1536),), 6: ((0, 256, 1, 2048), (256, 256, 0, 256)), 7: ((0, 512, 0, 512),)}


def _norm_proj(x, g, w_l, gains, tables, bd, prefetch=None):
    seq, d = x.shape
    tm = min(256, seq)
    steps = seq // tm

    def body(x_ref, g_ref, w_ref, gains_ref, c_ref, sn_ref, sp_ref, bd_ref, *rest):
        if prefetch is None:
            pbf_ref, gates_ref, qk_ref, h_ref, qkn_ref = rest
        else:
            win_ref, wout_ref, pbf_ref, gates_ref, qk_ref, h_ref, qkn_ref, oin_ref, oout_ref, *sems = rest
            copies = lambda: _weight_copies(win_ref, wout_ref, prefetch[2], oin_ref, oout_ref, sems)

            @pl.when(pl.program_id(0) == 0)
            def _():
                for cp in copies():
                    cp.start()

        xf = x_ref[...]
        rs = lax.rsqrt(jnp.mean(xf * xf, axis=-1, keepdims=True) + EPS)
        h = (xf * rs * g_ref[...]).astype(BF16)
        h_ref[...] = h
        targets = (gates_ref, qk_ref)
        for n in range(N_DEV):
            acc = _dot(h, w_ref[n])
            pbf_ref[:, n * COLS_PER_DEV:(n + 1) * COLS_PER_DEV] = acc.astype(BF16)
            for lo, width, tgt, dst in _F32_ROUTES.get(n, ()):
                targets[tgt][:, dst:dst + width] = acc[:, lo:lo + width]
            for half in range(2) if n in _QK_BLOCKS else ():
                col = (_QK_BLOCKS.index(n) * 2 + half) * 2 * LANES
                is_q = col < W_DIL
                xq = acc[:, half * 2 * LANES:(half + 1) * 2 * LANES]
                rsq = lax.rsqrt(_head_sums(xq * xq, bd_ref[...]) * (1.0 / HEAD_DIM) + EPS)
                y = _rope(xq * rsq * gains_ref[0 if is_q else 1], c_ref[...], sn_ref[...], sp_ref[...])
                qkn_ref[:, col:col + 2 * LANES] = y * QK_SCALE if is_q else y

        if prefetch is not None:
            @pl.when(pl.program_id(0) == steps - 1)
            def _():
                for cp in copies():
                    cp.wait()

    row = lambda w: pl.BlockSpec((tm, w), lambda i: (i, 0))
    any_spec = pl.BlockSpec(memory_space=pl.ANY)
    const = lambda shape: pl.BlockSpec(shape, lambda i: tuple(0 for _ in shape))
    in_specs = [row(d), const((1, d)), const((N_DEV, d, COLS_PER_DEV)), const((2, 1, 2 * LANES)),
                row(2 * LANES), row(2 * LANES), row(2 * LANES), const((2 * LANES, 2 * LANES))]
    out_specs = [row(IN_COLS), row(MIX), row(3 * W_DIL), row(d), row(2 * W_DIL)]
    out_shape = [jax.ShapeDtypeStruct((seq, IN_COLS), BF16), jax.ShapeDtypeStruct((seq, MIX), F32),
                 jax.ShapeDtypeStruct((seq, 3 * W_DIL), F32), jax.ShapeDtypeStruct((seq, d), BF16),
                 jax.ShapeDtypeStruct((seq, 2 * W_DIL), F32)]
    operands = (x, g, w_l, gains, *tables, bd)
    if prefetch is None:
        return pl.pallas_call(body, name="norm_proj", grid=(steps,), in_specs=in_specs, out_specs=out_specs,
                              out_shape=out_shape, compiler_params=_params(48))(*operands)
    return pl.pallas_call(
        body, name="norm_proj_gather", grid=(steps,), in_specs=in_specs + [any_spec, any_spec],
        out_specs=out_specs + [any_spec, any_spec], out_shape=out_shape + _gathered_shapes(*prefetch[:2]),
        scratch_shapes=_copy_sems(2), compiler_params=_params(48),
    )(*operands, *prefetch[:2])


def _head_sums(v, bd):
    hi = v.astype(BF16)
    lo = (v - hi.astype(F32)).astype(BF16)
    return _dot(hi, bd) + _dot(lo, bd)


def _stack_heads(x, lo):
    return jnp.concatenate([jnp.where(lo, x, 0.0), jnp.where(lo, 0.0, x)], axis=0)


def _unstack_heads(y, lo):
    return jnp.where(lo, y[:BLOCK], y[BLOCK:])


def _stacked_causal():
    row = lax.broadcasted_iota(jnp.int32, (2 * BLOCK, BLOCK), 0) & (BLOCK - 1)
    return lax.broadcasted_iota(jnp.int32, (2 * BLOCK, BLOCK), 1) < row


def _keep(x, *conds):
    for cond in conds:
        if cond is not None:
            x = jnp.where(cond, x, 0.0)
    return x


def _sb_weights(chains, tri):
    zs = [[_dot_nt(qs, kb) for kb in kbs] for qs, kbs, _, _, _ in chains]
    lss = [[jnp.minimum(z, 0.0) - jnp.log1p(jnp.exp(-jnp.abs(z))) for z in zc] for zc in zs]
    cts = [[_dot_exact(_keep(ls - z, mask, live), tri, SB_SUM_TERMS) for ls, z, mask, live in zip(lsc, zc, masks, lives)]
           for lsc, zc, (_, _, _, masks, lives) in zip(lss, zs, chains)]
    out = []
    for lsc, ctc, (_, _, r, masks, lives) in zip(lss, cts, chains):
        weights = []
        for ls, ct, mask, live in zip(lsc, ctc, masks, lives):
            weights.append(_keep(jnp.exp(ls + ct[:, :BLOCK] + r), mask, live))
            r = r + ct[:, BLOCK:]
        out.append((lsc, weights, r))
    return out


def _sb_peel(i, causal):
    return ([jnp.maximum(i - k, 0) for k in range(SB_PEEL)], [causal] + [None] * (SB_PEEL - 1),
            [None] + [i >= k for k in range(1, SB_PEEL)])


def _block_rows(j):
    return pl.ds(pl.multiple_of(j * BLOCK, BLOCK), BLOCK)


def _sb_fwd(proj_bf, tri_a):
    seq = proj_bf.shape[0]
    pairs = W_SB // LANES

    def body(q_ref, k_ref, v_ref, tri_ref, o_ref, r_ref, acc_ref):
        t = pl.program_id(1)
        lo = _lane_lo((BLOCK, LANES))
        causal = _stacked_causal()
        tri = tri_ref[...]
        qblocks = [t * SB_QBLOCKS + c for c in range(SB_QBLOCKS)]
        qss = [_stack_heads(q_ref[c * BLOCK:(c + 1) * BLOCK, :].astype(F32) * QK_SCALE, lo).astype(BF16)
               for c in range(SB_QBLOCKS)]

        def blocks(specs):
            rows = [[_block_rows(j) for j in js] for _, js, _, _, _ in specs]
            res = _sb_weights([(qs, [k_ref[rw, :] for rw in rws], r, masks, lives)
                               for (qs, _, r, masks, lives), rws in zip(specs, rows)], tri)
            outs = []
            for (_, weights, r), rws in zip(res, rows):
                out = 0.0
                for a, rw in zip(weights, rws):
                    a_hi = a.astype(BF16)
                    a_lo = (a - a_hi.astype(F32)).astype(BF16)
                    vb = v_ref[rw, :]
                    out = out + _dot(a_hi, vb) + _dot(a_lo, vb)
                outs.append((out, r))
            return outs

        def peel(qs, i):
            js, masks, lives = _sb_peel(i, causal)
            return qs, js, jnp.zeros((2 * BLOCK, LANES), F32), masks, lives

        for c, (out, r) in enumerate(blocks([peel(qs, i) for qs, i in zip(qss, qblocks)])):
            acc_ref[c], r_ref[c] = out, r

        for c in range(SB_QBLOCKS):
            def alive(c=c):
                return (jnp.max(r_ref[c]) > DEAD_LOG).astype(jnp.int32)

            def step(carry, c=c):
                (out, r_ref[c]), = blocks([(qss[c], [carry[0]], r_ref[c], [None], [None])])
                acc_ref[c] += out
                return carry[0] - 1, alive(c)

            lax.while_loop(lambda carry: jnp.logical_and(carry[0] >= 0, carry[1] > 0), step, (qblocks[c] - SB_PEEL, alive()))
            o_ref[c * BLOCK:(c + 1) * BLOCK, :] = _unstack_heads(acc_ref[c], lo)

    qtile = SB_QBLOCKS * BLOCK
    state = pltpu.VMEM((SB_QBLOCKS, 2 * BLOCK, LANES), F32)
    return pl.pallas_call(
        body, name="sb_fwd", grid=(pairs, seq // qtile),
        in_specs=[pl.BlockSpec((qtile, LANES), lambda p, t: (t, p)),
                  pl.BlockSpec((seq, LANES), lambda p, t: (0, pairs + p)),
                  pl.BlockSpec((seq, LANES), lambda p, t: (0, 2 * pairs + p)),
                  pl.BlockSpec((BLOCK, 2 * BLOCK), lambda p, t: (0, 0))],
        out_specs=pl.BlockSpec((qtile, LANES), lambda p, t: (t, p)),
        out_shape=jax.ShapeDtypeStruct((seq, W_SB), F32),
        scratch_shapes=[state, state],
        compiler_params=_params(40),
    )(proj_bf, proj_bf, proj_bf, tri_a)


def _class_rows(cls, first, count, r):
    c = cls[0] + SPLIT * cls[1] if isinstance(cls, tuple) else cls
    start = c + first * r
    return pl.ds(start, count) if r == 1 else pl.ds(start, count, stride=r)


def _class_reader(r, refs, scratches):
    if r != SPLIT * SPLIT:
        return lambda ref, cls, first, count: ref[_class_rows(cls, first, count, r), :]
    slabs = {}
    for ref, scr in zip(refs, scratches):
        for lo in range(SPLIT):
            scr[lo] = ref[pl.ds(lo, ref.shape[0] // SPLIT, stride=SPLIT), :]
        slabs[id(ref)] = scr

    def take(ref, cls, first, count):
        lo, hi = cls
        return slabs[id(ref)][lo, pl.ds(hi + SPLIT * first, count, stride=SPLIT), :]

    return take


def _class_scratches(r, shapes):
    return [pltpu.VMEM((SPLIT, rows // SPLIT, LANES), F32) for rows in shapes] if r == SPLIT * SPLIT else []


def _dil_tiling(seq, r):
    tile = min(DIL_TILE, seq)
    edge = BLOCK * r
    return tile, edge, tile // edge


def _band_mask(n_rows, n_keys):
    row = lax.broadcasted_iota(jnp.int32, (n_rows, n_keys), 0) & (BLOCK - 1)
    col = lax.broadcasted_iota(jnp.int32, (n_rows, n_keys), 1)
    return jnp.logical_and(col >= row, col <= row + BLOCK), col


def _for_each_group(r, per, g, run):
    loop = lambda lo, hi, fn: lax.fori_loop(lo, hi, lambda t, carry: (fn(t), carry)[1], 0)
    if per <= g:
        n_cls = g // per
        assert g % per == 0 and r % n_cls == 0 and (r != SPLIT * SPLIT or n_cls == SPLIT)
        cls = (lambda t, u: (u, t)) if r == SPLIT * SPLIT else (lambda t, u: t * n_cls + u)
        group = lambda t: [(cls(t, u), a, a == 0) for u in range(n_cls) for a in range(per)]
        if r == n_cls:
            run(group(0))
        else:
            loop(0, r // n_cls, lambda t: run(group(t)))
        return

    assert per % g == 0

    def one_class(c):
        run([(c, 0, True)] + [(c, a, False) for a in range(1, g)])
        loop(1, per // g, lambda t: run([(c, t * g + u, False) for u in range(g)]))

    if r == 1:
        one_class(0)
    else:
        loop(0, r, one_class)


def _dil_keys(blocks, take, cur_ref, before_ref):
    out = []
    for cls, a, first in blocks:
        if first:
            both = jnp.concatenate([take(before_ref, cls, 0, BLOCK), take(cur_ref, cls, 0, BLOCK)], axis=0)
        else:
            both = take(cur_ref, cls, (a - 1) * BLOCK, 2 * BLOCK)
        out.append(both.astype(BF16))
    return out


def _dil_fwd(qkn, qkv_raw, r, others=()):
    seq = qkn.shape[0]
    tile, edge, per = _dil_tiling(seq, r)
    pairs = W_DIL // LANES

    def body(q_ref, kc_ref, kp_ref, vc_ref, vp_ref, *rest):
        other_refs, (o_ref, ld_ref, *scratches) = rest[:2 * len(others)], rest[2 * len(others):]
        n = pl.program_id(1)
        lo = _lane_lo((BLOCK, LANES))
        band, col = _band_mask(2 * BLOCK, 2 * BLOCK)
        first_band = jnp.logical_and(band, jnp.logical_or(col >= BLOCK, n > 0))
        take = _class_reader(r, (q_ref, kc_ref, kp_ref, vc_ref, vp_ref), scratches)

        def run(blocks):
            rows = [_class_rows(c, a * BLOCK, BLOCK, r) for c, a, _ in blocks]
            keys = _dil_keys(blocks, take, kc_ref, kp_ref)
            vals = _dil_keys(blocks, take, vc_ref, vp_ref)
            scores = [_dot_nt(_stack_heads(take(q_ref, c, a * BLOCK, BLOCK), lo).astype(BF16), kb)
                      for (c, a, _), kb in zip(blocks, keys)]
            probs, sums, lds = [], [], []
            for s, (_, _, first) in zip(scores, blocks):
                s = jnp.where(first_band if first else band, s, NEG_BIG)
                m = jnp.max(s, axis=1, keepdims=True)
                p = jnp.exp(s - m)
                l = jnp.sum(p, axis=1, keepdims=True)
                probs.append(p.astype(BF16))
                sums.append(l)
                lds.append(m + jnp.log(l))
            outs = [_dot(p, vb) / l for p, vb, l in zip(probs, vals, sums)]
            for rw, o, ld in zip(rows, outs, lds):
                o_ref[rw, :] = _unstack_heads(o, lo)
                ld_ref[rw, :] = _unstack_heads(jnp.broadcast_to(ld, (2 * BLOCK, LANES)), lo)

        _for_each_group(r, per, DIL_GROUP_FWD, run)
        if others:
            o_ref[...], ld_ref[...] = _mix_patterns([ref[...] for ref in other_refs[0::2]] + [o_ref[...]],
                                                    [ref[...] for ref in other_refs[1::2]] + [ld_ref[...]])

    per_edge = tile // edge
    cur = lambda off: pl.BlockSpec((tile, LANES), lambda p, n: (n, off + p))
    before = lambda off: pl.BlockSpec((edge, LANES), lambda p, n: (jnp.maximum(n * per_edge - 1, 0), off + p))
    return pl.pallas_call(
        body, name=f"dil_fwd_r{r}", grid=(pairs, seq // tile),
        in_specs=[cur(0), cur(pairs), before(pairs), cur(2 * pairs), before(2 * pairs)] + [cur(0)] * (2 * len(others)),
        out_specs=[cur(0), cur(0)],
        out_shape=[jax.ShapeDtypeStruct((seq, W_DIL), F32), jax.ShapeDtypeStruct((seq, W_DIL), F32)],
        scratch_shapes=_class_scratches(r, (tile, tile, edge, tile, edge)),
        compiler_params=_params(40),
    )(qkn, qkn, qkn, qkv_raw, qkv_raw, *[a for pair in others for a in pair])


def _silu_parts(g):
    sig = jax.nn.sigmoid(g)
    return g * sig, sig * (1.0 + g * (1.0 - sig))


def _mix_patterns(os_, lds):
    m = lds[0]
    for v in lds[1:]:
        m = jnp.maximum(m, v)
    es = [jnp.exp(v - m) for v in lds]
    tot = es[0]
    for e in es[1:]:
        tot = tot + e
    inv = 1.0 / tot
    od = (es[0] * inv) * os_[0]
    for e, o in zip(es[1:], os_[1:]):
        od = od + (e * inv) * o
    return od, m + jnp.log(tot)


def _out_proj(x, oa, od, gates, wout_l):
    seq, d = x.shape
    tm = min(256, seq)

    def body(x_ref, oa_ref, od_ref, g_ref, w_ref, xn_ref, cat_ref):
        silu, _ = _silu_parts(g_ref[...])
        cat_ref[:, :W_SB] = (oa_ref[...] * silu[:, :W_SB]).astype(BF16)
        cat_ref[:, W_SB:] = (od_ref[...] * silu[:, W_SB:]).astype(BF16)
        y = x_ref[...]
        for b in range(N_DEV):
            y = y + _dot(cat_ref[:, b * ROWS_PER_DEV:(b + 1) * ROWS_PER_DEV], w_ref[b])
        xn_ref[...] = y

    row = lambda w: pl.BlockSpec((tm, w), lambda i: (i, 0))
    return pl.pallas_call(
        body, name="out_proj", grid=(seq // tm,),
        in_specs=[row(d), row(W_SB), row(W_DIL), row(MIX), pl.BlockSpec((N_DEV, ROWS_PER_DEV, d), lambda i: (0, 0, 0))],
        out_specs=[row(d), row(MIX)],
        out_shape=[jax.ShapeDtypeStruct((seq, d), F32), jax.ShapeDtypeStruct((seq, MIX), BF16)],
        compiler_params=_params(40),
    )(x, oa, od, gates, wout_l)


def _loss_head(y, target):
    seq, d = y.shape
    tm = min(512, seq)

    def body(y_ref, t_ref, part_ref, dy_ref):
        @pl.when(pl.program_id(0) == 0)
        def _():
            part_ref[...] = jnp.zeros_like(part_ref)

        diff = y_ref[...] - t_ref[...]
        dy_ref[...] = diff * (1.0 / d)
        part_ref[...] += jnp.sum((diff * diff).reshape(tm // 8, 8, d), axis=0) * (0.5 / d)

    row = pl.BlockSpec((tm, d), lambda i: (i, 0))
    return pl.pallas_call(
        body, name="loss_head", grid=(seq // tm,),
        in_specs=[row, row], out_specs=[pl.BlockSpec((8, d), lambda i: (0, 0)), row],
        out_shape=[jax.ShapeDtypeStruct((8, d), F32), jax.ShapeDtypeStruct((seq, d), F32)],
        compiler_params=_params(32),
    )(y, target)


def _out_proj_bwd(dy, wout_l, cat, gates, oa, od, bd):
    seq, d = dy.shape
    tm = min(256, seq)

    def body(dy_ref, w_ref, cat_ref, g_ref, oa_ref, od_ref, bd_ref, doa_ref, dod_ref, delta_ref, dg_ref, dw_ref, dcat):
        @pl.when(pl.program_id(0) == 0)
        def _():
            dw_ref[...] = jnp.zeros_like(dw_ref)

        dyb = dy_ref[...].astype(BF16)
        dw = _dot_tn(cat_ref[...], dyb)
        for b in range(N_DEV):
            dw_ref[b] += dw[b * ROWS_PER_DEV:(b + 1) * ROWS_PER_DEV, :]
            dcat[:, b * ROWS_PER_DEV:(b + 1) * ROWS_PER_DEV] = _dot_nt(dyb, w_ref[b])
        silu, dsilu = _silu_parts(g_ref[...])
        dc = dcat[...]
        dmix = dc * silu
        oa_v, od_v = oa_ref[...], od_ref[...]
        dg_ref[:, :W_SB] = (dc[:, :W_SB] * oa_v * dsilu[:, :W_SB]).astype(BF16)
        dg_ref[:, W_SB:] = (dc[:, W_SB:] * od_v * dsilu[:, W_SB:]).astype(BF16)
        doa_ref[...] = dmix[:, :W_SB]
        dod = dmix[:, W_SB:]
        dod_ref[...] = dod
        prod = dod * od_v
        for k in range(W_DIL // (2 * LANES)):
            sl = slice(k * 2 * LANES, (k + 1) * 2 * LANES)
            delta_ref[:, sl] = _head_sums(prod[:, sl], bd_ref[...])

    row = lambda w: pl.BlockSpec((tm, w), lambda i: (i, 0))
    slab = pl.BlockSpec((N_DEV, ROWS_PER_DEV, d), lambda i: (0, 0, 0))
    return pl.pallas_call(
        body, name="out_proj_bwd", grid=(seq // tm,),
        in_specs=[row(d), slab, row(MIX), row(MIX), row(W_SB), row(W_DIL),
                  pl.BlockSpec((2 * LANES, 2 * LANES), lambda i: (0, 0))],
        out_specs=[row(W_SB), row(W_DIL), row(W_DIL), row(MIX), slab],
        out_shape=[jax.ShapeDtypeStruct((seq, W_SB), F32), jax.ShapeDtypeStruct((seq, W_DIL), F32),
                   jax.ShapeDtypeStruct((seq, W_DIL), F32), jax.ShapeDtypeStruct((seq, MIX), BF16),
                   jax.ShapeDtypeStruct((N_DEV, ROWS_PER_DEV, d), F32)],
        scratch_shapes=[pltpu.VMEM((tm, MIX), F32)],
        compiler_params=_params(48),
    )(dy, wout_l, cat, gates, oa, od, bd)


def _sb_bwd(proj_bf, d_oa, oa, tri_a, tri_b):
    seq = proj_bf.shape[0]
    pairs = W_SB // LANES
    nq = seq // BLOCK

    def body(q_ref, k_ref, v_ref, do_ref, o_ref, tria_ref, trib_ref, dq_ref, dk_hbm, dv_hbm,
             r_ref, sfx_ref, dtot_ref, dq_acc, dk_acc, dv_acc, sems):
        p, t = pl.program_id(0), pl.program_id(1)

        @pl.when(t == 0)
        def _():
            dk_acc[...] = jnp.zeros_like(dk_acc)
            dv_acc[...] = jnp.zeros_like(dv_acc)

        lo = _lane_lo((BLOCK, LANES))
        tri, trib = tria_ref[...], trib_ref[...]
        causal = _stacked_causal()
        qblocks = [t * SB_QBLOCKS + c for c in range(SB_QBLOCKS)]
        qss, doss = [], []
        for c in range(SB_QBLOCKS):
            sl = slice(c * BLOCK, (c + 1) * BLOCK)
            qss.append(_stack_heads(q_ref[sl, :].astype(F32) * QK_SCALE, lo).astype(BF16))
            doss.append(_stack_heads(do_ref[sl, :], lo).astype(BF16))
            o2 = o_ref[sl, :]
            dtot_ref[c] = _dot_exact(doss[c].astype(F32) * jnp.concatenate([o2, o2], axis=0), tri[:, BLOCK:])

        def blocks(specs):
            rows = [[_block_rows(j) for j in js] for _, js, _, _, _, _ in specs]
            kbs = [[k_ref[rw, :] for rw in rws] for rws in rows]
            dovs = [[_dot_nt(doss[c], v_ref[rw, :]) for rw in rws] for (c, *_), rws in zip(specs, rows)]
            res = _sb_weights([(qss[c], kbc, r, masks, lives) for (c, _, r, _, masks, lives), kbc in zip(specs, kbs)], tri)
            pws = [[a * dov for a, dov in zip(weights, dovc)] for (_, weights, _), dovc in zip(res, dovs)]
            cps = [[_dot_exact(pw, trib, SB_SUM_TERMS) for pw in pwc] for pwc in pws]
            dzs, sfxs = [], []
            for (c, _, _, sfx, masks, lives), (lsc, _, _), pwc, cpc in zip(specs, res, pws, cps):
                dzc = []
                for ls, pw, cp, mask, live in zip(lsc, pwc, cpc, masks, lives):
                    beta = jnp.exp(ls)
                    before = dtot_ref[c] - sfx - cp[:, :BLOCK]
                    dzc.append(_keep(pw * (1.0 - beta) - before * beta, mask, live).astype(BF16))
                    sfx = sfx + cp[:, BLOCK:]
                dzs.append(dzc)
                sfxs.append(sfx)
            dqs = []
            for dzc, kbc in zip(dzs, kbs):
                dq = 0.0
                for dzb, kb in zip(dzc, kbc):
                    dq = dq + _dot(dzb, kb)
                dqs.append(dq)
            for (c, *_), dzc, (_, weights, _), rws in zip(specs, dzs, res, rows):
                for dzb, a, rw in zip(dzc, weights, rws):
                    dk_acc[rw, :] += _dot_tn(dzb, qss[c])
                    dv_acc[rw, :] += _dot_tn(a.astype(BF16), doss[c])
            return [(dq, r, sfx) for dq, (_, _, r), sfx in zip(dqs, res, sfxs)]

        def peel(c, i):
            js, masks, lives = _sb_peel(i, causal)
            zero = jnp.zeros((2 * BLOCK, LANES), F32)
            return c, js, zero, zero, masks, lives

        for c, (dq, r, sfx) in enumerate(blocks([peel(c, i) for c, i in enumerate(qblocks)])):
            dq_acc[c], r_ref[c], sfx_ref[c] = dq, r, sfx

        for c in range(SB_QBLOCKS):
            def alive(c=c):
                return (jnp.max(r_ref[c]) > DEAD_LOG).astype(jnp.int32)

            def step(carry, c=c):
                (dq, r_ref[c], sfx_ref[c]), = blocks([(c, [carry[0]], r_ref[c], sfx_ref[c], [None], [None])])
                dq_acc[c] += dq
                return carry[0] - 1, alive(c)

            lax.while_loop(lambda carry: jnp.logical_and(carry[0] >= 0, carry[1] > 0), step, (qblocks[c] - SB_PEEL, alive()))
            dq_ref[c * BLOCK:(c + 1) * BLOCK, :] = _unstack_heads(dq_acc[c], lo) * QK_SCALE

        @pl.when(t == nq // SB_QBLOCKS - 1)
        def _():
            outs = [pltpu.make_async_copy(dk_acc, dk_hbm.at[p], sems.at[0]),
                    pltpu.make_async_copy(dv_acc, dv_hbm.at[p], sems.at[1])]
            for cp in outs:
                cp.start()
            for cp in outs:
                cp.wait()

    qtile = SB_QBLOCKS * BLOCK
    blk = pl.BlockSpec((qtile, LANES), lambda p, t: (t, p))
    const = pl.BlockSpec((BLOCK, 2 * BLOCK), lambda p, t: (0, 0))
    any_spec = pl.BlockSpec(memory_space=pl.ANY)
    state = pltpu.VMEM((SB_QBLOCKS, 2 * BLOCK, LANES), F32)
    return pl.pallas_call(
        body, name="sb_bwd", grid=(pairs, seq // qtile),
        in_specs=[blk, pl.BlockSpec((seq, LANES), lambda p, t: (0, pairs + p)),
                  pl.BlockSpec((seq, LANES), lambda p, t: (0, 2 * pairs + p)), blk, blk, const, const],
        out_specs=[blk, any_spec, any_spec],
        out_shape=[jax.ShapeDtypeStruct((seq, W_SB), F32), jax.ShapeDtypeStruct((pairs, seq, LANES), F32),
                   jax.ShapeDtypeStruct((pairs, seq, LANES), F32)],
        scratch_shapes=[state, state, state, state, pltpu.VMEM((seq, LANES), F32), pltpu.VMEM((seq, LANES), F32),
                        pltpu.SemaphoreType.DMA((2,))],
        compiler_params=_params(56),
    )(proj_bf, proj_bf, proj_bf, d_oa, oa, tri_a, tri_b)


def _dil_bwd(qkn, qkv_raw, d_od, lse, delta, r, others=None):
    seq = qkn.shape[0]
    tile, edge, per = _dil_tiling(seq, r)
    ntile = seq // tile
    pairs = W_DIL // LANES

    def body(qc, doc, lsc, dlc, kc, kp, vc, vp, *rest):
        if others is None:
            dq_ref, dk_ref, dv_ref, dk_carry, dv_carry, *scratches = rest
        else:
            dq_in, dk_in, dv_in, dq_ref, dk_ref, dv_ref, dk_carry, dv_carry, *scratches = rest
        n = pl.program_id(1)

        @pl.when(n == 0)
        def _():
            dk_carry[...] = jnp.zeros_like(dk_carry)
            dv_carry[...] = jnp.zeros_like(dv_carry)

        if others is None:
            dk_ref[...] = dk_carry[...]
            dv_ref[...] = dv_carry[...]
        else:
            dk_ref[...] = dk_carry[...] + dk_in[...]
            dv_ref[...] = dv_carry[...] + dv_in[...]

        @pl.when(n < ntile)
        def _():
            lo = _lane_lo((BLOCK, LANES))
            band, col = _band_mask(2 * BLOCK, 2 * BLOCK)
            first_band = jnp.logical_and(band, jnp.logical_or(col >= BLOCK, n > 0))
            take = _class_reader(r, (qc, doc, lsc, dlc, kc, kp, vc, vp), scratches)

            def stacked_cols(b):
                other = pltpu.roll(b, HEAD_DIM, 1)
                rows = jnp.concatenate([jnp.where(lo, b, other), jnp.where(lo, other, b)], axis=0)
                return jnp.concatenate([rows, rows], axis=1)

            def run(blocks):
                rows = [_class_rows(c, a * BLOCK, BLOCK, r) for c, a, _ in blocks]
                own = lambda ref: [take(ref, c, a * BLOCK, BLOCK) for c, a, _ in blocks]
                keys = _dil_keys(blocks, take, kc, kp)
                vals = _dil_keys(blocks, take, vc, vp)
                qss = [_stack_heads(x, lo).astype(BF16) for x in own(qc)]
                doss = [_stack_heads(x, lo).astype(BF16) for x in own(doc)]
                scores = [_dot_nt(qs, kb) for qs, kb in zip(qss, keys)]
                dps = [_dot_nt(dos, vb) for dos, vb in zip(doss, vals)]
                pws, dss = [], []
                for lsb, dlb, s, dp, (_, _, first) in zip(own(lsc), own(dlc), scores, dps, blocks):
                    pw = jnp.where(first_band if first else band, jnp.exp(s - stacked_cols(lsb)), 0.0)
                    pws.append(pw.astype(BF16))
                    dss.append((pw * (dp - stacked_cols(dlb))).astype(BF16))
                dqs = [_dot(ds, kb) for ds, kb in zip(dss, keys)]
                dks = [_dot_tn(ds, qs) for ds, qs in zip(dss, qss)]
                dvs = [_dot_tn(pw, dos) for pw, dos in zip(pws, doss)]
                for (c, a, first), rw, dq, dk, dv in zip(blocks, rows, dqs, dks, dvs):
                    dq_ref[rw, :] = _unstack_heads(dq, lo)
                    if first:
                        last = _class_rows(c, (per - 1) * BLOCK, BLOCK, r)
                        dk_ref[last, :] += dk[:BLOCK]
                        dv_ref[last, :] += dv[:BLOCK]
                    else:
                        prev = _class_rows(c, (a - 1) * BLOCK, BLOCK, r)
                        dk_carry[prev, :] += dk[:BLOCK]
                        dv_carry[prev, :] += dv[:BLOCK]
                    dk_carry[rw, :] = dk[BLOCK:]
                    dv_carry[rw, :] = dv[BLOCK:]

            _for_each_group(r, per, DIL_GROUP_BWD, run)
            if others is not None:
                dq_ref[...] += dq_in[...]

    per_edge = tile // edge
    here = lambda n: jnp.minimum(n, ntile - 1)
    cur = lambda off: pl.BlockSpec((tile, LANES), lambda p, n: (here(n), off + p))
    before = lambda off: pl.BlockSpec((edge, LANES), lambda p, n: (jnp.maximum(here(n) * per_edge - 1, 0), off + p))
    lagged = pl.BlockSpec((tile, LANES), lambda p, n: (jnp.maximum(n - 1, 0), p))
    carry = pltpu.VMEM((tile, LANES), F32)
    return pl.pallas_call(
        body, name=f"dil_bwd_r{r}", grid=(pairs, ntile + 1),
        in_specs=[cur(0)] * 4 + [cur(pairs), before(pairs), cur(2 * pairs), before(2 * pairs)] +
                 ([] if others is None else [cur(0), lagged, lagged]),
        out_specs=[cur(0), lagged, lagged],
        out_shape=[jax.ShapeDtypeStruct((seq, W_DIL), F32)] * 3,
        scratch_shapes=[carry, carry] + _class_scratches(r, (tile, tile, tile, tile, tile, edge, tile, edge)),
        compiler_params=_params(48),
    )(qkn, d_od, lse, delta, qkn, qkn, qkv_raw, qkv_raw, *(others or ()))


def _assemble_dproj(d_qa, d_ka, d_va, d_gates, d_qd, d_kd, d_vd, qk_raw, gains, tables, bd):
    seq = qk_raw.shape[0]
    tm = min(256, seq)
    chunks = W_DIL // (2 * LANES)

    def body(dqa, dka0, dka1, dva0, dva1, dg, dqd, dkd, dvd, x_ref, g_ref, c_ref, sn_ref, sp_ref, bd_ref, dp_ref, gpart_ref):
        @pl.when(pl.program_id(0) == 0)
        def _():
            gpart_ref[...] = jnp.zeros_like(gpart_ref)

        def put(first_col, v):
            dp_ref[:, first_col:first_col + v.shape[1]] = v.astype(BF16)

        put(0, dqa[...])
        put(W_SB, dka0[...])
        put(W_SB + LANES, dka1[...])
        put(2 * W_SB, dva0[...])
        put(2 * W_SB + LANES, dva1[...])
        put(3 * W_SB, dg[:, :W_SB])
        put(4 * W_SB + 2 * W_DIL, dvd[...])
        put(4 * W_SB + 3 * W_DIL, dg[:, W_SB:])
        c, sn, sp, bdm = c_ref[...], sn_ref[...], sp_ref[...], bd_ref[...]
        for which, part in enumerate((dqd, dkd)):
            scale = QK_SCALE if which == 0 else 1.0
            for k in range(chunks):
                sl = slice(k * 2 * LANES, (k + 1) * 2 * LANES)
                dyv = part[:, sl] * scale
                dxn = _rope_t(dyv, c, sn, sp)
                x = x_ref[:, which * W_DIL + k * 2 * LANES:which * W_DIL + (k + 1) * 2 * LANES]
                rs = lax.rsqrt(_head_sums(x * x, bdm) * (1.0 / HEAD_DIM) + EPS)
                xhat = x * rs
                gpart_ref[which] += jnp.sum((dxn * xhat).reshape(tm // 8, 8, 2 * LANES), axis=0)
                dxhat = dxn * g_ref[which]
                mean = _head_sums(dxhat * xhat, bdm) * (1.0 / HEAD_DIM)
                put(4 * W_SB + which * W_DIL + k * 2 * LANES, rs * (dxhat - xhat * mean))

    row = lambda w: pl.BlockSpec((tm, w), lambda i: (i, 0))
    pair = lambda p: pl.BlockSpec((None, tm, LANES), lambda i: (p, i, 0))
    const = lambda shape: pl.BlockSpec(shape, lambda i: tuple(0 for _ in shape))
    return pl.pallas_call(
        body, name="assemble_dproj", grid=(seq // tm,),
        in_specs=[row(W_SB), pair(0), pair(1), pair(0), pair(1), row(MIX)] + [row(W_DIL)] * 3 +
                 [row(2 * W_DIL), const((2, 1, 2 * LANES)), row(2 * LANES), row(2 * LANES), row(2 * LANES),
                  const((2 * LANES, 2 * LANES))],
        out_specs=[row(IN_COLS), const((2, 8, 2 * LANES))],
        out_shape=[jax.ShapeDtypeStruct((seq, IN_COLS), BF16), jax.ShapeDtypeStruct((2, 8, 2 * LANES), F32)],
        compiler_params=_params(48),
    )(d_qa, d_ka, d_ka, d_va, d_va, d_gates, d_qd, d_kd, d_vd, qk_raw, gains, *tables, bd)


def _dw_in(h, dproj, dwout_l, rout, layer):
    seq, d = h.shape
    tm = min(2048, seq)
    steps = seq // tm

    def body(h_ref, dp_ref, dwout_ref, rout_in, dw_ref, rout_ref, *sems):
        del rout_in
        n, i = pl.program_id(0), pl.program_id(1)
        copies = lambda: _to_every_device(lambda to: (dwout_ref.at[to],), lambda me: (rout_ref.at[me, layer],), sems)

        @pl.when(jnp.logical_and(n == 0, i == 0))
        def _():
            for cp in copies():
                cp.start()

        @pl.when(i == 0)
        def _():
            dw_ref[...] = jnp.zeros_like(dw_ref)

        dw_ref[...] += _dot_tn(h_ref[...], dp_ref[...])

        @pl.when(jnp.logical_and(n == N_DEV - 1, i == steps - 1))
        def _():
            for cp in copies():
                cp.wait()

    any_spec = pl.BlockSpec(memory_space=pl.ANY)
    return pl.pallas_call(
        body, name="dw_in_exchange", grid=(N_DEV, steps),
        in_specs=[pl.BlockSpec((tm, d), lambda n, i: (i, 0)), pl.BlockSpec((tm, COLS_PER_DEV), lambda n, i: (i, n)),
                  any_spec, any_spec],
        out_specs=[pl.BlockSpec((None, d, COLS_PER_DEV), lambda n, i: (n, 0, 0)), any_spec],
        out_shape=[jax.ShapeDtypeStruct((N_DEV, d, COLS_PER_DEV), F32), jax.ShapeDtypeStruct(rout.shape, F32)],
        scratch_shapes=_copy_sems(1), input_output_aliases={3: 1},
        compiler_params=_params(40),
    )(h, dproj, dwout_l, rout)


def _dx_norm(dproj, w_l, x, g, dx_next, dwin_l, rin, layer):
    seq, d = x.shape
    tm = min(256, seq)
    steps = seq // tm

    def body(dp_ref, w_ref, x_ref, g_ref, dn_ref, dwin_ref, rin_in, dx_ref, gpart_ref, rin_ref, *sems):
        del rin_in
        copies = lambda: _to_every_device(lambda to: (dwin_ref.at[to],), lambda me: (rin_ref.at[me, layer],), sems)

        @pl.when(pl.program_id(0) == 0)
        def _():
            gpart_ref[...] = jnp.zeros_like(gpart_ref)
            for cp in copies():
                cp.start()

        dh = jnp.zeros((tm, d), F32)
        for n in range(N_DEV):
            dh = dh + _dot_nt(dp_ref[:, n * COLS_PER_DEV:(n + 1) * COLS_PER_DEV], w_ref[n])
        xf = x_ref[...]
        rs = lax.rsqrt(jnp.mean(xf * xf, axis=-1, keepdims=True) + EPS)
        xhat = xf * rs
        gpart_ref[...] += jnp.sum((dh * xhat).reshape(tm // 8, 8, d), axis=0)
        dxhat = dh * g_ref[...]
        mean = jnp.mean(dxhat * xhat, axis=-1, keepdims=True)
        dx_ref[...] = rs * (dxhat - xhat * mean) + dn_ref[...]

        @pl.when(pl.program_id(0) == steps - 1)
        def _():
            for cp in copies():
                cp.wait()

    row = lambda w: pl.BlockSpec((tm, w), lambda i: (i, 0))
    any_spec = pl.BlockSpec(memory_space=pl.ANY)
    return pl.pallas_call(
        body, name="dx_norm_exchange", grid=(steps,),
        in_specs=[row(IN_COLS), pl.BlockSpec((N_DEV, d, COLS_PER_DEV), lambda i: (0, 0, 0)), row(d),
                  pl.BlockSpec((1, d), lambda i: (0, 0)), row(d), any_spec, any_spec],
        out_specs=[row(d), pl.BlockSpec((8, d), lambda i: (0, 0)), any_spec],
        out_shape=[jax.ShapeDtypeStruct((seq, d), F32), jax.ShapeDtypeStruct((8, d), F32),
                   jax.ShapeDtypeStruct(rin.shape, F32)],
        scratch_shapes=_copy_sems(1), input_output_aliases={6: 2},
        compiler_params=_params(48),
    )(dproj, w_l, x, g, dx_next, dwin_l, rin)


def _exchange_small(small):
    def body(small_ref, out_ref, *sems):
        copies = _to_every_device(lambda to: (small_ref,), lambda me: (out_ref.at[me],), sems)
        for cp in copies:
            cp.start()
        for cp in copies:
            cp.wait()

    vmem = pl.BlockSpec(memory_space=pltpu.VMEM)
    return pl.pallas_call(
        body, name="exchange_small", in_specs=[vmem], out_specs=vmem,
        out_shape=jax.ShapeDtypeStruct((N_DEV,) + small.shape, F32), scratch_shapes=_copy_sems(1),
    )(small)


def _adamw_math(g, w, m, v):
    m = ADAM_B1 * m + (1.0 - ADAM_B1) * g
    v = ADAM_B2 * v + (1.0 - ADAM_B2) * (g * g)
    m_hat = m / (1.0 - ADAM_B1 ** ADAM_STEP)
    v_hat = v / (1.0 - ADAM_B2 ** ADAM_STEP)
    delta = -ADAM_LR * (m_hat / (jnp.sqrt(v_hat) + ADAM_EPS) + ADAM_WD * w)
    return delta, m, v


def _adamw(parts, w, m, v, name):
    nl, r, c = w.shape
    tr = min(r, (256 * 512) // c)

    def body(p_ref, w_ref, m_ref, v_ref, g_ref, d_ref, nm_ref, nv_ref):
        g = p_ref[0]
        for s in range(1, N_DEV):
            g = g + p_ref[s]
        g_ref[...] = g
        d_ref[...], nm_ref[...], nv_ref[...] = _adamw_math(g, w_ref[...], m_ref[...], v_ref[...])

    blk = pl.BlockSpec((None, tr, c), lambda l, i: (l, i, 0))
    return pl.pallas_call(
        body, name=name, grid=(nl, r // tr),
        in_specs=[pl.BlockSpec((N_DEV, None, tr, c), lambda l, i: (0, l, i, 0)), blk, blk, blk],
        out_specs=[blk] * 4, out_shape=[jax.ShapeDtypeStruct(w.shape, F32)] * 4,
        compiler_params=_params(32),
    )(parts, w, m, v)


def _adamw_small(parts, w, m, v):
    def body(p_ref, w_ref, m_ref, v_ref, g_ref, d_ref, nm_ref, nv_ref):
        g = p_ref[0]
        for s in range(1, N_DEV):
            g = g + p_ref[s]
        g_ref[...] = g
        d_ref[...], nm_ref[...], nv_ref[...] = _adamw_math(g, w_ref[...], m_ref[...], v_ref[...])

    vmem = pl.BlockSpec(memory_space=pltpu.VMEM)
    return pl.pallas_call(
        body, name="adamw_small", in_specs=[vmem] * 4, out_specs=[vmem] * 4,
        out_shape=[jax.ShapeDtypeStruct(w.shape, F32)] * 4,
    )(parts, w, m, v)


def _pack_small(a, b, c):
    pad = jnp.zeros((a.shape[0], SMALL_W - a.shape[1] - b.shape[1] - c.shape[1]), F32)
    return jnp.concatenate([a, b, c, pad], axis=1)


def _unpack_small(t, d):
    return t[:, :d], t[:, d:d + HEAD_DIM], t[:, d + HEAD_DIM:d + 2 * HEAD_DIM]


def kernel(x, norm_g, w_in, q_norm_g, k_norm_g, w_out, loss_target, m_norm_g, m_w_in, m_q_norm_g, m_k_norm_g, m_w_out,
           v_norm_g, v_w_in, v_q_norm_g, v_k_norm_g, v_w_out):
    depth, d, _ = w_in.shape
    seq = x.shape[1]
    tri_a, tri_b, bd = _tri_constants()
    tables = _rope_tables(seq)
    rep = (2 * LANES) // HEAD_DIM

    win_bf, wout_bf = _cast_bf16(w_in, "cast_w_in"), _cast_bf16(w_out, "cast_w_out")
    win_l, wout_l = _gather_weights(win_bf, wout_bf, 0)
    saved = []
    xl = x.reshape(seq, d)
    for layer in range(depth):
        gains = jnp.stack([jnp.tile(q_norm_g[layer], rep), jnp.tile(k_norm_g[layer], rep)])[:, None, :]
        if layer + 1 < depth:
            proj_bf, gates, qk_raw, h, qkn, win_next, wout_next = _norm_proj(
                xl, norm_g[layer][None, :], win_l, gains, tables, bd, prefetch=(win_bf, wout_bf, layer + 1))
        else:
            proj_bf, gates, qk_raw, h, qkn = _norm_proj(xl, norm_g[layer][None, :], win_l, gains, tables, bd)
            win_next = wout_next = None
        oa = _sb_fwd(proj_bf, tri_a)
        parts = [_dil_fwd(qkn, qk_raw, r) for _, r in DIL_PATTERNS[:-1]]
        od, lse = _dil_fwd(qkn, qk_raw, DIL_PATTERNS[-1][1], others=parts)
        x_next, cat = _out_proj(xl, oa, od, gates, wout_l)
        saved.append((xl, gains, proj_bf, gates, qk_raw, h, qkn, oa, cat, od, lse, win_l, wout_l))
        xl, win_l, wout_l = x_next, win_next, wout_next

    loss_part, dx = _loss_head(xl, loss_target.reshape(seq, d))
    loss = lax.psum(jnp.sum(loss_part), ("x", "y", "c"))

    rin = jnp.zeros((N_DEV, depth, d, COLS_PER_DEV), F32)
    rout = jnp.zeros((N_DEV, depth, ROWS_PER_DEV, d), F32)
    g_norm, g_q, g_k = [None] * depth, [None] * depth, [None] * depth
    for layer in reversed(range(depth)):
        xl, gains, proj_bf, gates, qk_raw, h, qkn, oa, cat, od, lse, win_l, wout_l = saved[layer]
        d_oa, d_od, delta, d_gates, dwout_l = _out_proj_bwd(dx, wout_l, cat, gates, oa, od, bd)
        d_qa, d_ka, d_va = _sb_bwd(proj_bf, d_oa, oa, tri_a, tri_b)
        d_dil = None
        for _, r in DIL_PATTERNS:
            d_dil = _dil_bwd(qkn, qk_raw, d_od, lse, delta, r, others=d_dil)
        dproj, gqk = _assemble_dproj(d_qa, d_ka, d_va, d_gates, *d_dil, qk_raw, gains, tables, bd)
        dwin_l, rout = _dw_in(h, dproj, dwout_l, rout, layer)
        dx, gn, rin = _dx_norm(dproj, win_l, xl, norm_g[layer][None, :], dx, dwin_l, rin, layer)
        g_norm[layer] = jnp.sum(gn, axis=0)
        gqk = jnp.sum(gqk, axis=1).reshape(2, rep, HEAD_DIM).sum(axis=1)
        g_q[layer], g_k[layer] = gqk[0], gqk[1]
    rsmall = _exchange_small(_pack_small(jnp.stack(g_norm), jnp.stack(g_q), jnp.stack(g_k)))

    g_in, d_in, nm_in, nv_in = _adamw(rin, w_in, m_w_in, v_w_in, "adamw_w_in")
    g_out, d_out, nm_out, nv_out = _adamw(rout, w_out, m_w_out, v_w_out, "adamw_w_out")
    small_out = _adamw_small(rsmall, _pack_small(norm_g, q_norm_g, k_norm_g), _pack_small(m_norm_g, m_q_norm_g, m_k_norm_g),
                             _pack_small(v_norm_g, v_q_norm_g, v_k_norm_g))
    (g_n, g_qn, g_kn), (d_n, d_qn, d_kn), (nm_n, nm_qn, nm_kn), (nv_n, nv_qn, nv_kn) = (_unpack_small(t, d) for t in small_out)

    return (loss, dx.reshape(x.shape), g_n, g_in, g_qn, g_kn, g_out, d_n, d_in, d_qn, d_kn, d_out,
            nm_n, nm_in, nm_qn, nm_kn, nm_out, nv_n, nv_in, nv_qn, nv_kn, nv_out)
```

```python
import math

import jax
import jax.numpy as jnp
from jax import lax
from jax.experimental import pallas as pl
from jax.experimental.pallas import tpu as pltpu

F32 = jnp.float32
BF16 = jnp.bfloat16

EPS = 1e-6
HEAD_DIM = 64
BLOCK = 128
LANES = 128
W_SB = 256
W_DIL = 768
MIX = W_SB + W_DIL
IN_COLS = 4 * W_SB + 4 * W_DIL
N_DEV = 8
COLS_PER_DEV = IN_COLS // N_DEV
ROWS_PER_DEV = MIX // N_DEV
QK_SCALE = 1.0 / math.sqrt(HEAD_DIM)
DIL_PATTERNS = ((128, 1), (512, 4), (2048, 16))
DIL_TILE = 2048
DIL_GROUP_FWD = 4
DIL_GROUP_BWD = 4
SPLIT = 4
ROPE_THETA = 500000.0
ROPE_DIM = HEAD_DIM // 4
ROPE_HALF = ROPE_DIM // 2
DEAD_LOG = -110.0
SB_PEEL = 3
SB_QBLOCKS = 2
SB_SUM_TERMS = 2
NEG_BIG = -1e30

ADAM_LR = 0.001
ADAM_B1 = 0.9
ADAM_B2 = 0.999
ADAM_EPS = 1e-08
ADAM_WD = 0.01
ADAM_STEP = 10

SMALL_W = 1280
MESH_ID = pl.DeviceIdType.MESH
MIB = 1 << 20


def _params(vmem_mib):
    return pltpu.CompilerParams(vmem_limit_bytes=vmem_mib * MIB)


def _dot(a, b):
    return jnp.dot(a, b, preferred_element_type=F32)


def _dot_nt(a, b):
    return lax.dot_general(a, b, (((1,), (1,)), ((), ())), preferred_element_type=F32)


def _dot_tn(a, b):
    return lax.dot_general(a, b, (((0,), (0,)), ((), ())), preferred_element_type=F32)


def _dot_exact(x, m01, terms=3):
    out = 0.0
    for _ in range(terms):
        part = x.astype(BF16)
        out = out + _dot(part, m01)
        x = x - part.astype(F32)
    return out


def _lane_lo(shape):
    return lax.broadcasted_iota(jnp.int32, shape, 1) < HEAD_DIM


def _tri_constants():
    j = jnp.arange(BLOCK)
    ones = jnp.ones((BLOCK, BLOCK), F32)
    excl = (j[:, None] > j[None, :]).astype(F32)
    incl = (j[:, None] >= j[None, :]).astype(F32)
    tri_a = jnp.concatenate([excl, ones], axis=1).astype(BF16)
    tri_b = jnp.concatenate([incl, ones], axis=1).astype(BF16)
    d = jnp.arange(2 * LANES)
    bd = (d[:, None] // HEAD_DIM == d[None, :] // HEAD_DIM).astype(BF16)
    return tri_a, tri_b, bd


def _rope_tables(seq):
    d = jnp.arange(2 * LANES) % HEAD_DIM
    inv_freq = 1.0 / (ROPE_THETA ** ((d % ROPE_HALF).astype(F32) * 2.0 / ROPE_DIM))
    ang = jnp.arange(seq).astype(F32)[:, None] * inv_freq[None, :]
    cos, sin = jnp.cos(ang), jnp.sin(ang)
    c = jnp.where(d < ROPE_DIM, cos, 1.0)
    s_next = jnp.where(d < ROPE_HALF, -sin, 0.0)
    s_prev = jnp.where((d >= ROPE_HALF) & (d < ROPE_DIM), sin, 0.0)
    return c, s_next, s_prev


def _roll_lanes(x, shift):
    return jnp.concatenate([pltpu.roll(x[:, :LANES], shift, 1), pltpu.roll(x[:, LANES:], shift, 1)], axis=1)


def _rope(x, c, s_next, s_prev):
    return x * c + _roll_lanes(x, LANES - ROPE_HALF) * s_next + _roll_lanes(x, ROPE_HALF) * s_prev


def _rope_t(dy, c, s_next, s_prev):
    return dy * c + _roll_lanes(dy * s_next, ROPE_HALF) + _roll_lanes(dy * s_prev, LANES - ROPE_HALF)


def _cast_bf16(w, name):
    nl, r, c = w.shape

    def body(w_ref, o_ref):
        o_ref[...] = w_ref[...].astype(BF16)

    return pl.pallas_call(
        body, name=name, grid=(nl,),
        in_specs=[pl.BlockSpec((None, r, c), lambda l: (l, 0, 0))],
        out_specs=pl.BlockSpec((None, r, c), lambda l: (l, 0, 0)),
        out_shape=jax.ShapeDtypeStruct(w.shape, BF16),
        compiler_params=_params(24),
    )(w)


def _flips():
    return [(dx, dy, dc) for dx in (0, 1) for dy in (0, 1) for dc in (0, 1) if (dx, dy, dc) != (0, 0, 0)]


def _place():
    x, y, c = lax.axis_index("x"), lax.axis_index("y"), lax.axis_index("c")
    return x, y, c, 4 * x + 2 * y + c


def _peer(x, y, c, flip):
    dx, dy, dc = flip
    return (1 - x if dx else x, 1 - y if dy else y, 1 - c if dc else c)


def _to_every_device(srcs_for, dsts_at, sems):
    send_sems, recv_sems, local_sems = sems
    x, y, c, me = _place()
    dsts = dsts_at(me)
    n = len(dsts)
    copies = [pltpu.make_async_copy(src, dst, local_sems.at[a]) for a, (src, dst) in enumerate(zip(srcs_for(me), dsts))]
    for k, flip in enumerate(_flips()):
        px, py, pc = _peer(x, y, c, flip)
        for a, (src, dst) in enumerate(zip(srcs_for(4 * px + 2 * py + pc), dsts)):
            copies.append(pltpu.make_async_remote_copy(
                src_ref=src, dst_ref=dst, send_sem=send_sems.at[n * k + a], recv_sem=recv_sems.at[n * k + a],
                device_id=(px, py, pc), device_id_type=MESH_ID))
    return copies


def _copy_sems(n):
    remote = n * (N_DEV - 1)
    return [pltpu.SemaphoreType.DMA((remote,)), pltpu.SemaphoreType.DMA((remote,)), pltpu.SemaphoreType.DMA((n,))]


def _weight_copies(win_ref, wout_ref, layer, oin_ref, oout_ref, sems):
    return _to_every_device(lambda to: (win_ref.at[layer], wout_ref.at[layer]), lambda me: (oin_ref.at[me], oout_ref.at[me]), sems)


def _gathered_shapes(win_bf, wout_bf):
    return [jax.ShapeDtypeStruct((N_DEV,) + win_bf.shape[1:], BF16), jax.ShapeDtypeStruct((N_DEV,) + wout_bf.shape[1:], BF16)]


def _gather_weights(win_bf, wout_bf, layer):
    def body(win_ref, wout_ref, oin_ref, oout_ref, *sems):
        copies = _weight_copies(win_ref, wout_ref, layer, oin_ref, oout_ref, sems)
        for cp in copies:
            cp.start()
        for cp in copies:
            cp.wait()

    any_spec = pl.BlockSpec(memory_space=pl.ANY)
    return pl.pallas_call(
        body, name="gather_weights", in_specs=[any_spec, any_spec], out_specs=[any_spec, any_spec],
        out_shape=_gathered_shapes(win_bf, wout_bf), scratch_shapes=_copy_sems(2),
    )(win_bf, wout_bf)


_QK_BLOCKS = (2, 3, 4)
_F32_ROUTES = {1: ((256, 256, 0, 0),), 2: ((0, 512, 1, 0),), 3: ((0, 512, 1, 512),), 4: ((0, 512, 1, 1024),),
               5: ((0, 512, 1, 1536),), 6: ((0, 256, 1, 2048), (256, 256, 0, 256)), 7: ((0, 512, 0, 512),)}


def _norm_proj(x, g, w_l, gains, tables, bd, prefetch=None):
    seq, d = x.shape
    tm = min(256, seq)
    steps = seq // tm

    def body(x_ref, g_ref, w_ref, gains_ref, c_ref, sn_ref, sp_ref, bd_ref, *rest):
        if prefetch is None:
            pbf_ref, gates_ref, qk_ref, h_ref, qkn_ref = rest
        else:
            win_ref, wout_ref, pbf_ref, gates_ref, qk_ref, h_ref, qkn_ref, oin_ref, oout_ref, *sems = rest
            copies = lambda: _weight_copies(win_ref, wout_ref, prefetch[2], oin_ref, oout_ref, sems)

            @pl.when(pl.program_id(0) == 0)
            def _():
                for cp in copies():
                    cp.start()

        xf = x_ref[...]
        rs = lax.rsqrt(jnp.mean(xf * xf, axis=-1, keepdims=True) + EPS)
        h = (xf * rs * g_ref[...]).astype(BF16)
        h_ref[...] = h
        targets = (gates_ref, qk_ref)
        for n in range(N_DEV):
            acc = _dot(h, w_ref[n])
            pbf_ref[:, n * COLS_PER_DEV:(n + 1) * COLS_PER_DEV] = acc.astype(BF16)
            for lo, width, tgt, dst in _F32_ROUTES.get(n, ()):
                targets[tgt][:, dst:dst + width] = acc[:, lo:lo + width]
            for half in range(2) if n in _QK_BLOCKS else ():
                col = (_QK_BLOCKS.index(n) * 2 + half) * 2 * LANES
                is_q = col < W_DIL
                xq = acc[:, half * 2 * LANES:(half + 1) * 2 * LANES]
                rsq = lax.rsqrt(_head_sums(xq * xq, bd_ref[...]) * (1.0 / HEAD_DIM) + EPS)
                y = _rope(xq * rsq * gains_ref[0 if is_q else 1], c_ref[...], sn_ref[...], sp_ref[...])
                qkn_ref[:, col:col + 2 * LANES] = y * QK_SCALE if is_q else y

        if prefetch is not None:
            @pl.when(pl.program_id(0) == steps - 1)
            def _():
                for cp in copies():
                    cp.wait()

    row = lambda w: pl.BlockSpec((tm, w), lambda i: (i, 0))
    any_spec = pl.BlockSpec(memory_space=pl.ANY)
    const = lambda shape: pl.BlockSpec(shape, lambda i: tuple(0 for _ in shape))
    in_specs = [row(d), const((1, d)), const((N_DEV, d, COLS_PER_DEV)), const((2, 1, 2 * LANES)),
                row(2 * LANES), row(2 * LANES), row(2 * LANES), const((2 * LANES, 2 * LANES))]
    out_specs = [row(IN_COLS), row(MIX), row(3 * W_DIL), row(d), row(2 * W_DIL)]
    out_shape = [jax.ShapeDtypeStruct((seq, IN_COLS), BF16), jax.ShapeDtypeStruct((seq, MIX), F32),
                 jax.ShapeDtypeStruct((seq, 3 * W_DIL), F32), jax.ShapeDtypeStruct((seq, d), BF16),
                 jax.ShapeDtypeStruct((seq, 2 * W_DIL), F32)]
    operands = (x, g, w_l, gains, *tables, bd)
    if prefetch is None:
        return pl.pallas_call(body, name="norm_proj", grid=(steps,), in_specs=in_specs, out_specs=out_specs,
                              out_shape=out_shape, compiler_params=_params(48))(*operands)
    return pl.pallas_call(
        body, name="norm_proj_gather", grid=(steps,), in_specs=in_specs + [any_spec, any_spec],
        out_specs=out_specs + [any_spec, any_spec], out_shape=out_shape + _gathered_shapes(*prefetch[:2]),
        scratch_shapes=_copy_sems(2), compiler_params=_params(48),
    )(*operands, *prefetch[:2])


def _head_sums(v, bd):
    hi = v.astype(BF16)
    lo = (v - hi.astype(F32)).astype(BF16)
    return _dot(hi, bd) + _dot(lo, bd)


def _stack_heads(x, lo):
    return jnp.concatenate([jnp.where(lo, x, 0.0), jnp.where(lo, 0.0, x)], axis=0)


def _unstack_heads(y, lo):
    return jnp.where(lo, y[:BLOCK], y[BLOCK:])


def _stacked_causal():
    row = lax.broadcasted_iota(jnp.int32, (2 * BLOCK, BLOCK), 0) & (BLOCK - 1)
    return lax.broadcasted_iota(jnp.int32, (2 * BLOCK, BLOCK), 1) < row


def _keep(x, *conds):
    for cond in conds:
        if cond is not None:
            x = jnp.where(cond, x, 0.0)
    return x


def _sb_weights(chains, tri):
    zs = [[_dot_nt(qs, kb) for kb in kbs] for qs, kbs, _, _, _ in chains]
    lss = [[jnp.minimum(z, 0.0) - jnp.log1p(jnp.exp(-jnp.abs(z))) for z in zc] for zc in zs]
    cts = [[_dot_exact(_keep(ls - z, mask, live), tri, SB_SUM_TERMS) for ls, z, mask, live in zip(lsc, zc, masks, lives)]
           for lsc, zc, (_, _, _, masks, lives) in zip(lss, zs, chains)]
    out = []
    for lsc, ctc, (_, _, r, masks, lives) in zip(lss, cts, chains):
        weights = []
        for ls, ct, mask, live in zip(lsc, ctc, masks, lives):
            weights.append(_keep(jnp.exp(ls + ct[:, :BLOCK] + r), mask, live))
            r = r + ct[:, BLOCK:]
        out.append((lsc, weights, r))
    return out


def _sb_peel(i, causal):
    return ([jnp.maximum(i - k, 0) for k in range(SB_PEEL)], [causal] + [None] * (SB_PEEL - 1),
            [None] + [i >= k for k in range(1, SB_PEEL)])


def _block_rows(j):
    return pl.ds(pl.multiple_of(j * BLOCK, BLOCK), BLOCK)


def _sb_fwd(proj_bf, tri_a):
    seq = proj_bf.shape[0]
    pairs = W_SB // LANES

    def body(q_ref, k_ref, v_ref, tri_ref, o_ref, r_ref, acc_ref):
        t = pl.program_id(1)
        lo = _lane_lo((BLOCK, LANES))
        causal = _stacked_causal()
        tri = tri_ref[...]
        qblocks = [t * SB_QBLOCKS + c for c in range(SB_QBLOCKS)]
        qss = [_stack_heads(q_ref[c * BLOCK:(c + 1) * BLOCK, :].astype(F32) * QK_SCALE, lo).astype(BF16)
               for c in range(SB_QBLOCKS)]

        def blocks(specs):
            rows = [[_block_rows(j) for j in js] for _, js, _, _, _ in specs]
            res = _sb_weights([(qs, [k_ref[rw, :] for rw in rws], r, masks, lives)
                               for (qs, _, r, masks, lives), rws in zip(specs, rows)], tri)
            outs = []
            for (_, weights, r), rws in zip(res, rows):
                out = 0.0
                for a, rw in zip(weights, rws):
                    a_hi = a.astype(BF16)
                    a_lo = (a - a_hi.astype(F32)).astype(BF16)
                    vb = v_ref[rw, :]
                    out = out + _dot(a_hi, vb) + _dot(a_lo, vb)
                outs.append((out, r))
            return outs

        def peel(qs, i):
            js, masks, lives = _sb_peel(i, causal)
            return qs, js, jnp.zeros((2 * BLOCK, LANES), F32), masks, lives

        for c, (out, r) in enumerate(blocks([peel(qs, i) for qs, i in zip(qss, qblocks)])):
            acc_ref[c], r_ref[c] = out, r

        for c in range(SB_QBLOCKS):
            def alive(c=c):
                return (jnp.max(r_ref[c]) > DEAD_LOG).astype(jnp.int32)

            def step(carry, c=c):
                (out, r_ref[c]), = blocks([(qss[c], [carry[0]], r_ref[c], [None], [None])])
                acc_ref[c] += out
                return carry[0] - 1, alive(c)

            lax.while_loop(lambda carry: jnp.logical_and(carry[0] >= 0, carry[1] > 0), step, (qblocks[c] - SB_PEEL, alive()))
            o_ref[c * BLOCK:(c + 1) * BLOCK, :] = _unstack_heads(acc_ref[c], lo)

    qtile = SB_QBLOCKS * BLOCK
    state = pltpu.VMEM((SB_QBLOCKS, 2 * BLOCK, LANES), F32)
    return pl.pallas_call(
        body, name="sb_fwd", grid=(pairs, seq // qtile),
        in_specs=[pl.BlockSpec((qtile, LANES), lambda p, t: (t, p)),
                  pl.BlockSpec((seq, LANES), lambda p, t: (0, pairs + p)),
                  pl.BlockSpec((seq, LANES), lambda p, t: (0, 2 * pairs + p)),
                  pl.BlockSpec((BLOCK, 2 * BLOCK), lambda p, t: (0, 0))],
        out_specs=pl.BlockSpec((qtile, LANES), lambda p, t: (t, p)),
        out_shape=jax.ShapeDtypeStruct((seq, W_SB), F32),
        scratch_shapes=[state, state],
        compiler_params=_params(40),
    )(proj_bf, proj_bf, proj_bf, tri_a)


def _class_rows(cls, first, count, r):
    c = cls[0] + SPLIT * cls[1] if isinstance(cls, tuple) else cls
    start = c + first * r
    return pl.ds(start, count) if r == 1 else pl.ds(start, count, stride=r)


def _class_reader(r, refs, scratches):
    if r != SPLIT * SPLIT:
        return lambda ref, cls, first, count: ref[_class_rows(cls, first, count, r), :]
    slabs = {}
    for ref, scr in zip(refs, scratches):
        for lo in range(SPLIT):
            scr[lo] = ref[pl.ds(lo, ref.shape[0] // SPLIT, stride=SPLIT), :]
        slabs[id(ref)] = scr

    def take(ref, cls, first, count):
        lo, hi = cls
        return slabs[id(ref)][lo, pl.ds(hi + SPLIT * first, count, stride=SPLIT), :]

    return take


def _class_scratches(r, shapes):
    return [pltpu.VMEM((SPLIT, rows // SPLIT, LANES), F32) for rows in shapes] if r == SPLIT * SPLIT else []


def _dil_tiling(seq, r):
    tile = min(DIL_TILE, seq)
    edge = BLOCK * r
    return tile, edge, tile // edge


def _band_mask(n_rows, n_keys):
    row = lax.broadcasted_iota(jnp.int32, (n_rows, n_keys), 0) & (BLOCK - 1)
    col = lax.broadcasted_iota(jnp.int32, (n_rows, n_keys), 1)
    return jnp.logical_and(col >= row, col <= row + BLOCK), col


def _for_each_group(r, per, g, run):
    loop = lambda lo, hi, fn: lax.fori_loop(lo, hi, lambda t, carry: (fn(t), carry)[1], 0)
    if per <= g:
        n_cls = g // per
        assert g % per == 0 and r % n_cls == 0 and (r != SPLIT * SPLIT or n_cls == SPLIT)
        cls = (lambda t, u: (u, t)) if r == SPLIT * SPLIT else (lambda t, u: t * n_cls + u)
        group = lambda t: [(cls(t, u), a, a == 0) for u in range(n_cls) for a in range(per)]
        if r == n_cls:
            run(group(0))
        else:
            loop(0, r // n_cls, lambda t: run(group(t)))
        return

    assert per % g == 0

    def one_class(c):
        run([(c, 0, True)] + [(c, a, False) for a in range(1, g)])
        loop(1, per // g, lambda t: run([(c, t * g + u, False) for u in range(g)]))

    if r == 1:
        one_class(0)
    else:
        loop(0, r, one_class)


def _dil_keys(blocks, take, cur_ref, before_ref):
    out = []
    for cls, a, first in blocks:
        if first:
            both = jnp.concatenate([take(before_ref, cls, 0, BLOCK), take(cur_ref, cls, 0, BLOCK)], axis=0)
        else:
            both = take(cur_ref, cls, (a - 1) * BLOCK, 2 * BLOCK)
        out.append(both.astype(BF16))
    return out


def _dil_fwd(qkn, qkv_raw, r, others=()):
    seq = qkn.shape[0]
    tile, edge, per = _dil_tiling(seq, r)
    pairs = W_DIL // LANES

    def body(q_ref, kc_ref, kp_ref, vc_ref, vp_ref, *rest):
        other_refs, (o_ref, ld_ref, *scratches) = rest[:2 * len(others)], rest[2 * len(others):]
        n = pl.program_id(1)
        lo = _lane_lo((BLOCK, LANES))
        band, col = _band_mask(2 * BLOCK, 2 * BLOCK)
        first_band = jnp.logical_and(band, jnp.logical_or(col >= BLOCK, n > 0))
        take = _class_reader(r, (q_ref, kc_ref, kp_ref, vc_ref, vp_ref), scratches)

        def run(blocks):
            rows = [_class_rows(c, a * BLOCK, BLOCK, r) for c, a, _ in blocks]
            keys = _dil_keys(blocks, take, kc_ref, kp_ref)
            vals = _dil_keys(blocks, take, vc_ref, vp_ref)
            scores = [_dot_nt(_stack_heads(take(q_ref, c, a * BLOCK, BLOCK), lo).astype(BF16), kb)
                      for (c, a, _), kb in zip(blocks, keys)]
            probs, sums, lds = [], [], []
            for s, (_, _, first) in zip(scores, blocks):
                s = jnp.where(first_band if first else band, s, NEG_BIG)
                m = jnp.max(s, axis=1, keepdims=True)
                p = jnp.exp(s - m)
                l = jnp.sum(p, axis=1, keepdims=True)
                probs.append(p.astype(BF16))
                sums.append(l)
                lds.append(m + jnp.log(l))
            outs = [_dot(p, vb) / l for p, vb, l in zip(probs, vals, sums)]
            for rw, o, ld in zip(rows, outs, lds):
                o_ref[rw, :] = _unstack_heads(o, lo)
                ld_ref[rw, :] = _unstack_heads(jnp.broadcast_to(ld, (2 * BLOCK, LANES)), lo)

        _for_each_group(r, per, DIL_GROUP_FWD, run)
        if others:
            o_ref[...], ld_ref[...] = _mix_patterns([ref[...] for ref in other_refs[0::2]] + [o_ref[...]],
                                                    [ref[...] for ref in other_refs[1::2]] + [ld_ref[...]])

    per_edge = tile // edge
    cur = lambda off: pl.BlockSpec((tile, LANES), lambda p, n: (n, off + p))
    before = lambda off: pl.BlockSpec((edge, LANES), lambda p, n: (jnp.maximum(n * per_edge - 1, 0), off + p))
    return pl.pallas_call(
        body, name=f"dil_fwd_r{r}", grid=(pairs, seq // tile),
        in_specs=[cur(0), cur(pairs), before(pairs), cur(2 * pairs), before(2 * pairs)] + [cur(0)] * (2 * len(others)),
        out_specs=[cur(0), cur(0)],
        out_shape=[jax.ShapeDtypeStruct((seq, W_DIL), F32), jax.ShapeDtypeStruct((seq, W_DIL), F32)],
        scratch_shapes=_class_scratches(r, (tile, tile, edge, tile, edge)),
        compiler_params=_params(40),
    )(qkn, qkn, qkn, qkv_raw, qkv_raw, *[a for pair in others for a in pair])


def _silu_parts(g):
    sig = jax.nn.sigmoid(g)
    return g * sig, sig * (1.0 + g * (1.0 - sig))


def _mix_patterns(os_, lds):
    m = lds[0]
    for v in lds[1:]:
        m = jnp.maximum(m, v)
    es = [jnp.exp(v - m) for v in lds]
    tot = es[0]
    for e in es[1:]:
        tot = tot + e
    inv = 1.0 / tot
    od = (es[0] * inv) * os_[0]
    for e, o in zip(es[1:], os_[1:]):
        od = od + (e * inv) * o
    return od, m + jnp.log(tot)


def _out_proj(x, oa, od, gates, wout_l):
    seq, d = x.shape
    tm = min(256, seq)

    def body(x_ref, oa_ref, od_ref, g_ref, w_ref, xn_ref, cat_ref):
        silu, _ = _silu_parts(g_ref[...])
        cat_ref[:, :W_SB] = (oa_ref[...] * silu[:, :W_SB]).astype(BF16)
        cat_ref[:, W_SB:] = (od_ref[...] * silu[:, W_SB:]).astype(BF16)
        y = x_ref[...]
        for b in range(N_DEV):
            y = y + _dot(cat_ref[:, b * ROWS_PER_DEV:(b + 1) * ROWS_PER_DEV], w_ref[b])
        xn_ref[...] = y

    row = lambda w: pl.BlockSpec((tm, w), lambda i: (i, 0))
    return pl.pallas_call(
        body, name="out_proj", grid=(seq // tm,),
        in_specs=[row(d), row(W_SB), row(W_DIL), row(MIX), pl.BlockSpec((N_DEV, ROWS_PER_DEV, d), lambda i: (0, 0, 0))],
        out_specs=[row(d), row(MIX)],
        out_shape=[jax.ShapeDtypeStruct((seq, d), F32), jax.ShapeDtypeStruct((seq, MIX), BF16)],
        compiler_params=_params(40),
    )(x, oa, od, gates, wout_l)


def _loss_head(y, target):
    seq, d = y.shape
    tm = min(512, seq)

    def body(y_ref, t_ref, part_ref, dy_ref):
        @pl.when(pl.program_id(0) == 0)
        def _():
            part_ref[...] = jnp.zeros_like(part_ref)

        diff = y_ref[...] - t_ref[...]
        dy_ref[...] = diff * (1.0 / d)
        part_ref[...] += jnp.sum((diff * diff).reshape(tm // 8, 8, d), axis=0) * (0.5 / d)

    row = pl.BlockSpec((tm, d), lambda i: (i, 0))
    return pl.pallas_call(
        body, name="loss_head", grid=(seq // tm,),
        in_specs=[row, row], out_specs=[pl.BlockSpec((8, d), lambda i: (0, 0)), row],
        out_shape=[jax.ShapeDtypeStruct((8, d), F32), jax.ShapeDtypeStruct((seq, d), F32)],
        compiler_params=_params(32),
    )(y, target)


def _out_proj_bwd(dy, wout_l, cat, gates, oa, od, bd):
    seq, d = dy.shape
    tm = min(256, seq)

    def body(dy_ref, w_ref, cat_ref, g_ref, oa_ref, od_ref, bd_ref, doa_ref, dod_ref, delta_ref, dg_ref, dw_ref, dcat):
        @pl.when(pl.program_id(0) == 0)
        def _():
            dw_ref[...] = jnp.zeros_like(dw_ref)

        dyb = dy_ref[...].astype(BF16)
        dw = _dot_tn(cat_ref[...], dyb)
        for b in range(N_DEV):
            dw_ref[b] += dw[b * ROWS_PER_DEV:(b + 1) * ROWS_PER_DEV, :]
            dcat[:, b * ROWS_PER_DEV:(b + 1) * ROWS_PER_DEV] = _dot_nt(dyb, w_ref[b])
        silu, dsilu = _silu_parts(g_ref[...])
        dc = dcat[...]
        dmix = dc * silu
        oa_v, od_v = oa_ref[...], od_ref[...]
        dg_ref[:, :W_SB] = (dc[:, :W_SB] * oa_v * dsilu[:, :W_SB]).astype(BF16)
        dg_ref[:, W_SB:] = (dc[:, W_SB:] * od_v * dsilu[:, W_SB:]).astype(BF16)
        doa_ref[...] = dmix[:, :W_SB]
        dod = dmix[:, W_SB:]
        dod_ref[...] = dod
        prod = dod * od_v
        for k in range(W_DIL // (2 * LANES)):
            sl = slice(k * 2 * LANES, (k + 1) * 2 * LANES)
            delta_ref[:, sl] = _head_sums(prod[:, sl], bd_ref[...])

    row = lambda w: pl.BlockSpec((tm, w), lambda i: (i, 0))
    slab = pl.BlockSpec((N_DEV, ROWS_PER_DEV, d), lambda i: (0, 0, 0))
    return pl.pallas_call(
        body, name="out_proj_bwd", grid=(seq // tm,),
        in_specs=[row(d), slab, row(MIX), row(MIX), row(W_SB), row(W_DIL),
                  pl.BlockSpec((2 * LANES, 2 * LANES), lambda i: (0, 0))],
        out_specs=[row(W_SB), row(W_DIL), row(W_DIL), row(MIX), slab],
        out_shape=[jax.ShapeDtypeStruct((seq, W_SB), F32), jax.ShapeDtypeStruct((seq, W_DIL), F32),
                   jax.ShapeDtypeStruct((seq, W_DIL), F32), jax.ShapeDtypeStruct((seq, MIX), BF16),
                   jax.ShapeDtypeStruct((N_DEV, ROWS_PER_DEV, d), F32)],
        scratch_shapes=[pltpu.VMEM((tm, MIX), F32)],
        compiler_params=_params(48),
    )(dy, wout_l, cat, gates, oa, od, bd)


def _sb_bwd(proj_bf, d_oa, oa, tri_a, tri_b):
    seq = proj_bf.shape[0]
    pairs = W_SB // LANES
    nq = seq // BLOCK

    def body(q_ref, k_ref, v_ref, do_ref, o_ref, tria_ref, trib_ref, dq_ref, dk_hbm, dv_hbm,
             r_ref, sfx_ref, dtot_ref, dq_acc, dk_acc, dv_acc, sems):
        p, t = pl.program_id(0), pl.program_id(1)

        @pl.when(t == 0)
        def _():
            dk_acc[...] = jnp.zeros_like(dk_acc)
            dv_acc[...] = jnp.zeros_like(dv_acc)

        lo = _lane_lo((BLOCK, LANES))
        tri, trib = tria_ref[...], trib_ref[...]
        causal = _stacked_causal()
        qblocks = [t * SB_QBLOCKS + c for c in range(SB_QBLOCKS)]
        qss, doss = [], []
        for c in range(SB_QBLOCKS):
            sl = slice(c * BLOCK, (c + 1) * BLOCK)
            qss.append(_stack_heads(q_ref[sl, :].astype(F32) * QK_SCALE, lo).astype(BF16))
            doss.append(_stack_heads(do_ref[sl, :], lo).astype(BF16))
            o2 = o_ref[sl, :]
            dtot_ref[c] = _dot_exact(doss[c].astype(F32) * jnp.concatenate([o2, o2], axis=0), tri[:, BLOCK:])

        def blocks(specs):
            rows = [[_block_rows(j) for j in js] for _, js, _, _, _, _ in specs]
            kbs = [[k_ref[rw, :] for rw in rws] for rws in rows]
            dovs = [[_dot_nt(doss[c], v_ref[rw, :]) for rw in rws] for (c, *_), rws in zip(specs, rows)]
            res = _sb_weights([(qss[c], kbc, r, masks, lives) for (c, _, r, _, masks, lives), kbc in zip(specs, kbs)], tri)
            pws = [[a * dov for a, dov in zip(weights, dovc)] for (_, weights, _), dovc in zip(res, dovs)]
            cps = [[_dot_exact(pw, trib, SB_SUM_TERMS) for pw in pwc] for pwc in pws]
            dzs, sfxs = [], []
            for (c, _, _, sfx, masks, lives), (lsc, _, _), pwc, cpc in zip(specs, res, pws, cps):
                dzc = []
                for ls, pw, cp, mask, live in zip(lsc, pwc, cpc, masks, lives):
                    beta = jnp.exp(ls)
                    before = dtot_ref[c] - sfx - cp[:, :BLOCK]
                    dzc.append(_keep(pw * (1.0 - beta) - before * beta, mask, live).astype(BF16))
                    sfx = sfx + cp[:, BLOCK:]
                dzs.append(dzc)
                sfxs.append(sfx)
            dqs = []
            for dzc, kbc in zip(dzs, kbs):
                dq = 0.0
                for dzb, kb in zip(dzc, kbc):
                    dq = dq + _dot(dzb, kb)
                dqs.append(dq)
            for (c, *_), dzc, (_, weights, _), rws in zip(specs, dzs, res, rows):
                for dzb, a, rw in zip(dzc, weights, rws):
                    dk_acc[rw, :] += _dot_tn(dzb, qss[c])
                    dv_acc[rw, :] += _dot_tn(a.astype(BF16), doss[c])
            return [(dq, r, sfx) for dq, (_, _, r), sfx in zip(dqs, res, sfxs)]

        def peel(c, i):
            js, masks, lives = _sb_peel(i, causal)
            zero = jnp.zeros((2 * BLOCK, LANES), F32)
            return c, js, zero, zero, masks, lives

        for c, (dq, r, sfx) in enumerate(blocks([peel(c, i) for c, i in enumerate(qblocks)])):
            dq_acc[c], r_ref[c], sfx_ref[c] = dq, r, sfx

        for c in range(SB_QBLOCKS):
            def alive(c=c):
                return (jnp.max(r_ref[c]) > DEAD_LOG).astype(jnp.int32)

            def step(carry, c=c):
                (dq, r_ref[c], sfx_ref[c]), = blocks([(c, [carry[0]], r_ref[c], sfx_ref[c], [None], [None])])
                dq_acc[c] += dq
                return carry[0] - 1, alive(c)

            lax.while_loop(lambda carry: jnp.logical_and(carry[0] >= 0, carry[1] > 0), step, (qblocks[c] - SB_PEEL, alive()))
            dq_ref[c * BLOCK:(c + 1) * BLOCK, :] = _unstack_heads(dq_acc[c], lo) * QK_SCALE

        @pl.when(t == nq // SB_QBLOCKS - 1)
        def _():
            outs = [pltpu.make_async_copy(dk_acc, dk_hbm.at[p], sems.at[0]),
                    pltpu.make_async_copy(dv_acc, dv_hbm.at[p], sems.at[1])]
            for cp in outs:
                cp.start()
            for cp in outs:
                cp.wait()

    qtile = SB_QBLOCKS * BLOCK
    blk = pl.BlockSpec((qtile, LANES), lambda p, t: (t, p))
    const = pl.BlockSpec((BLOCK, 2 * BLOCK), lambda p, t: (0, 0))
    any_spec = pl.BlockSpec(memory_space=pl.ANY)
    state = pltpu.VMEM((SB_QBLOCKS, 2 * BLOCK, LANES), F32)
    return pl.pallas_call(
        body, name="sb_bwd", grid=(pairs, seq // qtile),
        in_specs=[blk, pl.BlockSpec((seq, LANES), lambda p, t: (0, pairs + p)),
                  pl.BlockSpec((seq, LANES), lambda p, t: (0, 2 * pairs + p)), blk, blk, const, const],
        out_specs=[blk, any_spec, any_spec],
        out_shape=[jax.ShapeDtypeStruct((seq, W_SB), F32), jax.ShapeDtypeStruct((pairs, seq, LANES), F32),
                   jax.ShapeDtypeStruct((pairs, seq, LANES), F32)],
        scratch_shapes=[state, state, state, state, pltpu.VMEM((seq, LANES), F32), pltpu.VMEM((seq, LANES), F32),
                        pltpu.SemaphoreType.DMA((2,))],
        compiler_params=_params(56),
    )(proj_bf, proj_bf, proj_bf, d_oa, oa, tri_a, tri_b)


def _dil_bwd(qkn, qkv_raw, d_od, lse, delta, r, others=None):
    seq = qkn.shape[0]
    tile, edge, per = _dil_tiling(seq, r)
    ntile = seq // tile
    pairs = W_DIL // LANES

    def body(qc, doc, lsc, dlc, kc, kp, vc, vp, *rest):
        if others is None:
            dq_ref, dk_ref, dv_ref, dk_carry, dv_carry, *scratches = rest
        else:
            dq_in, dk_in, dv_in, dq_ref, dk_ref, dv_ref, dk_carry, dv_carry, *scratches = rest
        n = pl.program_id(1)

        @pl.when(n == 0)
        def _():
            dk_carry[...] = jnp.zeros_like(dk_carry)
            dv_carry[...] = jnp.zeros_like(dv_carry)

        if others is None:
            dk_ref[...] = dk_carry[...]
            dv_ref[...] = dv_carry[...]
        else:
            dk_ref[...] = dk_carry[...] + dk_in[...]
            dv_ref[...] = dv_carry[...] + dv_in[...]

        @pl.when(n < ntile)
        def _():
            lo = _lane_lo((BLOCK, LANES))
            band, col = _band_mask(2 * BLOCK, 2 * BLOCK)
            first_band = jnp.logical_and(band, jnp.logical_or(col >= BLOCK, n > 0))
            take = _class_reader(r, (qc, doc, lsc, dlc, kc, kp, vc, vp), scratches)

            def stacked_cols(b):
                other = pltpu.roll(b, HEAD_DIM, 1)
                rows = jnp.concatenate([jnp.where(lo, b, other), jnp.where(lo, other, b)], axis=0)
                return jnp.concatenate([rows, rows], axis=1)

            def run(blocks):
                rows = [_class_rows(c, a * BLOCK, BLOCK, r) for c, a, _ in blocks]
                own = lambda ref: [take(ref, c, a * BLOCK, BLOCK) for c, a, _ in blocks]
                keys = _dil_keys(blocks, take, kc, kp)
                vals = _dil_keys(blocks, take, vc, vp)
                qss = [_stack_heads(x, lo).astype(BF16) for x in own(qc)]
                doss = [_stack_heads(x, lo).astype(BF16) for x in own(doc)]
                scores = [_dot_nt(qs, kb) for qs, kb in zip(qss, keys)]
                dps = [_dot_nt(dos, vb) for dos, vb in zip(doss, vals)]
                pws, dss = [], []
                for lsb, dlb, s, dp, (_, _, first) in zip(own(lsc), own(dlc), scores, dps, blocks):
                    pw = jnp.where(first_band if first else band, jnp.exp(s - stacked_cols(lsb)), 0.0)
                    pws.append(pw.astype(BF16))
                    dss.append((pw * (dp - stacked_cols(dlb))).astype(BF16))
                dqs = [_dot(ds, kb) for ds, kb in zip(dss, keys)]
                dks = [_dot_tn(ds, qs) for ds, qs in zip(dss, qss)]
                dvs = [_dot_tn(pw, dos) for pw, dos in zip(pws, doss)]
                for (c, a, first), rw, dq, dk, dv in zip(blocks, rows, dqs, dks, dvs):
                    dq_ref[rw, :] = _unstack_heads(dq, lo)
                    if first:
                        last = _class_rows(c, (per - 1) * BLOCK, BLOCK, r)
                        dk_ref[last, :] += dk[:BLOCK]
                        dv_ref[last, :] += dv[:BLOCK]
                    else:
                        prev = _class_rows(c, (a - 1) * BLOCK, BLOCK, r)
                        dk_carry[prev, :] += dk[:BLOCK]
                        dv_carry[prev, :] += dv[:BLOCK]
                    dk_carry[rw, :] = dk[BLOCK:]
                    dv_carry[rw, :] = dv[BLOCK:]

            _for_each_group(r, per, DIL_GROUP_BWD, run)
            if others is not None:
                dq_ref[...] += dq_in[...]

    per_edge = tile // edge
    here = lambda n: jnp.minimum(n, ntile - 1)
    cur = lambda off: pl.BlockSpec((tile, LANES), lambda p, n: (here(n), off + p))
    before = lambda off: pl.BlockSpec((edge, LANES), lambda p, n: (jnp.maximum(here(n) * per_edge - 1, 0), off + p))
    lagged = pl.BlockSpec((tile, LANES), lambda p, n: (jnp.maximum(n - 1, 0), p))
    carry = pltpu.VMEM((tile, LANES), F32)
    return pl.pallas_call(
        body, name=f"dil_bwd_r{r}", grid=(pairs, ntile + 1),
        in_specs=[cur(0)] * 4 + [cur(pairs), before(pairs), cur(2 * pairs), before(2 * pairs)] +
                 ([] if others is None else [cur(0), lagged, lagged]),
        out_specs=[cur(0), lagged, lagged],
        out_shape=[jax.ShapeDtypeStruct((seq, W_DIL), F32)] * 3,
        scratch_shapes=[carry, carry] + _class_scratches(r, (tile, tile, tile, tile, tile, edge, tile, edge)),
        compiler_params=_params(48),
    )(qkn, d_od, lse, delta, qkn, qkn, qkv_raw, qkv_raw, *(others or ()))


def _assemble_dproj(d_qa, d_ka, d_va, d_gates, d_qd, d_kd, d_vd, qk_raw, gains, tables, bd):
    seq = qk_raw.shape[0]
    tm = min(256, seq)
    chunks = W_DIL // (2 * LANES)

    def body(dqa, dka0, dka1, dva0, dva1, dg, dqd, dkd, dvd, x_ref, g_ref, c_ref, sn_ref, sp_ref, bd_ref, dp_ref, gpart_ref):
        @pl.when(pl.program_id(0) == 0)
        def _():
            gpart_ref[...] = jnp.zeros_like(gpart_ref)

        def put(first_col, v):
            dp_ref[:, first_col:first_col + v.shape[1]] = v.astype(BF16)

        put(0, dqa[...])
        put(W_SB, dka0[...])
        put(W_SB + LANES, dka1[...])
        put(2 * W_SB, dva0[...])
        put(2 * W_SB + LANES, dva1[...])
        put(3 * W_SB, dg[:, :W_SB])
        put(4 * W_SB + 2 * W_DIL, dvd[...])
        put(4 * W_SB + 3 * W_DIL, dg[:, W_SB:])
        c, sn, sp, bdm = c_ref[...], sn_ref[...], sp_ref[...], bd_ref[...]
        for which, part in enumerate((dqd, dkd)):
            scale = QK_SCALE if which == 0 else 1.0
            for k in range(chunks):
                sl = slice(k * 2 * LANES, (k + 1) * 2 * LANES)
                dyv = part[:, sl] * scale
                dxn = _rope_t(dyv, c, sn, sp)
                x = x_ref[:, which * W_DIL + k * 2 * LANES:which * W_DIL + (k + 1) * 2 * LANES]
                rs = lax.rsqrt(_head_sums(x * x, bdm) * (1.0 / HEAD_DIM) + EPS)
                xhat = x * rs
                gpart_ref[which] += jnp.sum((dxn * xhat).reshape(tm // 8, 8, 2 * LANES), axis=0)
                dxhat = dxn * g_ref[which]
                mean = _head_sums(dxhat * xhat, bdm) * (1.0 / HEAD_DIM)
                put(4 * W_SB + which * W_DIL + k * 2 * LANES, rs * (dxhat - xhat * mean))

    row = lambda w: pl.BlockSpec((tm, w), lambda i: (i, 0))
    pair = lambda p: pl.BlockSpec((None, tm, LANES), lambda i: (p, i, 0))
    const = lambda shape: pl.BlockSpec(shape, lambda i: tuple(0 for _ in shape))
    return pl.pallas_call(
        body, name="assemble_dproj", grid=(seq // tm,),
        in_specs=[row(W_SB), pair(0), pair(1), pair(0), pair(1), row(MIX)] + [row(W_DIL)] * 3 +
                 [row(2 * W_DIL), const((2, 1, 2 * LANES)), row(2 * LANES), row(2 * LANES), row(2 * LANES),
                  const((2 * LANES, 2 * LANES))],
        out_specs=[row(IN_COLS), const((2, 8, 2 * LANES))],
        out_shape=[jax.ShapeDtypeStruct((seq, IN_COLS), BF16), jax.ShapeDtypeStruct((2, 8, 2 * LANES), F32)],
        compiler_params=_params(48),
    )(d_qa, d_ka, d_ka, d_va, d_va, d_gates, d_qd, d_kd, d_vd, qk_raw, gains, *tables, bd)


def _dw_in(h, dproj, dwout_l, rout, layer):
    seq, d = h.shape
    tm = min(2048, seq)
    steps = seq // tm

    def body(h_ref, dp_ref, dwout_ref, rout_in, dw_ref, rout_ref, *sems):
        del rout_in
        n, i = pl.program_id(0), pl.program_id(1)
        copies = lambda: _to_every_device(lambda to: (dwout_ref.at[to],), lambda me: (rout_ref.at[me, layer],), sems)

        @pl.when(jnp.logical_and(n == 0, i == 0))
        def _():
            for cp in copies():
                cp.start()

        @pl.when(i == 0)
        def _():
            dw_ref[...] = jnp.zeros_like(dw_ref)

        dw_ref[...] += _dot_tn(h_ref[...], dp_ref[...])

        @pl.when(jnp.logical_and(n == N_DEV - 1, i == steps - 1))
        def _():
            for cp in copies():
                cp.wait()

    any_spec = pl.BlockSpec(memory_space=pl.ANY)
    return pl.pallas_call(
        body, name="dw_in_exchange", grid=(N_DEV, steps),
        in_specs=[pl.BlockSpec((tm, d), lambda n, i: (i, 0)), pl.BlockSpec((tm, COLS_PER_DEV), lambda n, i: (i, n)),
                  any_spec, any_spec],
        out_specs=[pl.BlockSpec((None, d, COLS_PER_DEV), lambda n, i: (n, 0, 0)), any_spec],
        out_shape=[jax.ShapeDtypeStruct((N_DEV, d, COLS_PER_DEV), F32), jax.ShapeDtypeStruct(rout.shape, F32)],
        scratch_shapes=_copy_sems(1), input_output_aliases={3: 1},
        compiler_params=_params(40),
    )(h, dproj, dwout_l, rout)


def _dx_norm(dproj, w_l, x, g, dx_next, dwin_l, rin, layer):
    seq, d = x.shape
    tm = min(256, seq)
    steps = seq // tm

    def body(dp_ref, w_ref, x_ref, g_ref, dn_ref, dwin_ref, rin_in, dx_ref, gpart_ref, rin_ref, *sems):
        del rin_in
        copies = lambda: _to_every_device(lambda to: (dwin_ref.at[to],), lambda me: (rin_ref.at[me, layer],), sems)

        @pl.when(pl.program_id(0) == 0)
        def _():
            gpart_ref[...] = jnp.zeros_like(gpart_ref)
            for cp in copies():
                cp.start()

        dh = jnp.zeros((tm, d), F32)
        for n in range(N_DEV):
            dh = dh + _dot_nt(dp_ref[:, n * COLS_PER_DEV:(n + 1) * COLS_PER_DEV], w_ref[n])
        xf = x_ref[...]
        rs = lax.rsqrt(jnp.mean(xf * xf, axis=-1, keepdims=True) + EPS)
        xhat = xf * rs
        gpart_ref[...] += jnp.sum((dh * xhat).reshape(tm // 8, 8, d), axis=0)
        dxhat = dh * g_ref[...]
        mean = jnp.mean(dxhat * xhat, axis=-1, keepdims=True)
        dx_ref[...] = rs * (dxhat - xhat * mean) + dn_ref[...]

        @pl.when(pl.program_id(0) == steps - 1)
        def _():
            for cp in copies():
                cp.wait()

    row = lambda w: pl.BlockSpec((tm, w), lambda i: (i, 0))
    any_spec = pl.BlockSpec(memory_space=pl.ANY)
    return pl.pallas_call(
        body, name="dx_norm_exchange", grid=(steps,),
        in_specs=[row(IN_COLS), pl.BlockSpec((N_DEV, d, COLS_PER_DEV), lambda i: (0, 0, 0)), row(d),
                  pl.BlockSpec((1, d), lambda i: (0, 0)), row(d), any_spec, any_spec],
        out_specs=[row(d), pl.BlockSpec((8, d), lambda i: (0, 0)), any_spec],
        out_shape=[jax.ShapeDtypeStruct((seq, d), F32), jax.ShapeDtypeStruct((8, d), F32),
                   jax.ShapeDtypeStruct(rin.shape, F32)],
        scratch_shapes=_copy_sems(1), input_output_aliases={6: 2},
        compiler_params=_params(48),
    )(dproj, w_l, x, g, dx_next, dwin_l, rin)


def _exchange_small(small):
    def body(small_ref, out_ref, *sems):
        copies = _to_every_device(lambda to: (small_ref,), lambda me: (out_ref.at[me],), sems)
        for cp in copies:
            cp.start()
        for cp in copies:
            cp.wait()

    vmem = pl.BlockSpec(memory_space=pltpu.VMEM)
    return pl.pallas_call(
        body, name="exchange_small", in_specs=[vmem], out_specs=vmem,
        out_shape=jax.ShapeDtypeStruct((N_DEV,) + small.shape, F32), scratch_shapes=_copy_sems(1),
    )(small)


def _adamw_math(g, w, m, v):
    m = ADAM_B1 * m + (1.0 - ADAM_B1) * g
    v = ADAM_B2 * v + (1.0 - ADAM_B2) * (g * g)
    m_hat = m / (1.0 - ADAM_B1 ** ADAM_STEP)
    v_hat = v / (1.0 - ADAM_B2 ** ADAM_STEP)
    delta = -ADAM_LR * (m_hat / (jnp.sqrt(v_hat) + ADAM_EPS) + ADAM_WD * w)
    return delta, m, v


def _adamw(parts, w, m, v, name):
    nl, r, c = w.shape
    tr = min(r, (256 * 512) // c)

    def body(p_ref, w_ref, m_ref, v_ref, g_ref, d_ref, nm_ref, nv_ref):
        g = p_ref[0]
        for s in range(1, N_DEV):
            g = g + p_ref[s]
        g_ref[...] = g
        d_ref[...], nm_ref[...], nv_ref[...] = _adamw_math(g, w_ref[...], m_ref[...], v_ref[...])

    blk = pl.BlockSpec((None, tr, c), lambda l, i: (l, i, 0))
    return pl.pallas_call(
        body, name=name, grid=(nl, r // tr),
        in_specs=[pl.BlockSpec((N_DEV, None, tr, c), lambda l, i: (0, l, i, 0)), blk, blk, blk],
        out_specs=[blk] * 4, out_shape=[jax.ShapeDtypeStruct(w.shape, F32)] * 4,
        compiler_params=_params(32),
    )(parts, w, m, v)


def _adamw_small(parts, w, m, v):
    def body(p_ref, w_ref, m_ref, v_ref, g_ref, d_ref, nm_ref, nv_ref):
        g = p_ref[0]
        for s in range(1, N_DEV):
            g = g + p_ref[s]
        g_ref[...] = g
        d_ref[...], nm_ref[...], nv_ref[...] = _adamw_math(g, w_ref[...], m_ref[...], v_ref[...])

    vmem = pl.BlockSpec(memory_space=pltpu.VMEM)
    return pl.pallas_call(
        body, name="adamw_small", in_specs=[vmem] * 4, out_specs=[vmem] * 4,
        out_shape=[jax.ShapeDtypeStruct(w.shape, F32)] * 4,
    )(parts, w, m, v)


def _pack_small(a, b, c):
    pad = jnp.zeros((a.shape[0], SMALL_W - a.shape[1] - b.shape[1] - c.shape[1]), F32)
    return jnp.concatenate([a, b, c, pad], axis=1)


def _unpack_small(t, d):
    return t[:, :d], t[:, d:d + HEAD_DIM], t[:, d + HEAD_DIM:d + 2 * HEAD_DIM]


def kernel(x, norm_g, w_in, q_norm_g, k_norm_g, w_out, loss_target, m_norm_g, m_w_in, m_q_norm_g, m_k_norm_g, m_w_out,
           v_norm_g, v_w_in, v_q_norm_g, v_k_norm_g, v_w_out):
    depth, d, _ = w_in.shape
    seq = x.shape[1]
    tri_a, tri_b, bd = _tri_constants()
    tables = _rope_tables(seq)
    rep = (2 * LANES) // HEAD_DIM

    win_bf, wout_bf = _cast_bf16(w_in, "cast_w_in"), _cast_bf16(w_out, "cast_w_out")
    win_l, wout_l = _gather_weights(win_bf, wout_bf, 0)
    saved = []
    xl = x.reshape(seq, d)
    for layer in range(depth):
        gains = jnp.stack([jnp.tile(q_norm_g[layer], rep), jnp.tile(k_norm_g[layer], rep)])[:, None, :]
        if layer + 1 < depth:
            proj_bf, gates, qk_raw, h, qkn, win_next, wout_next = _norm_proj(
                xl, norm_g[layer][None, :], win_l, gains, tables, bd, prefetch=(win_bf, wout_bf, layer + 1))
        else:
            proj_bf, gates, qk_raw, h, qkn = _norm_proj(xl, norm_g[layer][None, :], win_l, gains, tables, bd)
            win_next = wout_next = None
        oa = _sb_fwd(proj_bf, tri_a)
        parts = [_dil_fwd(qkn, qk_raw, r) for _, r in DIL_PATTERNS[:-1]]
        od, lse = _dil_fwd(qkn, qk_raw, DIL_PATTERNS[-1][1], others=parts)
        x_next, cat = _out_proj(xl, oa, od, gates, wout_l)
        saved.append((xl, gains, proj_bf, gates, qk_raw, h, qkn, oa, cat, od, lse, win_l, wout_l))
        xl, win_l, wout_l = x_next, win_next, wout_next

    loss_part, dx = _loss_head(xl, loss_target.reshape(seq, d))
    loss = lax.psum(jnp.sum(loss_part), ("x", "y", "c"))

    rin = lax.empty((N_DEV, depth, d, COLS_PER_DEV), F32)
    rout = lax.empty((N_DEV, depth, ROWS_PER_DEV, d), F32)
    g_norm, g_q, g_k = [None] * depth, [None] * depth, [None] * depth
    for layer in reversed(range(depth)):
        xl, gains, proj_bf, gates, qk_raw, h, qkn, oa, cat, od, lse, win_l, wout_l = saved[layer]
        d_oa, d_od, delta, d_gates, dwout_l = _out_proj_bwd(dx, wout_l, cat, gates, oa, od, bd)
        d_qa, d_ka, d_va = _sb_bwd(proj_bf, d_oa, oa, tri_a, tri_b)
        d_dil = None
        for _, r in DIL_PATTERNS:
            d_dil = _dil_bwd(qkn, qk_raw, d_od, lse, delta, r, others=d_dil)
        dproj, gqk = _assemble_dproj(d_qa, d_ka, d_va, d_gates, *d_dil, qk_raw, gains, tables, bd)
        dwin_l, rout = _dw_in(h, dproj, dwout_l, rout, layer)
        dx, gn, rin = _dx_norm(dproj, win_l, xl, norm_g[layer][None, :], dx, dwin_l, rin, layer)
        g_norm[layer] = jnp.sum(gn, axis=0)
        gqk = jnp.sum(gqk, axis=1).reshape(2, rep, HEAD_DIM).sum(axis=1)
        g_q[layer], g_k[layer] = gqk[0], gqk[1]
    rsmall = _exchange_small(_pack_small(jnp.stack(g_norm), jnp.stack(g_q), jnp.stack(g_k)))

    g_in, d_in, nm_in, nv_in = _adamw(rin, w_in, m_w_in, v_w_in, "adamw_w_in")
    g_out, d_out, nm_out, nv_out = _adamw(rout, w_out, m_w_out, v_w_out, "adamw_w_out")
    small_out = _adamw_small(rsmall, _pack_small(norm_g, q_norm_g, k_norm_g), _pack_small(m_norm_g, m_q_norm_g, m_k_norm_g),
                             _pack_small(v_norm_g, v_q_norm_g, v_k_norm_g))
    (g_n, g_qn, g_kn), (d_n, d_qn, d_kn), (nm_n, nm_qn, nm_kn), (nv_n, nv_qn, nv_kn) = (_unpack_small(t, d) for t in small_out)

    return (loss, dx.reshape(x.shape), g_n, g_in, g_qn, g_kn, g_out, d_n, d_in, d_qn, d_kn, d_out,
            nm_n, nm_in, nm_qn, nm_kn, nm_out, nv_n, nv_in, nv_qn, nv_kn, nv_out)
```
